```python
import math
import jax, jax.numpy as jnp
from jax import lax
import numpy as np

D_MODEL = 1024
BATCH = 32
SEQ = 2048
DEPTH = 1
DEC_BATCH = 16
DEC_SEQ = 64
PAST_LEN = 1024

CHUNK = 64
D_MIX = D_MODEL
A_HEADS = 8
A_KV_HEADS = 2
A_GROUP = A_HEADS // A_KV_HEADS
HEAD_DIM = 64
A_WIDTH = A_HEADS * HEAD_DIM
KV_WIDTH = A_KV_HEADS * HEAD_DIM
WINDOW = 128
WIN_CHUNKS = WINDOW // CHUNK
B_HEADS = 4
B_WIDTH = D_MIX - A_WIDTH
B_DIM = B_WIDTH // B_HEADS
MLP_CHUNK = 128
N_GROUPS = 8
EXP_PER_GROUP = 8
N_EXPERTS = N_GROUPS * EXP_PER_GROUP
TOP_K = 2
D_EXPERT = 512
EPS = 1e-6
PROJ_SPLITS = [A_WIDTH, A_WIDTH + KV_WIDTH, A_WIDTH + 2 * KV_WIDTH, A_WIDTH + 2 * KV_WIDTH + B_WIDTH]
PROJ_COLS = A_WIDTH + 2 * KV_WIDTH + 2 * B_WIDTH

kernel_name = "hymba_swa_sink_gmlp_hiermoe_stream_step"


def rms_norm(x, g):
    xf = x.astype(jnp.float32)
    y = xf * lax.rsqrt(jnp.mean(xf * xf, axis=-1, keepdims=True) + EPS)
    return (y * g.astype(jnp.float32)).astype(x.dtype)


def mixer_inputs(h, w_in, g_q, g_k, g_v):
    b, t = h.shape[:2]
    z = h @ w_in
    q, k, v, u, gv = jnp.split(z, PROJ_SPLITS, axis=-1)
    q = rms_norm(q.reshape(b, t, A_KV_HEADS, A_GROUP, HEAD_DIM), g_q)
    k = rms_norm(k.reshape(b, t, A_KV_HEADS, HEAD_DIM), g_k)
    v = v.reshape(b, t, A_KV_HEADS, HEAD_DIM)
    u = jax.nn.gelu(u).reshape(b, t, B_HEADS, B_DIM)
    gv = rms_norm(jax.nn.gelu(gv).reshape(b, t, B_HEADS, B_DIM), g_v)
    return q, k, v, u, gv


def softmax_with_sink(s, sinks):
    sk = sinks.astype(jnp.float32)[:, :, None, None]
    m = jnp.maximum(jnp.max(s, axis=-1, keepdims=True), sk)
    e = jnp.exp(s - m)
    return e / (jnp.sum(e, axis=-1, keepdims=True) + jnp.exp(sk - m))


def window_attn_prompt(q, k, v, sinks):
    b, s = q.shape[:2]
    nc = s // CHUNK
    qc = q.reshape(b, nc, CHUNK, A_KV_HEADS, A_GROUP, HEAD_DIM)
    pad = ((0, 0), (WIN_CHUNKS * CHUNK, 0), (0, 0), (0, 0))
    kp = jnp.pad(k, pad).reshape(b, nc + WIN_CHUNKS, CHUNK, A_KV_HEADS, HEAD_DIM)
    vp = jnp.pad(v, pad).reshape(b, nc + WIN_CHUNKS, CHUNK, A_KV_HEADS, HEAD_DIM)
    kb = jnp.concatenate([kp[:, j:j + nc] for j in range(WIN_CHUNKS + 1)], axis=2)
    vb = jnp.concatenate([vp[:, j:j + nc] for j in range(WIN_CHUNKS + 1)], axis=2)
    key_chunk = (jnp.arange(nc)[:, None] - WIN_CHUNKS
                 + jnp.repeat(jnp.arange(WIN_CHUNKS + 1), CHUNK)[None, :])
    valid = key_chunk >= 0
    sc = jnp.einsum('bcqkgd,bcskd->bckgqs', qc, kb,
                    preferred_element_type=jnp.float32) * (HEAD_DIM ** -0.5)
    sc = jnp.where(valid[None, :, None, None, None, :], sc, -jnp.inf)
    p = softmax_with_sink(sc, sinks)
    o = jnp.einsum('bckgqs,bcskd->bcqkgd', p.astype(vb.dtype), vb)
    return o.reshape(b, s, A_WIDTH)


def window_attn_sample(q, k, v, cache_k, cache_v, sinks):
    b, t = q.shape[:2]
    k_all = jnp.concatenate([cache_k.astype(k.dtype), k], axis=1)
    v_all = jnp.concatenate([cache_v.astype(v.dtype), v], axis=1)
    sc = jnp.einsum('btkgd,bskd->bkgts', q, k_all,
                    preferred_element_type=jnp.float32) * (HEAD_DIM ** -0.5)
    p = softmax_with_sink(sc, sinks)
    o = jnp.einsum('bkgts,bskd->btkgd', p.astype(v_all.dtype), v_all)
    return o.reshape(b, t, A_WIDTH), k_all[:, -WINDOW:], v_all[:, -WINDOW:]


def causal_spatial_weights(w_s):
    mask = jnp.tril(jnp.ones((MLP_CHUNK, MLP_CHUNK), dtype=bool))
    return jnp.where(mask[None], w_s, jnp.zeros_like(w_s))


def spatial_gate_prompt(u, gv, w_s, b_s):
    b, s = u.shape[:2]
    n = s // MLP_CHUNK
    vc = gv.reshape(b, n, MLP_CHUNK, B_HEADS, B_DIM)
    mix = jnp.einsum('hij,bnjhd->bnihd', causal_spatial_weights(w_s), vc) + b_s.T[:, :, None]
    return (u.reshape(b, n, MLP_CHUNK, B_HEADS, B_DIM) * mix).reshape(b, s, B_WIDTH)


def spatial_gate_sample(u, gv, w_s, b_s):
    b, t = u.shape[:2]
    w = causal_spatial_weights(w_s)[:, :t, :t]
    mix = jnp.einsum('hij,bjhd->bihd', w, gv) + b_s[:, :t].T[:, :, None]
    return (u * mix).reshape(b, t, B_WIDTH)


def hier_moe(h, w_coarse, b_coarse, w_fine, b_fine, w_gate, w_up, w_down):
    shp = h.shape
    xf = h.reshape(-1, D_MODEL)
    n = xf.shape[0]
    lg = (xf @ w_coarse).astype(jnp.float32) + b_coarse.astype(jnp.float32)
    pg = jax.nn.softmax(lg, axis=-1)
    grp = jnp.argmax(lg, axis=-1)
    p_group = jnp.take_along_axis(pg, grp[:, None], axis=-1)
    lf_all = jnp.einsum('nd,gde->nge', xf, w_fine).astype(jnp.float32)
    lf = jnp.take_along_axis(lf_all, grp[:, None, None], axis=1)[:, 0] + b_fine[grp].astype(jnp.float32)
    top_v, top_i = lax.top_k(lf, TOP_K)
    gate = p_group * jax.nn.softmax(top_v, axis=-1)
    eid = (grp[:, None] * EXP_PER_GROUP + top_i).reshape(-1)
    tok = jnp.repeat(jnp.arange(n), TOP_K)
    wgt = gate.reshape(-1)
    order = jnp.argsort(eid)
    tok_s = tok[order]
    xs = xf[tok_s]
    sizes = jnp.bincount(eid, length=N_EXPERTS).astype(jnp.int32)
    hg = lax.ragged_dot(xs, w_gate, sizes)
    hu = lax.ragged_dot(xs, w_up, sizes)
    out = lax.ragged_dot(jax.nn.silu(hg) * hu, w_down, sizes)
    y = jnp.zeros_like(xf).at[tok_s].add(out * wgt[order][:, None].astype(out.dtype))
    return y.reshape(shp)


def setup_inputs(seed: int = 0) -> dict:
    key = jax.random.key(seed)
    ks = jax.random.split(key, 24)

    def nrm(k, shape, scale):
        return jax.random.normal(k, shape, jnp.float32) * scale

    return {
        "x_prompt": nrm(ks[0], (BATCH, SEQ, D_MODEL), 1.0),
        "x_sample": nrm(ks[1], (DEC_BATCH, DEC_SEQ, D_MODEL), 1.0),
        "cache_attn_k": nrm(ks[2], (DEPTH, DEC_BATCH, WINDOW, A_KV_HEADS, HEAD_DIM), 1.0),
        "cache_attn_v": nrm(ks[3], (DEPTH, DEC_BATCH, WINDOW, A_KV_HEADS, HEAD_DIM), 0.5),
        "g_mix": 1.0 + nrm(ks[4], (DEPTH, D_MODEL), 0.1),
        "w_in": nrm(ks[5], (DEPTH, D_MODEL, PROJ_COLS), D_MODEL ** -0.5),
        "g_q": 1.0 + nrm(ks[6], (DEPTH, HEAD_DIM), 0.1),
        "g_k": 1.0 + nrm(ks[7], (DEPTH, HEAD_DIM), 0.1),
        "g_v": 1.0 + nrm(ks[8], (DEPTH, B_HEADS, B_DIM), 0.1),
        "attn_sinks": nrm(ks[9], (DEPTH, A_KV_HEADS, A_GROUP), 0.5),
        "w_s": nrm(ks[10], (DEPTH, B_HEADS, MLP_CHUNK, MLP_CHUNK), MLP_CHUNK ** -0.5),
        "b_s": 1.0 + nrm(ks[11], (DEPTH, B_HEADS, MLP_CHUNK), 0.1),
        "w_out": nrm(ks[12], (DEPTH, D_MIX, D_MODEL), D_MIX ** -0.5),
        "g_ffn": 1.0 + nrm(ks[13], (DEPTH, D_MODEL), 0.1),
        "w_coarse": nrm(ks[14], (DEPTH, D_MODEL, N_GROUPS), D_MODEL ** -0.5),
        "b_coarse": nrm(ks[15], (DEPTH, N_GROUPS), 0.01),
        "w_fine": nrm(ks[16], (DEPTH, N_GROUPS, D_MODEL, EXP_PER_GROUP), D_MODEL ** -0.5),
        "b_fine": nrm(ks[17], (DEPTH, N_GROUPS, EXP_PER_GROUP), 0.01),
        "w_gate": nrm(ks[18], (DEPTH, N_EXPERTS, D_MODEL, D_EXPERT), D_MODEL ** -0.5),
        "w_up": nrm(ks[19], (DEPTH, N_EXPERTS, D_MODEL, D_EXPERT), D_MODEL ** -0.5),
        "w_down": nrm(ks[20], (DEPTH, N_EXPERTS, D_EXPERT, D_MODEL), D_EXPERT ** -0.5),
    }


def reference(x_prompt, x_sample, cache_attn_k, cache_attn_v, g_mix, w_in, g_q, g_k, g_v,
              attn_sinks, w_s, b_s, w_out, g_ffn, w_coarse, b_coarse, w_fine, b_fine,
              w_gate, w_up, w_down):
    xp, xs = x_prompt, x_sample
    kp_list, vp_list, ks_list, vs_list, gv_list = [], [], [], [], []
    for l in range(DEPTH):
        hp = rms_norm(xp, g_mix[l])
        q, k, v, u, gv = mixer_inputs(hp, w_in[l], g_q[l], g_k[l], g_v[l])
        a_out = window_attn_prompt(q, k, v, attn_sinks[l])
        b_out = spatial_gate_prompt(u, gv, w_s[l], b_s[l])
        xp = xp + jnp.concatenate([a_out, b_out], axis=-1) @ w_out[l]
        kp_list.append(k[:, -WINDOW:])
        vp_list.append(v[:, -WINDOW:])
        hs = rms_norm(xs, g_mix[l])
        q, k, v, u, gv = mixer_inputs(hs, w_in[l], g_q[l], g_k[l], g_v[l])
        a_out, nk, nv = window_attn_sample(q, k, v, cache_attn_k[l], cache_attn_v[l], attn_sinks[l])
        b_out = spatial_gate_sample(u, gv, w_s[l], b_s[l])
        xs = xs + jnp.concatenate([a_out, b_out], axis=-1) @ w_out[l]
        ks_list.append(nk)
        vs_list.append(nv)
        gv_list.append(gv)
        xp = xp + hier_moe(rms_norm(xp, g_ffn[l]), w_coarse[l], b_coarse[l], w_fine[l], b_fine[l],
                           w_gate[l], w_up[l], w_down[l])
        xs = xs + hier_moe(rms_norm(xs, g_ffn[l]), w_coarse[l], b_coarse[l], w_fine[l], b_fine[l],
                           w_gate[l], w_up[l], w_down[l])
    new_k_prompt = jnp.stack(kp_list)
    new_v_prompt = jnp.stack(vp_list)
    new_k_sample = jnp.stack(ks_list)
    new_v_sample = jnp.stack(vs_list)
    new_gv_sample = jnp.stack(gv_list)
    return (xp, xs, new_k_prompt, new_v_prompt, new_k_sample, new_v_sample, new_gv_sample)
```

```python
import functools

import jax
import jax.numpy as jnp
from jax import lax
from jax.experimental import pallas as pl
from jax.experimental.pallas import tpu as pltpu

D_MODEL = 1024
HEAD_DIM = 64
N_Q_HEADS = 8
N_KV_HEADS = 2
Q_WIDTH = N_Q_HEADS * HEAD_DIM
KV_WIDTH = N_KV_HEADS * HEAD_DIM
GATE_HEADS = 4
GATE_DIM = 128
GATE_WIDTH = GATE_HEADS * GATE_DIM
PROJ_COLS = Q_WIDTH + 2 * KV_WIDTH + 2 * GATE_WIDTH
CHUNK = 64
WINDOW = 128
MLP_CHUNK = 128
N_GROUPS = 8
EXP_PER_GROUP = 8
N_EXPERTS = N_GROUPS * EXP_PER_GROUP
D_EXPERT = 512
EPS = 1e-6

LANES = 128
ROUTE_LANES = 128
FINE_LANE0 = N_GROUPS
NEG = -1e30

TOKEN_TILE = 512
EXPERT_TILE = 256
VMEM_LIMIT = 48 * 1024 * 1024


def _rms(x, eps=EPS):
    return x * lax.rsqrt(jnp.mean(x * x, axis=-1, keepdims=True) + eps)


def _project(x, gmix, win, gq, gk, gvg, blk):
    h = (_rms(x) * gmix).astype(jnp.bfloat16)
    z = jnp.dot(h, win, preferred_element_type=jnp.float32)
    qk = z[:, :Q_WIDTH + KV_WIDTH]
    outs = []
    for j in range((Q_WIDTH + KV_WIDTH) // LANES):
        zj = qk[:, j * LANES:(j + 1) * LANES]
        ss = jnp.dot((zj * zj).astype(jnp.bfloat16), blk, preferred_element_type=jnp.float32)
        outs.append(zj * lax.rsqrt(ss * (1.0 / HEAD_DIM) + EPS))
    qn = jnp.concatenate(outs[:Q_WIDTH // LANES], axis=-1) * gq
    kn = outs[-1] * gk
    v = z[:, Q_WIDTH + KV_WIDTH:Q_WIDTH + 2 * KV_WIDTH]
    u0 = Q_WIDTH + 2 * KV_WIDTH
    ua = jax.nn.gelu(z[:, u0:u0 + GATE_WIDTH])
    ga = jax.nn.gelu(z[:, u0 + GATE_WIDTH:])
    gvn = jnp.concatenate(
        [_rms(ga[:, i * GATE_DIM:(i + 1) * GATE_DIM]) for i in range(GATE_HEADS)], axis=-1) * gvg
    return qn.astype(jnp.bfloat16), kn, v, ua, gvn


def _dup_halves(a):
    lo = lax.broadcasted_iota(jnp.int32, a.shape, 1) < HEAD_DIM
    r = pltpu.roll(a, HEAD_DIM, axis=1)
    return (jnp.where(lo, a, r).astype(jnp.bfloat16), jnp.where(lo, r, a).astype(jnp.bfloat16))


def _attn_block(qa, qb, k2, v2, bias, sk):
    r = qa.shape[0]
    lo = lax.broadcasted_iota(jnp.int32, (r, LANES), 1) < HEAD_DIM
    zero = jnp.zeros_like(qa)
    qs = jnp.concatenate([jnp.where(lo, qa, zero), jnp.where(lo, zero, qa),
                          jnp.where(lo, qb, zero), jnp.where(lo, zero, qb)], axis=0)
    s = lax.dot_general(qs, k2, (((1,), (1,)), ((), ())), preferred_element_type=jnp.float32)
    if bias is not None:
        s = s + bias
    m = jnp.maximum(jnp.max(s, axis=-1, keepdims=True), sk)
    e = jnp.exp(s - m)
    den = jnp.sum(e, axis=-1, keepdims=True) + jnp.exp(sk - m)
    o = jnp.dot(e.astype(jnp.bfloat16), v2, preferred_element_type=jnp.float32) * (1.0 / den)
    pa = jnp.where(lo, o[0:r], o[r:2 * r])
    pb = jnp.where(lo, o[2 * r:3 * r], o[3 * r:4 * r])
    return pa, pb


def _sink_rows(sink_ref, g, r):
    row = lax.broadcasted_iota(jnp.int32, (4 * r, 1), 0)
    s0, s1, s2, s3 = (sink_ref[4 * g + i] for i in range(4))
    return jnp.where(row < r, s0, jnp.where(row < 2 * r, s1, jnp.where(row < 3 * r, s2, s3)))


def _causal_ws(ws_ref, h, n):
    w = ws_ref[h][:n, :n]
    keep = (lax.broadcasted_iota(jnp.int32, (n, n), 0) >= lax.broadcasted_iota(jnp.int32, (n, n), 1))
    return jnp.where(keep, w, jnp.zeros_like(w))


def _post(cat, x, wout, gffn, wrh, wrl, br, tril, cnt):
    x1 = x + jnp.dot(cat, wout, preferred_element_type=jnp.float32)
    h2 = _rms(x1) * gffn
    hh = h2.astype(jnp.bfloat16)
    hl = (h2 - hh.astype(jnp.float32)).astype(jnp.bfloat16)
    lg = (jnp.dot(hh, wrh, preferred_element_type=jnp.float32)
          + jnp.dot(hl, wrh, preferred_element_type=jnp.float32)
          + jnp.dot(hh, wrl, preferred_element_type=jnp.float32)) + br
    t = lg.shape[0]
    lane = lax.broadcasted_iota(jnp.int32, (t, ROUTE_LANES), 1).astype(jnp.float32)
    far = float(ROUTE_LANES)
    lc = jnp.where(lane < N_GROUPS, lg, -jnp.inf)
    mc = jnp.max(lc, axis=-1, keepdims=True)
    grp = jnp.min(jnp.where(lc == mc, lane, far), axis=-1, keepdims=True)
    pg = 1.0 / jnp.sum(jnp.exp(lc - mc), axis=-1, keepdims=True)
    f0 = FINE_LANE0 + EXP_PER_GROUP * grp
    lf = jnp.where((lane >= f0) & (lane < f0 + EXP_PER_GROUP), lg, -jnp.inf)
    v1 = jnp.max(lf, axis=-1, keepdims=True)
    i1 = jnp.min(jnp.where(lf == v1, lane, far), axis=-1, keepdims=True)
    lf2 = jnp.where(lane == i1, -jnp.inf, lf)
    v2 = jnp.max(lf2, axis=-1, keepdims=True)
    i2 = jnp.min(jnp.where(lf2 == v2, lane, far), axis=-1, keepdims=True)
    tt = jnp.exp(v2 - v1)
    w1 = 1.0 / (1.0 + tt)
    w2 = tt * w1
    sel1 = lane == i1
    sel2 = lane == i2
    oh = jnp.where(sel1 | sel2, 1.0, 0.0)
    c = jnp.dot(tril, oh.astype(jnp.bfloat16), preferred_element_type=jnp.float32) + cnt
    r1 = jnp.sum(jnp.where(sel1, c, 0.0), axis=-1, keepdims=True)
    r2 = jnp.sum(jnp.where(sel2, c, 0.0), axis=-1, keepdims=True)
    new_cnt = cnt + jnp.sum(oh, axis=0, keepdims=True)
    slab = jnp.where(lane == 0, i1 - FINE_LANE0,
           jnp.where(lane == 1, i2 - FINE_LANE0,
           jnp.where(lane == 2, r1,
           jnp.where(lane == 3, r2,
           jnp.where(lane == 4, pg * w1,
           jnp.where(lane == 5, pg * w2, 0.0))))))
    return x1, hh, slab, new_cnt


def _prompt_kernel(sink_ref, x_ref, gmix_ref, win_ref, gq_ref, gk_ref, gvg_ref, blk_ref, ws_ref, bs_ref,
                   wout_ref, gffn_ref, wrh_ref, wrl_ref, br_ref, tril_ref, cnt_in_ref,
                   x1_ref, h2_ref, slab_ref, kwin_ref, vwin_ref, cnt_ref,
                   k2_scr, v2_scr, qn_scr, cat_scr, cnt_scr):
    b = pl.program_id(0)
    t = pl.program_id(1)
    nt = pl.num_programs(1)
    tt = x_ref.shape[1]
    nblk = tt // WINDOW

    @pl.when((b == 0) & (t == 0))
    def _():
        cnt_scr[...] = cnt_in_ref[...]

    @pl.when(t == 0)
    def _():
        k2_scr[:, 0:WINDOW, :] = jnp.zeros((N_KV_HEADS, WINDOW, LANES), jnp.bfloat16)
        v2_scr[:, 0:WINDOW, :] = jnp.zeros((N_KV_HEADS, WINDOW, LANES), jnp.bfloat16)

    x = x_ref[0]
    qn, kn, v, ua, gvn = _project(x, gmix_ref[...], win_ref[...], gq_ref[...], gk_ref[...], gvg_ref[...],
                                  blk_ref[...])
    qn_scr[...] = qn
    k0, k1 = _dup_halves(kn)
    v0, v1 = _dup_halves(v)
    k2_scr[0, WINDOW:, :] = k0
    k2_scr[1, WINDOW:, :] = k1
    v2_scr[0, WINDOW:, :] = v0
    v2_scr[1, WINDOW:, :] = v1

    @pl.when(t == nt - 1)
    def _():
        kwin_ref[0] = kn[tt - WINDOW:]
        vwin_ref[0] = v[tt - WINDOW:]

    rows = 4 * WINDOW
    band = 2 * WINDOW
    rr = lax.broadcasted_iota(jnp.int32, (rows, band), 0)
    kk = lax.broadcasted_iota(jnp.int32, (rows, band), 1)
    half = ((rr % WINDOW) >= CHUNK).astype(jnp.int32)
    allowed = (kk >= half * CHUNK) & (kk < (3 + half) * CHUNK)
    bias_mid = jnp.where(allowed, 0.0, NEG)
    bias_first = jnp.where(allowed & (kk >= WINDOW), 0.0, NEG)

    def attn_body(j, carry):
        r0 = pl.multiple_of(j * WINDOW, WINDOW)
        bias = jnp.where((t == 0) & (j == 0), bias_first, bias_mid)
        for g in range(N_KV_HEADS):
            c0 = g * 2 * LANES
            qa = qn_scr[pl.ds(r0, WINDOW), c0:c0 + LANES]
            qb = qn_scr[pl.ds(r0, WINDOW), c0 + LANES:c0 + 2 * LANES]
            k2 = k2_scr[g, pl.ds(r0, band), :]
            v2 = v2_scr[g, pl.ds(r0, band), :]
            pa, pb = _attn_block(qa, qb, k2, v2, bias, _sink_rows(sink_ref, g, WINDOW))
            cat_scr[pl.ds(r0, WINDOW), c0:c0 + LANES] = pa.astype(jnp.bfloat16)
            cat_scr[pl.ds(r0, WINDOW), c0 + LANES:c0 + 2 * LANES] = pb.astype(jnp.bfloat16)
        return carry

    lax.fori_loop(0, nblk, attn_body, 0)

    k2_scr[:, 0:WINDOW, :] = k2_scr[:, tt:tt + WINDOW, :]
    v2_scr[:, 0:WINDOW, :] = v2_scr[:, tt:tt + WINDOW, :]

    for h in range(GATE_HEADS):
        w = _causal_ws(ws_ref, h, MLP_CHUNK)
        bcol = bs_ref[:, h:h + 1]
        for c in range(tt // MLP_CHUNK):
            rs = slice(c * MLP_CHUNK, (c + 1) * MLP_CHUNK)
            cs = slice(h * GATE_DIM, (h + 1) * GATE_DIM)
            mix = jnp.dot(w, gvn[rs, cs].astype(jnp.bfloat16), preferred_element_type=jnp.float32) + bcol
            cat_scr[rs, Q_WIDTH + h * GATE_DIM:Q_WIDTH + (h + 1) * GATE_DIM] = (ua[rs, cs] * mix).astype(jnp.bfloat16)

    x1, hh, slab, new_cnt = _post(cat_scr[...], x, wout_ref[...], gffn_ref[...], wrh_ref[...], wrl_ref[...],
                                  br_ref[...], tril_ref[...], cnt_scr[...])
    x1_ref[...] = x1
    h2_ref[...] = hh
    slab_ref[...] = slab
    cnt_scr[...] = new_cnt
    cnt_ref[...] = new_cnt


def _sample_kernel(sink_ref, x_ref, ck_ref, cv_ref, gmix_ref, win_ref, gq_ref, gk_ref, gvg_ref, blk_ref, ws_ref,
                   bs_ref, wout_ref, gffn_ref, wrh_ref, wrl_ref, br_ref, tril_ref,
                   x1_ref, h2_ref, slab_ref, kn_ref, v_ref, gvn_ref, cnt_ref,
                   ck2_scr, cv2_scr, k2_scr, v2_scr, qn_scr, ua_scr, cat_scr, *, n_seq, seq_len):
    x = x_ref[...]
    qn, kn, v, ua, gvn = _project(x, gmix_ref[...], win_ref[...], gq_ref[...], gk_ref[...], gvg_ref[...],
                                  blk_ref[...])
    kn_ref[...] = kn
    v_ref[...] = v
    gvn_ref[...] = gvn
    qn_scr[...] = qn
    ua_scr[...] = ua
    for scr, val in ((k2_scr, kn), (v2_scr, v), (ck2_scr, ck_ref[...]), (cv2_scr, cv_ref[...])):
        a0, a1 = _dup_halves(val)
        scr[0] = a0
        scr[1] = a1

    ws = [_causal_ws(ws_ref, h, seq_len) for h in range(GATE_HEADS)]

    def seq_body(i, carry):
        r0 = pl.multiple_of(i * seq_len, seq_len)
        c0r = pl.multiple_of(i * WINDOW, WINDOW)
        for g in range(N_KV_HEADS):
            c0 = g * 2 * LANES
            qa = qn_scr[pl.ds(r0, seq_len), c0:c0 + LANES]
            qb = qn_scr[pl.ds(r0, seq_len), c0 + LANES:c0 + 2 * LANES]
            k2 = jnp.concatenate([ck2_scr[g, pl.ds(c0r, WINDOW), :], k2_scr[g, pl.ds(r0, seq_len), :]], axis=0)
            v2 = jnp.concatenate([cv2_scr[g, pl.ds(c0r, WINDOW), :], v2_scr[g, pl.ds(r0, seq_len), :]], axis=0)
            pa, pb = _attn_block(qa, qb, k2, v2, None, _sink_rows(sink_ref, g, seq_len))
            cat_scr[pl.ds(r0, seq_len), c0:c0 + LANES] = pa.astype(jnp.bfloat16)
            cat_scr[pl.ds(r0, seq_len), c0 + LANES:c0 + 2 * LANES] = pb.astype(jnp.bfloat16)
        for h in range(GATE_HEADS):
            cs = slice(h * GATE_DIM, (h + 1) * GATE_DIM)
            gv_h = gvn_ref[pl.ds(r0, seq_len), cs].astype(jnp.bfloat16)
            mix = jnp.dot(ws[h], gv_h, preferred_element_type=jnp.float32) + bs_ref[0:seq_len, h:h + 1]
            cat_scr[pl.ds(r0, seq_len), Q_WIDTH + h * GATE_DIM:Q_WIDTH + (h + 1) * GATE_DIM] = (
                ua_scr[pl.ds(r0, seq_len), cs] * mix).astype(jnp.bfloat16)
        return carry

    lax.fori_loop(0, n_seq, seq_body, 0)

    cnt0 = jnp.zeros((1, ROUTE_LANES), jnp.float32)
    x1, hh, slab, new_cnt = _post(cat_scr[...], x, wout_ref[...], gffn_ref[...], wrh_ref[...], wrl_ref[...],
                                  br_ref[...], tril_ref[...], cnt0)
    x1_ref[...] = x1
    h2_ref[...] = hh
    slab_ref[...] = slab
    cnt_ref[...] = new_cnt


def _expert_kernel(te_ref, nu_ref, xs_ref, wg_ref, wu_ref, wd_ref, o_ref):
    del te_ref

    @pl.when(pl.program_id(0) < nu_ref[0])
    def _():
        xs = xs_ref[...]
        hg = jnp.dot(xs, wg_ref[0], preferred_element_type=jnp.float32)
        hu = jnp.dot(xs, wu_ref[0], preferred_element_type=jnp.float32)
        a = (jax.nn.silu(hg) * hu).astype(jnp.bfloat16)
        o_ref[...] = jnp.dot(a, wd_ref[0], preferred_element_type=jnp.float32).astype(o_ref.dtype)


def _combine_kernel(x1_ref, o1_ref, o2_ref, slab_ref, y_ref):
    slab = slab_ref[...]
    g1 = slab[:, 4:5]
    g2 = slab[:, 5:6]
    y_ref[...] = x1_ref[...] + (g1 * o1_ref[...].astype(jnp.float32) + g2 * o2_ref[...].astype(jnp.float32))


def _const_spec(shape):
    nd = len(shape)
    return pl.BlockSpec(shape, lambda *_: (0,) * nd)


def kernel(x_prompt, x_sample, cache_attn_k, cache_attn_v, g_mix, w_in, g_q, g_k, g_v, attn_sinks, w_s, b_s,
           w_out, g_ffn, w_coarse, b_coarse, w_fine, b_fine, w_gate, w_up, w_down):
    nb, seq, d = x_prompt.shape
    ns, slen, _ = x_sample.shape
    n_p = nb * seq
    n_s = ns * slen
    n_tok = n_p + n_s
    tt = TOKEN_TILE
    assert seq % tt == 0 and n_s % tt == 0 and d == D_MODEL
    nt = seq // tt
    bf = jnp.bfloat16
    f32 = jnp.float32

    l = 0
    gmix = g_mix[l].reshape(1, d)
    win = w_in[l].astype(bf)
    gq = (jnp.tile(g_q[l], N_Q_HEADS) * (HEAD_DIM ** -0.5)).reshape(1, Q_WIDTH)
    gk = jnp.tile(g_k[l], N_KV_HEADS).reshape(1, KV_WIDTH)
    gvg = g_v[l].reshape(1, GATE_WIDTH)
    sinks = attn_sinks[l].reshape(N_Q_HEADS).astype(f32)
    ws = w_s[l].astype(bf)
    bs = b_s[l].T
    wout = w_out[l].astype(bf)
    gffn = g_ffn[l].reshape(1, d)
    wr = jnp.concatenate([w_coarse[l], jnp.transpose(w_fine[l], (1, 0, 2)).reshape(d, N_EXPERTS),
                          jnp.zeros((d, ROUTE_LANES - N_GROUPS - N_EXPERTS), f32)], axis=1)
    wrh = wr.astype(bf)
    wrl = (wr - wrh.astype(f32)).astype(bf)
    br = jnp.concatenate([b_coarse[l], b_fine[l].reshape(-1),
                          jnp.zeros((ROUTE_LANES - N_GROUPS - N_EXPERTS,), f32)]).reshape(1, ROUTE_LANES)
    ii = jnp.arange(LANES)
    blk = (ii[:, None] // HEAD_DIM == ii[None, :] // HEAD_DIM).astype(bf)
    wg = w_gate[l].astype(bf)
    wu = w_up[l].astype(bf)
    wd = w_down[l].astype(bf)

    def strict_tril(n):
        r = jnp.arange(n)
        return (r[:, None] > r[None, :]).astype(bf)

    weight_args = (gmix, win, gq, gk, gvg, blk, ws, bs, wout, gffn, wrh, wrl, br)
    weight_specs = [_const_spec(a.shape) for a in weight_args]
    smem_spec = pl.BlockSpec(memory_space=pltpu.SMEM)

    xs2 = x_sample.reshape(n_s, d)
    ck = cache_attn_k[l].reshape(ns * WINDOW, KV_WIDTH)
    cv = cache_attn_v[l].reshape(ns * WINDOW, KV_WIDTH)
    tok_out = lambda n, w, dt: jax.ShapeDtypeStruct((n, w), dt)
    x1_s, h2_s, slab_s, kn_s, v_s, gvn_s, cnt_s = pl.pallas_call(
        functools.partial(_sample_kernel, n_seq=ns, seq_len=slen),
        grid=(1,),
        in_specs=[smem_spec, _const_spec((n_s, d)), _const_spec(ck.shape), _const_spec(cv.shape)]
                 + weight_specs + [_const_spec((n_s, n_s))],
        out_specs=[_const_spec((n_s, d)), _const_spec((n_s, d)), _const_spec((n_s, ROUTE_LANES)),
                   _const_spec((n_s, KV_WIDTH)), _const_spec((n_s, KV_WIDTH)), _const_spec((n_s, GATE_WIDTH)),
                   _const_spec((1, ROUTE_LANES))],
        out_shape=[tok_out(n_s, d, f32), tok_out(n_s, d, bf), tok_out(n_s, ROUTE_LANES, f32),
                   jax.ShapeDtypeStruct((n_s, KV_WIDTH), f32), jax.ShapeDtypeStruct((n_s, KV_WIDTH), f32),
                   jax.ShapeDtypeStruct((n_s, GATE_WIDTH), f32), jax.ShapeDtypeStruct((1, ROUTE_LANES), f32)],
        scratch_shapes=[pltpu.VMEM((N_KV_HEADS, ns * WINDOW, LANES), bf), pltpu.VMEM((N_KV_HEADS, ns * WINDOW, LANES), bf),
                        pltpu.VMEM((N_KV_HEADS, n_s, LANES), bf), pltpu.VMEM((N_KV_HEADS, n_s, LANES), bf),
                        pltpu.VMEM((n_s, Q_WIDTH), bf), pltpu.VMEM((n_s, GATE_WIDTH), f32),
                        pltpu.VMEM((n_s, d), bf)],
        compiler_params=pltpu.CompilerParams(dimension_semantics=("arbitrary",), vmem_limit_bytes=VMEM_LIMIT),
        name="mixer_sample",
    )(sinks, xs2, ck, cv, *weight_args, strict_tril(n_s))

    x1_p, h2_p, slab_p, kwin, vwin, cnt = pl.pallas_call(
        _prompt_kernel,
        grid=(nb, nt),
        in_specs=[smem_spec, pl.BlockSpec((1, tt, d), lambda b, t: (b, t, 0))] + weight_specs
                 + [_const_spec((tt, tt)), _const_spec((1, ROUTE_LANES))],
        out_specs=[pl.BlockSpec((tt, d), lambda b, t: (b * nt + t, 0)),
                   pl.BlockSpec((tt, d), lambda b, t: (b * nt + t, 0)),
                   pl.BlockSpec((tt, ROUTE_LANES), lambda b, t: (b * nt + t, 0)),
                   pl.BlockSpec((1, WINDOW, KV_WIDTH), lambda b, t: (b, 0, 0)),
                   pl.BlockSpec((1, WINDOW, KV_WIDTH), lambda b, t: (b, 0, 0)),
                   _const_spec((1, ROUTE_LANES))],
        out_shape=[tok_out(n_p, d, f32), tok_out(n_p, d, bf), tok_out(n_p, ROUTE_LANES, f32),
                   jax.ShapeDtypeStruct((nb, WINDOW, KV_WIDTH), f32),
                   jax.ShapeDtypeStruct((nb, WINDOW, KV_WIDTH), f32),
                   jax.ShapeDtypeStruct((1, ROUTE_LANES), f32)],
        scratch_shapes=[pltpu.VMEM((N_KV_HEADS, tt + WINDOW, LANES), bf),
                        pltpu.VMEM((N_KV_HEADS, tt + WINDOW, LANES), bf),
                        pltpu.VMEM((tt, Q_WIDTH), bf), pltpu.VMEM((tt, d), bf),
                        pltpu.VMEM((1, ROUTE_LANES), f32)],
        compiler_params=pltpu.CompilerParams(dimension_semantics=("arbitrary", "arbitrary"),
                                             vmem_limit_bytes=VMEM_LIMIT),
        name="mixer_prompt",
    )(sinks, x_prompt, *weight_args, strict_tril(tt), cnt_s)
    slab_all = jnp.concatenate([slab_p[:, :8], slab_s[:, :8]], axis=0)

    tm = EXPERT_TILE
    counts = cnt[0, FINE_LANE0:FINE_LANE0 + N_EXPERTS].astype(jnp.int32)
    tiles = (counts + tm - 1) // tm
    tile_end = jnp.cumsum(tiles)
    offs = (tile_end - tiles) * tm
    n_used = tile_end[-1]
    n_tiles = (2 * n_tok) // tm + N_EXPERTS
    n_rows = n_tiles * tm
    tile_expert = jnp.minimum(jnp.searchsorted(tile_end, jnp.arange(n_tiles, dtype=jnp.int32), side="right"),
                              N_EXPERTS - 1).astype(jnp.int32)
    eid = slab_all[:, 0:2].astype(jnp.int32)
    rank = slab_all[:, 2:4].astype(jnp.int32)
    pos = offs[eid] + rank

    src = jnp.zeros((n_rows,), jnp.int32).at[pos.reshape(-1)].set(
        jnp.repeat(jnp.arange(n_tok, dtype=jnp.int32), 2))
    xs_sorted = jnp.concatenate([h2_p, h2_s], axis=0)[src]

    def row_map(i, te, nu):
        return (jnp.minimum(i, nu[0] - 1), 0)

    def w_map(i, te, nu):
        return (te[jnp.minimum(i, nu[0] - 1)], 0, 0)

    out_sorted = pl.pallas_call(
        _expert_kernel,
        grid_spec=pltpu.PrefetchScalarGridSpec(
            num_scalar_prefetch=2,
            grid=(n_tiles,),
            in_specs=[pl.BlockSpec((tm, d), row_map),
                      pl.BlockSpec((1, d, D_EXPERT), w_map),
                      pl.BlockSpec((1, d, D_EXPERT), w_map),
                      pl.BlockSpec((1, D_EXPERT, d), w_map)],
            out_specs=pl.BlockSpec((tm, d), row_map)),
        out_shape=jax.ShapeDtypeStruct((n_rows, d), bf),
        compiler_params=pltpu.CompilerParams(dimension_semantics=("arbitrary",), vmem_limit_bytes=VMEM_LIMIT),
        name="experts",
    )(tile_expert, n_used.reshape(1).astype(jnp.int32), xs_sorted, wg, wu, wd)

    o1 = out_sorted[pos[:, 0]]
    o2 = out_sorted[pos[:, 1]]

    def combine(x1, slab, blk0):
        n = x1.shape[0]
        tok = lambda i: (i, 0)
        off = lambda i: (blk0 + i, 0)
        return pl.pallas_call(
            _combine_kernel,
            grid=(n // tt,),
            in_specs=[pl.BlockSpec((tt, d), tok), pl.BlockSpec((tt, d), off), pl.BlockSpec((tt, d), off),
                      pl.BlockSpec((tt, ROUTE_LANES), tok)],
            out_specs=pl.BlockSpec((tt, d), tok),
            out_shape=jax.ShapeDtypeStruct((n, d), f32),
            compiler_params=pltpu.CompilerParams(dimension_semantics=("arbitrary",)),
            name="combine",
        )(x1, o1, o2, slab)

    y_p = combine(x1_p, slab_p, 0).reshape(nb, seq, d)
    y_s = combine(x1_s, slab_s, n_p // tt).reshape(ns, slen, d)

    kv_shape = (1, nb, WINDOW, N_KV_HEADS, HEAD_DIM)
    new_k_p = kwin.reshape(kv_shape)
    new_v_p = vwin.reshape(kv_shape)
    keep = WINDOW - slen
    ck4 = cache_attn_k[l][:, WINDOW - keep:]
    cv4 = cache_attn_v[l][:, WINDOW - keep:]
    new_k_s = jnp.concatenate([ck4, kn_s.reshape(ns, slen, N_KV_HEADS, HEAD_DIM)], axis=1)[None]
    new_v_s = jnp.concatenate([cv4, v_s.reshape(ns, slen, N_KV_HEADS, HEAD_DIM)], axis=1)[None]
    new_gv_s = gvn_s.reshape(1, ns, slen, GATE_HEADS, GATE_DIM)
    return (y_p, y_s, new_k_p, new_v_p, new_k_s, new_v_s, new_gv_s)
```

```python
import functools

import jax
import jax.numpy as jnp
from jax import lax
from jax.experimental import pallas as pl
from jax.experimental.pallas import tpu as pltpu
from jax.experimental.pallas import tpu_sc as plsc

D_MODEL = 1024
HEAD_DIM = 64
N_Q_HEADS = 8
N_KV_HEADS = 2
Q_WIDTH = N_Q_HEADS * HEAD_DIM
KV_WIDTH = N_KV_HEADS * HEAD_DIM
GATE_HEADS = 4
GATE_DIM = 128
GATE_WIDTH = GATE_HEADS * GATE_DIM
PROJ_COLS = Q_WIDTH + 2 * KV_WIDTH + 2 * GATE_WIDTH
CHUNK = 64
WINDOW = 128
MLP_CHUNK = 128
N_GROUPS = 8
EXP_PER_GROUP = 8
N_EXPERTS = N_GROUPS * EXP_PER_GROUP
D_EXPERT = 512
EPS = 1e-6

LANES = 128
ROUTE_LANES = 128
FINE_LANE0 = N_GROUPS
NEG = -1e30

TOKEN_TILE = 512
EXPERT_TILE = 512
SC_WINDOW = 128
SC_COLS = 256
VMEM_LIMIT = 48 * 1024 * 1024


def _pack_halves(x):
    w = x.shape[1] // 2
    b = lax.bitcast_convert_type(x.astype(jnp.bfloat16).astype(jnp.float32), jnp.uint32)
    return (b[:, :w] & jnp.uint32(0xFFFF0000)) | (b[:, w:] >> 16)


def _unpack_halves(p):
    hi = lax.bitcast_convert_type(p & jnp.uint32(0xFFFF0000), jnp.float32)
    lo = lax.bitcast_convert_type(p << 16, jnp.float32)
    return hi, lo


def _rms(x, eps=EPS):
    return x * lax.rsqrt(jnp.mean(x * x, axis=-1, keepdims=True) + eps)


def _project(x, gmix, win, gq, gk, gvg, blk):
    h = (_rms(x) * gmix).astype(jnp.bfloat16)
    z = jnp.dot(h, win, preferred_element_type=jnp.float32)
    qk = z[:, :Q_WIDTH + KV_WIDTH]
    outs = []
    for j in range((Q_WIDTH + KV_WIDTH) // LANES):
        zj = qk[:, j * LANES:(j + 1) * LANES]
        ss = jnp.dot((zj * zj).astype(jnp.bfloat16), blk, preferred_element_type=jnp.float32)
        outs.append(zj * lax.rsqrt(ss * (1.0 / HEAD_DIM) + EPS))
    qn = jnp.concatenate(outs[:Q_WIDTH // LANES], axis=-1) * gq
    kn = outs[-1] * gk
    v = z[:, Q_WIDTH + KV_WIDTH:Q_WIDTH + 2 * KV_WIDTH]
    u0 = Q_WIDTH + 2 * KV_WIDTH
    ua = jax.nn.gelu(z[:, u0:u0 + GATE_WIDTH])
    ga = jax.nn.gelu(z[:, u0 + GATE_WIDTH:])
    gvn = jnp.concatenate(
        [_rms(ga[:, i * GATE_DIM:(i + 1) * GATE_DIM]) for i in range(GATE_HEADS)], axis=-1) * gvg
    return qn.astype(jnp.bfloat16), kn, v, ua, gvn


def _dup_halves(a):
    lo = lax.broadcasted_iota(jnp.int32, a.shape, 1) < HEAD_DIM
    r = pltpu.roll(a, HEAD_DIM, axis=1)
    return (jnp.where(lo, a, r).astype(jnp.bfloat16), jnp.where(lo, r, a).astype(jnp.bfloat16))


def _attn_block(qa, qb, k2, v2, bias, sk):
    r = qa.shape[0]
    lo = lax.broadcasted_iota(jnp.int32, (r, LANES), 1) < HEAD_DIM
    zero = jnp.zeros_like(qa)
    qs = jnp.concatenate([jnp.where(lo, qa, zero), jnp.where(lo, zero, qa),
                          jnp.where(lo, qb, zero), jnp.where(lo, zero, qb)], axis=0)
    s = lax.dot_general(qs, k2, (((1,), (1,)), ((), ())), preferred_element_type=jnp.float32)
    if bias is not None:
        s = s + bias
    m = jnp.maximum(jnp.max(s, axis=-1, keepdims=True), sk)
    e = jnp.exp(s - m)
    den = jnp.sum(e, axis=-1, keepdims=True) + jnp.exp(sk - m)
    o = jnp.dot(e.astype(jnp.bfloat16), v2, preferred_element_type=jnp.float32) * (1.0 / den)
    pa = jnp.where(lo, o[0:r], o[r:2 * r])
    pb = jnp.where(lo, o[2 * r:3 * r], o[3 * r:4 * r])
    return pa, pb


def _sink_rows(sink_ref, g, r):
    row = lax.broadcasted_iota(jnp.int32, (4 * r, 1), 0)
    s0, s1, s2, s3 = (sink_ref[4 * g + i] for i in range(4))
    return jnp.where(row < r, s0, jnp.where(row < 2 * r, s1, jnp.where(row < 3 * r, s2, s3)))


def _causal_ws(ws_ref, h, n):
    w = ws_ref[h][:n, :n]
    keep = (lax.broadcasted_iota(jnp.int32, (n, n), 0) >= lax.broadcasted_iota(jnp.int32, (n, n), 1))
    return jnp.where(keep, w, jnp.zeros_like(w))


def _post(cat, x, wout, gffn, wrh, wrl, br, tril, cnt):
    x1 = x + jnp.dot(cat, wout, preferred_element_type=jnp.float32)
    h2 = _rms(x1) * gffn
    hh = h2.astype(jnp.bfloat16)
    hl = (h2 - hh.astype(jnp.float32)).astype(jnp.bfloat16)
    lg = (jnp.dot(hh, wrh, preferred_element_type=jnp.float32)
          + jnp.dot(hl, wrh, preferred_element_type=jnp.float32)
          + jnp.dot(hh, wrl, preferred_element_type=jnp.float32)) + br
    t = lg.shape[0]
    lane = lax.broadcasted_iota(jnp.int32, (t, ROUTE_LANES), 1).astype(jnp.float32)
    far = float(ROUTE_LANES)
    lc = jnp.where(lane < N_GROUPS, lg, -jnp.inf)
    mc = jnp.max(lc, axis=-1, keepdims=True)
    grp = jnp.min(jnp.where(lc == mc, lane, far), axis=-1, keepdims=True)
    pg = 1.0 / jnp.sum(jnp.exp(lc - mc), axis=-1, keepdims=True)
    f0 = FINE_LANE0 + EXP_PER_GROUP * grp
    lf = jnp.where((lane >= f0) & (lane < f0 + EXP_PER_GROUP), lg, -jnp.inf)
    v1 = jnp.max(lf, axis=-1, keepdims=True)
    i1 = jnp.min(jnp.where(lf == v1, lane, far), axis=-1, keepdims=True)
    lf2 = jnp.where(lane == i1, -jnp.inf, lf)
    v2 = jnp.max(lf2, axis=-1, keepdims=True)
    i2 = jnp.min(jnp.where(lf2 == v2, lane, far), axis=-1, keepdims=True)
    tt = jnp.exp(v2 - v1)
    w1 = 1.0 / (1.0 + tt)
    w2 = tt * w1
    sel1 = lane == i1
    sel2 = lane == i2
    oh = jnp.where(sel1 | sel2, 1.0, 0.0)
    c = jnp.dot(tril, oh.astype(jnp.bfloat16), preferred_element_type=jnp.float32) + cnt
    r1 = jnp.sum(jnp.where(sel1, c, 0.0), axis=-1, keepdims=True)
    r2 = jnp.sum(jnp.where(sel2, c, 0.0), axis=-1, keepdims=True)
    new_cnt = cnt + jnp.sum(oh, axis=0, keepdims=True)
    slab = jnp.where(lane == 0, i1 - FINE_LANE0,
           jnp.where(lane == 1, i2 - FINE_LANE0,
           jnp.where(lane == 2, r1,
           jnp.where(lane == 3, r2,
           jnp.where(lane == 4, pg * w1,
           jnp.where(lane == 5, pg * w2, 0.0))))))
    return x1, _pack_halves(h2), slab, new_cnt


def _prompt_kernel(sink_ref, x_ref, gmix_ref, win_ref, gq_ref, gk_ref, gvg_ref, blk_ref, ws_ref, bs_ref,
                   wout_ref, gffn_ref, wrh_ref, wrl_ref, br_ref, tril_ref, cnt_in_ref,
                   x1_ref, h2_ref, slab_ref, kwin_ref, vwin_ref, cnt_ref,
                   k2_scr, v2_scr, qn_scr, cat_scr, cnt_scr):
    b = pl.program_id(0)
    t = pl.program_id(1)
    nt = pl.num_programs(1)
    tt = x_ref.shape[1]
    nblk = tt // WINDOW

    @pl.when((b == 0) & (t == 0))
    def _():
        cnt_scr[...] = cnt_in_ref[...]

    @pl.when(t == 0)
    def _():
        k2_scr[:, 0:WINDOW, :] = jnp.zeros((N_KV_HEADS, WINDOW, LANES), jnp.bfloat16)
        v2_scr[:, 0:WINDOW, :] = jnp.zeros((N_KV_HEADS, WINDOW, LANES), jnp.bfloat16)

    x = x_ref[0]
    qn, kn, v, ua, gvn = _project(x, gmix_ref[...], win_ref[...], gq_ref[...], gk_ref[...], gvg_ref[...],
                                  blk_ref[...])
    qn_scr[...] = qn
    k0, k1 = _dup_halves(kn)
    v0, v1 = _dup_halves(v)
    k2_scr[0, WINDOW:, :] = k0
    k2_scr[1, WINDOW:, :] = k1
    v2_scr[0, WINDOW:, :] = v0
    v2_scr[1, WINDOW:, :] = v1

    @pl.when(t == nt - 1)
    def _():
        kwin_ref[0] = kn[tt - WINDOW:]
        vwin_ref[0] = v[tt - WINDOW:]

    rows = 4 * WINDOW
    band = 2 * WINDOW
    rr = lax.broadcasted_iota(jnp.int32, (rows, band), 0)
    kk = lax.broadcasted_iota(jnp.int32, (rows, band), 1)
    half = ((rr % WINDOW) >= CHUNK).astype(jnp.int32)
    allowed = (kk >= half * CHUNK) & (kk < (3 + half) * CHUNK)
    bias_mid = jnp.where(allowed, 0.0, NEG)
    bias_first = jnp.where(allowed & (kk >= WINDOW), 0.0, NEG)

    def attn_body(j, carry):
        r0 = pl.multiple_of(j * WINDOW, WINDOW)
        bias = jnp.where((t == 0) & (j == 0), bias_first, bias_mid)
        for g in range(N_KV_HEADS):
            c0 = g * 2 * LANES
            qa = qn_scr[pl.ds(r0, WINDOW), c0:c0 + LANES]
            qb = qn_scr[pl.ds(r0, WINDOW), c0 + LANES:c0 + 2 * LANES]
            k2 = k2_scr[g, pl.ds(r0, band), :]
            v2 = v2_scr[g, pl.ds(r0, band), :]
            pa, pb = _attn_block(qa, qb, k2, v2, bias, _sink_rows(sink_ref, g, WINDOW))
            cat_scr[pl.ds(r0, WINDOW), c0:c0 + LANES] = pa.astype(jnp.bfloat16)
            cat_scr[pl.ds(r0, WINDOW), c0 + LANES:c0 + 2 * LANES] = pb.astype(jnp.bfloat16)
        return carry

    lax.fori_loop(0, nblk, attn_body, 0)

    k2_scr[:, 0:WINDOW, :] = k2_scr[:, tt:tt + WINDOW, :]
    v2_scr[:, 0:WINDOW, :] = v2_scr[:, tt:tt + WINDOW, :]

    for h in range(GATE_HEADS):
        w = _causal_ws(ws_ref, h, MLP_CHUNK)
        bcol = bs_ref[:, h:h + 1]
        for c in range(tt // MLP_CHUNK):
            rs = slice(c * MLP_CHUNK, (c + 1) * MLP_CHUNK)
            cs = slice(h * GATE_DIM, (h + 1) * GATE_DIM)
            mix = jnp.dot(w, gvn[rs, cs].astype(jnp.bfloat16), preferred_element_type=jnp.float32) + bcol
            cat_scr[rs, Q_WIDTH + h * GATE_DIM:Q_WIDTH + (h + 1) * GATE_DIM] = (ua[rs, cs] * mix).astype(jnp.bfloat16)

    x1, hh, slab, new_cnt = _post(cat_scr[...], x, wout_ref[...], gffn_ref[...], wrh_ref[...], wrl_ref[...],
                                  br_ref[...], tril_ref[...], cnt_scr[...])
    x1_ref[...] = x1
    h2_ref[...] = hh
    slab_ref[...] = slab
    cnt_scr[...] = new_cnt
    cnt_ref[...] = new_cnt


def _sample_kernel(sink_ref, x_ref, ck_ref, cv_ref, gmix_ref, win_ref, gq_ref, gk_ref, gvg_ref, blk_ref, ws_ref,
                   bs_ref, wout_ref, gffn_ref, wrh_ref, wrl_ref, br_ref, tril_ref,
                   x1_ref, h2_ref, slab_ref, kn_ref, v_ref, gvn_ref, cnt_ref,
                   ck2_scr, cv2_scr, k2_scr, v2_scr, qn_scr, ua_scr, cat_scr, *, n_seq, seq_len):
    x = x_ref[...]
    qn, kn, v, ua, gvn = _project(x, gmix_ref[...], win_ref[...], gq_ref[...], gk_ref[...], gvg_ref[...],
                                  blk_ref[...])
    kn_ref[...] = kn
    v_ref[...] = v
    gvn_ref[...] = gvn
    qn_scr[...] = qn
    ua_scr[...] = ua
    for scr, val in ((k2_scr, kn), (v2_scr, v), (ck2_scr, ck_ref[...]), (cv2_scr, cv_ref[...])):
        a0, a1 = _dup_halves(val)
        scr[0] = a0
        scr[1] = a1

    ws = [_causal_ws(ws_ref, h, seq_len) for h in range(GATE_HEADS)]

    def seq_body(i, carry):
        r0 = pl.multiple_of(i * seq_len, seq_len)
        c0r = pl.multiple_of(i * WINDOW, WINDOW)
        for g in range(N_KV_HEADS):
            c0 = g * 2 * LANES
            qa = qn_scr[pl.ds(r0, seq_len), c0:c0 + LANES]
            qb = qn_scr[pl.ds(r0, seq_len), c0 + LANES:c0 + 2 * LANES]
            k2 = jnp.concatenate([ck2_scr[g, pl.ds(c0r, WINDOW), :], k2_scr[g, pl.ds(r0, seq_len), :]], axis=0)
            v2 = jnp.concatenate([cv2_scr[g, pl.ds(c0r, WINDOW), :], v2_scr[g, pl.ds(r0, seq_len), :]], axis=0)
            pa, pb = _attn_block(qa, qb, k2, v2, None, _sink_rows(sink_ref, g, seq_len))
            cat_scr[pl.ds(r0, seq_len), c0:c0 + LANES] = pa.astype(jnp.bfloat16)
            cat_scr[pl.ds(r0, seq_len), c0 + LANES:c0 + 2 * LANES] = pb.astype(jnp.bfloat16)
        for h in range(GATE_HEADS):
            cs = slice(h * GATE_DIM, (h + 1) * GATE_DIM)
            gv_h = gvn_ref[pl.ds(r0, seq_len), cs].astype(jnp.bfloat16)
            mix = jnp.dot(ws[h], gv_h, preferred_element_type=jnp.float32) + bs_ref[0:seq_len, h:h + 1]
            cat_scr[pl.ds(r0, seq_len), Q_WIDTH + h * GATE_DIM:Q_WIDTH + (h + 1) * GATE_DIM] = (
                ua_scr[pl.ds(r0, seq_len), cs] * mix).astype(jnp.bfloat16)
        return carry

    lax.fori_loop(0, n_seq, seq_body, 0)

    cnt0 = jnp.zeros((1, ROUTE_LANES), jnp.float32)
    x1, hh, slab, new_cnt = _post(cat_scr[...], x, wout_ref[...], gffn_ref[...], wrh_ref[...], wrl_ref[...],
                                  br_ref[...], tril_ref[...], cnt0)
    x1_ref[...] = x1
    h2_ref[...] = hh
    slab_ref[...] = slab
    cnt_ref[...] = new_cnt


def _expert_kernel(te_ref, nv_ref, nu_ref, xa_ref, xb_ref, wg_ref, wu_ref, wd_ref, oa_ref, ob_ref,
                   wg_s, wu_s, wd_s):
    i = pl.program_id(0)

    @pl.when(i < nu_ref[0])
    def _():
        @pl.when((i == 0) | (te_ref[i] != te_ref[jnp.maximum(i - 1, 0)]))
        def _():
            wg_s[...] = wg_ref[0].astype(jnp.bfloat16)
            wu_s[...] = wu_ref[0].astype(jnp.bfloat16)
            wd_s[...] = wd_ref[0].astype(jnp.bfloat16)

        live = lax.broadcasted_iota(jnp.int32, xa_ref.shape, 0) < nv_ref[i]
        ha, la = _unpack_halves(jnp.where(live, xa_ref[...], jnp.uint32(0)))
        hb, lb = _unpack_halves(jnp.where(live, xb_ref[...], jnp.uint32(0)))
        xs = jnp.concatenate([ha, hb, la, lb], axis=1).astype(jnp.bfloat16)
        hg = jnp.dot(xs, wg_s[...], preferred_element_type=jnp.float32)
        hu = jnp.dot(xs, wu_s[...], preferred_element_type=jnp.float32)
        a = (jax.nn.silu(hg) * hu).astype(jnp.bfloat16)
        packed = _pack_halves(jnp.dot(a, wd_s[...], preferred_element_type=jnp.float32))
        q = packed.shape[1] // 2
        oa_ref[...] = packed[:, :q]
        ob_ref[...] = packed[:, q:]


def _combine_kernel(x1_ref, o1a_ref, o1b_ref, o2a_ref, o2b_ref, slab_ref, y_ref):
    slab = slab_ref[...]
    g1 = slab[:, 4:5]
    g2 = slab[:, 5:6]
    q = o1a_ref.shape[1]
    for c, (r1, r2) in enumerate(((o1a_ref, o2a_ref), (o1b_ref, o2b_ref))):
        h1, l1 = _unpack_halves(r1[...])
        h2, l2 = _unpack_halves(r2[...])
        hs = slice(c * q, (c + 1) * q)
        ls = slice(2 * q + c * q, 2 * q + (c + 1) * q)
        y_ref[:, hs] = x1_ref[:, hs] + (g1 * h1 + g2 * h2)
        y_ref[:, ls] = x1_ref[:, ls] + (g1 * l1 + g2 * l2)


def _sc_mesh():
    return plsc.VectorSubcoreMesh(core_axis_name="core", subcore_axis_name="subcore")


def _sc_scatter_rows(parts, pos0, pos1, n_rows):
    n_piece = parts[0].shape[1] // SC_COLS

    def sc_kernel(*refs):
        x_refs = refs[:len(parts)]
        i0_hbm, i1_hbm = refs[len(parts):len(parts) + 2]
        o_refs = refs[len(parts) + 2:]
        tok0 = 0
        for x_hbm, arr in zip(x_refs, parts):
            blk0 = tok0 // SC_WINDOW
            for c in range(n_piece):
                def body(x_vmem, i0_vmem, i1_vmem, o_hbm=o_refs[c]):
                    pltpu.sync_copy(x_vmem, o_hbm.at[i0_vmem.at[0]])
                    pltpu.sync_copy(x_vmem, o_hbm.at[i1_vmem.at[0]])

                pltpu.emit_pipeline(
                    body,
                    grid=(arr.shape[0] // SC_WINDOW,),
                    in_specs=[pl.BlockSpec((SC_WINDOW, SC_COLS), lambda i, c=c: (i, c)),
                              pl.BlockSpec((1, SC_WINDOW), lambda i, blk0=blk0: (0, blk0 + i)),
                              pl.BlockSpec((1, SC_WINDOW), lambda i, blk0=blk0: (0, blk0 + i))],
                    out_specs=[],
                    core_axis_name=("core", "subcore"),
                    dimension_semantics=(pltpu.PARALLEL,),
                )(x_hbm, i0_hbm, i1_hbm)
            tok0 += arr.shape[0]

    piece = jax.ShapeDtypeStruct((n_rows, SC_COLS), jnp.uint32)
    return pl.kernel(sc_kernel, out_type=(piece,) * n_piece, mesh=_sc_mesh(), name="scatter_rows")(
        *parts, pos0, pos1)


def _sc_gather_rows(pieces, pos_list):
    n_tok = pos_list[0].shape[1]

    def sc_kernel(*refs):
        s_refs = refs[:len(pieces)]
        i_refs = refs[len(pieces):len(pieces) + len(pos_list)]
        o_refs = refs[len(pieces) + len(pos_list):]
        k = 0
        for i_hbm in i_refs:
            for s_hbm in s_refs:
                def body(i_vmem, o_vmem, s_hbm=s_hbm):
                    pltpu.sync_copy(s_hbm.at[i_vmem.at[0]], o_vmem)

                pltpu.emit_pipeline(
                    body,
                    grid=(n_tok // SC_WINDOW,),
                    in_specs=[pl.BlockSpec((1, SC_WINDOW), lambda i: (0, i))],
                    out_specs=[pl.BlockSpec((SC_WINDOW, SC_COLS), lambda i: (i, 0))],
                    core_axis_name=("core", "subcore"),
                    dimension_semantics=(pltpu.PARALLEL,),
                )(i_hbm, o_refs[k])
                k += 1

    out = jax.ShapeDtypeStruct((n_tok, SC_COLS), jnp.uint32)
    outs = pl.kernel(sc_kernel, out_type=(out,) * (len(pieces) * len(pos_list)), mesh=_sc_mesh(),
                     name="gather_rows")(*pieces, *pos_list)
    return [outs[j * len(pieces):(j + 1) * len(pieces)] for j in range(len(pos_list))]


def _const_spec(shape):
    nd = len(shape)
    return pl.BlockSpec(shape, lambda *_: (0,) * nd)


def kernel(x_prompt, x_sample, cache_attn_k, cache_attn_v, g_mix, w_in, g_q, g_k, g_v, attn_sinks, w_s, b_s,
           w_out, g_ffn, w_coarse, b_coarse, w_fine, b_fine, w_gate, w_up, w_down):
    nb, seq, d = x_prompt.shape
    ns, slen, _ = x_sample.shape
    n_p = nb * seq
    n_s = ns * slen
    n_tok = n_p + n_s
    tt = TOKEN_TILE
    assert seq % tt == 0 and n_s % tt == 0 and d == D_MODEL
    nt = seq // tt
    bf = jnp.bfloat16
    f32 = jnp.float32

    l = 0
    gmix = g_mix[l].reshape(1, d)
    win = w_in[l].astype(bf)
    gq = (jnp.tile(g_q[l], N_Q_HEADS) * (HEAD_DIM ** -0.5)).reshape(1, Q_WIDTH)
    gk = jnp.tile(g_k[l], N_KV_HEADS).reshape(1, KV_WIDTH)
    gvg = g_v[l].reshape(1, GATE_WIDTH)
    sinks = attn_sinks[l].reshape(N_Q_HEADS).astype(f32)
    ws = w_s[l].astype(bf)
    bs = b_s[l].T
    wout = w_out[l].astype(bf)
    gffn = g_ffn[l].reshape(1, d)
    wr = jnp.concatenate([w_coarse[l], jnp.transpose(w_fine[l], (1, 0, 2)).reshape(d, N_EXPERTS),
                          jnp.zeros((d, ROUTE_LANES - N_GROUPS - N_EXPERTS), f32)], axis=1)
    wrh = wr.astype(bf)
    wrl = (wr - wrh.astype(f32)).astype(bf)
    br = jnp.concatenate([b_coarse[l], b_fine[l].reshape(-1),
                          jnp.zeros((ROUTE_LANES - N_GROUPS - N_EXPERTS,), f32)]).reshape(1, ROUTE_LANES)
    ii = jnp.arange(LANES)
    blk = (ii[:, None] // HEAD_DIM == ii[None, :] // HEAD_DIM).astype(bf)
    u32 = jnp.uint32
    dh = d // 2

    def strict_tril(n):
        r = jnp.arange(n)
        return (r[:, None] > r[None, :]).astype(bf)

    weight_args = (gmix, win, gq, gk, gvg, blk, ws, bs, wout, gffn, wrh, wrl, br)
    weight_specs = [_const_spec(a.shape) for a in weight_args]
    smem_spec = pl.BlockSpec(memory_space=pltpu.SMEM)

    xs2 = x_sample.reshape(n_s, d)
    ck = cache_attn_k[l].reshape(ns * WINDOW, KV_WIDTH)
    cv = cache_attn_v[l].reshape(ns * WINDOW, KV_WIDTH)
    tok_out = lambda n, w, dt: jax.ShapeDtypeStruct((n, w), dt)
    x1_s, h2_s, slab_s, kn_s, v_s, gvn_s, cnt_s = pl.pallas_call(
        functools.partial(_sample_kernel, n_seq=ns, seq_len=slen),
        grid=(1,),
        in_specs=[smem_spec, _const_spec((n_s, d)), _const_spec(ck.shape), _const_spec(cv.shape)]
                 + weight_specs + [_const_spec((n_s, n_s))],
        out_specs=[_const_spec((n_s, d)), _const_spec((n_s, dh)), _const_spec((n_s, ROUTE_LANES)),
                   _const_spec((n_s, KV_WIDTH)), _const_spec((n_s, KV_WIDTH)), _const_spec((n_s, GATE_WIDTH)),
                   _const_spec((1, ROUTE_LANES))],
        out_shape=[tok_out(n_s, d, f32), tok_out(n_s, dh, u32), tok_out(n_s, ROUTE_LANES, f32),
                   jax.ShapeDtypeStruct((n_s, KV_WIDTH), f32), jax.ShapeDtypeStruct((n_s, KV_WIDTH), f32),
                   jax.ShapeDtypeStruct((n_s, GATE_WIDTH), f32), jax.ShapeDtypeStruct((1, ROUTE_LANES), f32)],
        scratch_shapes=[pltpu.VMEM((N_KV_HEADS, ns * WINDOW, LANES), bf), pltpu.VMEM((N_KV_HEADS, ns * WINDOW, LANES), bf),
                        pltpu.VMEM((N_KV_HEADS, n_s, LANES), bf), pltpu.VMEM((N_KV_HEADS, n_s, LANES), bf),
                        pltpu.VMEM((n_s, Q_WIDTH), bf), pltpu.VMEM((n_s, GATE_WIDTH), f32),
                        pltpu.VMEM((n_s, d), bf)],
        compiler_params=pltpu.CompilerParams(dimension_semantics=("arbitrary",), vmem_limit_bytes=VMEM_LIMIT),
        name="mixer_sample",
    )(sinks, xs2, ck, cv, *weight_args, strict_tril(n_s))

    x1_p, h2_p, slab_p, kwin, vwin, cnt = pl.pallas_call(
        _prompt_kernel,
        grid=(nb, nt),
        in_specs=[smem_spec, pl.BlockSpec((1, tt, d), lambda b, t: (b, t, 0))] + weight_specs
                 + [_const_spec((tt, tt)), _const_spec((1, ROUTE_LANES))],
        out_specs=[pl.BlockSpec((tt, d), lambda b, t: (b * nt + t, 0)),
                   pl.BlockSpec((tt, dh), lambda b, t: (b * nt + t, 0)),
                   pl.BlockSpec((tt, ROUTE_LANES), lambda b, t: (b * nt + t, 0)),
                   pl.BlockSpec((1, WINDOW, KV_WIDTH), lambda b, t: (b, 0, 0)),
                   pl.BlockSpec((1, WINDOW, KV_WIDTH), lambda b, t: (b, 0, 0)),
                   _const_spec((1, ROUTE_LANES))],
        out_shape=[tok_out(n_p, d, f32), tok_out(n_p, dh, u32), tok_out(n_p, ROUTE_LANES, f32),
                   jax.ShapeDtypeStruct((nb, WINDOW, KV_WIDTH), f32),
                   jax.ShapeDtypeStruct((nb, WINDOW, KV_WIDTH), f32),
                   jax.ShapeDtypeStruct((1, ROUTE_LANES), f32)],
        scratch_shapes=[pltpu.VMEM((N_KV_HEADS, tt + WINDOW, LANES), bf),
                        pltpu.VMEM((N_KV_HEADS, tt + WINDOW, LANES), bf),
                        pltpu.VMEM((tt, Q_WIDTH), bf), pltpu.VMEM((tt, d), bf),
                        pltpu.VMEM((1, ROUTE_LANES), f32)],
        compiler_params=pltpu.CompilerParams(dimension_semantics=("arbitrary", "arbitrary"),
                                             vmem_limit_bytes=VMEM_LIMIT),
        name="mixer_prompt",
    )(sinks, x_prompt, *weight_args, strict_tril(tt), cnt_s)
    slab_all = jnp.concatenate([slab_p[:, :8], slab_s[:, :8]], axis=0)

    tm = EXPERT_TILE
    counts = cnt[0, FINE_LANE0:FINE_LANE0 + N_EXPERTS].astype(jnp.int32)
    tiles = (counts + tm - 1) // tm
    tile_end = jnp.cumsum(tiles)
    offs = (tile_end - tiles) * tm
    n_used = tile_end[-1]
    n_tiles = (2 * n_tok) // tm + N_EXPERTS
    n_rows = n_tiles * tm
    tile_ids = jnp.arange(n_tiles, dtype=jnp.int32)
    tile_expert = jnp.minimum(jnp.sum(tile_ids[:, None] >= tile_end[None, :], axis=1), N_EXPERTS - 1).astype(jnp.int32)
    tile_in_expert = tile_ids - (tile_end - tiles)[tile_expert]
    tile_rows = jnp.clip(counts[tile_expert] - tile_in_expert * tm, 0, tm).astype(jnp.int32)
    eid = slab_all[:, 0:2].astype(jnp.int32)
    rank = slab_all[:, 2:4].astype(jnp.int32)
    off_of = jnp.sum(jnp.where(eid[:, :, None] == jnp.arange(N_EXPERTS, dtype=jnp.int32), offs, 0), axis=-1)
    pos = off_of + rank
    pos0 = pos[:, 0].reshape(1, n_tok)
    pos1 = pos[:, 1].reshape(1, n_tok)

    assert dh == 2 * SC_COLS and n_p % SC_WINDOW == 0 and n_s % SC_WINDOW == 0
    xs_a, xs_b = _sc_scatter_rows([h2_p, h2_s], pos0, pos1, n_rows)

    def row_map(i, te, nv, nu):
        return (jnp.minimum(i, nu[0] - 1), 0)

    def w_map(i, te, nv, nu):
        return (te[jnp.minimum(i, nu[0] - 1)], 0, 0)

    piece = jax.ShapeDtypeStruct((n_rows, SC_COLS), u32)
    out_a, out_b = pl.pallas_call(
        _expert_kernel,
        grid_spec=pltpu.PrefetchScalarGridSpec(
            num_scalar_prefetch=3,
            grid=(n_tiles,),
            in_specs=[pl.BlockSpec((tm, SC_COLS), row_map), pl.BlockSpec((tm, SC_COLS), row_map),
                      pl.BlockSpec((1, d, D_EXPERT), w_map),
                      pl.BlockSpec((1, d, D_EXPERT), w_map),
                      pl.BlockSpec((1, D_EXPERT, d), w_map)],
            out_specs=[pl.BlockSpec((tm, SC_COLS), row_map), pl.BlockSpec((tm, SC_COLS), row_map)],
            scratch_shapes=[pltpu.VMEM((d, D_EXPERT), bf), pltpu.VMEM((d, D_EXPERT), bf),
                            pltpu.VMEM((D_EXPERT, d), bf)]),
        out_shape=[piece, piece],
        compiler_params=pltpu.CompilerParams(dimension_semantics=("arbitrary",), vmem_limit_bytes=VMEM_LIMIT),
        name="experts",
    )(tile_expert, tile_rows, n_used.reshape(1).astype(jnp.int32), xs_a, xs_b, w_gate[l], w_up[l], w_down[l])

    (o1a, o1b), (o2a, o2b) = _sc_gather_rows([out_a, out_b], [pos0, pos1])

    def combine(x1, slab, blk0):
        n = x1.shape[0]
        tok = lambda i: (i, 0)
        off = lambda i: (blk0 + i, 0)
        return pl.pallas_call(
            _combine_kernel,
            grid=(n // tt,),
            in_specs=[pl.BlockSpec((tt, d), tok)] + [pl.BlockSpec((tt, SC_COLS), off)] * 4
                     + [pl.BlockSpec((tt, ROUTE_LANES), tok)],
            out_specs=pl.BlockSpec((tt, d), tok),
            out_shape=jax.ShapeDtypeStruct((n, d), f32),
            compiler_params=pltpu.CompilerParams(dimension_semantics=("arbitrary",)),
            name="combine",
        )(x1, o1a, o1b, o2a, o2b, slab)

    y_p = combine(x1_p, slab_p, 0).reshape(nb, seq, d)
    y_s = combine(x1_s, slab_s, n_p // tt).reshape(ns, slen, d)

    kv_shape = (1, nb, WINDOW, N_KV_HEADS, HEAD_DIM)
    new_k_p = kwin.reshape(kv_shape)
    new_v_p = vwin.reshape(kv_shape)
    keep = WINDOW - slen
    ck4 = cache_attn_k[l][:, WINDOW - keep:]
    cv4 = cache_attn_v[l][:, WINDOW - keep:]
    new_k_s = jnp.concatenate([ck4, kn_s.reshape(ns, slen, N_KV_HEADS, HEAD_DIM)], axis=1)[None]
    new_v_s = jnp.concatenate([cv4, v_s.reshape(ns, slen, N_KV_HEADS, HEAD_DIM)], axis=1)[None]
    new_gv_s = gvn_s.reshape(1, ns, slen, GATE_HEADS, GATE_DIM)
    return (y_p, y_s, new_k_p, new_v_p, new_k_s, new_v_s, new_gv_s)
```

```python
import functools

import jax
import jax.numpy as jnp
from jax import lax
from jax.experimental import pallas as pl
from jax.experimental.pallas import tpu as pltpu
from jax.experimental.pallas import tpu_sc as plsc

D_MODEL = 1024
HEAD_DIM = 64
N_Q_HEADS = 8
N_KV_HEADS = 2
Q_WIDTH = N_Q_HEADS * HEAD_DIM
KV_WIDTH = N_KV_HEADS * HEAD_DIM
GATE_HEADS = 4
GATE_DIM = 128
GATE_WIDTH = GATE_HEADS * GATE_DIM
PROJ_COLS = Q_WIDTH + 2 * KV_WIDTH + 2 * GATE_WIDTH
CHUNK = 64
WINDOW = 128
MLP_CHUNK = 128
N_GROUPS = 8
EXP_PER_GROUP = 8
N_EXPERTS = N_GROUPS * EXP_PER_GROUP
D_EXPERT = 512
EPS = 1e-6

LANES = 128
ROUTE_LANES = 128
FINE_LANE0 = N_GROUPS
NEG = -1e30

TOKEN_TILE = 512
ROW_SPLITS = 2
EXPERT_TILE = 512
SC_WINDOW = 128
SC_COLS = 256
VMEM_LIMIT = 48 * 1024 * 1024


def _pack_halves(x):
    w = x.shape[1] // 2
    b = lax.bitcast_convert_type(x.astype(jnp.bfloat16).astype(jnp.float32), jnp.uint32)
    return (b[:, :w] & jnp.uint32(0xFFFF0000)) | (b[:, w:] >> 16)


def _unpack_halves(p):
    hi = lax.bitcast_convert_type(p & jnp.uint32(0xFFFF0000), jnp.float32)
    lo = lax.bitcast_convert_type(p << 16, jnp.float32)
    return hi, lo


def _rms(x, eps=EPS):
    return x * lax.rsqrt(jnp.mean(x * x, axis=-1, keepdims=True) + eps)


def _project(x, gmix, win, gq, gk, gvg, blk):
    h = (_rms(x) * gmix).astype(jnp.bfloat16)
    z = jnp.dot(h, win, preferred_element_type=jnp.float32)
    qk = z[:, :Q_WIDTH + KV_WIDTH]
    outs = []
    for j in range((Q_WIDTH + KV_WIDTH) // LANES):
        zj = qk[:, j * LANES:(j + 1) * LANES]
        ss = jnp.dot((zj * zj).astype(jnp.bfloat16), blk, preferred_element_type=jnp.float32)
        outs.append(zj * lax.rsqrt(ss * (1.0 / HEAD_DIM) + EPS))
    qn = jnp.concatenate(outs[:Q_WIDTH // LANES], axis=-1) * gq
    kn = outs[-1] * gk
    v = z[:, Q_WIDTH + KV_WIDTH:Q_WIDTH + 2 * KV_WIDTH]
    u0 = Q_WIDTH + 2 * KV_WIDTH
    ua = jax.nn.gelu(z[:, u0:u0 + GATE_WIDTH])
    ga = jax.nn.gelu(z[:, u0 + GATE_WIDTH:])
    gvn = jnp.concatenate(
        [_rms(ga[:, i * GATE_DIM:(i + 1) * GATE_DIM]) for i in range(GATE_HEADS)], axis=-1) * gvg
    return qn.astype(jnp.bfloat16), kn, v, ua, gvn


def _dup_halves(a):
    lo = lax.broadcasted_iota(jnp.int32, a.shape, 1) < HEAD_DIM
    r = pltpu.roll(a, HEAD_DIM, axis=1)
    return (jnp.where(lo, a, r).astype(jnp.bfloat16), jnp.where(lo, r, a).astype(jnp.bfloat16))


def _attn_block(qa, qb, k2, v2, bias, sk):
    r = qa.shape[0]
    lo = lax.broadcasted_iota(jnp.int32, (r, LANES), 1) < HEAD_DIM
    zero = jnp.zeros_like(qa)
    qs = jnp.concatenate([jnp.where(lo, qa, zero), jnp.where(lo, zero, qa),
                          jnp.where(lo, qb, zero), jnp.where(lo, zero, qb)], axis=0)
    s = lax.dot_general(qs, k2, (((1,), (1,)), ((), ())), preferred_element_type=jnp.float32)
    if bias is not None:
        s = s + bias
    m = jnp.maximum(jnp.max(s, axis=-1, keepdims=True), sk)
    e = jnp.exp(s - m)
    den = jnp.sum(e, axis=-1, keepdims=True) + jnp.exp(sk - m)
    o = jnp.dot(e.astype(jnp.bfloat16), v2, preferred_element_type=jnp.float32) * (1.0 / den)
    pa = jnp.where(lo, o[0:r], o[r:2 * r])
    pb = jnp.where(lo, o[2 * r:3 * r], o[3 * r:4 * r])
    return pa, pb


def _sink_rows(sink_ref, g, r):
    row = lax.broadcasted_iota(jnp.int32, (4 * r, 1), 0)
    s0, s1, s2, s3 = (sink_ref[4 * g + i] for i in range(4))
    return jnp.where(row < r, s0, jnp.where(row < 2 * r, s1, jnp.where(row < 3 * r, s2, s3)))


def _causal_ws(ws_ref, h, n):
    w = ws_ref[h][:n, :n]
    keep = (lax.broadcasted_iota(jnp.int32, (n, n), 0) >= lax.broadcasted_iota(jnp.int32, (n, n), 1))
    return jnp.where(keep, w, jnp.zeros_like(w))


def _post(cat, x, wout, gffn, wrt, brt, triu, cnt):
    x1 = x + jnp.dot(cat, wout, preferred_element_type=jnp.float32)
    h2 = _rms(x1) * gffn
    t = x.shape[0]
    reps = t // LANES
    lgt = lax.dot_general(wrt, h2.astype(jnp.bfloat16), (((1,), (1,)), ((), ())),
                          preferred_element_type=jnp.float32) + jnp.tile(brt, (1, reps))
    ng = float(EXP_PER_GROUP)
    sub = lax.broadcasted_iota(jnp.int32, (EXP_PER_GROUP, t), 0).astype(jnp.float32)
    c = lgt[0:N_GROUPS]
    mc = jnp.max(c, axis=0, keepdims=True)
    grp = jnp.min(jnp.where(c == mc, sub, ng), axis=0, keepdims=True)
    pg = 1.0 / jnp.sum(jnp.exp(c - mc), axis=0, keepdims=True)
    lf = lgt[FINE_LANE0:FINE_LANE0 + EXP_PER_GROUP]
    for g in range(1, N_GROUPS):
        r0 = FINE_LANE0 + g * EXP_PER_GROUP
        lf = jnp.where(grp == float(g), lgt[r0:r0 + EXP_PER_GROUP], lf)
    v1 = jnp.max(lf, axis=0, keepdims=True)
    i1 = jnp.min(jnp.where(lf == v1, sub, ng), axis=0, keepdims=True)
    lf2 = jnp.where(sub == i1, -jnp.inf, lf)
    v2 = jnp.max(lf2, axis=0, keepdims=True)
    i2 = jnp.min(jnp.where(lf2 == v2, sub, ng), axis=0, keepdims=True)
    tt = jnp.exp(v2 - v1)
    w1 = 1.0 / (1.0 + tt)
    w2 = tt * w1
    e1 = grp * ng + i1
    e2 = grp * ng + i2
    row = lax.broadcasted_iota(jnp.int32, (N_EXPERTS, t), 0).astype(jnp.float32)
    sel1 = row == e1
    sel2 = row == e2
    oh = jnp.where(sel1 | sel2, 1.0, 0.0)
    cum = jnp.dot(oh.astype(jnp.bfloat16), triu, preferred_element_type=jnp.float32) + jnp.tile(cnt, (1, reps))
    r1 = jnp.sum(jnp.where(sel1, cum, 0.0), axis=0, keepdims=True)
    r2 = jnp.sum(jnp.where(sel2, cum, 0.0), axis=0, keepdims=True)
    new_cnt = cnt + jnp.sum(oh, axis=1, keepdims=True)
    slab_t = jnp.where(sub == 0.0, e1,
             jnp.where(sub == 1.0, e2,
             jnp.where(sub == 2.0, r1,
             jnp.where(sub == 3.0, r2,
             jnp.where(sub == 4.0, pg * w1,
             jnp.where(sub == 5.0, pg * w2, 0.0))))))
    slab = jnp.concatenate([slab_t, jnp.zeros((ROUTE_LANES - EXP_PER_GROUP, t), jnp.float32)], axis=0).T
    return x1, _pack_halves(h2), slab, new_cnt


def _prompt_kernel(sink_ref, x_ref, gmix_ref, win_ref, gq_ref, gk_ref, gvg_ref, blk_ref, ws_ref, bs_ref,
                   wout_ref, gffn_ref, wr_ref, br_ref, tril_ref, cnt_in_ref,
                   x1_ref, h2_ref, slab_ref, kwin_ref, vwin_ref, cnt_ref,
                   k2_scr, v2_scr, qn_scr, ua_scr, gvn_scr, cat_scr, cnt_scr):
    b = pl.program_id(0)
    t = pl.program_id(1)
    nt = pl.num_programs(1)
    tt = x_ref.shape[1]
    nblk = tt // WINDOW
    sub = tt // ROW_SPLITS

    @pl.when((b == 0) & (t == 0))
    def _():
        cnt_scr[...] = cnt_in_ref[...]

    @pl.when(t == 0)
    def _():
        k2_scr[:, 0:WINDOW, :] = jnp.zeros((N_KV_HEADS, WINDOW, LANES), jnp.bfloat16)
        v2_scr[:, 0:WINDOW, :] = jnp.zeros((N_KV_HEADS, WINDOW, LANES), jnp.bfloat16)

    for s in range(ROW_SPLITS):
        rs = slice(s * sub, (s + 1) * sub)
        ks = slice(WINDOW + s * sub, WINDOW + (s + 1) * sub)
        qn, kn, v, ua, gvn = _project(x_ref[0, rs, :], gmix_ref[...], win_ref[...], gq_ref[...], gk_ref[...],
                                      gvg_ref[...], blk_ref[...])
        qn_scr[rs, :] = qn
        ua_scr[rs, :] = ua
        gvn_scr[rs, :] = gvn.astype(jnp.bfloat16)
        k0, k1 = _dup_halves(kn)
        v0, v1 = _dup_halves(v)
        k2_scr[0, ks, :] = k0
        k2_scr[1, ks, :] = k1
        v2_scr[0, ks, :] = v0
        v2_scr[1, ks, :] = v1
        if s == ROW_SPLITS - 1:
            @pl.when(t == nt - 1)
            def _():
                kwin_ref[0] = kn[sub - WINDOW:]
                vwin_ref[0] = v[sub - WINDOW:]

    rows = 4 * WINDOW
    band = 2 * WINDOW
    rr = lax.broadcasted_iota(jnp.int32, (rows, band), 0)
    kk = lax.broadcasted_iota(jnp.int32, (rows, band), 1)
    half = ((rr % WINDOW) >= CHUNK).astype(jnp.int32)
    allowed = (kk >= half * CHUNK) & (kk < (3 + half) * CHUNK)
    bias_mid = jnp.where(allowed, 0.0, NEG)
    bias_first = jnp.where(allowed & (kk >= WINDOW), 0.0, NEG)

    def attn_body(j, carry):
        r0 = pl.multiple_of(j * WINDOW, WINDOW)
        bias = jnp.where((t == 0) & (j == 0), bias_first, bias_mid)
        for g in range(N_KV_HEADS):
            c0 = g * 2 * LANES
            qa = qn_scr[pl.ds(r0, WINDOW), c0:c0 + LANES]
            qb = qn_scr[pl.ds(r0, WINDOW), c0 + LANES:c0 + 2 * LANES]
            k2 = k2_scr[g, pl.ds(r0, band), :]
            v2 = v2_scr[g, pl.ds(r0, band), :]
            pa, pb = _attn_block(qa, qb, k2, v2, bias, _sink_rows(sink_ref, g, WINDOW))
            cat_scr[pl.ds(r0, WINDOW), c0:c0 + LANES] = pa.astype(jnp.bfloat16)
            cat_scr[pl.ds(r0, WINDOW), c0 + LANES:c0 + 2 * LANES] = pb.astype(jnp.bfloat16)
        return carry

    lax.fori_loop(0, nblk, attn_body, 0, unroll=True)

    k2_scr[:, 0:WINDOW, :] = k2_scr[:, tt:tt + WINDOW, :]
    v2_scr[:, 0:WINDOW, :] = v2_scr[:, tt:tt + WINDOW, :]

    for h in range(GATE_HEADS):
        w = _causal_ws(ws_ref, h, MLP_CHUNK)
        bcol = bs_ref[:, h:h + 1]
        for c in range(tt // MLP_CHUNK):
            rs = slice(c * MLP_CHUNK, (c + 1) * MLP_CHUNK)
            cs = slice(h * GATE_DIM, (h + 1) * GATE_DIM)
            mix = jnp.dot(w, gvn_scr[rs, cs], preferred_element_type=jnp.float32) + bcol
            cat_scr[rs, Q_WIDTH + h * GATE_DIM:Q_WIDTH + (h + 1) * GATE_DIM] = (ua_scr[rs, cs] * mix).astype(jnp.bfloat16)

    cnt = cnt_scr[...]
    for s in range(ROW_SPLITS):
        rs = slice(s * sub, (s + 1) * sub)
        x1, hp, slab, cnt = _post(cat_scr[rs, :], x_ref[0, rs, :], wout_ref[...], gffn_ref[...], wr_ref[...],
                                  br_ref[...], tril_ref[...], cnt)
        x1_ref[rs, :] = x1
        h2_ref[rs, :] = hp
        slab_ref[rs, :] = slab
    cnt_scr[...] = cnt
    cnt_ref[...] = cnt


def _sample_kernel(sink_ref, x_ref, ck_ref, cv_ref, gmix_ref, win_ref, gq_ref, gk_ref, gvg_ref, blk_ref, ws_ref,
                   bs_ref, wout_ref, gffn_ref, wr_ref, br_ref, tril_ref,
                   x1_ref, h2_ref, slab_ref, kn_ref, v_ref, gvn_ref, cnt_ref,
                   ck2_scr, cv2_scr, k2_scr, v2_scr, qn_scr, ua_scr, cat_scr, *, n_seq, seq_len):
    x = x_ref[...]
    qn, kn, v, ua, gvn = _project(x, gmix_ref[...], win_ref[...], gq_ref[...], gk_ref[...], gvg_ref[...],
                                  blk_ref[...])
    kn_ref[...] = kn
    v_ref[...] = v
    gvn_ref[...] = gvn
    qn_scr[...] = qn
    ua_scr[...] = ua
    for scr, val in ((k2_scr, kn), (v2_scr, v), (ck2_scr, ck_ref[...]), (cv2_scr, cv_ref[...])):
        a0, a1 = _dup_halves(val)
        scr[0] = a0
        scr[1] = a1

    ws = [_causal_ws(ws_ref, h, seq_len) for h in range(GATE_HEADS)]

    def seq_body(i, carry):
        r0 = pl.multiple_of(i * seq_len, seq_len)
        c0r = pl.multiple_of(i * WINDOW, WINDOW)
        for g in range(N_KV_HEADS):
            c0 = g * 2 * LANES
            qa = qn_scr[pl.ds(r0, seq_len), c0:c0 + LANES]
            qb = qn_scr[pl.ds(r0, seq_len), c0 + LANES:c0 + 2 * LANES]
            k2 = jnp.concatenate([ck2_scr[g, pl.ds(c0r, WINDOW), :], k2_scr[g, pl.ds(r0, seq_len), :]], axis=0)
            v2 = jnp.concatenate([cv2_scr[g, pl.ds(c0r, WINDOW), :], v2_scr[g, pl.ds(r0, seq_len), :]], axis=0)
            pa, pb = _attn_block(qa, qb, k2, v2, None, _sink_rows(sink_ref, g, seq_len))
            cat_scr[pl.ds(r0, seq_len), c0:c0 + LANES] = pa.astype(jnp.bfloat16)
            cat_scr[pl.ds(r0, seq_len), c0 + LANES:c0 + 2 * LANES] = pb.astype(jnp.bfloat16)
        for h in range(GATE_HEADS):
            cs = slice(h * GATE_DIM, (h + 1) * GATE_DIM)
            gv_h = gvn_ref[pl.ds(r0, seq_len), cs].astype(jnp.bfloat16)
            mix = jnp.dot(ws[h], gv_h, preferred_element_type=jnp.float32) + bs_ref[0:seq_len, h:h + 1]
            cat_scr[pl.ds(r0, seq_len), Q_WIDTH + h * GATE_DIM:Q_WIDTH + (h + 1) * GATE_DIM] = (
                ua_scr[pl.ds(r0, seq_len), cs] * mix).astype(jnp.bfloat16)
        return carry

    lax.fori_loop(0, n_seq, seq_body, 0)

    cnt0 = jnp.zeros((N_EXPERTS, LANES), jnp.float32)
    x1, hp, slab, new_cnt = _post(cat_scr[...], x, wout_ref[...], gffn_ref[...], wr_ref[...],
                                  br_ref[...], tril_ref[...], cnt0)
    x1_ref[...] = x1
    h2_ref[...] = hp
    slab_ref[...] = slab
    cnt_ref[...] = new_cnt


def _expert_kernel(te_ref, nv_ref, nu_ref, xa_ref, xb_ref, wg_ref, wu_ref, wd_ref, oa_ref, ob_ref,
                   wg_s, wu_s, wd_s):
    i = pl.program_id(0)

    @pl.when(i < nu_ref[0])
    def _():
        @pl.when((i == 0) | (te_ref[i] != te_ref[jnp.maximum(i - 1, 0)]))
        def _():
            wg_s[...] = wg_ref[0].astype(jnp.bfloat16)
            wu_s[...] = wu_ref[0].astype(jnp.bfloat16)
            wd_s[...] = wd_ref[0].astype(jnp.bfloat16)

        live = lax.broadcasted_iota(jnp.int32, xa_ref.shape, 0) < nv_ref[i]
        ha, la = _unpack_halves(jnp.where(live, xa_ref[...], jnp.uint32(0)))
        hb, lb = _unpack_halves(jnp.where(live, xb_ref[...], jnp.uint32(0)))
        xs = jnp.concatenate([ha, hb, la, lb], axis=1).astype(jnp.bfloat16)
        hg = jnp.dot(xs, wg_s[...], preferred_element_type=jnp.float32)
        hu = jnp.dot(xs, wu_s[...], preferred_element_type=jnp.float32)
        a = (jax.nn.silu(hg) * hu).astype(jnp.bfloat16)
        packed = _pack_halves(jnp.dot(a, wd_s[...], preferred_element_type=jnp.float32))
        q = packed.shape[1] // 2
        oa_ref[...] = packed[:, :q]
        ob_ref[...] = packed[:, q:]


def _combine_kernel(x1_ref, o1a_ref, o1b_ref, o2a_ref, o2b_ref, slab_ref, y_ref):
    slab = slab_ref[...]
    g1 = slab[:, 4:5]
    g2 = slab[:, 5:6]
    q = o1a_ref.shape[1]
    for c, (r1, r2) in enumerate(((o1a_ref, o2a_ref), (o1b_ref, o2b_ref))):
        h1, l1 = _unpack_halves(r1[...])
        h2, l2 = _unpack_halves(r2[...])
        hs = slice(c * q, (c + 1) * q)
        ls = slice(2 * q + c * q, 2 * q + (c + 1) * q)
        y_ref[:, hs] = x1_ref[:, hs] + (g1 * h1 + g2 * h2)
        y_ref[:, ls] = x1_ref[:, ls] + (g1 * l1 + g2 * l2)


def _sc_mesh():
    return plsc.VectorSubcoreMesh(core_axis_name="core", subcore_axis_name="subcore")


def _sc_scatter_rows(parts, pos0, pos1, n_rows):
    n_piece = parts[0].shape[1] // SC_COLS

    def sc_kernel(*refs):
        x_refs = refs[:len(parts)]
        i0_hbm, i1_hbm = refs[len(parts):len(parts) + 2]
        o_refs = refs[len(parts) + 2:]
        tok0 = 0
        for x_hbm, arr in zip(x_refs, parts):
            blk0 = tok0 // SC_WINDOW
            for c in range(n_piece):
                def body(x_vmem, i0_vmem, i1_vmem, o_hbm=o_refs[c]):
                    pltpu.sync_copy(x_vmem, o_hbm.at[i0_vmem.at[0]])
                    pltpu.sync_copy(x_vmem, o_hbm.at[i1_vmem.at[0]])

                pltpu.emit_pipeline(
                    body,
                    grid=(arr.shape[0] // SC_WINDOW,),
                    in_specs=[pl.BlockSpec((SC_WINDOW, SC_COLS), lambda i, c=c: (i, c)),
                              pl.BlockSpec((1, SC_WINDOW), lambda i, blk0=blk0: (0, blk0 + i)),
                              pl.BlockSpec((1, SC_WINDOW), lambda i, blk0=blk0: (0, blk0 + i))],
                    out_specs=[],
                    core_axis_name=("core", "subcore"),
                    dimension_semantics=(pltpu.PARALLEL,),
                )(x_hbm, i0_hbm, i1_hbm)
            tok0 += arr.shape[0]

    piece = jax.ShapeDtypeStruct((n_rows, SC_COLS), jnp.uint32)
    return pl.kernel(sc_kernel, out_type=(piece,) * n_piece, mesh=_sc_mesh(), name="scatter_rows")(
        *parts, pos0, pos1)


def _sc_gather_rows(pieces, pos_list):
    n_tok = pos_list[0].shape[1]

    def sc_kernel(*refs):
        s_refs = refs[:len(pieces)]
        i_refs = refs[len(pieces):len(pieces) + len(pos_list)]
        o_refs = refs[len(pieces) + len(pos_list):]
        k = 0
        for i_hbm in i_refs:
            for s_hbm in s_refs:
                def body(i_vmem, o_vmem, s_hbm=s_hbm):
                    pltpu.sync_copy(s_hbm.at[i_vmem.at[0]], o_vmem)

                pltpu.emit_pipeline(
                    body,
                    grid=(n_tok // SC_WINDOW,),
                    in_specs=[pl.BlockSpec((1, SC_WINDOW), lambda i: (0, i))],
                    out_specs=[pl.BlockSpec((SC_WINDOW, SC_COLS), lambda i: (i, 0))],
                    core_axis_name=("core", "subcore"),
                    dimension_semantics=(pltpu.PARALLEL,),
                )(i_hbm, o_refs[k])
                k += 1

    out = jax.ShapeDtypeStruct((n_tok, SC_COLS), jnp.uint32)
    outs = pl.kernel(sc_kernel, out_type=(out,) * (len(pieces) * len(pos_list)), mesh=_sc_mesh(),
                     name="gather_rows")(*pieces, *pos_list)
    return [outs[j * len(pieces):(j + 1) * len(pieces)] for j in range(len(pos_list))]


def _const_spec(shape):
    nd = len(shape)
    return pl.BlockSpec(shape, lambda *_: (0,) * nd)


def kernel(x_prompt, x_sample, cache_attn_k, cache_attn_v, g_mix, w_in, g_q, g_k, g_v, attn_sinks, w_s, b_s,
           w_out, g_ffn, w_coarse, b_coarse, w_fine, b_fine, w_gate, w_up, w_down):
    nb, seq, d = x_prompt.shape
    ns, slen, _ = x_sample.shape
    n_p = nb * seq
    n_s = ns * slen
    n_tok = n_p + n_s
    tt = TOKEN_TILE
    assert seq % tt == 0 and n_s % tt == 0 and d == D_MODEL
    nt = seq // tt
    bf = jnp.bfloat16
    f32 = jnp.float32

    l = 0
    gmix = g_mix[l].reshape(1, d)
    win = w_in[l].astype(bf)
    gq = (jnp.tile(g_q[l], N_Q_HEADS) * (HEAD_DIM ** -0.5)).reshape(1, Q_WIDTH)
    gk = jnp.tile(g_k[l], N_KV_HEADS).reshape(1, KV_WIDTH)
    gvg = g_v[l].reshape(1, GATE_WIDTH)
    sinks = attn_sinks[l].reshape(N_Q_HEADS).astype(f32)
    ws = w_s[l].astype(bf)
    bs = b_s[l].T
    wout = w_out[l].astype(bf)
    gffn = g_ffn[l].reshape(1, d)
    wr = jnp.concatenate([w_coarse[l], jnp.transpose(w_fine[l], (1, 0, 2)).reshape(d, N_EXPERTS),
                          jnp.zeros((d, ROUTE_LANES - N_GROUPS - N_EXPERTS), f32)], axis=1)
    wr = wr.astype(bf).T
    br = jnp.concatenate([b_coarse[l], b_fine[l].reshape(-1),
                          jnp.zeros((ROUTE_LANES - N_GROUPS - N_EXPERTS,), f32)])
    br = jnp.broadcast_to(br[:, None], (ROUTE_LANES, LANES))
    cnt_shape = (N_EXPERTS, LANES)
    ii = jnp.arange(LANES)
    blk = (ii[:, None] // HEAD_DIM == ii[None, :] // HEAD_DIM).astype(bf)
    u32 = jnp.uint32
    dh = d // 2

    def strict_tril(n):
        r = jnp.arange(n)
        return (r[:, None] < r[None, :]).astype(bf)

    weight_args = (gmix, win, gq, gk, gvg, blk, ws, bs, wout, gffn, wr, br)
    weight_specs = [_const_spec(a.shape) for a in weight_args]
    smem_spec = pl.BlockSpec(memory_space=pltpu.SMEM)

    xs2 = x_sample.reshape(n_s, d)
    ck = cache_attn_k[l].reshape(ns * WINDOW, KV_WIDTH)
    cv = cache_attn_v[l].reshape(ns * WINDOW, KV_WIDTH)
    tok_out = lambda n, w, dt: jax.ShapeDtypeStruct((n, w), dt)
    x1_s, h2_s, slab_s, kn_s, v_s, gvn_s, cnt_s = pl.pallas_call(
        functools.partial(_sample_kernel, n_seq=ns, seq_len=slen),
        grid=(1,),
        in_specs=[smem_spec, _const_spec((n_s, d)), _const_spec(ck.shape), _const_spec(cv.shape)]
                 + weight_specs + [_const_spec((n_s, n_s))],
        out_specs=[_const_spec((n_s, d)), _const_spec((n_s, dh)), _const_spec((n_s, ROUTE_LANES)),
                   _const_spec((n_s, KV_WIDTH)), _const_spec((n_s, KV_WIDTH)), _const_spec((n_s, GATE_WIDTH)),
                   _const_spec(cnt_shape)],
        out_shape=[tok_out(n_s, d, f32), tok_out(n_s, dh, u32), tok_out(n_s, ROUTE_LANES, f32),
                   jax.ShapeDtypeStruct((n_s, KV_WIDTH), f32), jax.ShapeDtypeStruct((n_s, KV_WIDTH), f32),
                   jax.ShapeDtypeStruct((n_s, GATE_WIDTH), f32), jax.ShapeDtypeStruct(cnt_shape, f32)],
        scratch_shapes=[pltpu.VMEM((N_KV_HEADS, ns * WINDOW, LANES), bf), pltpu.VMEM((N_KV_HEADS, ns * WINDOW, LANES), bf),
                        pltpu.VMEM((N_KV_HEADS, n_s, LANES), bf), pltpu.VMEM((N_KV_HEADS, n_s, LANES), bf),
                        pltpu.VMEM((n_s, Q_WIDTH), bf), pltpu.VMEM((n_s, GATE_WIDTH), f32),
                        pltpu.VMEM((n_s, d), bf)],
        compiler_params=pltpu.CompilerParams(dimension_semantics=("arbitrary",), vmem_limit_bytes=VMEM_LIMIT),
        name="mixer_sample",
    )(sinks, xs2, ck, cv, *weight_args, strict_tril(n_s))

    x1_p, h2_p, slab_p, kwin, vwin, cnt = pl.pallas_call(
        _prompt_kernel,
        grid=(nb, nt),
        in_specs=[smem_spec, pl.BlockSpec((1, tt, d), lambda b, t: (b, t, 0))] + weight_specs
                 + [_const_spec((tt // ROW_SPLITS, tt // ROW_SPLITS)), _const_spec(cnt_shape)],
        out_specs=[pl.BlockSpec((tt, d), lambda b, t: (b * nt + t, 0)),
                   pl.BlockSpec((tt, dh), lambda b, t: (b * nt + t, 0)),
                   pl.BlockSpec((tt, ROUTE_LANES), lambda b, t: (b * nt + t, 0)),
                   pl.BlockSpec((1, WINDOW, KV_WIDTH), lambda b, t: (b, 0, 0)),
                   pl.BlockSpec((1, WINDOW, KV_WIDTH), lambda b, t: (b, 0, 0)),
                   _const_spec(cnt_shape)],
        out_shape=[tok_out(n_p, d, f32), tok_out(n_p, dh, u32), tok_out(n_p, ROUTE_LANES, f32),
                   jax.ShapeDtypeStruct((nb, WINDOW, KV_WIDTH), f32),
                   jax.ShapeDtypeStruct((nb, WINDOW, KV_WIDTH), f32),
                   jax.ShapeDtypeStruct(cnt_shape, f32)],
        scratch_shapes=[pltpu.VMEM((N_KV_HEADS, tt + WINDOW, LANES), bf),
                        pltpu.VMEM((N_KV_HEADS, tt + WINDOW, LANES), bf),
                        pltpu.VMEM((tt, Q_WIDTH), bf), pltpu.VMEM((tt, GATE_WIDTH), f32),
                        pltpu.VMEM((tt, GATE_WIDTH), bf), pltpu.VMEM((tt, d), bf),
                        pltpu.VMEM(cnt_shape, f32)],
        compiler_params=pltpu.CompilerParams(dimension_semantics=("arbitrary", "arbitrary"),
                                             vmem_limit_bytes=VMEM_LIMIT),
        name="mixer_prompt",
    )(sinks, x_prompt, *weight_args, strict_tril(tt // ROW_SPLITS), cnt_s)
    slab_all = jnp.concatenate([slab_p[:, :8], slab_s[:, :8]], axis=0)

    tm = EXPERT_TILE
    counts = cnt[:, 0].astype(jnp.int32)
    tiles = (counts + tm - 1) // tm
    tile_end = jnp.cumsum(tiles)
    offs = (tile_end - tiles) * tm
    n_used = tile_end[-1]
    n_tiles = (2 * n_tok) // tm + N_EXPERTS
    n_rows = n_tiles * tm
    tile_ids = jnp.arange(n_tiles, dtype=jnp.int32)
    tile_expert = jnp.minimum(jnp.sum(tile_ids[:, None] >= tile_end[None, :], axis=1), N_EXPERTS - 1).astype(jnp.int32)
    tile_in_expert = tile_ids - (tile_end - tiles)[tile_expert]
    tile_rows = jnp.clip(counts[tile_expert] - tile_in_expert * tm, 0, tm).astype(jnp.int32)
    eid = slab_all[:, 0:2].astype(jnp.int32)
    rank = slab_all[:, 2:4].astype(jnp.int32)
    off_of = jnp.sum(jnp.where(eid[:, :, None] == jnp.arange(N_EXPERTS, dtype=jnp.int32), offs, 0), axis=-1)
    pos = off_of + rank
    pos0 = pos[:, 0].reshape(1, n_tok)
    pos1 = pos[:, 1].reshape(1, n_tok)

    assert dh == 2 * SC_COLS and n_p % SC_WINDOW == 0 and n_s % SC_WINDOW == 0
    xs_a, xs_b = _sc_scatter_rows([h2_p, h2_s], pos0, pos1, n_rows)

    def row_map(i, te, nv, nu):
        return (jnp.minimum(i, nu[0] - 1), 0)

    def w_map(i, te, nv, nu):
        return (te[jnp.minimum(i, nu[0] - 1)], 0, 0)

    piece = jax.ShapeDtypeStruct((n_rows, SC_COLS), u32)
    out_a, out_b = pl.pallas_call(
        _expert_kernel,
        grid_spec=pltpu.PrefetchScalarGridSpec(
            num_scalar_prefetch=3,
            grid=(n_tiles,),
            in_specs=[pl.BlockSpec((tm, SC_COLS), row_map), pl.BlockSpec((tm, SC_COLS), row_map),
                      pl.BlockSpec((1, d, D_EXPERT), w_map),
                      pl.BlockSpec((1, d, D_EXPERT), w_map),
                      pl.BlockSpec((1, D_EXPERT, d), w_map)],
            out_specs=[pl.BlockSpec((tm, SC_COLS), row_map), pl.BlockSpec((tm, SC_COLS), row_map)],
            scratch_shapes=[pltpu.VMEM((d, D_EXPERT), bf), pltpu.VMEM((d, D_EXPERT), bf),
                            pltpu.VMEM((D_EXPERT, d), bf)]),
        out_shape=[piece, piece],
        compiler_params=pltpu.CompilerParams(dimension_semantics=("arbitrary",), vmem_limit_bytes=VMEM_LIMIT),
        name="experts",
    )(tile_expert, tile_rows, n_used.reshape(1).astype(jnp.int32), xs_a, xs_b, w_gate[l], w_up[l], w_down[l])

    (o1a, o1b), (o2a, o2b) = _sc_gather_rows([out_a, out_b], [pos0, pos1])

    def combine(x1, slab, blk0):
        n = x1.shape[0]
        tok = lambda i: (i, 0)
        off = lambda i: (blk0 + i, 0)
        return pl.pallas_call(
            _combine_kernel,
            grid=(n // tt,),
            in_specs=[pl.BlockSpec((tt, d), tok)] + [pl.BlockSpec((tt, SC_COLS), off)] * 4
                     + [pl.BlockSpec((tt, ROUTE_LANES), tok)],
            out_specs=pl.BlockSpec((tt, d), tok),
            out_shape=jax.ShapeDtypeStruct((n, d), f32),
            compiler_params=pltpu.CompilerParams(dimension_semantics=("arbitrary",)),
            name="combine",
        )(x1, o1a, o1b, o2a, o2b, slab)

    y_p = combine(x1_p, slab_p, 0).reshape(nb, seq, d)
    y_s = combine(x1_s, slab_s, n_p // tt).reshape(ns, slen, d)

    kv_shape = (1, nb, WINDOW, N_KV_HEADS, HEAD_DIM)
    new_k_p = kwin.reshape(kv_shape)
    new_v_p = vwin.reshape(kv_shape)
    keep = WINDOW - slen
    ck4 = cache_attn_k[l][:, WINDOW - keep:]
    cv4 = cache_attn_v[l][:, WINDOW - keep:]
    new_k_s = jnp.concatenate([ck4, kn_s.reshape(ns, slen, N_KV_HEADS, HEAD_DIM)], axis=1)[None]
    new_v_s = jnp.concatenate([cv4, v_s.reshape(ns, slen, N_KV_HEADS, HEAD_DIM)], axis=1)[None]
    new_gv_s = gvn_s.reshape(1, ns, slen, GATE_HEADS, GATE_DIM)
    return (y_p, y_s, new_k_p, new_v_p, new_k_s, new_v_s, new_gv_s)
```

```python
import functools

import jax
import jax.numpy as jnp
from jax import lax
from jax.experimental import pallas as pl
from jax.experimental.pallas import tpu as pltpu
from jax.experimental.pallas import tpu_sc as plsc

D_MODEL = 1024
HEAD_DIM = 64
N_Q_HEADS = 8
N_KV_HEADS = 2
Q_WIDTH = N_Q_HEADS * HEAD_DIM
KV_WIDTH = N_KV_HEADS * HEAD_DIM
GATE_HEADS = 4
GATE_DIM = 128
GATE_WIDTH = GATE_HEADS * GATE_DIM
PROJ_COLS = Q_WIDTH + 2 * KV_WIDTH + 2 * GATE_WIDTH
CHUNK = 64
WINDOW = 128
MLP_CHUNK = 128
N_GROUPS = 8
EXP_PER_GROUP = 8
N_EXPERTS = N_GROUPS * EXP_PER_GROUP
D_EXPERT = 512
EPS = 1e-6

LANES = 128
ROUTE_LANES = 128
FINE_LANE0 = N_GROUPS
NEG = -1e30

TOKEN_TILE = 512
ROW_SPLITS = 2
EXPERT_TILE = 512
SC_WINDOW = 128
SC_COLS = 256
VMEM_LIMIT = 48 * 1024 * 1024


def _pack_halves(x):
    w = x.shape[1] // 2
    b = lax.bitcast_convert_type(x.astype(jnp.bfloat16).astype(jnp.float32), jnp.uint32)
    return (b[:, :w] & jnp.uint32(0xFFFF0000)) | (b[:, w:] >> 16)


def _unpack_halves(p):
    hi = lax.bitcast_convert_type(p & jnp.uint32(0xFFFF0000), jnp.float32)
    lo = lax.bitcast_convert_type(p << 16, jnp.float32)
    return hi, lo


def _rms(x, eps=EPS):
    return x * lax.rsqrt(jnp.mean(x * x, axis=-1, keepdims=True) + eps)


def _project(x, gmix, win, gq, gk, gvg, blk):
    h = (_rms(x) * gmix).astype(jnp.bfloat16)
    z = jnp.dot(h, win, preferred_element_type=jnp.float32)
    qk = z[:, :Q_WIDTH + KV_WIDTH]
    outs = []
    for j in range((Q_WIDTH + KV_WIDTH) // LANES):
        zj = qk[:, j * LANES:(j + 1) * LANES]
        ss = jnp.dot((zj * zj).astype(jnp.bfloat16), blk, preferred_element_type=jnp.float32)
        outs.append(zj * lax.rsqrt(ss * (1.0 / HEAD_DIM) + EPS))
    qn = jnp.concatenate(outs[:Q_WIDTH // LANES], axis=-1) * gq
    kn = outs[-1] * gk
    v = z[:, Q_WIDTH + KV_WIDTH:Q_WIDTH + 2 * KV_WIDTH]
    u0 = Q_WIDTH + 2 * KV_WIDTH
    ua = jax.nn.gelu(z[:, u0:u0 + GATE_WIDTH])
    ga = jax.nn.gelu(z[:, u0 + GATE_WIDTH:])
    gvn = jnp.concatenate(
        [_rms(ga[:, i * GATE_DIM:(i + 1) * GATE_DIM]) for i in range(GATE_HEADS)], axis=-1) * gvg
    return qn.astype(jnp.bfloat16), kn, v, ua, gvn


def _dup_halves(a):
    lo = lax.broadcasted_iota(jnp.int32, a.shape, 1) < HEAD_DIM
    r = pltpu.roll(a, HEAD_DIM, axis=1)
    return (jnp.where(lo, a, r).astype(jnp.bfloat16), jnp.where(lo, r, a).astype(jnp.bfloat16))


def _attn_block(qa, qb, k2, v2, bias, sk):
    r = qa.shape[0]
    lo = lax.broadcasted_iota(jnp.int32, (r, LANES), 1) < HEAD_DIM
    zero = jnp.zeros_like(qa)
    qs = jnp.concatenate([jnp.where(lo, qa, zero), jnp.where(lo, zero, qa),
                          jnp.where(lo, qb, zero), jnp.where(lo, zero, qb)], axis=0)
    s = lax.dot_general(qs, k2, (((1,), (1,)), ((), ())), preferred_element_type=jnp.float32)
    if bias is not None:
        s = s + bias
    m = jnp.maximum(jnp.max(s, axis=-1, keepdims=True), sk)
    e = jnp.exp(s - m)
    den = jnp.sum(e, axis=-1, keepdims=True) + jnp.exp(sk - m)
    o = jnp.dot(e.astype(jnp.bfloat16), v2, preferred_element_type=jnp.float32) * (1.0 / den)
    pa = jnp.where(lo, o[0:r], o[r:2 * r])
    pb = jnp.where(lo, o[2 * r:3 * r], o[3 * r:4 * r])
    return pa, pb


def _sink_rows(sink_ref, g, r):
    row = lax.broadcasted_iota(jnp.int32, (4 * r, 1), 0)
    s0, s1, s2, s3 = (sink_ref[4 * g + i] for i in range(4))
    return jnp.where(row < r, s0, jnp.where(row < 2 * r, s1, jnp.where(row < 3 * r, s2, s3)))


def _causal_ws(ws_ref, h, n):
    w = ws_ref[h][:n, :n]
    keep = (lax.broadcasted_iota(jnp.int32, (n, n), 0) >= lax.broadcasted_iota(jnp.int32, (n, n), 1))
    return jnp.where(keep, w, jnp.zeros_like(w))


def _post(cat, x, wout, gffn, wrt, brt, triu, cnt):
    x1 = x + jnp.dot(cat, wout, preferred_element_type=jnp.float32)
    h2 = _rms(x1) * gffn
    t = x.shape[0]
    reps = t // LANES
    lgt = lax.dot_general(wrt, h2.astype(jnp.bfloat16), (((1,), (1,)), ((), ())),
                          preferred_element_type=jnp.float32) + jnp.tile(brt, (1, reps))
    ng = float(EXP_PER_GROUP)
    sub = lax.broadcasted_iota(jnp.int32, (EXP_PER_GROUP, t), 0).astype(jnp.float32)
    c = lgt[0:N_GROUPS]
    mc = jnp.max(c, axis=0, keepdims=True)
    grp = jnp.min(jnp.where(c == mc, sub, ng), axis=0, keepdims=True)
    pg = 1.0 / jnp.sum(jnp.exp(c - mc), axis=0, keepdims=True)
    lf = lgt[FINE_LANE0:FINE_LANE0 + EXP_PER_GROUP]
    for g in range(1, N_GROUPS):
        r0 = FINE_LANE0 + g * EXP_PER_GROUP
        lf = jnp.where(grp == float(g), lgt[r0:r0 + EXP_PER_GROUP], lf)
    v1 = jnp.max(lf, axis=0, keepdims=True)
    i1 = jnp.min(jnp.where(lf == v1, sub, ng), axis=0, keepdims=True)
    lf2 = jnp.where(sub == i1, -jnp.inf, lf)
    v2 = jnp.max(lf2, axis=0, keepdims=True)
    i2 = jnp.min(jnp.where(lf2 == v2, sub, ng), axis=0, keepdims=True)
    tt = jnp.exp(v2 - v1)
    w1 = 1.0 / (1.0 + tt)
    w2 = tt * w1
    e1 = grp * ng + i1
    e2 = grp * ng + i2
    row = lax.broadcasted_iota(jnp.int32, (N_EXPERTS, t), 0).astype(jnp.float32)
    sel1 = row == e1
    sel2 = row == e2
    oh = jnp.where(sel1 | sel2, 1.0, 0.0)
    cum = jnp.dot(oh.astype(jnp.bfloat16), triu, preferred_element_type=jnp.float32) + jnp.tile(cnt, (1, reps))
    r1 = jnp.sum(jnp.where(sel1, cum, 0.0), axis=0, keepdims=True)
    r2 = jnp.sum(jnp.where(sel2, cum, 0.0), axis=0, keepdims=True)
    new_cnt = cnt + jnp.sum(oh, axis=1, keepdims=True)
    slab_t = jnp.where(sub == 0.0, e1,
             jnp.where(sub == 1.0, e2,
             jnp.where(sub == 2.0, r1,
             jnp.where(sub == 3.0, r2,
             jnp.where(sub == 4.0, pg * w1,
             jnp.where(sub == 5.0, pg * w2, 0.0))))))
    slab = jnp.concatenate([slab_t, jnp.zeros((ROUTE_LANES - EXP_PER_GROUP, t), jnp.float32)], axis=0).T
    return x1, _pack_halves(h2), slab, new_cnt


def _prompt_kernel(sink_ref, x_ref, gmix_ref, win_ref, gq_ref, gk_ref, gvg_ref, blk_ref, ws_ref, bs_ref,
                   wout_ref, gffn_ref, wr_ref, br_ref, tril_ref, cnt_in_ref,
                   x1_ref, h2_ref, slab_ref, kwin_ref, vwin_ref, cnt_ref,
                   k2_scr, v2_scr, qn_scr, ua_scr, gvn_scr, cat_scr, cnt_scr):
    b = pl.program_id(0)
    t = pl.program_id(1)
    nt = pl.num_programs(1)
    tt = x_ref.shape[1]
    nblk = tt // WINDOW
    sub = tt // ROW_SPLITS

    @pl.when((b == 0) & (t == 0))
    def _():
        cnt_scr[...] = cnt_in_ref[...]

    @pl.when(t == 0)
    def _():
        k2_scr[:, 0:WINDOW, :] = jnp.zeros((N_KV_HEADS, WINDOW, LANES), jnp.bfloat16)
        v2_scr[:, 0:WINDOW, :] = jnp.zeros((N_KV_HEADS, WINDOW, LANES), jnp.bfloat16)

    for s in range(ROW_SPLITS):
        rs = slice(s * sub, (s + 1) * sub)
        ks = slice(WINDOW + s * sub, WINDOW + (s + 1) * sub)
        qn, kn, v, ua, gvn = _project(x_ref[0, rs, :], gmix_ref[...], win_ref[...], gq_ref[...], gk_ref[...],
                                      gvg_ref[...], blk_ref[...])
        qn_scr[rs, :] = qn
        ua_scr[rs, :] = ua
        gvn_scr[rs, :] = gvn.astype(jnp.bfloat16)
        k0, k1 = _dup_halves(kn)
        v0, v1 = _dup_halves(v)
        k2_scr[0, ks, :] = k0
        k2_scr[1, ks, :] = k1
        v2_scr[0, ks, :] = v0
        v2_scr[1, ks, :] = v1
        if s == ROW_SPLITS - 1:
            @pl.when(t == nt - 1)
            def _():
                kwin_ref[0] = kn[sub - WINDOW:]
                vwin_ref[0] = v[sub - WINDOW:]

    rows = 4 * WINDOW
    band = 2 * WINDOW
    rr = lax.broadcasted_iota(jnp.int32, (rows, band), 0)
    kk = lax.broadcasted_iota(jnp.int32, (rows, band), 1)
    half = ((rr % WINDOW) >= CHUNK).astype(jnp.int32)
    allowed = (kk >= half * CHUNK) & (kk < (3 + half) * CHUNK)
    bias_mid = jnp.where(allowed, 0.0, NEG)
    bias_first = jnp.where(allowed & (kk >= WINDOW), 0.0, NEG)

    def attn_body(j, carry):
        r0 = pl.multiple_of(j * WINDOW, WINDOW)
        bias = jnp.where((t == 0) & (j == 0), bias_first, bias_mid)
        for g in range(N_KV_HEADS):
            c0 = g * 2 * LANES
            qa = qn_scr[pl.ds(r0, WINDOW), c0:c0 + LANES]
            qb = qn_scr[pl.ds(r0, WINDOW), c0 + LANES:c0 + 2 * LANES]
            k2 = k2_scr[g, pl.ds(r0, band), :]
            v2 = v2_scr[g, pl.ds(r0, band), :]
            pa, pb = _attn_block(qa, qb, k2, v2, bias, _sink_rows(sink_ref, g, WINDOW))
            cat_scr[pl.ds(r0, WINDOW), c0:c0 + LANES] = pa.astype(jnp.bfloat16)
            cat_scr[pl.ds(r0, WINDOW), c0 + LANES:c0 + 2 * LANES] = pb.astype(jnp.bfloat16)
        return carry

    lax.fori_loop(0, nblk, attn_body, 0, unroll=True)

    k2_scr[:, 0:WINDOW, :] = k2_scr[:, tt:tt + WINDOW, :]
    v2_scr[:, 0:WINDOW, :] = v2_scr[:, tt:tt + WINDOW, :]

    for h in range(GATE_HEADS):
        w = _causal_ws(ws_ref, h, MLP_CHUNK)
        bcol = bs_ref[:, h:h + 1]
        for c in range(tt // MLP_CHUNK):
            rs = slice(c * MLP_CHUNK, (c + 1) * MLP_CHUNK)
            cs = slice(h * GATE_DIM, (h + 1) * GATE_DIM)
            mix = jnp.dot(w, gvn_scr[rs, cs], preferred_element_type=jnp.float32) + bcol
            cat_scr[rs, Q_WIDTH + h * GATE_DIM:Q_WIDTH + (h + 1) * GATE_DIM] = (ua_scr[rs, cs] * mix).astype(jnp.bfloat16)

    cnt = cnt_scr[...]
    for s in range(ROW_SPLITS):
        rs = slice(s * sub, (s + 1) * sub)
        x1, hp, slab, cnt = _post(cat_scr[rs, :], x_ref[0, rs, :], wout_ref[...], gffn_ref[...], wr_ref[...],
                                  br_ref[...], tril_ref[...], cnt)
        x1_ref[rs, :] = x1
        h2_ref[rs, :] = hp
        slab_ref[rs, :] = slab
    cnt_scr[...] = cnt
    cnt_ref[...] = cnt


def _sample_kernel(sink_ref, x_ref, ck_ref, cv_ref, gmix_ref, win_ref, gq_ref, gk_ref, gvg_ref, blk_ref, ws_ref,
                   bs_ref, wout_ref, gffn_ref, wr_ref, br_ref, tril_ref,
                   x1_ref, h2_ref, slab_ref, kn_ref, v_ref, gvn_ref, cnt_ref,
                   ck2_scr, cv2_scr, k2_scr, v2_scr, qn_scr, ua_scr, cat_scr, *, n_seq, seq_len):
    x = x_ref[...]
    qn, kn, v, ua, gvn = _project(x, gmix_ref[...], win_ref[...], gq_ref[...], gk_ref[...], gvg_ref[...],
                                  blk_ref[...])
    kn_ref[...] = kn
    v_ref[...] = v
    gvn_ref[...] = gvn
    qn_scr[...] = qn
    ua_scr[...] = ua
    for scr, val in ((k2_scr, kn), (v2_scr, v), (ck2_scr, ck_ref[...]), (cv2_scr, cv_ref[...])):
        a0, a1 = _dup_halves(val)
        scr[0] = a0
        scr[1] = a1

    ws = [_causal_ws(ws_ref, h, seq_len) for h in range(GATE_HEADS)]

    def seq_body(i, carry):
        r0 = pl.multiple_of(i * seq_len, seq_len)
        c0r = pl.multiple_of(i * WINDOW, WINDOW)
        for g in range(N_KV_HEADS):
            c0 = g * 2 * LANES
            qa = qn_scr[pl.ds(r0, seq_len), c0:c0 + LANES]
            qb = qn_scr[pl.ds(r0, seq_len), c0 + LANES:c0 + 2 * LANES]
            k2 = jnp.concatenate([ck2_scr[g, pl.ds(c0r, WINDOW), :], k2_scr[g, pl.ds(r0, seq_len), :]], axis=0)
            v2 = jnp.concatenate([cv2_scr[g, pl.ds(c0r, WINDOW), :], v2_scr[g, pl.ds(r0, seq_len), :]], axis=0)
            pa, pb = _attn_block(qa, qb, k2, v2, None, _sink_rows(sink_ref, g, seq_len))
            cat_scr[pl.ds(r0, seq_len), c0:c0 + LANES] = pa.astype(jnp.bfloat16)
            cat_scr[pl.ds(r0, seq_len), c0 + LANES:c0 + 2 * LANES] = pb.astype(jnp.bfloat16)
        for h in range(GATE_HEADS):
            cs = slice(h * GATE_DIM, (h + 1) * GATE_DIM)
            gv_h = gvn_ref[pl.ds(r0, seq_len), cs].astype(jnp.bfloat16)
            mix = jnp.dot(ws[h], gv_h, preferred_element_type=jnp.float32) + bs_ref[0:seq_len, h:h + 1]
            cat_scr[pl.ds(r0, seq_len), Q_WIDTH + h * GATE_DIM:Q_WIDTH + (h + 1) * GATE_DIM] = (
                ua_scr[pl.ds(r0, seq_len), cs] * mix).astype(jnp.bfloat16)
        return carry

    lax.fori_loop(0, n_seq, seq_body, 0)

    cnt0 = jnp.zeros((N_EXPERTS, LANES), jnp.float32)
    x1, hp, slab, new_cnt = _post(cat_scr[...], x, wout_ref[...], gffn_ref[...], wr_ref[...],
                                  br_ref[...], tril_ref[...], cnt0)
    x1_ref[...] = x1
    h2_ref[...] = hp
    slab_ref[...] = slab
    cnt_ref[...] = new_cnt


def _expert_kernel(ts_ref, te_ref, nv_ref, nu_ref, xa_hbm, xb_hbm, wg_ref, wu_ref, wd_ref, oa_hbm, ob_hbm,
                   wg_s, wu_s, wd_s, xa_buf, xb_buf, oa_buf, ob_buf, in_sem, out_sem):
    e = pl.program_id(0)
    tm = xa_buf.shape[1]
    n_used = nu_ref[0]

    def rows_of(g):
        return pl.ds(pl.multiple_of(g * tm, tm), tm)

    def in_copies(g, slot):
        return (pltpu.make_async_copy(xa_hbm.at[rows_of(g)], xa_buf.at[slot], in_sem.at[0, slot]),
                pltpu.make_async_copy(xb_hbm.at[rows_of(g)], xb_buf.at[slot], in_sem.at[1, slot]))

    def out_copies(g, slot):
        return (pltpu.make_async_copy(oa_buf.at[slot], oa_hbm.at[rows_of(g)], out_sem.at[0, slot]),
                pltpu.make_async_copy(ob_buf.at[slot], ob_hbm.at[rows_of(g)], out_sem.at[1, slot]))

    @pl.when((e == 0) & (n_used > 0))
    def _():
        for c in in_copies(0, 0):
            c.start()

    g_lo = ts_ref[e]
    g_hi = te_ref[e]

    @pl.when(g_hi > g_lo)
    def _():
        wg_s[...] = wg_ref[0].astype(jnp.bfloat16)
        wu_s[...] = wu_ref[0].astype(jnp.bfloat16)
        wd_s[...] = wd_ref[0].astype(jnp.bfloat16)

    def tile_body(g, carry):
        slot = g % 2
        for c in in_copies(g, slot):
            c.wait()

        @pl.when(g + 1 < n_used)
        def _():
            for c in in_copies(g + 1, 1 - slot):
                c.start()

        live = lax.broadcasted_iota(jnp.int32, (tm, xa_buf.shape[2]), 0) < nv_ref[g]
        ha, la = _unpack_halves(jnp.where(live, xa_buf[slot], jnp.uint32(0)))
        hb, lb = _unpack_halves(jnp.where(live, xb_buf[slot], jnp.uint32(0)))
        xs = jnp.concatenate([ha, hb, la, lb], axis=1).astype(jnp.bfloat16)
        hg = jnp.dot(xs, wg_s[...], preferred_element_type=jnp.float32)
        hu = jnp.dot(xs, wu_s[...], preferred_element_type=jnp.float32)
        a = (jax.nn.silu(hg) * hu).astype(jnp.bfloat16)
        packed = _pack_halves(jnp.dot(a, wd_s[...], preferred_element_type=jnp.float32))
        q = packed.shape[1] // 2

        @pl.when(g >= 2)
        def _():
            for c in out_copies(g - 2, slot):
                c.wait()

        oa_buf[slot] = packed[:, :q]
        ob_buf[slot] = packed[:, q:]
        for c in out_copies(g, slot):
            c.start()
        return carry

    lax.fori_loop(g_lo, g_hi, tile_body, 0)

    @pl.when(e == pl.num_programs(0) - 1)
    def _():
        for back in (1, 2):
            @pl.when(n_used >= back)
            def _():
                g = n_used - back
                for c in out_copies(g, g % 2):
                    c.wait()


def _combine_kernel(x1_ref, o1a_ref, o1b_ref, o2a_ref, o2b_ref, slab_ref, y_ref):
    slab = slab_ref[...]
    g1 = slab[:, 4:5]
    g2 = slab[:, 5:6]
    q = o1a_ref.shape[1]
    for c, (r1, r2) in enumerate(((o1a_ref, o2a_ref), (o1b_ref, o2b_ref))):
        h1, l1 = _unpack_halves(r1[...])
        h2, l2 = _unpack_halves(r2[...])
        hs = slice(c * q, (c + 1) * q)
        ls = slice(2 * q + c * q, 2 * q + (c + 1) * q)
        y_ref[:, hs] = x1_ref[:, hs] + (g1 * h1 + g2 * h2)
        y_ref[:, ls] = x1_ref[:, ls] + (g1 * l1 + g2 * l2)


def _sc_mesh():
    return plsc.VectorSubcoreMesh(core_axis_name="core", subcore_axis_name="subcore")


def _sc_scatter_rows(parts, pos0, pos1, n_rows):
    n_piece = parts[0].shape[1] // SC_COLS

    def sc_kernel(*refs):
        x_refs = refs[:len(parts)]
        i0_hbm, i1_hbm = refs[len(parts):len(parts) + 2]
        o_refs = refs[len(parts) + 2:]
        tok0 = 0
        for x_hbm, arr in zip(x_refs, parts):
            blk0 = tok0 // SC_WINDOW
            for c in range(n_piece):
                def body(x_vmem, i0_vmem, i1_vmem, o_hbm=o_refs[c]):
                    pltpu.sync_copy(x_vmem, o_hbm.at[i0_vmem.at[0]])
                    pltpu.sync_copy(x_vmem, o_hbm.at[i1_vmem.at[0]])

                pltpu.emit_pipeline(
                    body,
                    grid=(arr.shape[0] // SC_WINDOW,),
                    in_specs=[pl.BlockSpec((SC_WINDOW, SC_COLS), lambda i, c=c: (i, c)),
                              pl.BlockSpec((1, SC_WINDOW), lambda i, blk0=blk0: (0, blk0 + i)),
                              pl.BlockSpec((1, SC_WINDOW), lambda i, blk0=blk0: (0, blk0 + i))],
                    out_specs=[],
                    core_axis_name=("core", "subcore"),
                    dimension_semantics=(pltpu.PARALLEL,),
                )(x_hbm, i0_hbm, i1_hbm)
            tok0 += arr.shape[0]

    piece = jax.ShapeDtypeStruct((n_rows, SC_COLS), jnp.uint32)
    return pl.kernel(sc_kernel, out_type=(piece,) * n_piece, mesh=_sc_mesh(), name="scatter_rows")(
        *parts, pos0, pos1)


def _sc_gather_rows(pieces, pos_list):
    n_tok = pos_list[0].shape[1]

    def sc_kernel(*refs):
        s_refs = refs[:len(pieces)]
        i_refs = refs[len(pieces):len(pieces) + len(pos_list)]
        o_refs = refs[len(pieces) + len(pos_list):]
        k = 0
        for i_hbm in i_refs:
            for s_hbm in s_refs:
                def body(i_vmem, o_vmem, s_hbm=s_hbm):
                    pltpu.sync_copy(s_hbm.at[i_vmem.at[0]], o_vmem)

                pltpu.emit_pipeline(
                    body,
                    grid=(n_tok // SC_WINDOW,),
                    in_specs=[pl.BlockSpec((1, SC_WINDOW), lambda i: (0, i))],
                    out_specs=[pl.BlockSpec((SC_WINDOW, SC_COLS), lambda i: (i, 0))],
                    core_axis_name=("core", "subcore"),
                    dimension_semantics=(pltpu.PARALLEL,),
                )(i_hbm, o_refs[k])
                k += 1

    out = jax.ShapeDtypeStruct((n_tok, SC_COLS), jnp.uint32)
    outs = pl.kernel(sc_kernel, out_type=(out,) * (len(pieces) * len(pos_list)), mesh=_sc_mesh(),
                     name="gather_rows")(*pieces, *pos_list)
    return [outs[j * len(pieces):(j + 1) * len(pieces)] for j in range(len(pos_list))]


def _const_spec(shape):
    nd = len(shape)
    return pl.BlockSpec(shape, lambda *_: (0,) * nd)


def kernel(x_prompt, x_sample, cache_attn_k, cache_attn_v, g_mix, w_in, g_q, g_k, g_v, attn_sinks, w_s, b_s,
           w_out, g_ffn, w_coarse, b_coarse, w_fine, b_fine, w_gate, w_up, w_down):
    nb, seq, d = x_prompt.shape
    ns, slen, _ = x_sample.shape
    n_p = nb * seq
    n_s = ns * slen
    n_tok = n_p + n_s
    tt = TOKEN_TILE
    assert seq % tt == 0 and n_s % tt == 0 and d == D_MODEL
    nt = seq // tt
    bf = jnp.bfloat16
    f32 = jnp.float32

    l = 0
    gmix = g_mix[l].reshape(1, d)
    win = w_in[l].astype(bf)
    gq = (jnp.tile(g_q[l], N_Q_HEADS) * (HEAD_DIM ** -0.5)).reshape(1, Q_WIDTH)
    gk = jnp.tile(g_k[l], N_KV_HEADS).reshape(1, KV_WIDTH)
    gvg = g_v[l].reshape(1, GATE_WIDTH)
    sinks = attn_sinks[l].reshape(N_Q_HEADS).astype(f32)
    ws = w_s[l].astype(bf)
    bs = b_s[l].T
    wout = w_out[l].astype(bf)
    gffn = g_ffn[l].reshape(1, d)
    wr = jnp.concatenate([w_coarse[l], jnp.transpose(w_fine[l], (1, 0, 2)).reshape(d, N_EXPERTS),
                          jnp.zeros((d, ROUTE_LANES - N_GROUPS - N_EXPERTS), f32)], axis=1)
    wr = wr.astype(bf).T
    br = jnp.concatenate([b_coarse[l], b_fine[l].reshape(-1),
                          jnp.zeros((ROUTE_LANES - N_GROUPS - N_EXPERTS,), f32)])
    br = jnp.broadcast_to(br[:, None], (ROUTE_LANES, LANES))
    cnt_shape = (N_EXPERTS, LANES)
    ii = jnp.arange(LANES)
    blk = (ii[:, None] // HEAD_DIM == ii[None, :] // HEAD_DIM).astype(bf)
    u32 = jnp.uint32
    dh = d // 2

    def strict_tril(n):
        r = jnp.arange(n)
        return (r[:, None] < r[None, :]).astype(bf)

    weight_args = (gmix, win, gq, gk, gvg, blk, ws, bs, wout, gffn, wr, br)
    weight_specs = [_const_spec(a.shape) for a in weight_args]
    smem_spec = pl.BlockSpec(memory_space=pltpu.SMEM)

    xs2 = x_sample.reshape(n_s, d)
    ck = cache_attn_k[l].reshape(ns * WINDOW, KV_WIDTH)
    cv = cache_attn_v[l].reshape(ns * WINDOW, KV_WIDTH)
    tok_out = lambda n, w, dt: jax.ShapeDtypeStruct((n, w), dt)
    x1_s, h2_s, slab_s, kn_s, v_s, gvn_s, cnt_s = pl.pallas_call(
        functools.partial(_sample_kernel, n_seq=ns, seq_len=slen),
        grid=(1,),
        in_specs=[smem_spec, _const_spec((n_s, d)), _const_spec(ck.shape), _const_spec(cv.shape)]
                 + weight_specs + [_const_spec((n_s, n_s))],
        out_specs=[_const_spec((n_s, d)), _const_spec((n_s, dh)), _const_spec((n_s, ROUTE_LANES)),
                   _const_spec((n_s, KV_WIDTH)), _const_spec((n_s, KV_WIDTH)), _const_spec((n_s, GATE_WIDTH)),
                   _const_spec(cnt_shape)],
        out_shape=[tok_out(n_s, d, f32), tok_out(n_s, dh, u32), tok_out(n_s, ROUTE_LANES, f32),
                   jax.ShapeDtypeStruct((n_s, KV_WIDTH), f32), jax.ShapeDtypeStruct((n_s, KV_WIDTH), f32),
                   jax.ShapeDtypeStruct((n_s, GATE_WIDTH), f32), jax.ShapeDtypeStruct(cnt_shape, f32)],
        scratch_shapes=[pltpu.VMEM((N_KV_HEADS, ns * WINDOW, LANES), bf), pltpu.VMEM((N_KV_HEADS, ns * WINDOW, LANES), bf),
                        pltpu.VMEM((N_KV_HEADS, n_s, LANES), bf), pltpu.VMEM((N_KV_HEADS, n_s, LANES), bf),
                        pltpu.VMEM((n_s, Q_WIDTH), bf), pltpu.VMEM((n_s, GATE_WIDTH), f32),
                        pltpu.VMEM((n_s, d), bf)],
        compiler_params=pltpu.CompilerParams(dimension_semantics=("arbitrary",), vmem_limit_bytes=VMEM_LIMIT),
        name="mixer_sample",
    )(sinks, xs2, ck, cv, *weight_args, strict_tril(n_s))

    x1_p, h2_p, slab_p, kwin, vwin, cnt = pl.pallas_call(
        _prompt_kernel,
        grid=(nb, nt),
        in_specs=[smem_spec, pl.BlockSpec((1, tt, d), lambda b, t: (b, t, 0))] + weight_specs
                 + [_const_spec((tt // ROW_SPLITS, tt // ROW_SPLITS)), _const_spec(cnt_shape)],
        out_specs=[pl.BlockSpec((tt, d), lambda b, t: (b * nt + t, 0)),
                   pl.BlockSpec((tt, dh), lambda b, t: (b * nt + t, 0)),
                   pl.BlockSpec((tt, ROUTE_LANES), lambda b, t: (b * nt + t, 0)),
                   pl.BlockSpec((1, WINDOW, KV_WIDTH), lambda b, t: (b, 0, 0)),
                   pl.BlockSpec((1, WINDOW, KV_WIDTH), lambda b, t: (b, 0, 0)),
                   _const_spec(cnt_shape)],
        out_shape=[tok_out(n_p, d, f32), tok_out(n_p, dh, u32), tok_out(n_p, ROUTE_LANES, f32),
                   jax.ShapeDtypeStruct((nb, WINDOW, KV_WIDTH), f32),
                   jax.ShapeDtypeStruct((nb, WINDOW, KV_WIDTH), f32),
                   jax.ShapeDtypeStruct(cnt_shape, f32)],
        scratch_shapes=[pltpu.VMEM((N_KV_HEADS, tt + WINDOW, LANES), bf),
                        pltpu.VMEM((N_KV_HEADS, tt + WINDOW, LANES), bf),
                        pltpu.VMEM((tt, Q_WIDTH), bf), pltpu.VMEM((tt, GATE_WIDTH), f32),
                        pltpu.VMEM((tt, GATE_WIDTH), bf), pltpu.VMEM((tt, d), bf),
                        pltpu.VMEM(cnt_shape, f32)],
        compiler_params=pltpu.CompilerParams(dimension_semantics=("arbitrary", "arbitrary"),
                                             vmem_limit_bytes=VMEM_LIMIT),
        name="mixer_prompt",
    )(sinks, x_prompt, *weight_args, strict_tril(tt // ROW_SPLITS), cnt_s)
    slab_all = jnp.concatenate([slab_p[:, :8], slab_s[:, :8]], axis=0)

    tm = EXPERT_TILE
    counts = cnt[:, 0].astype(jnp.int32)
    tiles = (counts + tm - 1) // tm
    tile_end = jnp.cumsum(tiles)
    offs = (tile_end - tiles) * tm
    n_used = tile_end[-1]
    n_tiles = (2 * n_tok) // tm + N_EXPERTS
    n_rows = n_tiles * tm
    tile_ids = jnp.arange(n_tiles, dtype=jnp.int32)
    tile_expert = jnp.minimum(jnp.sum(tile_ids[:, None] >= tile_end[None, :], axis=1), N_EXPERTS - 1).astype(jnp.int32)
    tile_in_expert = tile_ids - (tile_end - tiles)[tile_expert]
    tile_rows = jnp.clip(counts[tile_expert] - tile_in_expert * tm, 0, tm).astype(jnp.int32)
    eid = slab_all[:, 0:2].astype(jnp.int32)
    rank = slab_all[:, 2:4].astype(jnp.int32)
    off_of = jnp.sum(jnp.where(eid[:, :, None] == jnp.arange(N_EXPERTS, dtype=jnp.int32), offs, 0), axis=-1)
    pos = off_of + rank
    pos0 = pos[:, 0].reshape(1, n_tok)
    pos1 = pos[:, 1].reshape(1, n_tok)

    assert dh == 2 * SC_COLS and n_p % SC_WINDOW == 0 and n_s % SC_WINDOW == 0
    xs_a, xs_b = _sc_scatter_rows([h2_p, h2_s], pos0, pos1, n_rows)

    w_map = lambda e, *_: (e, 0, 0)
    hbm = pl.BlockSpec(memory_space=pl.ANY)
    piece = jax.ShapeDtypeStruct((n_rows, SC_COLS), u32)
    tile_buf = pltpu.VMEM((2, tm, SC_COLS), u32)
    out_a, out_b = pl.pallas_call(
        _expert_kernel,
        grid_spec=pltpu.PrefetchScalarGridSpec(
            num_scalar_prefetch=4,
            grid=(N_EXPERTS,),
            in_specs=[hbm, hbm,
                      pl.BlockSpec((1, d, D_EXPERT), w_map),
                      pl.BlockSpec((1, d, D_EXPERT), w_map),
                      pl.BlockSpec((1, D_EXPERT, d), w_map)],
            out_specs=[hbm, hbm],
            scratch_shapes=[pltpu.VMEM((d, D_EXPERT), bf), pltpu.VMEM((d, D_EXPERT), bf),
                            pltpu.VMEM((D_EXPERT, d), bf),
                            tile_buf, tile_buf, tile_buf, tile_buf,
                            pltpu.SemaphoreType.DMA((2, 2)), pltpu.SemaphoreType.DMA((2, 2))]),
        out_shape=[piece, piece],
        compiler_params=pltpu.CompilerParams(dimension_semantics=("arbitrary",), vmem_limit_bytes=VMEM_LIMIT),
        name="experts",
    )((tile_end - tiles).astype(jnp.int32), tile_end.astype(jnp.int32), tile_rows,
      n_used.reshape(1).astype(jnp.int32), xs_a, xs_b, w_gate[l], w_up[l], w_down[l])

    (o1a, o1b), (o2a, o2b) = _sc_gather_rows([out_a, out_b], [pos0, pos1])

    def combine(x1, slab, blk0):
        n = x1.shape[0]
        tok = lambda i: (i, 0)
        off = lambda i: (blk0 + i, 0)
        return pl.pallas_call(
            _combine_kernel,
            grid=(n // tt,),
            in_specs=[pl.BlockSpec((tt, d), tok)] + [pl.BlockSpec((tt, SC_COLS), off)] * 4
                     + [pl.BlockSpec((tt, ROUTE_LANES), tok)],
            out_specs=pl.BlockSpec((tt, d), tok),
            out_shape=jax.ShapeDtypeStruct((n, d), f32),
            compiler_params=pltpu.CompilerParams(dimension_semantics=("arbitrary",)),
            name="combine",
        )(x1, o1a, o1b, o2a, o2b, slab)

    y_p = combine(x1_p, slab_p, 0).reshape(nb, seq, d)
    y_s = combine(x1_s, slab_s, n_p // tt).reshape(ns, slen, d)

    kv_shape = (1, nb, WINDOW, N_KV_HEADS, HEAD_DIM)
    new_k_p = kwin.reshape(kv_shape)
    new_v_p = vwin.reshape(kv_shape)
    keep = WINDOW - slen
    ck4 = cache_attn_k[l][:, WINDOW - keep:]
    cv4 = cache_attn_v[l][:, WINDOW - keep:]
    new_k_s = jnp.concatenate([ck4, kn_s.reshape(ns, slen, N_KV_HEADS, HEAD_DIM)], axis=1)[None]
    new_v_s = jnp.concatenate([cv4, v_s.reshape(ns, slen, N_KV_HEADS, HEAD_DIM)], axis=1)[None]
    new_gv_s = gvn_s.reshape(1, ns, slen, GATE_HEADS, GATE_DIM)
    return (y_p, y_s, new_k_p, new_v_p, new_k_s, new_v_s, new_gv_s)
```

```python
import functools

import jax
import jax.numpy as jnp
from jax import lax
from jax.experimental import pallas as pl
from jax.experimental.pallas import tpu as pltpu
from jax.experimental.pallas import tpu_sc as plsc

D_MODEL = 1024
HEAD_DIM = 64
N_Q_HEADS = 8
N_KV_HEADS = 2
Q_WIDTH = N_Q_HEADS * HEAD_DIM
KV_WIDTH = N_KV_HEADS * HEAD_DIM
GATE_HEADS = 4
GATE_DIM = 128
GATE_WIDTH = GATE_HEADS * GATE_DIM
PROJ_COLS = Q_WIDTH + 2 * KV_WIDTH + 2 * GATE_WIDTH
CHUNK = 64
WINDOW = 128
MLP_CHUNK = 128
N_GROUPS = 8
EXP_PER_GROUP = 8
N_EXPERTS = N_GROUPS * EXP_PER_GROUP
D_EXPERT = 512
EPS = 1e-6

LANES = 128
ROUTE_LANES = 128
FINE_LANE0 = N_GROUPS
NEG = -1e30

TOKEN_TILE = 512
ROW_SPLITS = 2
EXPERT_TILE = 512
COMBINE_CHUNKS = 4
SC_WINDOW = 128
SC_COLS = 256
VMEM_LIMIT = 48 * 1024 * 1024


def _pack_halves(x):
    w = x.shape[1] // 2
    b = lax.bitcast_convert_type(x.astype(jnp.bfloat16).astype(jnp.float32), jnp.uint32)
    return (b[:, :w] & jnp.uint32(0xFFFF0000)) | (b[:, w:] >> 16)


def _unpack_halves(p):
    hi = lax.bitcast_convert_type(p & jnp.uint32(0xFFFF0000), jnp.float32)
    lo = lax.bitcast_convert_type(p << 16, jnp.float32)
    return hi, lo


def _rms(x, eps=EPS):
    return x * lax.rsqrt(jnp.mean(x * x, axis=-1, keepdims=True) + eps)


def _project(x, gmix, win, gq, gk, gvg, blk):
    h = (_rms(x) * gmix).astype(jnp.bfloat16)
    z = jnp.dot(h, win, preferred_element_type=jnp.float32)
    qk = z[:, :Q_WIDTH + KV_WIDTH]
    outs = []
    for j in range((Q_WIDTH + KV_WIDTH) // LANES):
        zj = qk[:, j * LANES:(j + 1) * LANES]
        ss = jnp.dot((zj * zj).astype(jnp.bfloat16), blk, preferred_element_type=jnp.float32)
        outs.append(zj * lax.rsqrt(ss * (1.0 / HEAD_DIM) + EPS))
    qn = jnp.concatenate(outs[:Q_WIDTH // LANES], axis=-1) * gq
    kn = outs[-1] * gk
    v = z[:, Q_WIDTH + KV_WIDTH:Q_WIDTH + 2 * KV_WIDTH]
    u0 = Q_WIDTH + 2 * KV_WIDTH
    ua = jax.nn.gelu(z[:, u0:u0 + GATE_WIDTH])
    ga = jax.nn.gelu(z[:, u0 + GATE_WIDTH:])
    gvn = jnp.concatenate(
        [_rms(ga[:, i * GATE_DIM:(i + 1) * GATE_DIM]) for i in range(GATE_HEADS)], axis=-1) * gvg
    return qn.astype(jnp.bfloat16), kn, v, ua, gvn


def _dup_halves(a):
    lo = lax.broadcasted_iota(jnp.int32, a.shape, 1) < HEAD_DIM
    r = pltpu.roll(a, HEAD_DIM, axis=1)
    return (jnp.where(lo, a, r).astype(jnp.bfloat16), jnp.where(lo, r, a).astype(jnp.bfloat16))


def _attn_block(qa, qb, k2, v2, bias, sk):
    r = qa.shape[0]
    lo = lax.broadcasted_iota(jnp.int32, (r, LANES), 1) < HEAD_DIM
    zero = jnp.zeros_like(qa)
    qs = jnp.concatenate([jnp.where(lo, qa, zero), jnp.where(lo, zero, qa),
                          jnp.where(lo, qb, zero), jnp.where(lo, zero, qb)], axis=0)
    s = lax.dot_general(qs, k2, (((1,), (1,)), ((), ())), preferred_element_type=jnp.float32)
    if bias is not None:
        s = s + bias
    m = jnp.maximum(jnp.max(s, axis=-1, keepdims=True), sk)
    e = jnp.exp(s - m)
    den = jnp.sum(e, axis=-1, keepdims=True) + jnp.exp(sk - m)
    o = jnp.dot(e.astype(jnp.bfloat16), v2, preferred_element_type=jnp.float32) * (1.0 / den)
    pa = jnp.where(lo, o[0:r], o[r:2 * r])
    pb = jnp.where(lo, o[2 * r:3 * r], o[3 * r:4 * r])
    return pa, pb


def _sink_rows(sink_ref, g, r):
    row = lax.broadcasted_iota(jnp.int32, (4 * r, 1), 0)
    s0, s1, s2, s3 = (sink_ref[4 * g + i] for i in range(4))
    return jnp.where(row < r, s0, jnp.where(row < 2 * r, s1, jnp.where(row < 3 * r, s2, s3)))


def _causal_ws(ws_ref, h, n):
    w = ws_ref[h][:n, :n]
    keep = (lax.broadcasted_iota(jnp.int32, (n, n), 0) >= lax.broadcasted_iota(jnp.int32, (n, n), 1))
    return jnp.where(keep, w, jnp.zeros_like(w))


def _post(cat, x, wout, gffn, wrt, brt, triu, cnt):
    x1 = x + jnp.dot(cat, wout, preferred_element_type=jnp.float32)
    h2 = _rms(x1) * gffn
    t = x.shape[0]
    reps = t // LANES
    lgt = lax.dot_general(wrt, h2.astype(jnp.bfloat16), (((1,), (1,)), ((), ())),
                          preferred_element_type=jnp.float32) + jnp.tile(brt, (1, reps))
    ng = float(EXP_PER_GROUP)
    sub = lax.broadcasted_iota(jnp.int32, (EXP_PER_GROUP, t), 0).astype(jnp.float32)
    c = lgt[0:N_GROUPS]
    mc = jnp.max(c, axis=0, keepdims=True)
    grp = jnp.min(jnp.where(c == mc, sub, ng), axis=0, keepdims=True)
    pg = 1.0 / jnp.sum(jnp.exp(c - mc), axis=0, keepdims=True)
    lf = lgt[FINE_LANE0:FINE_LANE0 + EXP_PER_GROUP]
    for g in range(1, N_GROUPS):
        r0 = FINE_LANE0 + g * EXP_PER_GROUP
        lf = jnp.where(grp == float(g), lgt[r0:r0 + EXP_PER_GROUP], lf)
    v1 = jnp.max(lf, axis=0, keepdims=True)
    i1 = jnp.min(jnp.where(lf == v1, sub, ng), axis=0, keepdims=True)
    lf2 = jnp.where(sub == i1, -jnp.inf, lf)
    v2 = jnp.max(lf2, axis=0, keepdims=True)
    i2 = jnp.min(jnp.where(lf2 == v2, sub, ng), axis=0, keepdims=True)
    tt = jnp.exp(v2 - v1)
    w1 = 1.0 / (1.0 + tt)
    w2 = tt * w1
    e1 = grp * ng + i1
    e2 = grp * ng + i2
    row = lax.broadcasted_iota(jnp.int32, (N_EXPERTS, t), 0).astype(jnp.float32)
    sel1 = row == e1
    sel2 = row == e2
    oh = jnp.where(sel1 | sel2, 1.0, 0.0)
    cum = jnp.dot(oh.astype(jnp.bfloat16), triu, preferred_element_type=jnp.float32) + jnp.tile(cnt, (1, reps))
    r1 = jnp.sum(jnp.where(sel1, cum, 0.0), axis=0, keepdims=True)
    r2 = jnp.sum(jnp.where(sel2, cum, 0.0), axis=0, keepdims=True)
    new_cnt = cnt + jnp.sum(oh, axis=1, keepdims=True)
    slab_t = jnp.where(sub == 0.0, e1,
             jnp.where(sub == 1.0, e2,
             jnp.where(sub == 2.0, r1,
             jnp.where(sub == 3.0, r2,
             jnp.where(sub == 4.0, pg * w1,
             jnp.where(sub == 5.0, pg * w2, 0.0))))))
    slab = jnp.concatenate([slab_t, jnp.zeros((ROUTE_LANES - EXP_PER_GROUP, t), jnp.float32)], axis=0).T
    return x1, _pack_halves(h2), slab, slab_t, new_cnt


def _prompt_kernel(sink_ref, x_ref, gmix_ref, win_ref, gq_ref, gk_ref, gvg_ref, blk_ref, ws_ref, bs_ref,
                   wout_ref, gffn_ref, wr_ref, br_ref, tril_ref, cnt_in_ref,
                   x1_ref, h2_ref, slab_ref, slabt_ref, kwin_ref, vwin_ref, cnt_ref,
                   k2_scr, v2_scr, qn_scr, ua_scr, gvn_scr, cat_scr, cnt_scr):
    b = pl.program_id(0)
    t = pl.program_id(1)
    nt = pl.num_programs(1)
    tt = x_ref.shape[1]
    nblk = tt // WINDOW
    sub = tt // ROW_SPLITS

    @pl.when((b == 0) & (t == 0))
    def _():
        cnt_scr[...] = cnt_in_ref[...]

    @pl.when(t == 0)
    def _():
        k2_scr[:, 0:WINDOW, :] = jnp.zeros((N_KV_HEADS, WINDOW, LANES), jnp.bfloat16)
        v2_scr[:, 0:WINDOW, :] = jnp.zeros((N_KV_HEADS, WINDOW, LANES), jnp.bfloat16)

    for s in range(ROW_SPLITS):
        rs = slice(s * sub, (s + 1) * sub)
        ks = slice(WINDOW + s * sub, WINDOW + (s + 1) * sub)
        qn, kn, v, ua, gvn = _project(x_ref[0, rs, :], gmix_ref[...], win_ref[...], gq_ref[...], gk_ref[...],
                                      gvg_ref[...], blk_ref[...])
        qn_scr[rs, :] = qn
        ua_scr[rs, :] = ua
        gvn_scr[rs, :] = gvn.astype(jnp.bfloat16)
        k0, k1 = _dup_halves(kn)
        v0, v1 = _dup_halves(v)
        k2_scr[0, ks, :] = k0
        k2_scr[1, ks, :] = k1
        v2_scr[0, ks, :] = v0
        v2_scr[1, ks, :] = v1
        if s == ROW_SPLITS - 1:
            @pl.when(t == nt - 1)
            def _():
                kwin_ref[0] = kn[sub - WINDOW:]
                vwin_ref[0] = v[sub - WINDOW:]

    rows = 4 * WINDOW
    band = 2 * WINDOW
    rr = lax.broadcasted_iota(jnp.int32, (rows, band), 0)
    kk = lax.broadcasted_iota(jnp.int32, (rows, band), 1)
    half = ((rr % WINDOW) >= CHUNK).astype(jnp.int32)
    allowed = (kk >= half * CHUNK) & (kk < (3 + half) * CHUNK)
    bias_mid = jnp.where(allowed, 0.0, NEG)
    bias_first = jnp.where(allowed & (kk >= WINDOW), 0.0, NEG)

    def attn_body(j, carry):
        r0 = pl.multiple_of(j * WINDOW, WINDOW)
        bias = jnp.where((t == 0) & (j == 0), bias_first, bias_mid)
        for g in range(N_KV_HEADS):
            c0 = g * 2 * LANES
            qa = qn_scr[pl.ds(r0, WINDOW), c0:c0 + LANES]
            qb = qn_scr[pl.ds(r0, WINDOW), c0 + LANES:c0 + 2 * LANES]
            k2 = k2_scr[g, pl.ds(r0, band), :]
            v2 = v2_scr[g, pl.ds(r0, band), :]
            pa, pb = _attn_block(qa, qb, k2, v2, bias, _sink_rows(sink_ref, g, WINDOW))
            cat_scr[pl.ds(r0, WINDOW), c0:c0 + LANES] = pa.astype(jnp.bfloat16)
            cat_scr[pl.ds(r0, WINDOW), c0 + LANES:c0 + 2 * LANES] = pb.astype(jnp.bfloat16)
        return carry

    lax.fori_loop(0, nblk, attn_body, 0, unroll=True)

    k2_scr[:, 0:WINDOW, :] = k2_scr[:, tt:tt + WINDOW, :]
    v2_scr[:, 0:WINDOW, :] = v2_scr[:, tt:tt + WINDOW, :]

    for h in range(GATE_HEADS):
        w = _causal_ws(ws_ref, h, MLP_CHUNK)
        bcol = bs_ref[:, h:h + 1]
        for c in range(tt // MLP_CHUNK):
            rs = slice(c * MLP_CHUNK, (c + 1) * MLP_CHUNK)
            cs = slice(h * GATE_DIM, (h + 1) * GATE_DIM)
            mix = jnp.dot(w, gvn_scr[rs, cs], preferred_element_type=jnp.float32) + bcol
            cat_scr[rs, Q_WIDTH + h * GATE_DIM:Q_WIDTH + (h + 1) * GATE_DIM] = (ua_scr[rs, cs] * mix).astype(jnp.bfloat16)

    cnt = cnt_scr[...]
    for s in range(ROW_SPLITS):
        rs = slice(s * sub, (s + 1) * sub)
        x1, hp, slab, slab_t, cnt = _post(cat_scr[rs, :], x_ref[0, rs, :], wout_ref[...], gffn_ref[...],
                                          wr_ref[...], br_ref[...], tril_ref[...], cnt)
        x1_ref[rs, :] = x1
        h2_ref[rs, :] = hp
        slab_ref[rs, :] = slab
        slabt_ref[:, rs] = slab_t
    cnt_scr[...] = cnt
    cnt_ref[...] = cnt


def _sample_kernel(sink_ref, x_ref, ck_ref, cv_ref, gmix_ref, win_ref, gq_ref, gk_ref, gvg_ref, blk_ref, ws_ref,
                   bs_ref, wout_ref, gffn_ref, wr_ref, br_ref, tril_ref,
                   x1_ref, h2_ref, slab_ref, slabt_ref, kn_ref, v_ref, gvn_ref, cnt_ref,
                   ck2_scr, cv2_scr, k2_scr, v2_scr, qn_scr, ua_scr, cat_scr, *, n_seq, seq_len):
    x = x_ref[...]
    qn, kn, v, ua, gvn = _project(x, gmix_ref[...], win_ref[...], gq_ref[...], gk_ref[...], gvg_ref[...],
                                  blk_ref[...])
    kn_ref[...] = kn
    v_ref[...] = v
    gvn_ref[...] = gvn
    qn_scr[...] = qn
    ua_scr[...] = ua
    for scr, val in ((k2_scr, kn), (v2_scr, v), (ck2_scr, ck_ref[...]), (cv2_scr, cv_ref[...])):
        a0, a1 = _dup_halves(val)
        scr[0] = a0
        scr[1] = a1

    ws = [_causal_ws(ws_ref, h, seq_len) for h in range(GATE_HEADS)]

    def seq_body(i, carry):
        r0 = pl.multiple_of(i * seq_len, seq_len)
        c0r = pl.multiple_of(i * WINDOW, WINDOW)
        for g in range(N_KV_HEADS):
            c0 = g * 2 * LANES
            qa = qn_scr[pl.ds(r0, seq_len), c0:c0 + LANES]
            qb = qn_scr[pl.ds(r0, seq_len), c0 + LANES:c0 + 2 * LANES]
            k2 = jnp.concatenate([ck2_scr[g, pl.ds(c0r, WINDOW), :], k2_scr[g, pl.ds(r0, seq_len), :]], axis=0)
            v2 = jnp.concatenate([cv2_scr[g, pl.ds(c0r, WINDOW), :], v2_scr[g, pl.ds(r0, seq_len), :]], axis=0)
            pa, pb = _attn_block(qa, qb, k2, v2, None, _sink_rows(sink_ref, g, seq_len))
            cat_scr[pl.ds(r0, seq_len), c0:c0 + LANES] = pa.astype(jnp.bfloat16)
            cat_scr[pl.ds(r0, seq_len), c0 + LANES:c0 + 2 * LANES] = pb.astype(jnp.bfloat16)
        for h in range(GATE_HEADS):
            cs = slice(h * GATE_DIM, (h + 1) * GATE_DIM)
            gv_h = gvn_ref[pl.ds(r0, seq_len), cs].astype(jnp.bfloat16)
            mix = jnp.dot(ws[h], gv_h, preferred_element_type=jnp.float32) + bs_ref[0:seq_len, h:h + 1]
            cat_scr[pl.ds(r0, seq_len), Q_WIDTH + h * GATE_DIM:Q_WIDTH + (h + 1) * GATE_DIM] = (
                ua_scr[pl.ds(r0, seq_len), cs] * mix).astype(jnp.bfloat16)
        return carry

    lax.fori_loop(0, n_seq, seq_body, 0)

    cnt0 = jnp.zeros((N_EXPERTS, LANES), jnp.float32)
    x1, hp, slab, slab_t, new_cnt = _post(cat_scr[...], x, wout_ref[...], gffn_ref[...], wr_ref[...],
                                          br_ref[...], tril_ref[...], cnt0)
    x1_ref[...] = x1
    h2_ref[...] = hp
    slab_ref[...] = slab
    slabt_ref[...] = slab_t
    cnt_ref[...] = new_cnt


def _expert_kernel(ts_ref, te_ref, nv_ref, nu_ref, xa_hbm, xb_hbm, wg_ref, wu_ref, wd_ref, oa_hbm, ob_hbm,
                   wg_s, wu_s, wd_s, xa_buf, xb_buf, oa_buf, ob_buf, in_sem, out_sem):
    e = pl.program_id(0)
    tm = xa_buf.shape[1]
    n_used = nu_ref[0]

    def rows_of(g):
        return pl.ds(pl.multiple_of(g * tm, tm), tm)

    def in_copies(g, slot):
        return (pltpu.make_async_copy(xa_hbm.at[rows_of(g)], xa_buf.at[slot], in_sem.at[0, slot]),
                pltpu.make_async_copy(xb_hbm.at[rows_of(g)], xb_buf.at[slot], in_sem.at[1, slot]))

    def out_copies(g, slot):
        return (pltpu.make_async_copy(oa_buf.at[slot], oa_hbm.at[rows_of(g)], out_sem.at[0, slot]),
                pltpu.make_async_copy(ob_buf.at[slot], ob_hbm.at[rows_of(g)], out_sem.at[1, slot]))

    @pl.when((e == 0) & (n_used > 0))
    def _():
        for c in in_copies(0, 0):
            c.start()

    g_lo = ts_ref[e]
    g_hi = te_ref[e]

    @pl.when(g_hi > g_lo)
    def _():
        wg_s[...] = wg_ref[0].astype(jnp.bfloat16)
        wu_s[...] = wu_ref[0].astype(jnp.bfloat16)
        wd_s[...] = wd_ref[0].astype(jnp.bfloat16)

    def tile_body(g, carry):
        slot = g % 2
        for c in in_copies(g, slot):
            c.wait()

        @pl.when(g + 1 < n_used)
        def _():
            for c in in_copies(g + 1, 1 - slot):
                c.start()

        live = lax.broadcasted_iota(jnp.int32, (tm, xa_buf.shape[2]), 0) < nv_ref[g]
        ha, la = _unpack_halves(jnp.where(live, xa_buf[slot], jnp.uint32(0)))
        hb, lb = _unpack_halves(jnp.where(live, xb_buf[slot], jnp.uint32(0)))
        xs = jnp.concatenate([ha, hb, la, lb], axis=1).astype(jnp.bfloat16)
        hg = jnp.dot(xs, wg_s[...], preferred_element_type=jnp.float32)
        hu = jnp.dot(xs, wu_s[...], preferred_element_type=jnp.float32)
        a = (jax.nn.silu(hg) * hu).astype(jnp.bfloat16)
        packed = _pack_halves(jnp.dot(a, wd_s[...], preferred_element_type=jnp.float32))
        q = packed.shape[1] // 2

        @pl.when(g >= 2)
        def _():
            for c in out_copies(g - 2, slot):
                c.wait()

        oa_buf[slot] = packed[:, :q]
        ob_buf[slot] = packed[:, q:]
        for c in out_copies(g, slot):
            c.start()
        return carry

    lax.fori_loop(g_lo, g_hi, tile_body, 0)

    @pl.when(e == pl.num_programs(0) - 1)
    def _():
        for back in (1, 2):
            @pl.when(n_used >= back)
            def _():
                g = n_used - back
                for c in out_copies(g, g % 2):
                    c.wait()


def _combine_kernel(x1_ref, o1a_ref, o1b_ref, o2a_ref, o2b_ref, slab_ref, *rest):
    y_ref = rest[-1]
    slab = slab_ref[...]
    g1 = slab[:, 4:5]
    g2 = slab[:, 5:6]
    q = o1a_ref.shape[1]
    for c, (r1, r2) in enumerate(((o1a_ref, o2a_ref), (o1b_ref, o2b_ref))):
        h1, l1 = _unpack_halves(r1[...])
        h2, l2 = _unpack_halves(r2[...])
        hs = slice(c * q, (c + 1) * q)
        ls = slice(2 * q + c * q, 2 * q + (c + 1) * q)
        y_ref[:, hs] = x1_ref[:, hs] + (g1 * h1 + g2 * h2)
        y_ref[:, ls] = x1_ref[:, ls] + (g1 * l1 + g2 * l2)


def _sc_mesh():
    return plsc.VectorSubcoreMesh(core_axis_name="core", subcore_axis_name="subcore")


def _sc_scatter_rows(parts, pos0, pos1, n_rows):
    n_piece = parts[0].shape[1] // SC_COLS

    def sc_kernel(*refs):
        x_refs = refs[:len(parts)]
        i0_hbm, i1_hbm = refs[len(parts):len(parts) + 2]
        o_refs = refs[len(parts) + 2:]
        tok0 = 0
        for x_hbm, arr in zip(x_refs, parts):
            blk0 = tok0 // SC_WINDOW
            for c in range(n_piece):
                def body(x_vmem, i0_vmem, i1_vmem, o_hbm=o_refs[c]):
                    pltpu.sync_copy(x_vmem, o_hbm.at[i0_vmem.at[0]])
                    pltpu.sync_copy(x_vmem, o_hbm.at[i1_vmem.at[0]])

                pltpu.emit_pipeline(
                    body,
                    grid=(arr.shape[0] // SC_WINDOW,),
                    in_specs=[pl.BlockSpec((SC_WINDOW, SC_COLS), lambda i, c=c: (i, c)),
                              pl.BlockSpec((1, SC_WINDOW), lambda i, blk0=blk0: (0, blk0 + i)),
                              pl.BlockSpec((1, SC_WINDOW), lambda i, blk0=blk0: (0, blk0 + i))],
                    out_specs=[],
                    core_axis_name=("core", "subcore"),
                    dimension_semantics=(pltpu.PARALLEL,),
                )(x_hbm, i0_hbm, i1_hbm)
            tok0 += arr.shape[0]

    piece = jax.ShapeDtypeStruct((n_rows, SC_COLS), jnp.uint32)
    return pl.kernel(sc_kernel, out_type=(piece,) * n_piece, mesh=_sc_mesh(), name="scatter_rows")(
        *parts, pos0, pos1)


def _sc_gather_rows(pieces, pos_list):
    n_tok = pos_list[0].shape[1]

    def sc_kernel(*refs):
        s_refs = refs[:len(pieces)]
        i_refs = refs[len(pieces):len(pieces) + len(pos_list)]
        o_refs = refs[len(pieces) + len(pos_list):]
        k = 0
        for i_hbm in i_refs:
            for s_hbm in s_refs:
                def body(i_vmem, o_vmem, s_hbm=s_hbm):
                    pltpu.sync_copy(s_hbm.at[i_vmem.at[0]], o_vmem)

                pltpu.emit_pipeline(
                    body,
                    grid=(n_tok // SC_WINDOW,),
                    in_specs=[pl.BlockSpec((1, SC_WINDOW), lambda i: (0, i))],
                    out_specs=[pl.BlockSpec((SC_WINDOW, SC_COLS), lambda i: (i, 0))],
                    core_axis_name=("core", "subcore"),
                    dimension_semantics=(pltpu.PARALLEL,),
                )(i_hbm, o_refs[k])
                k += 1

    out = jax.ShapeDtypeStruct((n_tok, SC_COLS), jnp.uint32)
    outs = pl.kernel(sc_kernel, out_type=(out,) * (len(pieces) * len(pos_list)), mesh=_sc_mesh(),
                     name="gather_rows")(*pieces, *pos_list)
    return [outs[j * len(pieces):(j + 1) * len(pieces)] for j in range(len(pos_list))]


def _const_spec(shape):
    nd = len(shape)
    return pl.BlockSpec(shape, lambda *_: (0,) * nd)


def kernel(x_prompt, x_sample, cache_attn_k, cache_attn_v, g_mix, w_in, g_q, g_k, g_v, attn_sinks, w_s, b_s,
           w_out, g_ffn, w_coarse, b_coarse, w_fine, b_fine, w_gate, w_up, w_down):
    nb, seq, d = x_prompt.shape
    ns, slen, _ = x_sample.shape
    n_p = nb * seq
    n_s = ns * slen
    n_tok = n_p + n_s
    tt = TOKEN_TILE
    assert seq % tt == 0 and n_s % tt == 0 and d == D_MODEL
    nt = seq // tt
    bf = jnp.bfloat16
    f32 = jnp.float32

    l = 0
    gmix = g_mix[l].reshape(1, d)
    win = w_in[l].astype(bf)
    gq = (jnp.tile(g_q[l], N_Q_HEADS) * (HEAD_DIM ** -0.5)).reshape(1, Q_WIDTH)
    gk = jnp.tile(g_k[l], N_KV_HEADS).reshape(1, KV_WIDTH)
    gvg = g_v[l].reshape(1, GATE_WIDTH)
    sinks = attn_sinks[l].reshape(N_Q_HEADS).astype(f32)
    ws = w_s[l].astype(bf)
    bs = b_s[l].T
    wout = w_out[l].astype(bf)
    gffn = g_ffn[l].reshape(1, d)
    wr = jnp.concatenate([w_coarse[l], jnp.transpose(w_fine[l], (1, 0, 2)).reshape(d, N_EXPERTS),
                          jnp.zeros((d, ROUTE_LANES - N_GROUPS - N_EXPERTS), f32)], axis=1)
    wr = wr.astype(bf).T
    br = jnp.concatenate([b_coarse[l], b_fine[l].reshape(-1),
                          jnp.zeros((ROUTE_LANES - N_GROUPS - N_EXPERTS,), f32)])
    br = jnp.broadcast_to(br[:, None], (ROUTE_LANES, LANES))
    cnt_shape = (N_EXPERTS, LANES)
    ii = jnp.arange(LANES)
    blk = (ii[:, None] // HEAD_DIM == ii[None, :] // HEAD_DIM).astype(bf)
    u32 = jnp.uint32
    dh = d // 2

    def strict_tril(n):
        r = jnp.arange(n)
        return (r[:, None] < r[None, :]).astype(bf)

    weight_args = (gmix, win, gq, gk, gvg, blk, ws, bs, wout, gffn, wr, br)
    weight_specs = [_const_spec(a.shape) for a in weight_args]
    smem_spec = pl.BlockSpec(memory_space=pltpu.SMEM)

    xs2 = x_sample.reshape(n_s, d)
    ck = cache_attn_k[l].reshape(ns * WINDOW, KV_WIDTH)
    cv = cache_attn_v[l].reshape(ns * WINDOW, KV_WIDTH)
    tok_out = lambda n, w, dt: jax.ShapeDtypeStruct((n, w), dt)
    x1_s, h2_s, slab_s, slabt_s, kn_s, v_s, gvn_s, cnt_s = pl.pallas_call(
        functools.partial(_sample_kernel, n_seq=ns, seq_len=slen),
        grid=(1,),
        in_specs=[smem_spec, _const_spec((n_s, d)), _const_spec(ck.shape), _const_spec(cv.shape)]
                 + weight_specs + [_const_spec((n_s, n_s))],
        out_specs=[_const_spec((n_s, d)), _const_spec((n_s, dh)), _const_spec((n_s, ROUTE_LANES)),
                   _const_spec((EXP_PER_GROUP, n_s)),
                   _const_spec((n_s, KV_WIDTH)), _const_spec((n_s, KV_WIDTH)), _const_spec((n_s, GATE_WIDTH)),
                   _const_spec(cnt_shape)],
        out_shape=[tok_out(n_s, d, f32), tok_out(n_s, dh, u32), tok_out(n_s, ROUTE_LANES, f32),
                   jax.ShapeDtypeStruct((EXP_PER_GROUP, n_s), f32),
                   jax.ShapeDtypeStruct((n_s, KV_WIDTH), f32), jax.ShapeDtypeStruct((n_s, KV_WIDTH), f32),
                   jax.ShapeDtypeStruct((n_s, GATE_WIDTH), f32), jax.ShapeDtypeStruct(cnt_shape, f32)],
        scratch_shapes=[pltpu.VMEM((N_KV_HEADS, ns * WINDOW, LANES), bf), pltpu.VMEM((N_KV_HEADS, ns * WINDOW, LANES), bf),
                        pltpu.VMEM((N_KV_HEADS, n_s, LANES), bf), pltpu.VMEM((N_KV_HEADS, n_s, LANES), bf),
                        pltpu.VMEM((n_s, Q_WIDTH), bf), pltpu.VMEM((n_s, GATE_WIDTH), f32),
                        pltpu.VMEM((n_s, d), bf)],
        compiler_params=pltpu.CompilerParams(dimension_semantics=("arbitrary",), vmem_limit_bytes=VMEM_LIMIT),
        name="mixer_sample",
    )(sinks, xs2, ck, cv, *weight_args, strict_tril(n_s))

    x1_p, h2_p, slab_p, slabt_p, kwin, vwin, cnt = pl.pallas_call(
        _prompt_kernel,
        grid=(nb, nt),
        in_specs=[smem_spec, pl.BlockSpec((1, tt, d), lambda b, t: (b, t, 0))] + weight_specs
                 + [_const_spec((tt // ROW_SPLITS, tt // ROW_SPLITS)), _const_spec(cnt_shape)],
        out_specs=[pl.BlockSpec((tt, d), lambda b, t: (b * nt + t, 0)),
                   pl.BlockSpec((tt, dh), lambda b, t: (b * nt + t, 0)),
                   pl.BlockSpec((tt, ROUTE_LANES), lambda b, t: (b * nt + t, 0)),
                   pl.BlockSpec((EXP_PER_GROUP, tt), lambda b, t: (0, b * nt + t)),
                   pl.BlockSpec((1, WINDOW, KV_WIDTH), lambda b, t: (b, 0, 0)),
                   pl.BlockSpec((1, WINDOW, KV_WIDTH), lambda b, t: (b, 0, 0)),
                   _const_spec(cnt_shape)],
        out_shape=[tok_out(n_p, d, f32), tok_out(n_p, dh, u32), tok_out(n_p, ROUTE_LANES, f32),
                   jax.ShapeDtypeStruct((EXP_PER_GROUP, n_p), f32),
                   jax.ShapeDtypeStruct((nb, WINDOW, KV_WIDTH), f32),
                   jax.ShapeDtypeStruct((nb, WINDOW, KV_WIDTH), f32),
                   jax.ShapeDtypeStruct(cnt_shape, f32)],
        scratch_shapes=[pltpu.VMEM((N_KV_HEADS, tt + WINDOW, LANES), bf),
                        pltpu.VMEM((N_KV_HEADS, tt + WINDOW, LANES), bf),
                        pltpu.VMEM((tt, Q_WIDTH), bf), pltpu.VMEM((tt, GATE_WIDTH), f32),
                        pltpu.VMEM((tt, GATE_WIDTH), bf), pltpu.VMEM((tt, d), bf),
                        pltpu.VMEM(cnt_shape, f32)],
        compiler_params=pltpu.CompilerParams(dimension_semantics=("arbitrary", "arbitrary"),
                                             vmem_limit_bytes=VMEM_LIMIT),
        name="mixer_prompt",
    )(sinks, x_prompt, *weight_args, strict_tril(tt // ROW_SPLITS), cnt_s)
    slabt = jnp.concatenate([slabt_p, slabt_s], axis=1)

    tm = EXPERT_TILE
    counts = cnt[:, 0].astype(jnp.int32)
    tiles = (counts + tm - 1) // tm
    tile_end = jnp.cumsum(tiles)
    offs = (tile_end - tiles) * tm
    n_used = tile_end[-1]
    n_tiles = (2 * n_tok) // tm + N_EXPERTS
    n_rows = n_tiles * tm
    tile_ids = jnp.arange(n_tiles, dtype=jnp.int32)
    tile_expert = jnp.minimum(jnp.sum(tile_ids[:, None] >= tile_end[None, :], axis=1), N_EXPERTS - 1).astype(jnp.int32)
    tile_in_expert = tile_ids - (tile_end - tiles)[tile_expert]
    tile_rows = jnp.clip(counts[tile_expert] - tile_in_expert * tm, 0, tm).astype(jnp.int32)
    eid = slabt[0:2].astype(jnp.int32)
    rank = slabt[2:4].astype(jnp.int32)
    off_of = jnp.sum(jnp.where(eid[:, :, None] == jnp.arange(N_EXPERTS, dtype=jnp.int32), offs, 0), axis=-1)
    pos = off_of + rank
    pos0 = pos[0:1]
    pos1 = pos[1:2]

    assert dh == 2 * SC_COLS and n_p % SC_WINDOW == 0 and n_s % SC_WINDOW == 0
    xs_a, xs_b = _sc_scatter_rows([h2_p, h2_s], pos0, pos1, n_rows)

    w_map = lambda e, *_: (e, 0, 0)
    hbm = pl.BlockSpec(memory_space=pl.ANY)
    piece = jax.ShapeDtypeStruct((n_rows, SC_COLS), u32)
    tile_buf = pltpu.VMEM((2, tm, SC_COLS), u32)
    out_a, out_b = pl.pallas_call(
        _expert_kernel,
        grid_spec=pltpu.PrefetchScalarGridSpec(
            num_scalar_prefetch=4,
            grid=(N_EXPERTS,),
            in_specs=[hbm, hbm,
                      pl.BlockSpec((1, d, D_EXPERT), w_map),
                      pl.BlockSpec((1, d, D_EXPERT), w_map),
                      pl.BlockSpec((1, D_EXPERT, d), w_map)],
            out_specs=[hbm, hbm],
            scratch_shapes=[pltpu.VMEM((d, D_EXPERT), bf), pltpu.VMEM((d, D_EXPERT), bf),
                            pltpu.VMEM((D_EXPERT, d), bf),
                            tile_buf, tile_buf, tile_buf, tile_buf,
                            pltpu.SemaphoreType.DMA((2, 2)), pltpu.SemaphoreType.DMA((2, 2))]),
        out_shape=[piece, piece],
        compiler_params=pltpu.CompilerParams(dimension_semantics=("arbitrary",), vmem_limit_bytes=VMEM_LIMIT),
        name="experts",
    )((tile_end - tiles).astype(jnp.int32), tile_end.astype(jnp.int32), tile_rows,
      n_used.reshape(1).astype(jnp.int32), xs_a, xs_b, w_gate[l], w_up[l], w_down[l])

    def combine(x1, slab, pos_base, tok0, n, y_prev):
        lo = pos_base + tok0
        p0 = lax.slice(pos0, (0, lo), (1, lo + n))
        p1 = lax.slice(pos1, (0, lo), (1, lo + n))
        (o1a, o1b), (o2a, o2b) = _sc_gather_rows([out_a, out_b], [p0, p1])
        blk0 = tok0 // tt
        tok = lambda i: (blk0 + i, 0)
        loc = lambda i: (i, 0)
        args = [x1, o1a, o1b, o2a, o2b, slab]
        in_specs = ([pl.BlockSpec((tt, d), tok)] + [pl.BlockSpec((tt, SC_COLS), loc)] * 4
                    + [pl.BlockSpec((tt, ROUTE_LANES), tok)])
        aliases = {}
        if y_prev is not None:
            args.append(y_prev)
            in_specs.append(pl.BlockSpec(memory_space=pl.ANY))
            aliases = {len(args) - 1: 0}
        return pl.pallas_call(
            _combine_kernel,
            grid=(n // tt,),
            in_specs=in_specs,
            out_specs=pl.BlockSpec((tt, d), tok),
            out_shape=jax.ShapeDtypeStruct(x1.shape, f32),
            input_output_aliases=aliases,
            compiler_params=pltpu.CompilerParams(dimension_semantics=("arbitrary",)),
            name="combine",
        )(*args)

    y_p = None
    chunk = n_p // COMBINE_CHUNKS
    assert chunk % tt == 0 and chunk % SC_WINDOW == 0
    for c in range(COMBINE_CHUNKS):
        y_p = combine(x1_p, slab_p, 0, c * chunk, chunk, y_p)
    y_p = y_p.reshape(nb, seq, d)
    y_s = combine(x1_s, slab_s, n_p, 0, n_s, None).reshape(ns, slen, d)

    kv_shape = (1, nb, WINDOW, N_KV_HEADS, HEAD_DIM)
    new_k_p = kwin.reshape(kv_shape)
    new_v_p = vwin.reshape(kv_shape)
    keep = WINDOW - slen
    ck4 = cache_attn_k[l][:, WINDOW - keep:]
    cv4 = cache_attn_v[l][:, WINDOW - keep:]
    new_k_s = jnp.concatenate([ck4, kn_s.reshape(ns, slen, N_KV_HEADS, HEAD_DIM)], axis=1)[None]
    new_v_s = jnp.concatenate([cv4, v_s.reshape(ns, slen, N_KV_HEADS, HEAD_DIM)], axis=1)[None]
    new_gv_s = gvn_s.reshape(1, ns, slen, GATE_HEADS, GATE_DIM)
    return (y_p, y_s, new_k_p, new_v_p, new_k_s, new_v_s, new_gv_s)
```

```python
import functools

import jax
import jax.numpy as jnp
from jax import lax
from jax.experimental import pallas as pl
from jax.experimental.pallas import tpu as pltpu
from jax.experimental.pallas import tpu_sc as plsc

D_MODEL = 1024
HEAD_DIM = 64
N_Q_HEADS = 8
N_KV_HEADS = 2
Q_WIDTH = N_Q_HEADS * HEAD_DIM
KV_WIDTH = N_KV_HEADS * HEAD_DIM
GATE_HEADS = 4
GATE_DIM = 128
GATE_WIDTH = GATE_HEADS * GATE_DIM
PROJ_COLS = Q_WIDTH + 2 * KV_WIDTH + 2 * GATE_WIDTH
CHUNK = 64
WINDOW = 128
MLP_CHUNK = 128
N_GROUPS = 8
EXP_PER_GROUP = 8
N_EXPERTS = N_GROUPS * EXP_PER_GROUP
D_EXPERT = 512
EPS = 1e-6

LANES = 128
ROUTE_LANES = 128
FINE_LANE0 = N_GROUPS
NEG = -1e30

TOKEN_TILE = 512
ROW_SPLITS = 2
EXPERT_TILE = 256
POS_BLOCK_TILES = 64
SC_WINDOW = 128
SC_COLS = 256
VMEM_LIMIT = 48 * 1024 * 1024


def _pack_halves(x):
    w = x.shape[1] // 2
    b = lax.bitcast_convert_type(x.astype(jnp.bfloat16).astype(jnp.float32), jnp.uint32)
    return (b[:, :w] & jnp.uint32(0xFFFF0000)) | (b[:, w:] >> 16)


def _unpack_halves(p):
    hi = lax.bitcast_convert_type(p & jnp.uint32(0xFFFF0000), jnp.float32)
    lo = lax.bitcast_convert_type(p << 16, jnp.float32)
    return hi, lo


def _rms(x, eps=EPS):
    return x * lax.rsqrt(jnp.mean(x * x, axis=-1, keepdims=True) + eps)


def _project(x, gmix, win, gq, gk, gvg, blk):
    h = (_rms(x) * gmix).astype(jnp.bfloat16)
    z = jnp.dot(h, win, preferred_element_type=jnp.float32)
    qk = z[:, :Q_WIDTH + KV_WIDTH]
    outs = []
    for j in range((Q_WIDTH + KV_WIDTH) // LANES):
        zj = qk[:, j * LANES:(j + 1) * LANES]
        ss = jnp.dot((zj * zj).astype(jnp.bfloat16), blk, preferred_element_type=jnp.float32)
        outs.append(zj * lax.rsqrt(ss * (1.0 / HEAD_DIM) + EPS))
    qn = jnp.concatenate(outs[:Q_WIDTH // LANES], axis=-1) * gq
    kn = outs[-1] * gk
    v = z[:, Q_WIDTH + KV_WIDTH:Q_WIDTH + 2 * KV_WIDTH]
    u0 = Q_WIDTH + 2 * KV_WIDTH
    ua = jax.nn.gelu(z[:, u0:u0 + GATE_WIDTH])
    ga = jax.nn.gelu(z[:, u0 + GATE_WIDTH:])
    gvn = jnp.concatenate(
        [_rms(ga[:, i * GATE_DIM:(i + 1) * GATE_DIM]) for i in range(GATE_HEADS)], axis=-1) * gvg
    return qn.astype(jnp.bfloat16), kn, v, ua, gvn


def _dup_halves(a):
    lo = lax.broadcasted_iota(jnp.int32, a.shape, 1) < HEAD_DIM
    r = pltpu.roll(a, HEAD_DIM, axis=1)
    return (jnp.where(lo, a, r).astype(jnp.bfloat16), jnp.where(lo, r, a).astype(jnp.bfloat16))


def _attn_block(qa, qb, k2, v2, bias, sk):
    r = qa.shape[0]
    lo = lax.broadcasted_iota(jnp.int32, (r, LANES), 1) < HEAD_DIM
    zero = jnp.zeros_like(qa)
    qs = jnp.concatenate([jnp.where(lo, qa, zero), jnp.where(lo, zero, qa),
                          jnp.where(lo, qb, zero), jnp.where(lo, zero, qb)], axis=0)
    s = lax.dot_general(qs, k2, (((1,), (1,)), ((), ())), preferred_element_type=jnp.float32)
    if bias is not None:
        s = s + bias
    m = jnp.maximum(jnp.max(s, axis=-1, keepdims=True), sk)
    e = jnp.exp(s - m)
    den = jnp.sum(e, axis=-1, keepdims=True) + jnp.exp(sk - m)
    o = jnp.dot(e.astype(jnp.bfloat16), v2, preferred_element_type=jnp.float32) * (1.0 / den)
    pa = jnp.where(lo, o[0:r], o[r:2 * r])
    pb = jnp.where(lo, o[2 * r:3 * r], o[3 * r:4 * r])
    return pa, pb


def _sink_rows(sink_ref, g, r):
    row = lax.broadcasted_iota(jnp.int32, (4 * r, 1), 0)
    s0, s1, s2, s3 = (sink_ref[4 * g + i] for i in range(4))
    return jnp.where(row < r, s0, jnp.where(row < 2 * r, s1, jnp.where(row < 3 * r, s2, s3)))


def _causal_ws(ws_ref, h, n):
    w = ws_ref[h][:n, :n]
    keep = (lax.broadcasted_iota(jnp.int32, (n, n), 0) >= lax.broadcasted_iota(jnp.int32, (n, n), 1))
    return jnp.where(keep, w, jnp.zeros_like(w))


def _post(cat, x, wout, gffn, wrt, brt, triu, cnt):
    x1 = x + jnp.dot(cat, wout, preferred_element_type=jnp.float32)
    h2 = _rms(x1) * gffn
    t = x.shape[0]
    reps = t // LANES
    lgt = lax.dot_general(wrt, h2.astype(jnp.bfloat16), (((1,), (1,)), ((), ())),
                          preferred_element_type=jnp.float32) + jnp.tile(brt, (1, reps))
    ng = float(EXP_PER_GROUP)
    sub = lax.broadcasted_iota(jnp.int32, (EXP_PER_GROUP, t), 0).astype(jnp.float32)
    c = lgt[0:N_GROUPS]
    mc = jnp.max(c, axis=0, keepdims=True)
    grp = jnp.min(jnp.where(c == mc, sub, ng), axis=0, keepdims=True)
    pg = 1.0 / jnp.sum(jnp.exp(c - mc), axis=0, keepdims=True)
    lf = lgt[FINE_LANE0:FINE_LANE0 + EXP_PER_GROUP]
    for g in range(1, N_GROUPS):
        r0 = FINE_LANE0 + g * EXP_PER_GROUP
        lf = jnp.where(grp == float(g), lgt[r0:r0 + EXP_PER_GROUP], lf)
    v1 = jnp.max(lf, axis=0, keepdims=True)
    i1 = jnp.min(jnp.where(lf == v1, sub, ng), axis=0, keepdims=True)
    lf2 = jnp.where(sub == i1, -jnp.inf, lf)
    v2 = jnp.max(lf2, axis=0, keepdims=True)
    i2 = jnp.min(jnp.where(lf2 == v2, sub, ng), axis=0, keepdims=True)
    tt = jnp.exp(v2 - v1)
    w1 = 1.0 / (1.0 + tt)
    w2 = tt * w1
    e1 = grp * ng + i1
    e2 = grp * ng + i2
    row = lax.broadcasted_iota(jnp.int32, (N_EXPERTS, t), 0).astype(jnp.float32)
    sel1 = row == e1
    sel2 = row == e2
    oh = jnp.where(sel1 | sel2, 1.0, 0.0)
    cum = jnp.dot(oh.astype(jnp.bfloat16), triu, preferred_element_type=jnp.float32) + jnp.tile(cnt, (1, reps))
    r1 = jnp.sum(jnp.where(sel1, cum, 0.0), axis=0, keepdims=True)
    r2 = jnp.sum(jnp.where(sel2, cum, 0.0), axis=0, keepdims=True)
    new_cnt = cnt + jnp.sum(oh, axis=1, keepdims=True)
    slab_t = jnp.where(sub == 0.0, e1,
             jnp.where(sub == 1.0, e2,
             jnp.where(sub == 2.0, r1,
             jnp.where(sub == 3.0, r2,
             jnp.where(sub == 4.0, pg * w1,
             jnp.where(sub == 5.0, pg * w2, 0.0))))))
    slab = jnp.concatenate([slab_t, jnp.zeros((ROUTE_LANES - EXP_PER_GROUP, t), jnp.float32)], axis=0).T
    return x1, _pack_halves(h2), slab, slab_t, new_cnt


def _prompt_kernel(sink_ref, x_ref, gmix_ref, win_ref, gq_ref, gk_ref, gvg_ref, blk_ref, ws_ref, bs_ref,
                   wout_ref, gffn_ref, wr_ref, br_ref, tril_ref, cnt_in_ref,
                   x1_ref, h2_ref, slab_ref, slabt_ref, kwin_ref, vwin_ref, cnt_ref,
                   k2_scr, v2_scr, qn_scr, ua_scr, gvn_scr, cat_scr, cnt_scr):
    b = pl.program_id(0)
    t = pl.program_id(1)
    nt = pl.num_programs(1)
    tt = x_ref.shape[1]
    nblk = tt // WINDOW
    sub = tt // ROW_SPLITS

    @pl.when((b == 0) & (t == 0))
    def _():
        cnt_scr[...] = cnt_in_ref[...]

    @pl.when(t == 0)
    def _():
        k2_scr[:, 0:WINDOW, :] = jnp.zeros((N_KV_HEADS, WINDOW, LANES), jnp.bfloat16)
        v2_scr[:, 0:WINDOW, :] = jnp.zeros((N_KV_HEADS, WINDOW, LANES), jnp.bfloat16)

    for s in range(ROW_SPLITS):
        rs = slice(s * sub, (s + 1) * sub)
        ks = slice(WINDOW + s * sub, WINDOW + (s + 1) * sub)
        qn, kn, v, ua, gvn = _project(x_ref[0, rs, :], gmix_ref[...], win_ref[...], gq_ref[...], gk_ref[...],
                                      gvg_ref[...], blk_ref[...])
        qn_scr[rs, :] = qn
        ua_scr[rs, :] = ua
        gvn_scr[rs, :] = gvn.astype(jnp.bfloat16)
        k0, k1 = _dup_halves(kn)
        v0, v1 = _dup_halves(v)
        k2_scr[0, ks, :] = k0
        k2_scr[1, ks, :] = k1
        v2_scr[0, ks, :] = v0
        v2_scr[1, ks, :] = v1
        if s == ROW_SPLITS - 1:
            @pl.when(t == nt - 1)
            def _():
                kwin_ref[0] = kn[sub - WINDOW:]
                vwin_ref[0] = v[sub - WINDOW:]

    rows = 4 * WINDOW
    band = 2 * WINDOW
    rr = lax.broadcasted_iota(jnp.int32, (rows, band), 0)
    kk = lax.broadcasted_iota(jnp.int32, (rows, band), 1)
    half = ((rr % WINDOW) >= CHUNK).astype(jnp.int32)
    allowed = (kk >= half * CHUNK) & (kk < (3 + half) * CHUNK)
    bias_mid = jnp.where(allowed, 0.0, NEG)
    bias_first = jnp.where(allowed & (kk >= WINDOW), 0.0, NEG)

    def attn_body(j, carry):
        r0 = pl.multiple_of(j * WINDOW, WINDOW)
        bias = jnp.where((t == 0) & (j == 0), bias_first, bias_mid)
        for g in range(N_KV_HEADS):
            c0 = g * 2 * LANES
            qa = qn_scr[pl.ds(r0, WINDOW), c0:c0 + LANES]
            qb = qn_scr[pl.ds(r0, WINDOW), c0 + LANES:c0 + 2 * LANES]
            k2 = k2_scr[g, pl.ds(r0, band), :]
            v2 = v2_scr[g, pl.ds(r0, band), :]
            pa, pb = _attn_block(qa, qb, k2, v2, bias, _sink_rows(sink_ref, g, WINDOW))
            cat_scr[pl.ds(r0, WINDOW), c0:c0 + LANES] = pa.astype(jnp.bfloat16)
            cat_scr[pl.ds(r0, WINDOW), c0 + LANES:c0 + 2 * LANES] = pb.astype(jnp.bfloat16)
        return carry

    lax.fori_loop(0, nblk, attn_body, 0, unroll=True)

    k2_scr[:, 0:WINDOW, :] = k2_scr[:, tt:tt + WINDOW, :]
    v2_scr[:, 0:WINDOW, :] = v2_scr[:, tt:tt + WINDOW, :]

    for h in range(GATE_HEADS):
        w = _causal_ws(ws_ref, h, MLP_CHUNK)
        bcol = bs_ref[:, h:h + 1]
        for c in range(tt // MLP_CHUNK):
            rs = slice(c * MLP_CHUNK, (c + 1) * MLP_CHUNK)
            cs = slice(h * GATE_DIM, (h + 1) * GATE_DIM)
            mix = jnp.dot(w, gvn_scr[rs, cs], preferred_element_type=jnp.float32) + bcol
            cat_scr[rs, Q_WIDTH + h * GATE_DIM:Q_WIDTH + (h + 1) * GATE_DIM] = (ua_scr[rs, cs] * mix).astype(jnp.bfloat16)

    cnt = cnt_scr[...]
    for s in range(ROW_SPLITS):
        rs = slice(s * sub, (s + 1) * sub)
        x1, hp, slab, slab_t, cnt = _post(cat_scr[rs, :], x_ref[0, rs, :], wout_ref[...], gffn_ref[...],
                                          wr_ref[...], br_ref[...], tril_ref[...], cnt)
        x1_ref[rs, :] = x1
        h2_ref[rs, :] = hp
        slab_ref[rs, :] = slab
        slabt_ref[:, rs] = slab_t
    cnt_scr[...] = cnt
    cnt_ref[...] = cnt


def _sample_kernel(sink_ref, x_ref, ck_ref, cv_ref, gmix_ref, win_ref, gq_ref, gk_ref, gvg_ref, blk_ref, ws_ref,
                   bs_ref, wout_ref, gffn_ref, wr_ref, br_ref, tril_ref,
                   x1_ref, h2_ref, slab_ref, slabt_ref, kn_ref, v_ref, gvn_ref, cnt_ref,
                   ck2_scr, cv2_scr, k2_scr, v2_scr, qn_scr, ua_scr, cat_scr, *, n_seq, seq_len):
    x = x_ref[...]
    qn, kn, v, ua, gvn = _project(x, gmix_ref[...], win_ref[...], gq_ref[...], gk_ref[...], gvg_ref[...],
                                  blk_ref[...])
    kn_ref[...] = kn
    v_ref[...] = v
    gvn_ref[...] = gvn
    qn_scr[...] = qn
    ua_scr[...] = ua
    for scr, val in ((k2_scr, kn), (v2_scr, v), (ck2_scr, ck_ref[...]), (cv2_scr, cv_ref[...])):
        a0, a1 = _dup_halves(val)
        scr[0] = a0
        scr[1] = a1

    ws = [_causal_ws(ws_ref, h, seq_len) for h in range(GATE_HEADS)]

    def seq_body(i, carry):
        r0 = pl.multiple_of(i * seq_len, seq_len)
        c0r = pl.multiple_of(i * WINDOW, WINDOW)
        for g in range(N_KV_HEADS):
            c0 = g * 2 * LANES
            qa = qn_scr[pl.ds(r0, seq_len), c0:c0 + LANES]
            qb = qn_scr[pl.ds(r0, seq_len), c0 + LANES:c0 + 2 * LANES]
            k2 = jnp.concatenate([ck2_scr[g, pl.ds(c0r, WINDOW), :], k2_scr[g, pl.ds(r0, seq_len), :]], axis=0)
            v2 = jnp.concatenate([cv2_scr[g, pl.ds(c0r, WINDOW), :], v2_scr[g, pl.ds(r0, seq_len), :]], axis=0)
            pa, pb = _attn_block(qa, qb, k2, v2, None, _sink_rows(sink_ref, g, seq_len))
            cat_scr[pl.ds(r0, seq_len), c0:c0 + LANES] = pa.astype(jnp.bfloat16)
            cat_scr[pl.ds(r0, seq_len), c0 + LANES:c0 + 2 * LANES] = pb.astype(jnp.bfloat16)
        for h in range(GATE_HEADS):
            cs = slice(h * GATE_DIM, (h + 1) * GATE_DIM)
            gv_h = gvn_ref[pl.ds(r0, seq_len), cs].astype(jnp.bfloat16)
            mix = jnp.dot(ws[h], gv_h, preferred_element_type=jnp.float32) + bs_ref[0:seq_len, h:h + 1]
            cat_scr[pl.ds(r0, seq_len), Q_WIDTH + h * GATE_DIM:Q_WIDTH + (h + 1) * GATE_DIM] = (
                ua_scr[pl.ds(r0, seq_len), cs] * mix).astype(jnp.bfloat16)
        return carry

    lax.fori_loop(0, n_seq, seq_body, 0)

    cnt0 = jnp.zeros((N_EXPERTS, LANES), jnp.float32)
    x1, hp, slab, slab_t, new_cnt = _post(cat_scr[...], x, wout_ref[...], gffn_ref[...], wr_ref[...],
                                          br_ref[...], tril_ref[...], cnt0)
    x1_ref[...] = x1
    h2_ref[...] = hp
    slab_ref[...] = slab
    slabt_ref[...] = slab_t
    cnt_ref[...] = new_cnt


def _expert_kernel(ts_ref, te_ref, nv_ref, nu_ref, xa_hbm, xb_hbm, wg_ref, wu_ref, wd_ref, oa_hbm, ob_hbm,
                   wg_s, wu_s, wd_s, xa_buf, xb_buf, oa_buf, ob_buf, in_sem, out_sem):
    e = pl.program_id(0)
    tm = xa_buf.shape[1]
    n_used = nu_ref[0]

    def rows_of(g):
        return pl.ds(pl.multiple_of(g * tm, tm), tm)

    def in_copies(g, slot):
        return (pltpu.make_async_copy(xa_hbm.at[rows_of(g)], xa_buf.at[slot], in_sem.at[0, slot]),
                pltpu.make_async_copy(xb_hbm.at[rows_of(g)], xb_buf.at[slot], in_sem.at[1, slot]))

    def out_copies(g, slot):
        return (pltpu.make_async_copy(oa_buf.at[slot], oa_hbm.at[rows_of(g)], out_sem.at[0, slot]),
                pltpu.make_async_copy(ob_buf.at[slot], ob_hbm.at[rows_of(g)], out_sem.at[1, slot]))

    @pl.when((e == 0) & (n_used > 0))
    def _():
        for c in in_copies(0, 0):
            c.start()

    g_lo = ts_ref[e]
    g_hi = te_ref[e]

    @pl.when(g_hi > g_lo)
    def _():
        wg_s[...] = wg_ref[0].astype(jnp.bfloat16)
        wu_s[...] = wu_ref[0].astype(jnp.bfloat16)
        wd_s[...] = wd_ref[0].astype(jnp.bfloat16)

    def tile_body(g, carry):
        slot = g % 2
        for c in in_copies(g, slot):
            c.wait()

        @pl.when(g + 1 < n_used)
        def _():
            for c in in_copies(g + 1, 1 - slot):
                c.start()

        live = lax.broadcasted_iota(jnp.int32, (tm, xa_buf.shape[2]), 0) < nv_ref[g]
        ha, la = _unpack_halves(jnp.where(live, xa_buf[slot], jnp.uint32(0)))
        hb, lb = _unpack_halves(jnp.where(live, xb_buf[slot], jnp.uint32(0)))
        xs = jnp.concatenate([ha, hb, la, lb], axis=1).astype(jnp.bfloat16)
        hg = jnp.dot(xs, wg_s[...], preferred_element_type=jnp.float32)
        hu = jnp.dot(xs, wu_s[...], preferred_element_type=jnp.float32)
        a = (jax.nn.silu(hg) * hu).astype(jnp.bfloat16)
        packed = _pack_halves(jnp.dot(a, wd_s[...], preferred_element_type=jnp.float32))
        q = packed.shape[1] // 2

        @pl.when(g >= 2)
        def _():
            for c in out_copies(g - 2, slot):
                c.wait()

        oa_buf[slot] = packed[:, :q]
        ob_buf[slot] = packed[:, q:]
        for c in out_copies(g, slot):
            c.start()
        return carry

    lax.fori_loop(g_lo, g_hi, tile_body, 0)

    @pl.when(e == pl.num_programs(0) - 1)
    def _():
        for back in (1, 2):
            @pl.when(n_used >= back)
            def _():
                g = n_used - back
                for c in out_copies(g, g % 2):
                    c.wait()


def _pos_kernel(slabt_ref, offs_ref, pos0_ref, pos1_ref):
    st = slabt_ref[...]
    t = st.shape[1]
    row = lax.broadcasted_iota(jnp.int32, (N_EXPERTS, t), 0).astype(jnp.float32)
    offs = jnp.tile(offs_ref[...], (1, t // LANES))
    for s, out in enumerate((pos0_ref, pos1_ref)):
        first = jnp.sum(jnp.where(row == st[s:s + 1], offs, 0.0), axis=0, keepdims=True)
        out[...] = (first + st[2 + s:3 + s]).astype(jnp.int32)


def _combine_kernel(x1_ref, o1a_ref, o1b_ref, o2a_ref, o2b_ref, slab_ref, y_ref):
    slab = slab_ref[...]
    g1 = slab[:, 4:5]
    g2 = slab[:, 5:6]
    q = o1a_ref.shape[1]
    for c, (r1, r2) in enumerate(((o1a_ref, o2a_ref), (o1b_ref, o2b_ref))):
        h1, l1 = _unpack_halves(r1[...])
        h2, l2 = _unpack_halves(r2[...])
        hs = slice(c * q, (c + 1) * q)
        ls = slice(2 * q + c * q, 2 * q + (c + 1) * q)
        y_ref[:, hs] = x1_ref[:, hs] + (g1 * h1 + g2 * h2)
        y_ref[:, ls] = x1_ref[:, ls] + (g1 * l1 + g2 * l2)


def _sc_mesh():
    return plsc.VectorSubcoreMesh(core_axis_name="core", subcore_axis_name="subcore")


def _sc_scatter_rows(parts, pos0, pos1, n_rows):
    n_piece = parts[0].shape[1] // SC_COLS

    def sc_kernel(*refs):
        x_refs = refs[:len(parts)]
        i0_hbm, i1_hbm = refs[len(parts):len(parts) + 2]
        o_refs = refs[len(parts) + 2:]
        tok0 = 0
        for x_hbm, arr in zip(x_refs, parts):
            blk0 = tok0 // SC_WINDOW
            for c in range(n_piece):
                def body(x_vmem, i0_vmem, i1_vmem, o_hbm=o_refs[c]):
                    pltpu.sync_copy(x_vmem, o_hbm.at[i0_vmem.at[0]])
                    pltpu.sync_copy(x_vmem, o_hbm.at[i1_vmem.at[0]])

                pltpu.emit_pipeline(
                    body,
                    grid=(arr.shape[0] // SC_WINDOW,),
                    in_specs=[pl.BlockSpec((SC_WINDOW, SC_COLS), lambda i, c=c: (i, c)),
                              pl.BlockSpec((1, SC_WINDOW), lambda i, blk0=blk0: (0, blk0 + i)),
                              pl.BlockSpec((1, SC_WINDOW), lambda i, blk0=blk0: (0, blk0 + i))],
                    out_specs=[],
                    core_axis_name=("core", "subcore"),
                    dimension_semantics=(pltpu.PARALLEL,),
                )(x_hbm, i0_hbm, i1_hbm)
            tok0 += arr.shape[0]

    piece = jax.ShapeDtypeStruct((n_rows, SC_COLS), jnp.uint32)
    return pl.kernel(sc_kernel, out_type=(piece,) * n_piece, mesh=_sc_mesh(), name="scatter_rows")(
        *parts, pos0, pos1)


def _sc_gather_rows(pieces, pos_list):
    n_tok = pos_list[0].shape[1]

    def sc_kernel(*refs):
        s_refs = refs[:len(pieces)]
        i_refs = refs[len(pieces):len(pieces) + len(pos_list)]
        o_refs = refs[len(pieces) + len(pos_list):]
        k = 0
        for i_hbm in i_refs:
            for s_hbm in s_refs:
                def body(i_vmem, o_vmem, s_hbm=s_hbm):
                    pltpu.sync_copy(s_hbm.at[i_vmem.at[0]], o_vmem)

                pltpu.emit_pipeline(
                    body,
                    grid=(n_tok // SC_WINDOW,),
                    in_specs=[pl.BlockSpec((1, SC_WINDOW), lambda i: (0, i))],
                    out_specs=[pl.BlockSpec((SC_WINDOW, SC_COLS), lambda i: (i, 0))],
                    core_axis_name=("core", "subcore"),
                    dimension_semantics=(pltpu.PARALLEL,),
                )(i_hbm, o_refs[k])
                k += 1

    out = jax.ShapeDtypeStruct((n_tok, SC_COLS), jnp.uint32)
    outs = pl.kernel(sc_kernel, out_type=(out,) * (len(pieces) * len(pos_list)), mesh=_sc_mesh(),
                     name="gather_rows")(*pieces, *pos_list)
    return [outs[j * len(pieces):(j + 1) * len(pieces)] for j in range(len(pos_list))]


def _const_spec(shape):
    nd = len(shape)
    return pl.BlockSpec(shape, lambda *_: (0,) * nd)


def kernel(x_prompt, x_sample, cache_attn_k, cache_attn_v, g_mix, w_in, g_q, g_k, g_v, attn_sinks, w_s, b_s,
           w_out, g_ffn, w_coarse, b_coarse, w_fine, b_fine, w_gate, w_up, w_down):
    nb, seq, d = x_prompt.shape
    ns, slen, _ = x_sample.shape
    n_p = nb * seq
    n_s = ns * slen
    n_tok = n_p + n_s
    tt = TOKEN_TILE
    assert seq % tt == 0 and n_s % tt == 0 and d == D_MODEL
    nt = seq // tt
    bf = jnp.bfloat16
    f32 = jnp.float32

    l = 0
    gmix = g_mix[l].reshape(1, d)
    win = w_in[l].astype(bf)
    gq = (jnp.tile(g_q[l], N_Q_HEADS) * (HEAD_DIM ** -0.5)).reshape(1, Q_WIDTH)
    gk = jnp.tile(g_k[l], N_KV_HEADS).reshape(1, KV_WIDTH)
    gvg = g_v[l].reshape(1, GATE_WIDTH)
    sinks = attn_sinks[l].reshape(N_Q_HEADS).astype(f32)
    ws = w_s[l].astype(bf)
    bs = b_s[l].T
    wout = w_out[l].astype(bf)
    gffn = g_ffn[l].reshape(1, d)
    wr = jnp.concatenate([w_coarse[l], jnp.transpose(w_fine[l], (1, 0, 2)).reshape(d, N_EXPERTS),
                          jnp.zeros((d, ROUTE_LANES - N_GROUPS - N_EXPERTS), f32)], axis=1)
    wr = wr.astype(bf).T
    br = jnp.concatenate([b_coarse[l], b_fine[l].reshape(-1),
                          jnp.zeros((ROUTE_LANES - N_GROUPS - N_EXPERTS,), f32)])
    br = jnp.broadcast_to(br[:, None], (ROUTE_LANES, LANES))
    cnt_shape = (N_EXPERTS, LANES)
    ii = jnp.arange(LANES)
    blk = (ii[:, None] // HEAD_DIM == ii[None, :] // HEAD_DIM).astype(bf)
    u32 = jnp.uint32
    dh = d // 2

    def strict_tril(n):
        r = jnp.arange(n)
        return (r[:, None] < r[None, :]).astype(bf)

    weight_args = (gmix, win, gq, gk, gvg, blk, ws, bs, wout, gffn, wr, br)
    weight_specs = [_const_spec(a.shape) for a in weight_args]
    smem_spec = pl.BlockSpec(memory_space=pltpu.SMEM)

    xs2 = x_sample.reshape(n_s, d)
    ck = cache_attn_k[l].reshape(ns * WINDOW, KV_WIDTH)
    cv = cache_attn_v[l].reshape(ns * WINDOW, KV_WIDTH)
    tok_out = lambda n, w, dt: jax.ShapeDtypeStruct((n, w), dt)
    x1_s, h2_s, slab_s, slabt_s, kn_s, v_s, gvn_s, cnt_s = pl.pallas_call(
        functools.partial(_sample_kernel, n_seq=ns, seq_len=slen),
        grid=(1,),
        in_specs=[smem_spec, _const_spec((n_s, d)), _const_spec(ck.shape), _const_spec(cv.shape)]
                 + weight_specs + [_const_spec((n_s, n_s))],
        out_specs=[_const_spec((n_s, d)), _const_spec((n_s, dh)), _const_spec((n_s, ROUTE_LANES)),
                   _const_spec((EXP_PER_GROUP, n_s)),
                   _const_spec((n_s, KV_WIDTH)), _const_spec((n_s, KV_WIDTH)), _const_spec((n_s, GATE_WIDTH)),
                   _const_spec(cnt_shape)],
        out_shape=[tok_out(n_s, d, f32), tok_out(n_s, dh, u32), tok_out(n_s, ROUTE_LANES, f32),
                   jax.ShapeDtypeStruct((EXP_PER_GROUP, n_s), f32),
                   jax.ShapeDtypeStruct((n_s, KV_WIDTH), f32), jax.ShapeDtypeStruct((n_s, KV_WIDTH), f32),
                   jax.ShapeDtypeStruct((n_s, GATE_WIDTH), f32), jax.ShapeDtypeStruct(cnt_shape, f32)],
        scratch_shapes=[pltpu.VMEM((N_KV_HEADS, ns * WINDOW, LANES), bf), pltpu.VMEM((N_KV_HEADS, ns * WINDOW, LANES), bf),
                        pltpu.VMEM((N_KV_HEADS, n_s, LANES), bf), pltpu.VMEM((N_KV_HEADS, n_s, LANES), bf),
                        pltpu.VMEM((n_s, Q_WIDTH), bf), pltpu.VMEM((n_s, GATE_WIDTH), f32),
                        pltpu.VMEM((n_s, d), bf)],
        compiler_params=pltpu.CompilerParams(dimension_semantics=("arbitrary",), vmem_limit_bytes=VMEM_LIMIT),
        name="mixer_sample",
    )(sinks, xs2, ck, cv, *weight_args, strict_tril(n_s))

    x1_p, h2_p, slab_p, slabt_p, kwin, vwin, cnt = pl.pallas_call(
        _prompt_kernel,
        grid=(nb, nt),
        in_specs=[smem_spec, pl.BlockSpec((1, tt, d), lambda b, t: (b, t, 0))] + weight_specs
                 + [_const_spec((tt // ROW_SPLITS, tt // ROW_SPLITS)), _const_spec(cnt_shape)],
        out_specs=[pl.BlockSpec((tt, d), lambda b, t: (b * nt + t, 0)),
                   pl.BlockSpec((tt, dh), lambda b, t: (b * nt + t, 0)),
                   pl.BlockSpec((tt, ROUTE_LANES), lambda b, t: (b * nt + t, 0)),
                   pl.BlockSpec((EXP_PER_GROUP, tt), lambda b, t: (0, b * nt + t)),
                   pl.BlockSpec((1, WINDOW, KV_WIDTH), lambda b, t: (b, 0, 0)),
                   pl.BlockSpec((1, WINDOW, KV_WIDTH), lambda b, t: (b, 0, 0)),
                   _const_spec(cnt_shape)],
        out_shape=[tok_out(n_p, d, f32), tok_out(n_p, dh, u32), tok_out(n_p, ROUTE_LANES, f32),
                   jax.ShapeDtypeStruct((EXP_PER_GROUP, n_p), f32),
                   jax.ShapeDtypeStruct((nb, WINDOW, KV_WIDTH), f32),
                   jax.ShapeDtypeStruct((nb, WINDOW, KV_WIDTH), f32),
                   jax.ShapeDtypeStruct(cnt_shape, f32)],
        scratch_shapes=[pltpu.VMEM((N_KV_HEADS, tt + WINDOW, LANES), bf),
                        pltpu.VMEM((N_KV_HEADS, tt + WINDOW, LANES), bf),
                        pltpu.VMEM((tt, Q_WIDTH), bf), pltpu.VMEM((tt, GATE_WIDTH), f32),
                        pltpu.VMEM((tt, GATE_WIDTH), bf), pltpu.VMEM((tt, d), bf),
                        pltpu.VMEM(cnt_shape, f32)],
        compiler_params=pltpu.CompilerParams(dimension_semantics=("arbitrary", "arbitrary"),
                                             vmem_limit_bytes=VMEM_LIMIT),
        name="mixer_prompt",
    )(sinks, x_prompt, *weight_args, strict_tril(tt // ROW_SPLITS), cnt_s)
    slabt = jnp.concatenate([slabt_p, slabt_s], axis=1)

    tm = EXPERT_TILE
    counts = cnt[:, 0].astype(jnp.int32)
    tiles = (counts + tm - 1) // tm
    tile_end = jnp.cumsum(tiles)
    offs = (tile_end - tiles) * tm
    n_used = tile_end[-1]
    n_tiles = (2 * n_tok) // tm + N_EXPERTS
    n_rows = n_tiles * tm
    tile_ids = jnp.arange(n_tiles, dtype=jnp.int32)
    tile_expert = jnp.minimum(jnp.sum(tile_ids[:, None] >= tile_end[None, :], axis=1), N_EXPERTS - 1).astype(jnp.int32)
    tile_in_expert = tile_ids - (tile_end - tiles)[tile_expert]
    tile_rows = jnp.clip(counts[tile_expert] - tile_in_expert * tm, 0, tm).astype(jnp.int32)
    lane_tiles = n_tok // LANES
    pos_blk = LANES * max(k for k in range(1, POS_BLOCK_TILES + 1) if lane_tiles % k == 0)
    offs_b = jnp.broadcast_to(offs.astype(f32)[:, None], cnt_shape)
    pos_row = jax.ShapeDtypeStruct((1, n_tok), jnp.int32)
    pos0, pos1 = pl.pallas_call(
        _pos_kernel,
        grid=(n_tok // pos_blk,),
        in_specs=[pl.BlockSpec((EXP_PER_GROUP, pos_blk), lambda i: (0, i)), _const_spec(cnt_shape)],
        out_specs=[pl.BlockSpec((1, pos_blk), lambda i: (0, i)), pl.BlockSpec((1, pos_blk), lambda i: (0, i))],
        out_shape=[pos_row, pos_row],
        compiler_params=pltpu.CompilerParams(dimension_semantics=("arbitrary",)),
        name="sorted_pos",
    )(slabt, offs_b)

    assert dh == 2 * SC_COLS and n_p % SC_WINDOW == 0 and n_s % SC_WINDOW == 0
    xs_a, xs_b = _sc_scatter_rows([h2_p, h2_s], pos0, pos1, n_rows)

    w_map = lambda e, *_: (e, 0, 0)
    hbm = pl.BlockSpec(memory_space=pl.ANY)
    piece = jax.ShapeDtypeStruct((n_rows, SC_COLS), u32)
    tile_buf = pltpu.VMEM((2, tm, SC_COLS), u32)
    out_a, out_b = pl.pallas_call(
        _expert_kernel,
        grid_spec=pltpu.PrefetchScalarGridSpec(
            num_scalar_prefetch=4,
            grid=(N_EXPERTS,),
            in_specs=[hbm, hbm,
                      pl.BlockSpec((1, d, D_EXPERT), w_map),
                      pl.BlockSpec((1, d, D_EXPERT), w_map),
                      pl.BlockSpec((1, D_EXPERT, d), w_map)],
            out_specs=[hbm, hbm],
            scratch_shapes=[pltpu.VMEM((d, D_EXPERT), bf), pltpu.VMEM((d, D_EXPERT), bf),
                            pltpu.VMEM((D_EXPERT, d), bf),
                            tile_buf, tile_buf, tile_buf, tile_buf,
                            pltpu.SemaphoreType.DMA((2, 2)), pltpu.SemaphoreType.DMA((2, 2))]),
        out_shape=[piece, piece],
        compiler_params=pltpu.CompilerParams(dimension_semantics=("arbitrary",), vmem_limit_bytes=VMEM_LIMIT),
        name="experts",
    )((tile_end - tiles).astype(jnp.int32), tile_end.astype(jnp.int32), tile_rows,
      n_used.reshape(1).astype(jnp.int32), xs_a, xs_b, w_gate[l], w_up[l], w_down[l])

    (o1a, o1b), (o2a, o2b) = _sc_gather_rows([out_a, out_b], [pos0, pos1])

    def combine(x1, slab, blk0):
        n = x1.shape[0]
        tok = lambda i: (i, 0)
        off = lambda i: (blk0 + i, 0)
        return pl.pallas_call(
            _combine_kernel,
            grid=(n // tt,),
            in_specs=[pl.BlockSpec((tt, d), tok)] + [pl.BlockSpec((tt, SC_COLS), off)] * 4
                     + [pl.BlockSpec((tt, ROUTE_LANES), tok)],
            out_specs=pl.BlockSpec((tt, d), tok),
            out_shape=jax.ShapeDtypeStruct((n, d), f32),
            compiler_params=pltpu.CompilerParams(dimension_semantics=("arbitrary",)),
            name="combine",
        )(x1, o1a, o1b, o2a, o2b, slab)

    y_p = combine(x1_p, slab_p, 0).reshape(nb, seq, d)
    y_s = combine(x1_s, slab_s, n_p // tt).reshape(ns, slen, d)

    kv_shape = (1, nb, WINDOW, N_KV_HEADS, HEAD_DIM)
    new_k_p = kwin.reshape(kv_shape)
    new_v_p = vwin.reshape(kv_shape)
    keep = WINDOW - slen
    ck4 = cache_attn_k[l][:, WINDOW - keep:]
    cv4 = cache_attn_v[l][:, WINDOW - keep:]
    new_k_s = jnp.concatenate([ck4, kn_s.reshape(ns, slen, N_KV_HEADS, HEAD_DIM)], axis=1)[None]
    new_v_s = jnp.concatenate([cv4, v_s.reshape(ns, slen, N_KV_HEADS, HEAD_DIM)], axis=1)[None]
    new_gv_s = gvn_s.reshape(1, ns, slen, GATE_HEADS, GATE_DIM)
    return (y_p, y_s, new_k_p, new_v_p, new_k_s, new_v_s, new_gv_s)
```

```python
import functools

import jax
import jax.numpy as jnp
from jax import lax
from jax.experimental import pallas as pl
from jax.experimental.pallas import tpu as pltpu
from jax.experimental.pallas import tpu_sc as plsc

D_MODEL = 1024
HEAD_DIM = 64
N_Q_HEADS = 8
N_KV_HEADS = 2
Q_WIDTH = N_Q_HEADS * HEAD_DIM
KV_WIDTH = N_KV_HEADS * HEAD_DIM
GATE_HEADS = 4
GATE_DIM = 128
GATE_WIDTH = GATE_HEADS * GATE_DIM
PROJ_COLS = Q_WIDTH + 2 * KV_WIDTH + 2 * GATE_WIDTH
CHUNK = 64
WINDOW = 128
MLP_CHUNK = 128
N_GROUPS = 8
EXP_PER_GROUP = 8
N_EXPERTS = N_GROUPS * EXP_PER_GROUP
D_EXPERT = 512
EPS = 1e-6

LANES = 128
ROUTE_LANES = 128
FINE_LANE0 = N_GROUPS
NEG = -1e30

TOKEN_TILE = 512
ROW_SPLITS = 2
EXPERT_TILE = 512
EXPERT_BUFFERS = 3
POS_BLOCK_TILES = 64
SC_WINDOW = 128
SC_COLS = 256
VMEM_LIMIT = 48 * 1024 * 1024


def _pack_halves(x):
    w = x.shape[1] // 2
    b = lax.bitcast_convert_type(x.astype(jnp.bfloat16).astype(jnp.float32), jnp.uint32)
    return (b[:, :w] & jnp.uint32(0xFFFF0000)) | (b[:, w:] >> 16)


def _unpack_halves(p):
    hi = lax.bitcast_convert_type(p & jnp.uint32(0xFFFF0000), jnp.float32)
    lo = lax.bitcast_convert_type(p << 16, jnp.float32)
    return hi, lo


def _rms(x, eps=EPS):
    return x * lax.rsqrt(jnp.mean(x * x, axis=-1, keepdims=True) + eps)


def _project(x, gmix, win, gq, gk, gvg, blk):
    h = (_rms(x) * gmix).astype(jnp.bfloat16)
    z = jnp.dot(h, win, preferred_element_type=jnp.float32)
    qk = z[:, :Q_WIDTH + KV_WIDTH]
    outs = []
    for j in range((Q_WIDTH + KV_WIDTH) // LANES):
        zj = qk[:, j * LANES:(j + 1) * LANES]
        ss = jnp.dot((zj * zj).astype(jnp.bfloat16), blk, preferred_element_type=jnp.float32)
        outs.append(zj * lax.rsqrt(ss * (1.0 / HEAD_DIM) + EPS))
    qn = jnp.concatenate(outs[:Q_WIDTH // LANES], axis=-1) * gq
    kn = outs[-1] * gk
    v = z[:, Q_WIDTH + KV_WIDTH:Q_WIDTH + 2 * KV_WIDTH]
    u0 = Q_WIDTH + 2 * KV_WIDTH
    ua = jax.nn.gelu(z[:, u0:u0 + GATE_WIDTH])
    ga = jax.nn.gelu(z[:, u0 + GATE_WIDTH:])
    gvn = jnp.concatenate(
        [_rms(ga[:, i * GATE_DIM:(i + 1) * GATE_DIM]) for i in range(GATE_HEADS)], axis=-1) * gvg
    return qn.astype(jnp.bfloat16), kn, v, ua, gvn


def _dup_halves(a):
    lo = lax.broadcasted_iota(jnp.int32, a.shape, 1) < HEAD_DIM
    r = pltpu.roll(a, HEAD_DIM, axis=1)
    return (jnp.where(lo, a, r).astype(jnp.bfloat16), jnp.where(lo, r, a).astype(jnp.bfloat16))


def _attn_block(qa, qb, k2, v2, bias, sk):
    r = qa.shape[0]
    lo = lax.broadcasted_iota(jnp.int32, (r, LANES), 1) < HEAD_DIM
    zero = jnp.zeros_like(qa)
    qs = jnp.concatenate([jnp.where(lo, qa, zero), jnp.where(lo, zero, qa),
                          jnp.where(lo, qb, zero), jnp.where(lo, zero, qb)], axis=0)
    s = lax.dot_general(qs, k2, (((1,), (1,)), ((), ())), preferred_element_type=jnp.float32)
    if bias is not None:
        s = s + bias
    m = jnp.maximum(jnp.max(s, axis=-1, keepdims=True), sk)
    e = jnp.exp(s - m)
    den = jnp.sum(e, axis=-1, keepdims=True) + jnp.exp(sk - m)
    o = jnp.dot(e.astype(jnp.bfloat16), v2, preferred_element_type=jnp.float32) * (1.0 / den)
    pa = jnp.where(lo, o[0:r], o[r:2 * r])
    pb = jnp.where(lo, o[2 * r:3 * r], o[3 * r:4 * r])
    return pa, pb


def _sink_rows(sink_ref, g, r):
    row = lax.broadcasted_iota(jnp.int32, (4 * r, 1), 0)
    s0, s1, s2, s3 = (sink_ref[4 * g + i] for i in range(4))
    return jnp.where(row < r, s0, jnp.where(row < 2 * r, s1, jnp.where(row < 3 * r, s2, s3)))


def _causal_ws(ws_ref, h, n):
    w = ws_ref[h][:n, :n]
    keep = (lax.broadcasted_iota(jnp.int32, (n, n), 0) >= lax.broadcasted_iota(jnp.int32, (n, n), 1))
    return jnp.where(keep, w, jnp.zeros_like(w))


def _post(cat, x, wout, gffn, wrt, brt, triu, cnt):
    x1 = x + jnp.dot(cat, wout, preferred_element_type=jnp.float32)
    h2 = _rms(x1) * gffn
    t = x.shape[0]
    reps = t // LANES
    lgt = lax.dot_general(wrt, h2.astype(jnp.bfloat16), (((1,), (1,)), ((), ())),
                          preferred_element_type=jnp.float32) + jnp.tile(brt, (1, reps))
    ng = float(EXP_PER_GROUP)
    sub = lax.broadcasted_iota(jnp.int32, (EXP_PER_GROUP, t), 0).astype(jnp.float32)
    c = lgt[0:N_GROUPS]
    mc = jnp.max(c, axis=0, keepdims=True)
    grp = jnp.min(jnp.where(c == mc, sub, ng), axis=0, keepdims=True)
    pg = 1.0 / jnp.sum(jnp.exp(c - mc), axis=0, keepdims=True)
    lf = lgt[FINE_LANE0:FINE_LANE0 + EXP_PER_GROUP]
    for g in range(1, N_GROUPS):
        r0 = FINE_LANE0 + g * EXP_PER_GROUP
        lf = jnp.where(grp == float(g), lgt[r0:r0 + EXP_PER_GROUP], lf)
    v1 = jnp.max(lf, axis=0, keepdims=True)
    i1 = jnp.min(jnp.where(lf == v1, sub, ng), axis=0, keepdims=True)
    lf2 = jnp.where(sub == i1, -jnp.inf, lf)
    v2 = jnp.max(lf2, axis=0, keepdims=True)
    i2 = jnp.min(jnp.where(lf2 == v2, sub, ng), axis=0, keepdims=True)
    tt = jnp.exp(v2 - v1)
    w1 = 1.0 / (1.0 + tt)
    w2 = tt * w1
    e1 = grp * ng + i1
    e2 = grp * ng + i2
    row = lax.broadcasted_iota(jnp.int32, (N_EXPERTS, t), 0).astype(jnp.float32)
    sel1 = row == e1
    sel2 = row == e2
    oh = jnp.where(sel1 | sel2, 1.0, 0.0)
    cum = jnp.dot(oh.astype(jnp.bfloat16), triu, preferred_element_type=jnp.float32) + jnp.tile(cnt, (1, reps))
    r1 = jnp.sum(jnp.where(sel1, cum, 0.0), axis=0, keepdims=True)
    r2 = jnp.sum(jnp.where(sel2, cum, 0.0), axis=0, keepdims=True)
    new_cnt = cnt + jnp.sum(oh, axis=1, keepdims=True)
    slab_t = jnp.where(sub == 0.0, e1,
             jnp.where(sub == 1.0, e2,
             jnp.where(sub == 2.0, r1,
             jnp.where(sub == 3.0, r2,
             jnp.where(sub == 4.0, pg * w1,
             jnp.where(sub == 5.0, pg * w2, 0.0))))))
    slab = jnp.concatenate([slab_t, jnp.zeros((ROUTE_LANES - EXP_PER_GROUP, t), jnp.float32)], axis=0).T
    return x1, _pack_halves(h2), slab, slab_t, new_cnt


def _prompt_kernel(sink_ref, x_ref, gmix_ref, win_ref, gq_ref, gk_ref, gvg_ref, blk_ref, ws_ref, bs_ref,
                   wout_ref, gffn_ref, wr_ref, br_ref, tril_ref, cnt_in_ref,
                   x1_ref, h2_ref, slab_ref, slabt_ref, kwin_ref, vwin_ref, cnt_ref,
                   k2_scr, v2_scr, qn_scr, ua_scr, gvn_scr, cat_scr, cnt_scr):
    b = pl.program_id(0)
    t = pl.program_id(1)
    nt = pl.num_programs(1)
    tt = x_ref.shape[1]
    nblk = tt // WINDOW
    sub = tt // ROW_SPLITS

    @pl.when((b == 0) & (t == 0))
    def _():
        cnt_scr[...] = cnt_in_ref[...]

    @pl.when(t == 0)
    def _():
        k2_scr[:, 0:WINDOW, :] = jnp.zeros((N_KV_HEADS, WINDOW, LANES), jnp.bfloat16)
        v2_scr[:, 0:WINDOW, :] = jnp.zeros((N_KV_HEADS, WINDOW, LANES), jnp.bfloat16)

    for s in range(ROW_SPLITS):
        rs = slice(s * sub, (s + 1) * sub)
        ks = slice(WINDOW + s * sub, WINDOW + (s + 1) * sub)
        qn, kn, v, ua, gvn = _project(x_ref[0, rs, :], gmix_ref[...], win_ref[...], gq_ref[...], gk_ref[...],
                                      gvg_ref[...], blk_ref[...])
        qn_scr[rs, :] = qn
        ua_scr[rs, :] = ua
        gvn_scr[rs, :] = gvn.astype(jnp.bfloat16)
        k0, k1 = _dup_halves(kn)
        v0, v1 = _dup_halves(v)
        k2_scr[0, ks, :] = k0
        k2_scr[1, ks, :] = k1
        v2_scr[0, ks, :] = v0
        v2_scr[1, ks, :] = v1
        if s == ROW_SPLITS - 1:
            @pl.when(t == nt - 1)
            def _():
                kwin_ref[0] = kn[sub - WINDOW:]
                vwin_ref[0] = v[sub - WINDOW:]

    rows = 4 * WINDOW
    band = 2 * WINDOW
    rr = lax.broadcasted_iota(jnp.int32, (rows, band), 0)
    kk = lax.broadcasted_iota(jnp.int32, (rows, band), 1)
    half = ((rr % WINDOW) >= CHUNK).astype(jnp.int32)
    allowed = (kk >= half * CHUNK) & (kk < (3 + half) * CHUNK)
    bias_mid = jnp.where(allowed, 0.0, NEG)
    bias_first = jnp.where(allowed & (kk >= WINDOW), 0.0, NEG)

    def attn_body(j, carry):
        r0 = pl.multiple_of(j * WINDOW, WINDOW)
        bias = jnp.where((t == 0) & (j == 0), bias_first, bias_mid)
        for g in range(N_KV_HEADS):
            c0 = g * 2 * LANES
            qa = qn_scr[pl.ds(r0, WINDOW), c0:c0 + LANES]
            qb = qn_scr[pl.ds(r0, WINDOW), c0 + LANES:c0 + 2 * LANES]
            k2 = k2_scr[g, pl.ds(r0, band), :]
            v2 = v2_scr[g, pl.ds(r0, band), :]
            pa, pb = _attn_block(qa, qb, k2, v2, bias, _sink_rows(sink_ref, g, WINDOW))
            cat_scr[pl.ds(r0, WINDOW), c0:c0 + LANES] = pa.astype(jnp.bfloat16)
            cat_scr[pl.ds(r0, WINDOW), c0 + LANES:c0 + 2 * LANES] = pb.astype(jnp.bfloat16)
        return carry

    lax.fori_loop(0, nblk, attn_body, 0, unroll=True)

    k2_scr[:, 0:WINDOW, :] = k2_scr[:, tt:tt + WINDOW, :]
    v2_scr[:, 0:WINDOW, :] = v2_scr[:, tt:tt + WINDOW, :]

    for h in range(GATE_HEADS):
        w = _causal_ws(ws_ref, h, MLP_CHUNK)
        bcol = bs_ref[:, h:h + 1]
        for c in range(tt // MLP_CHUNK):
            rs = slice(c * MLP_CHUNK, (c + 1) * MLP_CHUNK)
            cs = slice(h * GATE_DIM, (h + 1) * GATE_DIM)
            mix = jnp.dot(w, gvn_scr[rs, cs], preferred_element_type=jnp.float32) + bcol
            cat_scr[rs, Q_WIDTH + h * GATE_DIM:Q_WIDTH + (h + 1) * GATE_DIM] = (ua_scr[rs, cs] * mix).astype(jnp.bfloat16)

    cnt = cnt_scr[...]
    for s in range(ROW_SPLITS):
        rs = slice(s * sub, (s + 1) * sub)
        x1, hp, slab, slab_t, cnt = _post(cat_scr[rs, :], x_ref[0, rs, :], wout_ref[...], gffn_ref[...],
                                          wr_ref[...], br_ref[...], tril_ref[...], cnt)
        x1_ref[rs, :] = x1
        h2_ref[rs, :] = hp
        slab_ref[rs, :] = slab
        slabt_ref[:, rs] = slab_t
    cnt_scr[...] = cnt
    cnt_ref[...] = cnt


def _sample_kernel(sink_ref, x_ref, ck_ref, cv_ref, gmix_ref, win_ref, gq_ref, gk_ref, gvg_ref, blk_ref, ws_ref,
                   bs_ref, wout_ref, gffn_ref, wr_ref, br_ref, tril_ref,
                   x1_ref, h2_ref, slab_ref, slabt_ref, kn_ref, v_ref, gvn_ref, cnt_ref,
                   ck2_scr, cv2_scr, k2_scr, v2_scr, qn_scr, ua_scr, cat_scr, *, n_seq, seq_len):
    x = x_ref[...]
    qn, kn, v, ua, gvn = _project(x, gmix_ref[...], win_ref[...], gq_ref[...], gk_ref[...], gvg_ref[...],
                                  blk_ref[...])
    kn_ref[...] = kn
    v_ref[...] = v
    gvn_ref[...] = gvn
    qn_scr[...] = qn
    ua_scr[...] = ua
    for scr, val in ((k2_scr, kn), (v2_scr, v), (ck2_scr, ck_ref[...]), (cv2_scr, cv_ref[...])):
        a0, a1 = _dup_halves(val)
        scr[0] = a0
        scr[1] = a1

    ws = [_causal_ws(ws_ref, h, seq_len) for h in range(GATE_HEADS)]

    def seq_body(i, carry):
        r0 = pl.multiple_of(i * seq_len, seq_len)
        c0r = pl.multiple_of(i * WINDOW, WINDOW)
        for g in range(N_KV_HEADS):
            c0 = g * 2 * LANES
            qa = qn_scr[pl.ds(r0, seq_len), c0:c0 + LANES]
            qb = qn_scr[pl.ds(r0, seq_len), c0 + LANES:c0 + 2 * LANES]
            k2 = jnp.concatenate([ck2_scr[g, pl.ds(c0r, WINDOW), :], k2_scr[g, pl.ds(r0, seq_len), :]], axis=0)
            v2 = jnp.concatenate([cv2_scr[g, pl.ds(c0r, WINDOW), :], v2_scr[g, pl.ds(r0, seq_len), :]], axis=0)
            pa, pb = _attn_block(qa, qb, k2, v2, None, _sink_rows(sink_ref, g, seq_len))
            cat_scr[pl.ds(r0, seq_len), c0:c0 + LANES] = pa.astype(jnp.bfloat16)
            cat_scr[pl.ds(r0, seq_len), c0 + LANES:c0 + 2 * LANES] = pb.astype(jnp.bfloat16)
        for h in range(GATE_HEADS):
            cs = slice(h * GATE_DIM, (h + 1) * GATE_DIM)
            gv_h = gvn_ref[pl.ds(r0, seq_len), cs].astype(jnp.bfloat16)
            mix = jnp.dot(ws[h], gv_h, preferred_element_type=jnp.float32) + bs_ref[0:seq_len, h:h + 1]
            cat_scr[pl.ds(r0, seq_len), Q_WIDTH + h * GATE_DIM:Q_WIDTH + (h + 1) * GATE_DIM] = (
                ua_scr[pl.ds(r0, seq_len), cs] * mix).astype(jnp.bfloat16)
        return carry

    lax.fori_loop(0, n_seq, seq_body, 0)

    cnt0 = jnp.zeros((N_EXPERTS, LANES), jnp.float32)
    x1, hp, slab, slab_t, new_cnt = _post(cat_scr[...], x, wout_ref[...], gffn_ref[...], wr_ref[...],
                                          br_ref[...], tril_ref[...], cnt0)
    x1_ref[...] = x1
    h2_ref[...] = hp
    slab_ref[...] = slab
    slabt_ref[...] = slab_t
    cnt_ref[...] = new_cnt


def _expert_kernel(ts_ref, te_ref, cnt_ref, nu_ref, xa_hbm, xb_hbm, wg_ref, wu_ref, wd_ref, oa_hbm, ob_hbm,
                   wg_s, wu_s, wd_s, xa_buf, xb_buf, oa_buf, ob_buf, in_sem, out_sem):
    e = pl.program_id(0)
    nbuf, tm, _ = xa_buf.shape
    n_used = nu_ref[0]

    def rows_of(g):
        return pl.ds(pl.multiple_of(g * tm, tm), tm)

    def in_copies(g, slot):
        return (pltpu.make_async_copy(xa_hbm.at[rows_of(g)], xa_buf.at[slot], in_sem.at[0, slot]),
                pltpu.make_async_copy(xb_hbm.at[rows_of(g)], xb_buf.at[slot], in_sem.at[1, slot]))

    def out_copies(g, slot):
        return (pltpu.make_async_copy(oa_buf.at[slot], oa_hbm.at[rows_of(g)], out_sem.at[0, slot]),
                pltpu.make_async_copy(ob_buf.at[slot], ob_hbm.at[rows_of(g)], out_sem.at[1, slot]))

    @pl.when(e == 0)
    def _():
        for g in range(nbuf - 1):
            @pl.when(g < n_used)
            def _():
                for c in in_copies(g, g):
                    c.start()

    g_lo = ts_ref[e]
    g_hi = te_ref[e]

    @pl.when(g_hi > g_lo)
    def _():
        wg_s[...] = wg_ref[0].astype(jnp.bfloat16)
        wu_s[...] = wu_ref[0].astype(jnp.bfloat16)
        wd_s[...] = wd_ref[0].astype(jnp.bfloat16)

    def tile_body(g, slot):
        for c in in_copies(g, slot):
            c.wait()
        ahead = g + (nbuf - 1)
        ahead_slot = jnp.where(slot == 0, nbuf - 1, slot - 1)

        @pl.when(ahead < n_used)
        def _():
            for c in in_copies(ahead, ahead_slot):
                c.start()

        live = lax.broadcasted_iota(jnp.int32, (tm, xa_buf.shape[2]), 0) < cnt_ref[e] - (g - g_lo) * tm
        ha, la = _unpack_halves(jnp.where(live, xa_buf[slot], jnp.uint32(0)))
        hb, lb = _unpack_halves(jnp.where(live, xb_buf[slot], jnp.uint32(0)))
        xs = jnp.concatenate([ha, hb, la, lb], axis=1).astype(jnp.bfloat16)
        hg = jnp.dot(xs, wg_s[...], preferred_element_type=jnp.float32)
        hu = jnp.dot(xs, wu_s[...], preferred_element_type=jnp.float32)
        a = (jax.nn.silu(hg) * hu).astype(jnp.bfloat16)
        packed = _pack_halves(jnp.dot(a, wd_s[...], preferred_element_type=jnp.float32))
        q = packed.shape[1] // 2

        @pl.when(g >= nbuf)
        def _():
            for c in out_copies(g - nbuf, slot):
                c.wait()

        oa_buf[slot] = packed[:, :q]
        ob_buf[slot] = packed[:, q:]
        for c in out_copies(g, slot):
            c.start()
        return jnp.where(slot == nbuf - 1, 0, slot + 1)

    lax.fori_loop(g_lo, g_hi, tile_body, g_lo % nbuf)

    @pl.when(e == pl.num_programs(0) - 1)
    def _():
        for back in range(1, nbuf + 1):
            @pl.when(n_used >= back)
            def _():
                g = n_used - back
                for c in out_copies(g, g % nbuf):
                    c.wait()


def _pos_kernel(slabt_ref, offs_ref, pos0_ref, pos1_ref):
    st = slabt_ref[...]
    t = st.shape[1]
    row = lax.broadcasted_iota(jnp.int32, (N_EXPERTS, t), 0).astype(jnp.float32)
    offs = jnp.tile(offs_ref[...], (1, t // LANES))
    for s, out in enumerate((pos0_ref, pos1_ref)):
        first = jnp.sum(jnp.where(row == st[s:s + 1], offs, 0.0), axis=0, keepdims=True)
        out[...] = (first + st[2 + s:3 + s]).astype(jnp.int32)


def _combine_kernel(x1_ref, o1a_ref, o1b_ref, o2a_ref, o2b_ref, slab_ref, y_ref):
    slab = slab_ref[...]
    g1 = slab[:, 4:5]
    g2 = slab[:, 5:6]
    q = o1a_ref.shape[1]
    for c, (r1, r2) in enumerate(((o1a_ref, o2a_ref), (o1b_ref, o2b_ref))):
        h1, l1 = _unpack_halves(r1[...])
        h2, l2 = _unpack_halves(r2[...])
        hs = slice(c * q, (c + 1) * q)
        ls = slice(2 * q + c * q, 2 * q + (c + 1) * q)
        y_ref[:, hs] = x1_ref[:, hs] + (g1 * h1 + g2 * h2)
        y_ref[:, ls] = x1_ref[:, ls] + (g1 * l1 + g2 * l2)


def _sc_mesh():
    return plsc.VectorSubcoreMesh(core_axis_name="core", subcore_axis_name="subcore")


def _sc_scatter_rows(parts, pos0, pos1, n_rows):
    n_piece = parts[0].shape[1] // SC_COLS

    def sc_kernel(*refs):
        x_refs = refs[:len(parts)]
        i0_hbm, i1_hbm = refs[len(parts):len(parts) + 2]
        o_refs = refs[len(parts) + 2:]
        tok0 = 0
        for x_hbm, arr in zip(x_refs, parts):
            blk0 = tok0 // SC_WINDOW
            for c in range(n_piece):
                def body(x_vmem, i0_vmem, i1_vmem, o_hbm=o_refs[c]):
                    pltpu.sync_copy(x_vmem, o_hbm.at[i0_vmem.at[0]])
                    pltpu.sync_copy(x_vmem, o_hbm.at[i1_vmem.at[0]])

                pltpu.emit_pipeline(
                    body,
                    grid=(arr.shape[0] // SC_WINDOW,),
                    in_specs=[pl.BlockSpec((SC_WINDOW, SC_COLS), lambda i, c=c: (i, c)),
                              pl.BlockSpec((1, SC_WINDOW), lambda i, blk0=blk0: (0, blk0 + i)),
                              pl.BlockSpec((1, SC_WINDOW), lambda i, blk0=blk0: (0, blk0 + i))],
                    out_specs=[],
                    core_axis_name=("core", "subcore"),
                    dimension_semantics=(pltpu.PARALLEL,),
                )(x_hbm, i0_hbm, i1_hbm)
            tok0 += arr.shape[0]

    piece = jax.ShapeDtypeStruct((n_rows, SC_COLS), jnp.uint32)
    return pl.kernel(sc_kernel, out_type=(piece,) * n_piece, mesh=_sc_mesh(), name="scatter_rows")(
        *parts, pos0, pos1)


def _sc_gather_rows(pieces, pos_list):
    n_tok = pos_list[0].shape[1]

    def sc_kernel(*refs):
        s_refs = refs[:len(pieces)]
        i_refs = refs[len(pieces):len(pieces) + len(pos_list)]
        o_refs = refs[len(pieces) + len(pos_list):]
        k = 0
        for i_hbm in i_refs:
            for s_hbm in s_refs:
                def body(i_vmem, o_vmem, s_hbm=s_hbm):
                    pltpu.sync_copy(s_hbm.at[i_vmem.at[0]], o_vmem)

                pltpu.emit_pipeline(
                    body,
                    grid=(n_tok // SC_WINDOW,),
                    in_specs=[pl.BlockSpec((1, SC_WINDOW), lambda i: (0, i))],
                    out_specs=[pl.BlockSpec((SC_WINDOW, SC_COLS), lambda i: (i, 0))],
                    core_axis_name=("core", "subcore"),
                    dimension_semantics=(pltpu.PARALLEL,),
                )(i_hbm, o_refs[k])
                k += 1

    out = jax.ShapeDtypeStruct((n_tok, SC_COLS), jnp.uint32)
    outs = pl.kernel(sc_kernel, out_type=(out,) * (len(pieces) * len(pos_list)), mesh=_sc_mesh(),
                     name="gather_rows")(*pieces, *pos_list)
    return [outs[j * len(pieces):(j + 1) * len(pieces)] for j in range(len(pos_list))]


def _const_spec(shape):
    nd = len(shape)
    return pl.BlockSpec(shape, lambda *_: (0,) * nd)


def kernel(x_prompt, x_sample, cache_attn_k, cache_attn_v, g_mix, w_in, g_q, g_k, g_v, attn_sinks, w_s, b_s,
           w_out, g_ffn, w_coarse, b_coarse, w_fine, b_fine, w_gate, w_up, w_down):
    nb, seq, d = x_prompt.shape
    ns, slen, _ = x_sample.shape
    n_p = nb * seq
    n_s = ns * slen
    n_tok = n_p + n_s
    tt = TOKEN_TILE
    assert seq % tt == 0 and n_s % tt == 0 and d == D_MODEL
    nt = seq // tt
    bf = jnp.bfloat16
    f32 = jnp.float32

    l = 0
    gmix = g_mix[l].reshape(1, d)
    win = w_in[l].astype(bf)
    gq = (jnp.tile(g_q[l], N_Q_HEADS) * (HEAD_DIM ** -0.5)).reshape(1, Q_WIDTH)
    gk = jnp.tile(g_k[l], N_KV_HEADS).reshape(1, KV_WIDTH)
    gvg = g_v[l].reshape(1, GATE_WIDTH)
    sinks = attn_sinks[l].reshape(N_Q_HEADS).astype(f32)
    ws = w_s[l].astype(bf)
    bs = b_s[l].T
    wout = w_out[l].astype(bf)
    gffn = g_ffn[l].reshape(1, d)
    wr = jnp.concatenate([w_coarse[l], jnp.transpose(w_fine[l], (1, 0, 2)).reshape(d, N_EXPERTS),
                          jnp.zeros((d, ROUTE_LANES - N_GROUPS - N_EXPERTS), f32)], axis=1)
    wr = wr.astype(bf).T
    br = jnp.concatenate([b_coarse[l], b_fine[l].reshape(-1),
                          jnp.zeros((ROUTE_LANES - N_GROUPS - N_EXPERTS,), f32)])
    br = jnp.broadcast_to(br[:, None], (ROUTE_LANES, LANES))
    cnt_shape = (N_EXPERTS, LANES)
    ii = jnp.arange(LANES)
    blk = (ii[:, None] // HEAD_DIM == ii[None, :] // HEAD_DIM).astype(bf)
    u32 = jnp.uint32
    dh = d // 2

    def strict_tril(n):
        r = jnp.arange(n)
        return (r[:, None] < r[None, :]).astype(bf)

    weight_args = (gmix, win, gq, gk, gvg, blk, ws, bs, wout, gffn, wr, br)
    weight_specs = [_const_spec(a.shape) for a in weight_args]
    smem_spec = pl.BlockSpec(memory_space=pltpu.SMEM)

    xs2 = x_sample.reshape(n_s, d)
    ck = cache_attn_k[l].reshape(ns * WINDOW, KV_WIDTH)
    cv = cache_attn_v[l].reshape(ns * WINDOW, KV_WIDTH)
    tok_out = lambda n, w, dt: jax.ShapeDtypeStruct((n, w), dt)
    x1_s, h2_s, slab_s, slabt_s, kn_s, v_s, gvn_s, cnt_s = pl.pallas_call(
        functools.partial(_sample_kernel, n_seq=ns, seq_len=slen),
        grid=(1,),
        in_specs=[smem_spec, _const_spec((n_s, d)), _const_spec(ck.shape), _const_spec(cv.shape)]
                 + weight_specs + [_const_spec((n_s, n_s))],
        out_specs=[_const_spec((n_s, d)), _const_spec((n_s, dh)), _const_spec((n_s, ROUTE_LANES)),
                   _const_spec((EXP_PER_GROUP, n_s)),
                   _const_spec((n_s, KV_WIDTH)), _const_spec((n_s, KV_WIDTH)), _const_spec((n_s, GATE_WIDTH)),
                   _const_spec(cnt_shape)],
        out_shape=[tok_out(n_s, d, f32), tok_out(n_s, dh, u32), tok_out(n_s, ROUTE_LANES, f32),
                   jax.ShapeDtypeStruct((EXP_PER_GROUP, n_s), f32),
                   jax.ShapeDtypeStruct((n_s, KV_WIDTH), f32), jax.ShapeDtypeStruct((n_s, KV_WIDTH), f32),
                   jax.ShapeDtypeStruct((n_s, GATE_WIDTH), f32), jax.ShapeDtypeStruct(cnt_shape, f32)],
        scratch_shapes=[pltpu.VMEM((N_KV_HEADS, ns * WINDOW, LANES), bf), pltpu.VMEM((N_KV_HEADS, ns * WINDOW, LANES), bf),
                        pltpu.VMEM((N_KV_HEADS, n_s, LANES), bf), pltpu.VMEM((N_KV_HEADS, n_s, LANES), bf),
                        pltpu.VMEM((n_s, Q_WIDTH), bf), pltpu.VMEM((n_s, GATE_WIDTH), f32),
                        pltpu.VMEM((n_s, d), bf)],
        compiler_params=pltpu.CompilerParams(dimension_semantics=("arbitrary",), vmem_limit_bytes=VMEM_LIMIT),
        name="mixer_sample",
    )(sinks, xs2, ck, cv, *weight_args, strict_tril(n_s))

    x1_p, h2_p, slab_p, slabt_p, kwin, vwin, cnt = pl.pallas_call(
        _prompt_kernel,
        grid=(nb, nt),
        in_specs=[smem_spec, pl.BlockSpec((1, tt, d), lambda b, t: (b, t, 0))] + weight_specs
                 + [_const_spec((tt // ROW_SPLITS, tt // ROW_SPLITS)), _const_spec(cnt_shape)],
        out_specs=[pl.BlockSpec((tt, d), lambda b, t: (b * nt + t, 0)),
                   pl.BlockSpec((tt, dh), lambda b, t: (b * nt + t, 0)),
                   pl.BlockSpec((tt, ROUTE_LANES), lambda b, t: (b * nt + t, 0)),
                   pl.BlockSpec((EXP_PER_GROUP, tt), lambda b, t: (0, b * nt + t)),
                   pl.BlockSpec((1, WINDOW, KV_WIDTH), lambda b, t: (b, 0, 0)),
                   pl.BlockSpec((1, WINDOW, KV_WIDTH), lambda b, t: (b, 0, 0)),
                   _const_spec(cnt_shape)],
        out_shape=[tok_out(n_p, d, f32), tok_out(n_p, dh, u32), tok_out(n_p, ROUTE_LANES, f32),
                   jax.ShapeDtypeStruct((EXP_PER_GROUP, n_p), f32),
                   jax.ShapeDtypeStruct((nb, WINDOW, KV_WIDTH), f32),
                   jax.ShapeDtypeStruct((nb, WINDOW, KV_WIDTH), f32),
                   jax.ShapeDtypeStruct(cnt_shape, f32)],
        scratch_shapes=[pltpu.VMEM((N_KV_HEADS, tt + WINDOW, LANES), bf),
                        pltpu.VMEM((N_KV_HEADS, tt + WINDOW, LANES), bf),
                        pltpu.VMEM((tt, Q_WIDTH), bf), pltpu.VMEM((tt, GATE_WIDTH), f32),
                        pltpu.VMEM((tt, GATE_WIDTH), bf), pltpu.VMEM((tt, d), bf),
                        pltpu.VMEM(cnt_shape, f32)],
        compiler_params=pltpu.CompilerParams(dimension_semantics=("arbitrary", "arbitrary"),
                                             vmem_limit_bytes=VMEM_LIMIT),
        name="mixer_prompt",
    )(sinks, x_prompt, *weight_args, strict_tril(tt // ROW_SPLITS), cnt_s)
    slabt = jnp.concatenate([slabt_p, slabt_s], axis=1)

    tm = EXPERT_TILE
    counts = cnt[:, 0].astype(jnp.int32)
    tiles = (counts + tm - 1) // tm
    tile_end = jnp.cumsum(tiles)
    offs = (tile_end - tiles) * tm
    n_used = tile_end[-1]
    n_tiles = (2 * n_tok) // tm + N_EXPERTS
    n_rows = n_tiles * tm
    lane_tiles = n_tok // LANES
    pos_blk = LANES * max(k for k in range(1, POS_BLOCK_TILES + 1) if lane_tiles % k == 0)
    offs_b = jnp.broadcast_to(offs.astype(f32)[:, None], cnt_shape)
    pos_row = jax.ShapeDtypeStruct((1, n_tok), jnp.int32)
    pos0, pos1 = pl.pallas_call(
        _pos_kernel,
        grid=(n_tok // pos_blk,),
        in_specs=[pl.BlockSpec((EXP_PER_GROUP, pos_blk), lambda i: (0, i)), _const_spec(cnt_shape)],
        out_specs=[pl.BlockSpec((1, pos_blk), lambda i: (0, i)), pl.BlockSpec((1, pos_blk), lambda i: (0, i))],
        out_shape=[pos_row, pos_row],
        compiler_params=pltpu.CompilerParams(dimension_semantics=("arbitrary",)),
        name="sorted_pos",
    )(slabt, offs_b)

    assert dh == 2 * SC_COLS and n_p % SC_WINDOW == 0 and n_s % SC_WINDOW == 0
    xs_a, xs_b = _sc_scatter_rows([h2_p, h2_s], pos0, pos1, n_rows)

    w_map = lambda e, *_: (e, 0, 0)
    hbm = pl.BlockSpec(memory_space=pl.ANY)
    piece = jax.ShapeDtypeStruct((n_rows, SC_COLS), u32)
    tile_buf = pltpu.VMEM((EXPERT_BUFFERS, tm, SC_COLS), u32)
    tile_sems = pltpu.SemaphoreType.DMA((2, EXPERT_BUFFERS))
    out_a, out_b = pl.pallas_call(
        _expert_kernel,
        grid_spec=pltpu.PrefetchScalarGridSpec(
            num_scalar_prefetch=4,
            grid=(N_EXPERTS,),
            in_specs=[hbm, hbm,
                      pl.BlockSpec((1, d, D_EXPERT), w_map),
                      pl.BlockSpec((1, d, D_EXPERT), w_map),
                      pl.BlockSpec((1, D_EXPERT, d), w_map)],
            out_specs=[hbm, hbm],
            scratch_shapes=[pltpu.VMEM((d, D_EXPERT), bf), pltpu.VMEM((d, D_EXPERT), bf),
                            pltpu.VMEM((D_EXPERT, d), bf),
                            tile_buf, tile_buf, tile_buf, tile_buf, tile_sems, tile_sems]),
        out_shape=[piece, piece],
        compiler_params=pltpu.CompilerParams(dimension_semantics=("arbitrary",), vmem_limit_bytes=VMEM_LIMIT),
        name="experts",
    )((tile_end - tiles).astype(jnp.int32), tile_end.astype(jnp.int32), counts,
      n_used.reshape(1).astype(jnp.int32), xs_a, xs_b, w_gate[l], w_up[l], w_down[l])

    (o1a, o1b), (o2a, o2b) = _sc_gather_rows([out_a, out_b], [pos0, pos1])

    def combine(x1, slab, blk0):
        n = x1.shape[0]
        tok = lambda i: (i, 0)
        off = lambda i: (blk0 + i, 0)
        return pl.pallas_call(
            _combine_kernel,
            grid=(n // tt,),
            in_specs=[pl.BlockSpec((tt, d), tok)] + [pl.BlockSpec((tt, SC_COLS), off)] * 4
                     + [pl.BlockSpec((tt, ROUTE_LANES), tok)],
            out_specs=pl.BlockSpec((tt, d), tok),
            out_shape=jax.ShapeDtypeStruct((n, d), f32),
            compiler_params=pltpu.CompilerParams(dimension_semantics=("arbitrary",)),
            name="combine",
        )(x1, o1a, o1b, o2a, o2b, slab)

    y_p = combine(x1_p, slab_p, 0).reshape(nb, seq, d)
    y_s = combine(x1_s, slab_s, n_p // tt).reshape(ns, slen, d)

    kv_shape = (1, nb, WINDOW, N_KV_HEADS, HEAD_DIM)
    new_k_p = kwin.reshape(kv_shape)
    new_v_p = vwin.reshape(kv_shape)
    keep = WINDOW - slen
    ck4 = cache_attn_k[l][:, WINDOW - keep:]
    cv4 = cache_attn_v[l][:, WINDOW - keep:]
    new_k_s = jnp.concatenate([ck4, kn_s.reshape(ns, slen, N_KV_HEADS, HEAD_DIM)], axis=1)[None]
    new_v_s = jnp.concatenate([cv4, v_s.reshape(ns, slen, N_KV_HEADS, HEAD_DIM)], axis=1)[None]
    new_gv_s = gvn_s.reshape(1, ns, slen, GATE_HEADS, GATE_DIM)
    return (y_p, y_s, new_k_p, new_v_p, new_k_s, new_v_s, new_gv_s)
```

```python
import functools

import jax
import jax.numpy as jnp
from jax import lax
from jax.experimental import pallas as pl
from jax.experimental.pallas import tpu as pltpu
from jax.experimental.pallas import tpu_sc as plsc

D_MODEL = 1024
HEAD_DIM = 64
N_Q_HEADS = 8
N_KV_HEADS = 2
Q_WIDTH = N_Q_HEADS * HEAD_DIM
KV_WIDTH = N_KV_HEADS * HEAD_DIM
GATE_HEADS = 4
GATE_DIM = 128
GATE_WIDTH = GATE_HEADS * GATE_DIM
PROJ_COLS = Q_WIDTH + 2 * KV_WIDTH + 2 * GATE_WIDTH
CHUNK = 64
WINDOW = 128
MLP_CHUNK = 128
N_GROUPS = 8
EXP_PER_GROUP = 8
N_EXPERTS = N_GROUPS * EXP_PER_GROUP
D_EXPERT = 512
EPS = 1e-6

LANES = 128
ROUTE_LANES = 128
FINE_LANE0 = N_GROUPS
NEG = -1e30

TOKEN_TILE = 512
ROW_SPLITS = 2
EXPERT_TILE = 512
EXPERT_BUFFERS = 3
POS_BLOCK_TILES = 64
SC_WINDOW = 128
SC_COLS = 256
VMEM_LIMIT = 40 * 1024 * 1024


def _pack_halves(x):
    w = x.shape[1] // 2
    b = lax.bitcast_convert_type(x.astype(jnp.bfloat16).astype(jnp.float32), jnp.uint32)
    return (b[:, :w] & jnp.uint32(0xFFFF0000)) | (b[:, w:] >> 16)


def _unpack_halves(p):
    hi = lax.bitcast_convert_type(p & jnp.uint32(0xFFFF0000), jnp.float32)
    lo = lax.bitcast_convert_type(p << 16, jnp.float32)
    return hi, lo


def _rms(x, eps=EPS):
    return x * lax.rsqrt(jnp.mean(x * x, axis=-1, keepdims=True) + eps)


def _in_proj(x, gmix, win):
    h = (_rms(x) * gmix).astype(jnp.bfloat16)
    return jnp.dot(h, win, preferred_element_type=jnp.float32)


def _heads(z, gq, gk, gvg, blk):
    qk = z[:, :Q_WIDTH + KV_WIDTH]
    outs = []
    for j in range((Q_WIDTH + KV_WIDTH) // LANES):
        zj = qk[:, j * LANES:(j + 1) * LANES]
        ss = jnp.dot((zj * zj).astype(jnp.bfloat16), blk, preferred_element_type=jnp.float32)
        outs.append(zj * lax.rsqrt(ss * (1.0 / HEAD_DIM) + EPS))
    qn = jnp.concatenate(outs[:Q_WIDTH // LANES], axis=-1) * gq
    kn = outs[-1] * gk
    v = z[:, Q_WIDTH + KV_WIDTH:Q_WIDTH + 2 * KV_WIDTH]
    u0 = Q_WIDTH + 2 * KV_WIDTH
    ua = jax.nn.gelu(z[:, u0:u0 + GATE_WIDTH])
    ga = jax.nn.gelu(z[:, u0 + GATE_WIDTH:])
    gvn = jnp.concatenate(
        [_rms(ga[:, i * GATE_DIM:(i + 1) * GATE_DIM]) for i in range(GATE_HEADS)], axis=-1) * gvg
    return qn.astype(jnp.bfloat16), kn, v, ua, gvn


def _dup_halves(a):
    lo = lax.broadcasted_iota(jnp.int32, a.shape, 1) < HEAD_DIM
    r = pltpu.roll(a, HEAD_DIM, axis=1)
    return (jnp.where(lo, a, r).astype(jnp.bfloat16), jnp.where(lo, r, a).astype(jnp.bfloat16))


def _attn_block(qa, qb, k2, v2, bias, sk):
    r = qa.shape[0]
    lo = lax.broadcasted_iota(jnp.int32, (r, LANES), 1) < HEAD_DIM
    zero = jnp.zeros_like(qa)
    qs = jnp.concatenate([jnp.where(lo, qa, zero), jnp.where(lo, zero, qa),
                          jnp.where(lo, qb, zero), jnp.where(lo, zero, qb)], axis=0)
    s = lax.dot_general(qs, k2, (((1,), (1,)), ((), ())), preferred_element_type=jnp.float32)
    if bias is not None:
        s = s + bias
    m = jnp.maximum(jnp.max(s, axis=-1, keepdims=True), sk)
    e = jnp.exp(s - m)
    den = jnp.sum(e, axis=-1, keepdims=True) + jnp.exp(sk - m)
    o = jnp.dot(e.astype(jnp.bfloat16), v2, preferred_element_type=jnp.float32) * (1.0 / den)
    pa = jnp.where(lo, o[0:r], o[r:2 * r])
    pb = jnp.where(lo, o[2 * r:3 * r], o[3 * r:4 * r])
    return pa, pb


def _sink_rows(sink_ref, g, r):
    row = lax.broadcasted_iota(jnp.int32, (4 * r, 1), 0)
    s0, s1, s2, s3 = (sink_ref[4 * g + i] for i in range(4))
    return jnp.where(row < r, s0, jnp.where(row < 2 * r, s1, jnp.where(row < 3 * r, s2, s3)))


def _causal_ws(ws_ref, h, n):
    w = ws_ref[h][:n, :n]
    keep = (lax.broadcasted_iota(jnp.int32, (n, n), 0) >= lax.broadcasted_iota(jnp.int32, (n, n), 1))
    return jnp.where(keep, w, jnp.zeros_like(w))


def _out_proj(cat, x, wout):
    return x + jnp.dot(cat, wout, preferred_element_type=jnp.float32)


def _route(x1, gffn, wrt, brt, triu, cnt):
    h2 = _rms(x1) * gffn
    t = x1.shape[0]
    reps = t // LANES
    lgt = lax.dot_general(wrt, h2.astype(jnp.bfloat16), (((1,), (1,)), ((), ())),
                          preferred_element_type=jnp.float32) + jnp.tile(brt, (1, reps))
    ng = float(EXP_PER_GROUP)
    sub = lax.broadcasted_iota(jnp.int32, (EXP_PER_GROUP, t), 0).astype(jnp.float32)
    c = lgt[0:N_GROUPS]
    mc = jnp.max(c, axis=0, keepdims=True)
    grp = jnp.min(jnp.where(c == mc, sub, ng), axis=0, keepdims=True)
    pg = 1.0 / jnp.sum(jnp.exp(c - mc), axis=0, keepdims=True)
    lf = lgt[FINE_LANE0:FINE_LANE0 + EXP_PER_GROUP]
    for g in range(1, N_GROUPS):
        r0 = FINE_LANE0 + g * EXP_PER_GROUP
        lf = jnp.where(grp == float(g), lgt[r0:r0 + EXP_PER_GROUP], lf)
    v1 = jnp.max(lf, axis=0, keepdims=True)
    i1 = jnp.min(jnp.where(lf == v1, sub, ng), axis=0, keepdims=True)
    lf2 = jnp.where(sub == i1, -jnp.inf, lf)
    v2 = jnp.max(lf2, axis=0, keepdims=True)
    i2 = jnp.min(jnp.where(lf2 == v2, sub, ng), axis=0, keepdims=True)
    tt = jnp.exp(v2 - v1)
    w1 = 1.0 / (1.0 + tt)
    w2 = tt * w1
    e1 = grp * ng + i1
    e2 = grp * ng + i2
    row = lax.broadcasted_iota(jnp.int32, (N_EXPERTS, t), 0).astype(jnp.float32)
    sel1 = row == e1
    sel2 = row == e2
    oh = jnp.where(sel1 | sel2, 1.0, 0.0)
    cum = jnp.dot(oh.astype(jnp.bfloat16), triu, preferred_element_type=jnp.float32) + jnp.tile(cnt, (1, reps))
    r1 = jnp.sum(jnp.where(sel1, cum, 0.0), axis=0, keepdims=True)
    r2 = jnp.sum(jnp.where(sel2, cum, 0.0), axis=0, keepdims=True)
    new_cnt = cnt + jnp.sum(oh, axis=1, keepdims=True)
    slab_t = jnp.where(sub == 0.0, e1,
             jnp.where(sub == 1.0, e2,
             jnp.where(sub == 2.0, r1,
             jnp.where(sub == 3.0, r2,
             jnp.where(sub == 4.0, pg * w1,
             jnp.where(sub == 5.0, pg * w2, 0.0))))))
    slab = jnp.concatenate([slab_t, jnp.zeros((ROUTE_LANES - EXP_PER_GROUP, t), jnp.float32)], axis=0).T
    return _pack_halves(h2), slab, slab_t, new_cnt


def _prompt_kernel(sink_ref, x_ref, gmix_ref, win_ref, gq_ref, gk_ref, gvg_ref, blk_ref, ws_ref, bs_ref,
                   wout_ref, gffn_ref, wr_ref, br_ref, tril_ref, cnt_in_ref,
                   x1_ref, h2_ref, slab_ref, slabt_ref, kwin_ref, vwin_ref, cnt_ref,
                   k2_scr, v2_scr, qn_scr, ua_scr, gvn_scr, cat_scr, cnt_scr):
    b = pl.program_id(0)
    t = pl.program_id(1)
    nt = pl.num_programs(1)
    tt = x_ref.shape[1]
    nblk = tt // WINDOW
    sub = tt // ROW_SPLITS

    @pl.when((b == 0) & (t == 0))
    def _():
        cnt_scr[...] = cnt_in_ref[...]

    @pl.when(t == 0)
    def _():
        k2_scr[:, 0:WINDOW, :] = jnp.zeros((N_KV_HEADS, WINDOW, LANES), jnp.bfloat16)
        v2_scr[:, 0:WINDOW, :] = jnp.zeros((N_KV_HEADS, WINDOW, LANES), jnp.bfloat16)

    pieces = [slice(s * sub, (s + 1) * sub) for s in range(ROW_SPLITS)]
    for s, rs in enumerate(pieces):
        ks = slice(WINDOW + s * sub, WINDOW + (s + 1) * sub)
        z = _in_proj(x_ref[0, rs, :], gmix_ref[...], win_ref[...])
        qn, kn, v, ua, gvn = _heads(z, gq_ref[...], gk_ref[...], gvg_ref[...], blk_ref[...])
        qn_scr[rs, :] = qn
        ua_scr[rs, :] = ua
        gvn_scr[rs, :] = gvn.astype(jnp.bfloat16)
        k0, k1 = _dup_halves(kn)
        v0, v1 = _dup_halves(v)
        k2_scr[0, ks, :] = k0
        k2_scr[1, ks, :] = k1
        v2_scr[0, ks, :] = v0
        v2_scr[1, ks, :] = v1
        if s == ROW_SPLITS - 1:
            @pl.when(t == nt - 1)
            def _():
                kwin_ref[0] = kn[sub - WINDOW:]
                vwin_ref[0] = v[sub - WINDOW:]

    rows = 4 * WINDOW
    band = 2 * WINDOW
    rr = lax.broadcasted_iota(jnp.int32, (rows, band), 0)
    kk = lax.broadcasted_iota(jnp.int32, (rows, band), 1)
    half = ((rr % WINDOW) >= CHUNK).astype(jnp.int32)
    allowed = (kk >= half * CHUNK) & (kk < (3 + half) * CHUNK)
    bias_mid = jnp.where(allowed, 0.0, NEG)
    bias_first = jnp.where(allowed & (kk >= WINDOW), 0.0, NEG)

    def attn_body(j, carry):
        r0 = pl.multiple_of(j * WINDOW, WINDOW)
        bias = jnp.where((t == 0) & (j == 0), bias_first, bias_mid)
        for g in range(N_KV_HEADS):
            c0 = g * 2 * LANES
            qa = qn_scr[pl.ds(r0, WINDOW), c0:c0 + LANES]
            qb = qn_scr[pl.ds(r0, WINDOW), c0 + LANES:c0 + 2 * LANES]
            k2 = k2_scr[g, pl.ds(r0, band), :]
            v2 = v2_scr[g, pl.ds(r0, band), :]
            pa, pb = _attn_block(qa, qb, k2, v2, bias, _sink_rows(sink_ref, g, WINDOW))
            cat_scr[pl.ds(r0, WINDOW), c0:c0 + LANES] = pa.astype(jnp.bfloat16)
            cat_scr[pl.ds(r0, WINDOW), c0 + LANES:c0 + 2 * LANES] = pb.astype(jnp.bfloat16)
        return carry

    lax.fori_loop(0, nblk, attn_body, 0, unroll=True)

    k2_scr[:, 0:WINDOW, :] = k2_scr[:, tt:tt + WINDOW, :]
    v2_scr[:, 0:WINDOW, :] = v2_scr[:, tt:tt + WINDOW, :]

    for h in range(GATE_HEADS):
        w = _causal_ws(ws_ref, h, MLP_CHUNK)
        bcol = bs_ref[:, h:h + 1]
        for c in range(tt // MLP_CHUNK):
            rs = slice(c * MLP_CHUNK, (c + 1) * MLP_CHUNK)
            cs = slice(h * GATE_DIM, (h + 1) * GATE_DIM)
            mix = jnp.dot(w, gvn_scr[rs, cs], preferred_element_type=jnp.float32) + bcol
            cat_scr[rs, Q_WIDTH + h * GATE_DIM:Q_WIDTH + (h + 1) * GATE_DIM] = (ua_scr[rs, cs] * mix).astype(jnp.bfloat16)

    cnt = cnt_scr[...]
    x1_next = _out_proj(cat_scr[pieces[0], :], x_ref[0, pieces[0], :], wout_ref[...])
    for s, rs in enumerate(pieces):
        x1 = x1_next
        if s + 1 < ROW_SPLITS:
            x1_next = _out_proj(cat_scr[pieces[s + 1], :], x_ref[0, pieces[s + 1], :], wout_ref[...])
        x1_ref[rs, :] = x1
        hp, slab, slab_t, cnt = _route(x1, gffn_ref[...], wr_ref[...], br_ref[...], tril_ref[...], cnt)
        h2_ref[rs, :] = hp
        slab_ref[rs, :] = slab
        slabt_ref[:, rs] = slab_t
    cnt_scr[...] = cnt
    cnt_ref[...] = cnt


def _sample_kernel(sink_ref, x_ref, ck_ref, cv_ref, gmix_ref, win_ref, gq_ref, gk_ref, gvg_ref, blk_ref, ws_ref,
                   bs_ref, wout_ref, gffn_ref, wr_ref, br_ref, tril_ref,
                   x1_ref, h2_ref, slab_ref, slabt_ref, kn_ref, v_ref, gvn_ref, cnt_ref,
                   ck2_scr, cv2_scr, k2_scr, v2_scr, qn_scr, ua_scr, cat_scr, *, n_seq, seq_len):
    x = x_ref[...]
    qn, kn, v, ua, gvn = _heads(_in_proj(x, gmix_ref[...], win_ref[...]), gq_ref[...], gk_ref[...], gvg_ref[...],
                                blk_ref[...])
    kn_ref[...] = kn
    v_ref[...] = v
    gvn_ref[...] = gvn
    qn_scr[...] = qn
    ua_scr[...] = ua
    for scr, val in ((k2_scr, kn), (v2_scr, v), (ck2_scr, ck_ref[...]), (cv2_scr, cv_ref[...])):
        a0, a1 = _dup_halves(val)
        scr[0] = a0
        scr[1] = a1

    ws = [_causal_ws(ws_ref, h, seq_len) for h in range(GATE_HEADS)]

    def seq_body(i, carry):
        r0 = pl.multiple_of(i * seq_len, seq_len)
        c0r = pl.multiple_of(i * WINDOW, WINDOW)
        for g in range(N_KV_HEADS):
            c0 = g * 2 * LANES
            qa = qn_scr[pl.ds(r0, seq_len), c0:c0 + LANES]
            qb = qn_scr[pl.ds(r0, seq_len), c0 + LANES:c0 + 2 * LANES]
            k2 = jnp.concatenate([ck2_scr[g, pl.ds(c0r, WINDOW), :], k2_scr[g, pl.ds(r0, seq_len), :]], axis=0)
            v2 = jnp.concatenate([cv2_scr[g, pl.ds(c0r, WINDOW), :], v2_scr[g, pl.ds(r0, seq_len), :]], axis=0)
            pa, pb = _attn_block(qa, qb, k2, v2, None, _sink_rows(sink_ref, g, seq_len))
            cat_scr[pl.ds(r0, seq_len), c0:c0 + LANES] = pa.astype(jnp.bfloat16)
            cat_scr[pl.ds(r0, seq_len), c0 + LANES:c0 + 2 * LANES] = pb.astype(jnp.bfloat16)
        for h in range(GATE_HEADS):
            cs = slice(h * GATE_DIM, (h + 1) * GATE_DIM)
            gv_h = gvn_ref[pl.ds(r0, seq_len), cs].astype(jnp.bfloat16)
            mix = jnp.dot(ws[h], gv_h, preferred_element_type=jnp.float32) + bs_ref[0:seq_len, h:h + 1]
            cat_scr[pl.ds(r0, seq_len), Q_WIDTH + h * GATE_DIM:Q_WIDTH + (h + 1) * GATE_DIM] = (
                ua_scr[pl.ds(r0, seq_len), cs] * mix).astype(jnp.bfloat16)
        return carry

    lax.fori_loop(0, n_seq, seq_body, 0)

    cnt0 = jnp.zeros((N_EXPERTS, LANES), jnp.float32)
    x1 = _out_proj(cat_scr[...], x, wout_ref[...])
    hp, slab, slab_t, new_cnt = _route(x1, gffn_ref[...], wr_ref[...], br_ref[...], tril_ref[...], cnt0)
    x1_ref[...] = x1
    h2_ref[...] = hp
    slab_ref[...] = slab
    slabt_ref[...] = slab_t
    cnt_ref[...] = new_cnt


def _expert_kernel(ts_ref, te_ref, cnt_ref, nu_ref, xa_hbm, xb_hbm, wg_ref, wu_ref, wd_ref, oa_hbm, ob_hbm,
                   wg_s, wu_s, wd_s, xa_buf, xb_buf, oa_buf, ob_buf, in_sem, out_sem):
    e = pl.program_id(0)
    nbuf, tm, _ = xa_buf.shape
    n_used = nu_ref[0]

    def rows_of(g):
        return pl.ds(pl.multiple_of(g * tm, tm), tm)

    def in_copies(g, slot):
        return (pltpu.make_async_copy(xa_hbm.at[rows_of(g)], xa_buf.at[slot], in_sem.at[0, slot]),
                pltpu.make_async_copy(xb_hbm.at[rows_of(g)], xb_buf.at[slot], in_sem.at[1, slot]))

    def out_copies(g, slot):
        return (pltpu.make_async_copy(oa_buf.at[slot], oa_hbm.at[rows_of(g)], out_sem.at[0, slot]),
                pltpu.make_async_copy(ob_buf.at[slot], ob_hbm.at[rows_of(g)], out_sem.at[1, slot]))

    @pl.when(e == 0)
    def _():
        for g in range(nbuf - 1):
            @pl.when(g < n_used)
            def _():
                for c in in_copies(g, g):
                    c.start()

    g_lo = ts_ref[e]
    g_hi = te_ref[e]

    @pl.when(g_hi > g_lo)
    def _():
        wg_s[...] = wg_ref[0].astype(jnp.bfloat16)
        wu_s[...] = wu_ref[0].astype(jnp.bfloat16)
        wd_s[...] = wd_ref[0].astype(jnp.bfloat16)

    def tile_body(g, slot):
        for c in in_copies(g, slot):
            c.wait()
        ahead = g + (nbuf - 1)
        ahead_slot = jnp.where(slot == 0, nbuf - 1, slot - 1)

        @pl.when(ahead < n_used)
        def _():
            for c in in_copies(ahead, ahead_slot):
                c.start()

        live = lax.broadcasted_iota(jnp.int32, (tm, xa_buf.shape[2]), 0) < cnt_ref[e] - (g - g_lo) * tm
        ha, la = _unpack_halves(jnp.where(live, xa_buf[slot], jnp.uint32(0)))
        hb, lb = _unpack_halves(jnp.where(live, xb_buf[slot], jnp.uint32(0)))
        xs = jnp.concatenate([ha, hb, la, lb], axis=1).astype(jnp.bfloat16)
        hg = jnp.dot(xs, wg_s[...], preferred_element_type=jnp.float32)
        hu = jnp.dot(xs, wu_s[...], preferred_element_type=jnp.float32)
        a = (jax.nn.silu(hg) * hu).astype(jnp.bfloat16)
        packed = _pack_halves(jnp.dot(a, wd_s[...], preferred_element_type=jnp.float32))
        q = packed.shape[1] // 2

        @pl.when(g >= nbuf)
        def _():
            for c in out_copies(g - nbuf, slot):
                c.wait()

        oa_buf[slot] = packed[:, :q]
        ob_buf[slot] = packed[:, q:]
        for c in out_copies(g, slot):
            c.start()
        return jnp.where(slot == nbuf - 1, 0, slot + 1)

    lax.fori_loop(g_lo, g_hi, tile_body, g_lo % nbuf)

    @pl.when(e == pl.num_programs(0) - 1)
    def _():
        for back in range(1, nbuf + 1):
            @pl.when(n_used >= back)
            def _():
                g = n_used - back
                for c in out_copies(g, g % nbuf):
                    c.wait()


def _pos_kernel(slabt_ref, offs_ref, pos0_ref, pos1_ref):
    st = slabt_ref[...]
    t = st.shape[1]
    row = lax.broadcasted_iota(jnp.int32, (N_EXPERTS, t), 0).astype(jnp.float32)
    offs = jnp.tile(offs_ref[...], (1, t // LANES))
    for s, out in enumerate((pos0_ref, pos1_ref)):
        first = jnp.sum(jnp.where(row == st[s:s + 1], offs, 0.0), axis=0, keepdims=True)
        out[...] = (first + st[2 + s:3 + s]).astype(jnp.int32)


def _combine_kernel(x1_ref, o1a_ref, o1b_ref, o2a_ref, o2b_ref, slab_ref, y_ref):
    slab = slab_ref[...]
    g1 = slab[:, 4:5]
    g2 = slab[:, 5:6]
    q = o1a_ref.shape[1]
    for c, (r1, r2) in enumerate(((o1a_ref, o2a_ref), (o1b_ref, o2b_ref))):
        h1, l1 = _unpack_halves(r1[...])
        h2, l2 = _unpack_halves(r2[...])
        hs = slice(c * q, (c + 1) * q)
        ls = slice(2 * q + c * q, 2 * q + (c + 1) * q)
        y_ref[:, hs] = x1_ref[:, hs] + (g1 * h1 + g2 * h2)
        y_ref[:, ls] = x1_ref[:, ls] + (g1 * l1 + g2 * l2)


def _sc_mesh():
    return plsc.VectorSubcoreMesh(core_axis_name="core", subcore_axis_name="subcore")


def _sc_scatter_rows(parts, pos0, pos1, n_rows):
    n_piece = parts[0].shape[1] // SC_COLS

    def sc_kernel(*refs):
        x_refs = refs[:len(parts)]
        i0_hbm, i1_hbm = refs[len(parts):len(parts) + 2]
        o_refs = refs[len(parts) + 2:]
        tok0 = 0
        for x_hbm, arr in zip(x_refs, parts):
            blk0 = tok0 // SC_WINDOW
            for c in range(n_piece):
                def body(x_vmem, i0_vmem, i1_vmem, o_hbm=o_refs[c]):
                    pltpu.sync_copy(x_vmem, o_hbm.at[i0_vmem.at[0]])
                    pltpu.sync_copy(x_vmem, o_hbm.at[i1_vmem.at[0]])

                pltpu.emit_pipeline(
                    body,
                    grid=(arr.shape[0] // SC_WINDOW,),
                    in_specs=[pl.BlockSpec((SC_WINDOW, SC_COLS), lambda i, c=c: (i, c)),
                              pl.BlockSpec((1, SC_WINDOW), lambda i, blk0=blk0: (0, blk0 + i)),
                              pl.BlockSpec((1, SC_WINDOW), lambda i, blk0=blk0: (0, blk0 + i))],
                    out_specs=[],
                    core_axis_name=("core", "subcore"),
                    dimension_semantics=(pltpu.PARALLEL,),
                )(x_hbm, i0_hbm, i1_hbm)
            tok0 += arr.shape[0]

    piece = jax.ShapeDtypeStruct((n_rows, SC_COLS), jnp.uint32)
    return pl.kernel(sc_kernel, out_type=(piece,) * n_piece, mesh=_sc_mesh(), name="scatter_rows")(
        *parts, pos0, pos1)


def _sc_gather_rows(pieces, pos_list):
    n_tok = pos_list[0].shape[1]

    def sc_kernel(*refs):
        s_refs = refs[:len(pieces)]
        i_refs = refs[len(pieces):len(pieces) + len(pos_list)]
        o_refs = refs[len(pieces) + len(pos_list):]
        k = 0
        for i_hbm in i_refs:
            for s_hbm in s_refs:
                def body(i_vmem, o_vmem, s_hbm=s_hbm):
                    pltpu.sync_copy(s_hbm.at[i_vmem.at[0]], o_vmem)

                pltpu.emit_pipeline(
                    body,
                    grid=(n_tok // SC_WINDOW,),
                    in_specs=[pl.BlockSpec((1, SC_WINDOW), lambda i: (0, i))],
                    out_specs=[pl.BlockSpec((SC_WINDOW, SC_COLS), lambda i: (i, 0))],
                    core_axis_name=("core", "subcore"),
                    dimension_semantics=(pltpu.PARALLEL,),
                )(i_hbm, o_refs[k])
                k += 1

    out = jax.ShapeDtypeStruct((n_tok, SC_COLS), jnp.uint32)
    outs = pl.kernel(sc_kernel, out_type=(out,) * (len(pieces) * len(pos_list)), mesh=_sc_mesh(),
                     name="gather_rows")(*pieces, *pos_list)
    return [outs[j * len(pieces):(j + 1) * len(pieces)] for j in range(len(pos_list))]


def _const_spec(shape, single_buffer=False):
    nd = len(shape)
    mode = pl.Buffered(1) if single_buffer else None
    return pl.BlockSpec(shape, lambda *_: (0,) * nd, pipeline_mode=mode)


def kernel(x_prompt, x_sample, cache_attn_k, cache_attn_v, g_mix, w_in, g_q, g_k, g_v, attn_sinks, w_s, b_s,
           w_out, g_ffn, w_coarse, b_coarse, w_fine, b_fine, w_gate, w_up, w_down):
    nb, seq, d = x_prompt.shape
    ns, slen, _ = x_sample.shape
    n_p = nb * seq
    n_s = ns * slen
    n_tok = n_p + n_s
    tt = TOKEN_TILE
    assert seq % tt == 0 and n_s % tt == 0 and d == D_MODEL
    nt = seq // tt
    bf = jnp.bfloat16
    f32 = jnp.float32

    l = 0
    gmix = g_mix[l].reshape(1, d)
    win = w_in[l].astype(bf)
    gq = (jnp.tile(g_q[l], N_Q_HEADS) * (HEAD_DIM ** -0.5)).reshape(1, Q_WIDTH)
    gk = jnp.tile(g_k[l], N_KV_HEADS).reshape(1, KV_WIDTH)
    gvg = g_v[l].reshape(1, GATE_WIDTH)
    sinks = attn_sinks[l].reshape(N_Q_HEADS).astype(f32)
    ws = w_s[l].astype(bf)
    bs = b_s[l].T
    wout = w_out[l].astype(bf)
    gffn = g_ffn[l].reshape(1, d)
    wr = jnp.concatenate([w_coarse[l], jnp.transpose(w_fine[l], (1, 0, 2)).reshape(d, N_EXPERTS),
                          jnp.zeros((d, ROUTE_LANES - N_GROUPS - N_EXPERTS), f32)], axis=1)
    wr = wr.astype(bf).T
    br = jnp.concatenate([b_coarse[l], b_fine[l].reshape(-1),
                          jnp.zeros((ROUTE_LANES - N_GROUPS - N_EXPERTS,), f32)])
    br = jnp.broadcast_to(br[:, None], (ROUTE_LANES, LANES))
    cnt_shape = (N_EXPERTS, LANES)
    ii = jnp.arange(LANES)
    blk = (ii[:, None] // HEAD_DIM == ii[None, :] // HEAD_DIM).astype(bf)
    u32 = jnp.uint32
    dh = d // 2

    def strict_tril(n):
        r = jnp.arange(n)
        return (r[:, None] < r[None, :]).astype(bf)

    weight_args = (gmix, win, gq, gk, gvg, blk, ws, bs, wout, gffn, wr, br)
    weight_specs = [_const_spec(a.shape, single_buffer=True) for a in weight_args]
    smem_spec = pl.BlockSpec(memory_space=pltpu.SMEM)

    xs2 = x_sample.reshape(n_s, d)
    ck = cache_attn_k[l].reshape(ns * WINDOW, KV_WIDTH)
    cv = cache_attn_v[l].reshape(ns * WINDOW, KV_WIDTH)
    tok_out = lambda n, w, dt: jax.ShapeDtypeStruct((n, w), dt)
    x1_s, h2_s, slab_s, slabt_s, kn_s, v_s, gvn_s, cnt_s = pl.pallas_call(
        functools.partial(_sample_kernel, n_seq=ns, seq_len=slen),
        grid=(1,),
        in_specs=[smem_spec, _const_spec((n_s, d)), _const_spec(ck.shape), _const_spec(cv.shape)]
                 + weight_specs + [_const_spec((n_s, n_s))],
        out_specs=[_const_spec((n_s, d)), _const_spec((n_s, dh)), _const_spec((n_s, ROUTE_LANES)),
                   _const_spec((EXP_PER_GROUP, n_s)),
                   _const_spec((n_s, KV_WIDTH)), _const_spec((n_s, KV_WIDTH)), _const_spec((n_s, GATE_WIDTH)),
                   _const_spec(cnt_shape)],
        out_shape=[tok_out(n_s, d, f32), tok_out(n_s, dh, u32), tok_out(n_s, ROUTE_LANES, f32),
                   jax.ShapeDtypeStruct((EXP_PER_GROUP, n_s), f32),
                   jax.ShapeDtypeStruct((n_s, KV_WIDTH), f32), jax.ShapeDtypeStruct((n_s, KV_WIDTH), f32),
                   jax.ShapeDtypeStruct((n_s, GATE_WIDTH), f32), jax.ShapeDtypeStruct(cnt_shape, f32)],
        scratch_shapes=[pltpu.VMEM((N_KV_HEADS, ns * WINDOW, LANES), bf), pltpu.VMEM((N_KV_HEADS, ns * WINDOW, LANES), bf),
                        pltpu.VMEM((N_KV_HEADS, n_s, LANES), bf), pltpu.VMEM((N_KV_HEADS, n_s, LANES), bf),
                        pltpu.VMEM((n_s, Q_WIDTH), bf), pltpu.VMEM((n_s, GATE_WIDTH), f32),
                        pltpu.VMEM((n_s, d), bf)],
        compiler_params=pltpu.CompilerParams(dimension_semantics=("arbitrary",), vmem_limit_bytes=VMEM_LIMIT),
        name="mixer_sample",
    )(sinks, xs2, ck, cv, *weight_args, strict_tril(n_s))

    x1_p, h2_p, slab_p, slabt_p, kwin, vwin, cnt = pl.pallas_call(
        _prompt_kernel,
        grid=(nb, nt),
        in_specs=[smem_spec, pl.BlockSpec((1, tt, d), lambda b, t: (b, t, 0))] + weight_specs
                 + [_const_spec((tt // ROW_SPLITS, tt // ROW_SPLITS)), _const_spec(cnt_shape)],
        out_specs=[pl.BlockSpec((tt, d), lambda b, t: (b * nt + t, 0)),
                   pl.BlockSpec((tt, dh), lambda b, t: (b * nt + t, 0)),
                   pl.BlockSpec((tt, ROUTE_LANES), lambda b, t: (b * nt + t, 0)),
                   pl.BlockSpec((EXP_PER_GROUP, tt), lambda b, t: (0, b * nt + t)),
                   pl.BlockSpec((1, WINDOW, KV_WIDTH), lambda b, t: (b, 0, 0)),
                   pl.BlockSpec((1, WINDOW, KV_WIDTH), lambda b, t: (b, 0, 0)),
                   _const_spec(cnt_shape)],
        out_shape=[tok_out(n_p, d, f32), tok_out(n_p, dh, u32), tok_out(n_p, ROUTE_LANES, f32),
                   jax.ShapeDtypeStruct((EXP_PER_GROUP, n_p), f32),
                   jax.ShapeDtypeStruct((nb, WINDOW, KV_WIDTH), f32),
                   jax.ShapeDtypeStruct((nb, WINDOW, KV_WIDTH), f32),
                   jax.ShapeDtypeStruct(cnt_shape, f32)],
        scratch_shapes=[pltpu.VMEM((N_KV_HEADS, tt + WINDOW, LANES), bf),
                        pltpu.VMEM((N_KV_HEADS, tt + WINDOW, LANES), bf),
                        pltpu.VMEM((tt, Q_WIDTH), bf), pltpu.VMEM((tt, GATE_WIDTH), f32),
                        pltpu.VMEM((tt, GATE_WIDTH), bf), pltpu.VMEM((tt, d), bf),
                        pltpu.VMEM(cnt_shape, f32)],
        compiler_params=pltpu.CompilerParams(dimension_semantics=("arbitrary", "arbitrary"),
                                             vmem_limit_bytes=VMEM_LIMIT),
        name="mixer_prompt",
    )(sinks, x_prompt, *weight_args, strict_tril(tt // ROW_SPLITS), cnt_s)
    slabt = jnp.concatenate([slabt_p, slabt_s], axis=1)

    tm = EXPERT_TILE
    counts = cnt[:, 0].astype(jnp.int32)
    tiles = (counts + tm - 1) // tm
    tile_end = jnp.cumsum(tiles)
    offs = (tile_end - tiles) * tm
    n_used = tile_end[-1]
    n_tiles = (2 * n_tok) // tm + N_EXPERTS
    n_rows = n_tiles * tm
    lane_tiles = n_tok // LANES
    pos_blk = LANES * max(k for k in range(1, POS_BLOCK_TILES + 1) if lane_tiles % k == 0)
    offs_b = jnp.broadcast_to(offs.astype(f32)[:, None], cnt_shape)
    pos_row = jax.ShapeDtypeStruct((1, n_tok), jnp.int32)
    pos0, pos1 = pl.pallas_call(
        _pos_kernel,
        grid=(n_tok // pos_blk,),
        in_specs=[pl.BlockSpec((EXP_PER_GROUP, pos_blk), lambda i: (0, i)), _const_spec(cnt_shape)],
        out_specs=[pl.BlockSpec((1, pos_blk), lambda i: (0, i)), pl.BlockSpec((1, pos_blk), lambda i: (0, i))],
        out_shape=[pos_row, pos_row],
        compiler_params=pltpu.CompilerParams(dimension_semantics=("arbitrary",)),
        name="sorted_pos",
    )(slabt, offs_b)

    assert dh == 2 * SC_COLS and n_p % SC_WINDOW == 0 and n_s % SC_WINDOW == 0
    xs_a, xs_b = _sc_scatter_rows([h2_p, h2_s], pos0, pos1, n_rows)

    w_map = lambda e, *_: (e, 0, 0)
    hbm = pl.BlockSpec(memory_space=pl.ANY)
    piece = jax.ShapeDtypeStruct((n_rows, SC_COLS), u32)
    tile_buf = pltpu.VMEM((EXPERT_BUFFERS, tm, SC_COLS), u32)
    tile_sems = pltpu.SemaphoreType.DMA((2, EXPERT_BUFFERS))
    out_a, out_b = pl.pallas_call(
        _expert_kernel,
        grid_spec=pltpu.PrefetchScalarGridSpec(
            num_scalar_prefetch=4,
            grid=(N_EXPERTS,),
            in_specs=[hbm, hbm,
                      pl.BlockSpec((1, d, D_EXPERT), w_map),
                      pl.BlockSpec((1, d, D_EXPERT), w_map),
                      pl.BlockSpec((1, D_EXPERT, d), w_map)],
            out_specs=[hbm, hbm],
            scratch_shapes=[pltpu.VMEM((d, D_EXPERT), bf), pltpu.VMEM((d, D_EXPERT), bf),
                            pltpu.VMEM((D_EXPERT, d), bf),
                            tile_buf, tile_buf, tile_buf, tile_buf, tile_sems, tile_sems]),
        out_shape=[piece, piece],
        compiler_params=pltpu.CompilerParams(dimension_semantics=("arbitrary",), vmem_limit_bytes=VMEM_LIMIT),
        name="experts",
    )((tile_end - tiles).astype(jnp.int32), tile_end.astype(jnp.int32), counts,
      n_used.reshape(1).astype(jnp.int32), xs_a, xs_b, w_gate[l], w_up[l], w_down[l])

    (o1a, o1b), (o2a, o2b) = _sc_gather_rows([out_a, out_b], [pos0, pos1])

    def combine(x1, slab, blk0):
        n = x1.shape[0]
        tok = lambda i: (i, 0)
        off = lambda i: (blk0 + i, 0)
        return pl.pallas_call(
            _combine_kernel,
            grid=(n // tt,),
            in_specs=[pl.BlockSpec((tt, d), tok)] + [pl.BlockSpec((tt, SC_COLS), off)] * 4
                     + [pl.BlockSpec((tt, ROUTE_LANES), tok)],
            out_specs=pl.BlockSpec((tt, d), tok),
            out_shape=jax.ShapeDtypeStruct((n, d), f32),
            compiler_params=pltpu.CompilerParams(dimension_semantics=("arbitrary",)),
            name="combine",
        )(x1, o1a, o1b, o2a, o2b, slab)

    y_p = combine(x1_p, slab_p, 0).reshape(nb, seq, d)
    y_s = combine(x1_s, slab_s, n_p // tt).reshape(ns, slen, d)

    kv_shape = (1, nb, WINDOW, N_KV_HEADS, HEAD_DIM)
    new_k_p = kwin.reshape(kv_shape)
    new_v_p = vwin.reshape(kv_shape)
    keep = WINDOW - slen
    ck4 = cache_attn_k[l][:, WINDOW - keep:]
    cv4 = cache_attn_v[l][:, WINDOW - keep:]
    new_k_s = jnp.concatenate([ck4, kn_s.reshape(ns, slen, N_KV_HEADS, HEAD_DIM)], axis=1)[None]
    new_v_s = jnp.concatenate([cv4, v_s.reshape(ns, slen, N_KV_HEADS, HEAD_DIM)], axis=1)[None]
    new_gv_s = gvn_s.reshape(1, ns, slen, GATE_HEADS, GATE_DIM)
    return (y_p, y_s, new_k_p, new_v_p, new_k_s, new_v_s, new_gv_s)
```

```python
import functools

import jax
import jax.numpy as jnp
from jax import lax
from jax.experimental import pallas as pl
from jax.experimental.pallas import tpu as pltpu
from jax.experimental.pallas import tpu_sc as plsc

D_MODEL = 1024
HEAD_DIM = 64
N_Q_HEADS = 8
N_KV_HEADS = 2
Q_WIDTH = N_Q_HEADS * HEAD_DIM
KV_WIDTH = N_KV_HEADS * HEAD_DIM
GATE_HEADS = 4
GATE_DIM = 128
GATE_WIDTH = GATE_HEADS * GATE_DIM
PROJ_COLS = Q_WIDTH + 2 * KV_WIDTH + 2 * GATE_WIDTH
CHUNK = 64
WINDOW = 128
MLP_CHUNK = 128
N_GROUPS = 8
EXP_PER_GROUP = 8
N_EXPERTS = N_GROUPS * EXP_PER_GROUP
D_EXPERT = 512
EPS = 1e-6

LANES = 128
ROUTE_LANES = 128
FINE_LANE0 = N_GROUPS
NEG = -1e30

TOKEN_TILE = 512
ROW_SPLITS = 2
EXPERT_TILE = 512
EXPERT_BUFFERS = 3
POS_BLOCK_TILES = 64
SC_WINDOW = 128
SC_COLS = 256
VMEM_LIMIT = 40 * 1024 * 1024


def _pack_halves(x):
    w = x.shape[1] // 2
    b = lax.bitcast_convert_type(x.astype(jnp.bfloat16).astype(jnp.float32), jnp.uint32)
    return (b[:, :w] & jnp.uint32(0xFFFF0000)) | (b[:, w:] >> 16)


def _unpack_halves(p):
    hi = lax.bitcast_convert_type(p & jnp.uint32(0xFFFF0000), jnp.float32)
    lo = lax.bitcast_convert_type(p << 16, jnp.float32)
    return hi, lo


def _rms(x, eps=EPS):
    return x * lax.rsqrt(jnp.mean(x * x, axis=-1, keepdims=True) + eps)


def _in_proj(x, gmix, win):
    h = (_rms(x) * gmix).astype(jnp.bfloat16)
    return jnp.dot(h, win, preferred_element_type=jnp.float32)


def _heads(z, gq, gk, gvg, blk):
    qk = z[:, :Q_WIDTH + KV_WIDTH]
    outs = []
    for j in range((Q_WIDTH + KV_WIDTH) // LANES):
        zj = qk[:, j * LANES:(j + 1) * LANES]
        ss = jnp.dot((zj * zj).astype(jnp.bfloat16), blk, preferred_element_type=jnp.float32)
        outs.append(zj * lax.rsqrt(ss * (1.0 / HEAD_DIM) + EPS))
    qn = jnp.concatenate(outs[:Q_WIDTH // LANES], axis=-1) * gq
    kn = outs[-1] * gk
    v = z[:, Q_WIDTH + KV_WIDTH:Q_WIDTH + 2 * KV_WIDTH]
    u0 = Q_WIDTH + 2 * KV_WIDTH
    ua = jax.nn.gelu(z[:, u0:u0 + GATE_WIDTH])
    ga = jax.nn.gelu(z[:, u0 + GATE_WIDTH:])
    gvn = jnp.concatenate(
        [_rms(ga[:, i * GATE_DIM:(i + 1) * GATE_DIM]) for i in range(GATE_HEADS)], axis=-1) * gvg
    return qn.astype(jnp.bfloat16), kn, v, ua, gvn


def _dup_halves(a):
    lo = lax.broadcasted_iota(jnp.int32, a.shape, 1) < HEAD_DIM
    r = pltpu.roll(a, HEAD_DIM, axis=1)
    return (jnp.where(lo, a, r).astype(jnp.bfloat16), jnp.where(lo, r, a).astype(jnp.bfloat16))


def _attn_block(qa, qb, k2, v2, bias, sk):
    r = qa.shape[0]
    lo = lax.broadcasted_iota(jnp.int32, (r, LANES), 1) < HEAD_DIM
    zero = jnp.zeros_like(qa)
    qs = jnp.concatenate([jnp.where(lo, qa, zero), jnp.where(lo, zero, qa),
                          jnp.where(lo, qb, zero), jnp.where(lo, zero, qb)], axis=0)
    s = lax.dot_general(qs, k2, (((1,), (1,)), ((), ())), preferred_element_type=jnp.float32)
    if bias is not None:
        s = s + bias
    m = jnp.maximum(jnp.max(s, axis=-1, keepdims=True), sk)
    e = jnp.exp(s - m)
    den = jnp.sum(e, axis=-1, keepdims=True) + jnp.exp(sk - m)
    o = jnp.dot(e.astype(jnp.bfloat16), v2, preferred_element_type=jnp.float32) * (1.0 / den)
    pa = jnp.where(lo, o[0:r], o[r:2 * r])
    pb = jnp.where(lo, o[2 * r:3 * r], o[3 * r:4 * r])
    return pa, pb


def _sink_rows(sink_ref, g, r):
    row = lax.broadcasted_iota(jnp.int32, (4 * r, 1), 0)
    s0, s1, s2, s3 = (sink_ref[4 * g + i] for i in range(4))
    return jnp.where(row < r, s0, jnp.where(row < 2 * r, s1, jnp.where(row < 3 * r, s2, s3)))


def _causal_ws(ws_ref, h, n):
    w = ws_ref[h][:n, :n]
    keep = (lax.broadcasted_iota(jnp.int32, (n, n), 0) >= lax.broadcasted_iota(jnp.int32, (n, n), 1))
    return jnp.where(keep, w, jnp.zeros_like(w))


def _out_proj(cat, x, wout):
    return x + jnp.dot(cat, wout, preferred_element_type=jnp.float32)


def _route(x1, gffn, wrt, brt, triu, cnt):
    h2 = _rms(x1) * gffn
    t = x1.shape[0]
    reps = t // LANES
    lgt = lax.dot_general(wrt, h2.astype(jnp.bfloat16), (((1,), (1,)), ((), ())),
                          preferred_element_type=jnp.float32) + jnp.tile(brt, (1, reps))
    ng = float(EXP_PER_GROUP)
    sub = lax.broadcasted_iota(jnp.int32, (EXP_PER_GROUP, t), 0).astype(jnp.float32)
    c = lgt[0:N_GROUPS]
    mc = jnp.max(c, axis=0, keepdims=True)
    grp = jnp.min(jnp.where(c == mc, sub, ng), axis=0, keepdims=True)
    pg = 1.0 / jnp.sum(jnp.exp(c - mc), axis=0, keepdims=True)
    lf = lgt[FINE_LANE0:FINE_LANE0 + EXP_PER_GROUP]
    for g in range(1, N_GROUPS):
        r0 = FINE_LANE0 + g * EXP_PER_GROUP
        lf = jnp.where(grp == float(g), lgt[r0:r0 + EXP_PER_GROUP], lf)
    v1 = jnp.max(lf, axis=0, keepdims=True)
    i1 = jnp.min(jnp.where(lf == v1, sub, ng), axis=0, keepdims=True)
    lf2 = jnp.where(sub == i1, -jnp.inf, lf)
    v2 = jnp.max(lf2, axis=0, keepdims=True)
    i2 = jnp.min(jnp.where(lf2 == v2, sub, ng), axis=0, keepdims=True)
    tt = jnp.exp(v2 - v1)
    w1 = 1.0 / (1.0 + tt)
    w2 = tt * w1
    e1 = grp * ng + i1
    e2 = grp * ng + i2
    row = lax.broadcasted_iota(jnp.int32, (N_EXPERTS, t), 0).astype(jnp.float32)
    sel1 = row == e1
    sel2 = row == e2
    oh = jnp.where(sel1 | sel2, 1.0, 0.0)
    cum = jnp.dot(oh.astype(jnp.bfloat16), triu, preferred_element_type=jnp.float32) + jnp.tile(cnt, (1, reps))
    r1 = jnp.sum(jnp.where(sel1, cum, 0.0), axis=0, keepdims=True)
    r2 = jnp.sum(jnp.where(sel2, cum, 0.0), axis=0, keepdims=True)
    new_cnt = cnt + jnp.sum(oh, axis=1, keepdims=True)
    slab_t = jnp.where(sub == 0.0, e1,
             jnp.where(sub == 1.0, e2,
             jnp.where(sub == 2.0, r1,
             jnp.where(sub == 3.0, r2,
             jnp.where(sub == 4.0, pg * w1,
             jnp.where(sub == 5.0, pg * w2, 0.0))))))
    return _pack_halves(h2), slab_t, new_cnt


def _prompt_kernel(sink_ref, x_ref, gmix_ref, win_ref, gq_ref, gk_ref, gvg_ref, blk_ref, ws_ref, bs_ref,
                   wout_ref, gffn_ref, wr_ref, br_ref, tril_ref, cnt_in_ref,
                   x1_ref, h2_ref, slabt_ref, kwin_ref, vwin_ref, cnt_ref,
                   k2_scr, v2_scr, qn_scr, ua_scr, gvn_scr, cat_scr, cnt_scr):
    b = pl.program_id(0)
    t = pl.program_id(1)
    nt = pl.num_programs(1)
    tt = x_ref.shape[1]
    nblk = tt // WINDOW
    sub = tt // ROW_SPLITS

    @pl.when((b == 0) & (t == 0))
    def _():
        cnt_scr[...] = cnt_in_ref[...]

    @pl.when(t == 0)
    def _():
        k2_scr[:, 0:WINDOW, :] = jnp.zeros((N_KV_HEADS, WINDOW, LANES), jnp.bfloat16)
        v2_scr[:, 0:WINDOW, :] = jnp.zeros((N_KV_HEADS, WINDOW, LANES), jnp.bfloat16)

    pieces = [slice(s * sub, (s + 1) * sub) for s in range(ROW_SPLITS)]
    for s, rs in enumerate(pieces):
        ks = slice(WINDOW + s * sub, WINDOW + (s + 1) * sub)
        z = _in_proj(x_ref[0, rs, :], gmix_ref[...], win_ref[...])
        qn, kn, v, ua, gvn = _heads(z, gq_ref[...], gk_ref[...], gvg_ref[...], blk_ref[...])
        qn_scr[rs, :] = qn
        ua_scr[rs, :] = ua
        gvn_scr[rs, :] = gvn.astype(jnp.bfloat16)
        k0, k1 = _dup_halves(kn)
        v0, v1 = _dup_halves(v)
        k2_scr[0, ks, :] = k0
        k2_scr[1, ks, :] = k1
        v2_scr[0, ks, :] = v0
        v2_scr[1, ks, :] = v1
        if s == ROW_SPLITS - 1:
            @pl.when(t == nt - 1)
            def _():
                kwin_ref[0] = kn[sub - WINDOW:]
                vwin_ref[0] = v[sub - WINDOW:]

    rows = 4 * WINDOW
    band = 2 * WINDOW
    rr = lax.broadcasted_iota(jnp.int32, (rows, band), 0)
    kk = lax.broadcasted_iota(jnp.int32, (rows, band), 1)
    half = ((rr % WINDOW) >= CHUNK).astype(jnp.int32)
    allowed = (kk >= half * CHUNK) & (kk < (3 + half) * CHUNK)
    bias_mid = jnp.where(allowed, 0.0, NEG)
    bias_first = jnp.where(allowed & (kk >= WINDOW), 0.0, NEG)

    def attn_body(j, carry):
        r0 = pl.multiple_of(j * WINDOW, WINDOW)
        bias = jnp.where((t == 0) & (j == 0), bias_first, bias_mid)
        for g in range(N_KV_HEADS):
            c0 = g * 2 * LANES
            qa = qn_scr[pl.ds(r0, WINDOW), c0:c0 + LANES]
            qb = qn_scr[pl.ds(r0, WINDOW), c0 + LANES:c0 + 2 * LANES]
            k2 = k2_scr[g, pl.ds(r0, band), :]
            v2 = v2_scr[g, pl.ds(r0, band), :]
            pa, pb = _attn_block(qa, qb, k2, v2, bias, _sink_rows(sink_ref, g, WINDOW))
            cat_scr[pl.ds(r0, WINDOW), c0:c0 + LANES] = pa.astype(jnp.bfloat16)
            cat_scr[pl.ds(r0, WINDOW), c0 + LANES:c0 + 2 * LANES] = pb.astype(jnp.bfloat16)
        return carry

    lax.fori_loop(0, nblk, attn_body, 0, unroll=True)

    k2_scr[:, 0:WINDOW, :] = k2_scr[:, tt:tt + WINDOW, :]
    v2_scr[:, 0:WINDOW, :] = v2_scr[:, tt:tt + WINDOW, :]

    for h in range(GATE_HEADS):
        w = _causal_ws(ws_ref, h, MLP_CHUNK)
        bcol = bs_ref[:, h:h + 1]
        for c in range(tt // MLP_CHUNK):
            rs = slice(c * MLP_CHUNK, (c + 1) * MLP_CHUNK)
            cs = slice(h * GATE_DIM, (h + 1) * GATE_DIM)
            mix = jnp.dot(w, gvn_scr[rs, cs], preferred_element_type=jnp.float32) + bcol
            cat_scr[rs, Q_WIDTH + h * GATE_DIM:Q_WIDTH + (h + 1) * GATE_DIM] = (ua_scr[rs, cs] * mix).astype(jnp.bfloat16)

    cnt = cnt_scr[...]
    x1_next = _out_proj(cat_scr[pieces[0], :], x_ref[0, pieces[0], :], wout_ref[...])
    for s, rs in enumerate(pieces):
        x1 = x1_next
        if s + 1 < ROW_SPLITS:
            x1_next = _out_proj(cat_scr[pieces[s + 1], :], x_ref[0, pieces[s + 1], :], wout_ref[...])
        x1_ref[rs, :] = x1
        hp, slab_t, cnt = _route(x1, gffn_ref[...], wr_ref[...], br_ref[...], tril_ref[...], cnt)
        h2_ref[rs, :] = hp
        slabt_ref[:, rs] = slab_t
    cnt_scr[...] = cnt
    cnt_ref[...] = cnt


def _sample_kernel(sink_ref, x_ref, ck_ref, cv_ref, gmix_ref, win_ref, gq_ref, gk_ref, gvg_ref, blk_ref, ws_ref,
                   bs_ref, wout_ref, gffn_ref, wr_ref, br_ref, tril_ref,
                   x1_ref, h2_ref, slabt_ref, kn_ref, v_ref, gvn_ref, cnt_ref,
                   ck2_scr, cv2_scr, k2_scr, v2_scr, qn_scr, ua_scr, cat_scr, *, n_seq, seq_len):
    x = x_ref[...]
    qn, kn, v, ua, gvn = _heads(_in_proj(x, gmix_ref[...], win_ref[...]), gq_ref[...], gk_ref[...], gvg_ref[...],
                                blk_ref[...])
    kn_ref[...] = kn
    v_ref[...] = v
    gvn_ref[...] = gvn
    qn_scr[...] = qn
    ua_scr[...] = ua
    for scr, val in ((k2_scr, kn), (v2_scr, v), (ck2_scr, ck_ref[...]), (cv2_scr, cv_ref[...])):
        a0, a1 = _dup_halves(val)
        scr[0] = a0
        scr[1] = a1

    ws = [_causal_ws(ws_ref, h, seq_len) for h in range(GATE_HEADS)]

    def seq_body(i, carry):
        r0 = pl.multiple_of(i * seq_len, seq_len)
        c0r = pl.multiple_of(i * WINDOW, WINDOW)
        for g in range(N_KV_HEADS):
            c0 = g * 2 * LANES
            qa = qn_scr[pl.ds(r0, seq_len), c0:c0 + LANES]
            qb = qn_scr[pl.ds(r0, seq_len), c0 + LANES:c0 + 2 * LANES]
            k2 = jnp.concatenate([ck2_scr[g, pl.ds(c0r, WINDOW), :], k2_scr[g, pl.ds(r0, seq_len), :]], axis=0)
            v2 = jnp.concatenate([cv2_scr[g, pl.ds(c0r, WINDOW), :], v2_scr[g, pl.ds(r0, seq_len), :]], axis=0)
            pa, pb = _attn_block(qa, qb, k2, v2, None, _sink_rows(sink_ref, g, seq_len))
            cat_scr[pl.ds(r0, seq_len), c0:c0 + LANES] = pa.astype(jnp.bfloat16)
            cat_scr[pl.ds(r0, seq_len), c0 + LANES:c0 + 2 * LANES] = pb.astype(jnp.bfloat16)
        for h in range(GATE_HEADS):
            cs = slice(h * GATE_DIM, (h + 1) * GATE_DIM)
            gv_h = gvn_ref[pl.ds(r0, seq_len), cs].astype(jnp.bfloat16)
            mix = jnp.dot(ws[h], gv_h, preferred_element_type=jnp.float32) + bs_ref[0:seq_len, h:h + 1]
            cat_scr[pl.ds(r0, seq_len), Q_WIDTH + h * GATE_DIM:Q_WIDTH + (h + 1) * GATE_DIM] = (
                ua_scr[pl.ds(r0, seq_len), cs] * mix).astype(jnp.bfloat16)
        return carry

    lax.fori_loop(0, n_seq, seq_body, 0)

    cnt0 = jnp.zeros((N_EXPERTS, LANES), jnp.float32)
    x1 = _out_proj(cat_scr[...], x, wout_ref[...])
    hp, slab_t, new_cnt = _route(x1, gffn_ref[...], wr_ref[...], br_ref[...], tril_ref[...], cnt0)
    x1_ref[...] = x1
    h2_ref[...] = hp
    slabt_ref[...] = slab_t
    cnt_ref[...] = new_cnt


def _expert_kernel(ts_ref, te_ref, cnt_ref, nu_ref, xa_hbm, xb_hbm, wg_ref, wu_ref, wd_ref, oa_hbm, ob_hbm,
                   wg_s, wu_s, wd_s, xa_buf, xb_buf, oa_buf, ob_buf, in_sem, out_sem):
    e = pl.program_id(0)
    nbuf, tm, _ = xa_buf.shape
    n_used = nu_ref[0]

    def rows_of(g):
        return pl.ds(pl.multiple_of(g * tm, tm), tm)

    def in_copies(g, slot):
        return (pltpu.make_async_copy(xa_hbm.at[rows_of(g)], xa_buf.at[slot], in_sem.at[0, slot]),
                pltpu.make_async_copy(xb_hbm.at[rows_of(g)], xb_buf.at[slot], in_sem.at[1, slot]))

    def out_copies(g, slot):
        return (pltpu.make_async_copy(oa_buf.at[slot], oa_hbm.at[rows_of(g)], out_sem.at[0, slot]),
                pltpu.make_async_copy(ob_buf.at[slot], ob_hbm.at[rows_of(g)], out_sem.at[1, slot]))

    @pl.when(e == 0)
    def _():
        for g in range(nbuf - 1):
            @pl.when(g < n_used)
            def _():
                for c in in_copies(g, g):
                    c.start()

    g_lo = ts_ref[e]
    g_hi = te_ref[e]

    @pl.when(g_hi > g_lo)
    def _():
        wg_s[...] = wg_ref[0].astype(jnp.bfloat16)
        wu_s[...] = wu_ref[0].astype(jnp.bfloat16)
        wd_s[...] = wd_ref[0].astype(jnp.bfloat16)

    def tile_body(g, slot):
        for c in in_copies(g, slot):
            c.wait()
        ahead = g + (nbuf - 1)
        ahead_slot = jnp.where(slot == 0, nbuf - 1, slot - 1)

        @pl.when(ahead < n_used)
        def _():
            for c in in_copies(ahead, ahead_slot):
                c.start()

        live = lax.broadcasted_iota(jnp.int32, (tm, xa_buf.shape[2]), 0) < cnt_ref[e] - (g - g_lo) * tm
        ha, la = _unpack_halves(jnp.where(live, xa_buf[slot], jnp.uint32(0)))
        hb, lb = _unpack_halves(jnp.where(live, xb_buf[slot], jnp.uint32(0)))
        xs = jnp.concatenate([ha, hb, la, lb], axis=1).astype(jnp.bfloat16)
        hg = jnp.dot(xs, wg_s[...], preferred_element_type=jnp.float32)
        hu = jnp.dot(xs, wu_s[...], preferred_element_type=jnp.float32)
        a = (jax.nn.silu(hg) * hu).astype(jnp.bfloat16)
        packed = _pack_halves(jnp.dot(a, wd_s[...], preferred_element_type=jnp.float32))
        q = packed.shape[1] // 2

        @pl.when(g >= nbuf)
        def _():
            for c in out_copies(g - nbuf, slot):
                c.wait()

        oa_buf[slot] = packed[:, :q]
        ob_buf[slot] = packed[:, q:]
        for c in out_copies(g, slot):
            c.start()
        return jnp.where(slot == nbuf - 1, 0, slot + 1)

    lax.fori_loop(g_lo, g_hi, tile_body, g_lo % nbuf)

    @pl.when(e == pl.num_programs(0) - 1)
    def _():
        for back in range(1, nbuf + 1):
            @pl.when(n_used >= back)
            def _():
                g = n_used - back
                for c in out_copies(g, g % nbuf):
                    c.wait()


def _pos_kernel(slabt_ref, offs_ref, pos0_ref, pos1_ref):
    st = slabt_ref[...]
    t = st.shape[1]
    row = lax.broadcasted_iota(jnp.int32, (N_EXPERTS, t), 0).astype(jnp.float32)
    offs = jnp.tile(offs_ref[...], (1, t // LANES))
    for s, out in enumerate((pos0_ref, pos1_ref)):
        first = jnp.sum(jnp.where(row == st[s:s + 1], offs, 0.0), axis=0, keepdims=True)
        out[...] = (first + st[2 + s:3 + s]).astype(jnp.int32)


def _combine_kernel(x1_ref, o1a_ref, o1b_ref, o2a_ref, o2b_ref, slabt_ref, y_ref):
    st = slabt_ref[...]
    slab = jnp.concatenate([st, jnp.zeros((LANES - st.shape[0], st.shape[1]), jnp.float32)], axis=0).T
    g1 = slab[:, 4:5]
    g2 = slab[:, 5:6]
    q = o1a_ref.shape[1]
    for c, (r1, r2) in enumerate(((o1a_ref, o2a_ref), (o1b_ref, o2b_ref))):
        h1, l1 = _unpack_halves(r1[...])
        h2, l2 = _unpack_halves(r2[...])
        hs = slice(c * q, (c + 1) * q)
        ls = slice(2 * q + c * q, 2 * q + (c + 1) * q)
        y_ref[:, hs] = x1_ref[:, hs] + (g1 * h1 + g2 * h2)
        y_ref[:, ls] = x1_ref[:, ls] + (g1 * l1 + g2 * l2)


def _sc_mesh():
    return plsc.VectorSubcoreMesh(core_axis_name="core", subcore_axis_name="subcore")


def _sc_scatter_rows(parts, pos0, pos1, n_rows):
    n_piece = parts[0].shape[1] // SC_COLS

    def sc_kernel(*refs):
        x_refs = refs[:len(parts)]
        i0_hbm, i1_hbm = refs[len(parts):len(parts) + 2]
        o_refs = refs[len(parts) + 2:]
        tok0 = 0
        for x_hbm, arr in zip(x_refs, parts):
            blk0 = tok0 // SC_WINDOW
            for c in range(n_piece):
                def body(x_vmem, i0_vmem, i1_vmem, o_hbm=o_refs[c]):
                    pltpu.sync_copy(x_vmem, o_hbm.at[i0_vmem.at[0]])
                    pltpu.sync_copy(x_vmem, o_hbm.at[i1_vmem.at[0]])

                pltpu.emit_pipeline(
                    body,
                    grid=(arr.shape[0] // SC_WINDOW,),
                    in_specs=[pl.BlockSpec((SC_WINDOW, SC_COLS), lambda i, c=c: (i, c)),
                              pl.BlockSpec((1, SC_WINDOW), lambda i, blk0=blk0: (0, blk0 + i)),
                              pl.BlockSpec((1, SC_WINDOW), lambda i, blk0=blk0: (0, blk0 + i))],
                    out_specs=[],
                    core_axis_name=("core", "subcore"),
                    dimension_semantics=(pltpu.PARALLEL,),
                )(x_hbm, i0_hbm, i1_hbm)
            tok0 += arr.shape[0]

    piece = jax.ShapeDtypeStruct((n_rows, SC_COLS), jnp.uint32)
    return pl.kernel(sc_kernel, out_type=(piece,) * n_piece, mesh=_sc_mesh(), name="scatter_rows")(
        *parts, pos0, pos1)


def _sc_gather_rows(pieces, pos_list):
    n_tok = pos_list[0].shape[1]

    def sc_kernel(*refs):
        s_refs = refs[:len(pieces)]
        i_refs = refs[len(pieces):len(pieces) + len(pos_list)]
        o_refs = refs[len(pieces) + len(pos_list):]
        k = 0
        for i_hbm in i_refs:
            for s_hbm in s_refs:
                def body(i_vmem, o_vmem, s_hbm=s_hbm):
                    pltpu.sync_copy(s_hbm.at[i_vmem.at[0]], o_vmem)

                pltpu.emit_pipeline(
                    body,
                    grid=(n_tok // SC_WINDOW,),
                    in_specs=[pl.BlockSpec((1, SC_WINDOW), lambda i: (0, i))],
                    out_specs=[pl.BlockSpec((SC_WINDOW, SC_COLS), lambda i: (i, 0))],
                    core_axis_name=("core", "subcore"),
                    dimension_semantics=(pltpu.PARALLEL,),
                )(i_hbm, o_refs[k])
                k += 1

    out = jax.ShapeDtypeStruct((n_tok, SC_COLS), jnp.uint32)
    outs = pl.kernel(sc_kernel, out_type=(out,) * (len(pieces) * len(pos_list)), mesh=_sc_mesh(),
                     name="gather_rows")(*pieces, *pos_list)
    return [outs[j * len(pieces):(j + 1) * len(pieces)] for j in range(len(pos_list))]


def _const_spec(shape, single_buffer=False):
    nd = len(shape)
    mode = pl.Buffered(1) if single_buffer else None
    return pl.BlockSpec(shape, lambda *_: (0,) * nd, pipeline_mode=mode)


def kernel(x_prompt, x_sample, cache_attn_k, cache_attn_v, g_mix, w_in, g_q, g_k, g_v, attn_sinks, w_s, b_s,
           w_out, g_ffn, w_coarse, b_coarse, w_fine, b_fine, w_gate, w_up, w_down):
    nb, seq, d = x_prompt.shape
    ns, slen, _ = x_sample.shape
    n_p = nb * seq
    n_s = ns * slen
    n_tok = n_p + n_s
    tt = TOKEN_TILE
    assert seq % tt == 0 and n_s % tt == 0 and d == D_MODEL
    nt = seq // tt
    bf = jnp.bfloat16
    f32 = jnp.float32

    l = 0
    gmix = g_mix[l].reshape(1, d)
    win = w_in[l].astype(bf)
    gq = (jnp.tile(g_q[l], N_Q_HEADS) * (HEAD_DIM ** -0.5)).reshape(1, Q_WIDTH)
    gk = jnp.tile(g_k[l], N_KV_HEADS).reshape(1, KV_WIDTH)
    gvg = g_v[l].reshape(1, GATE_WIDTH)
    sinks = attn_sinks[l].reshape(N_Q_HEADS).astype(f32)
    ws = w_s[l].astype(bf)
    bs = b_s[l].T
    wout = w_out[l].astype(bf)
    gffn = g_ffn[l].reshape(1, d)
    wr = jnp.concatenate([w_coarse[l], jnp.transpose(w_fine[l], (1, 0, 2)).reshape(d, N_EXPERTS),
                          jnp.zeros((d, ROUTE_LANES - N_GROUPS - N_EXPERTS), f32)], axis=1)
    wr = wr.astype(bf).T
    br = jnp.concatenate([b_coarse[l], b_fine[l].reshape(-1),
                          jnp.zeros((ROUTE_LANES - N_GROUPS - N_EXPERTS,), f32)])
    br = jnp.broadcast_to(br[:, None], (ROUTE_LANES, LANES))
    cnt_shape = (N_EXPERTS, LANES)
    ii = jnp.arange(LANES)
    blk = (ii[:, None] // HEAD_DIM == ii[None, :] // HEAD_DIM).astype(bf)
    u32 = jnp.uint32
    dh = d // 2

    def strict_tril(n):
        r = jnp.arange(n)
        return (r[:, None] < r[None, :]).astype(bf)

    weight_args = (gmix, win, gq, gk, gvg, blk, ws, bs, wout, gffn, wr, br)
    weight_specs = [_const_spec(a.shape, single_buffer=True) for a in weight_args]
    smem_spec = pl.BlockSpec(memory_space=pltpu.SMEM)

    xs2 = x_sample.reshape(n_s, d)
    ck = cache_attn_k[l].reshape(ns * WINDOW, KV_WIDTH)
    cv = cache_attn_v[l].reshape(ns * WINDOW, KV_WIDTH)
    tok_out = lambda n, w, dt: jax.ShapeDtypeStruct((n, w), dt)
    x1_s, h2_s, slabt_s, kn_s, v_s, gvn_s, cnt_s = pl.pallas_call(
        functools.partial(_sample_kernel, n_seq=ns, seq_len=slen),
        grid=(1,),
        in_specs=[smem_spec, _const_spec((n_s, d)), _const_spec(ck.shape), _const_spec(cv.shape)]
                 + weight_specs + [_const_spec((n_s, n_s))],
        out_specs=[_const_spec((n_s, d)), _const_spec((n_s, dh)),
                   _const_spec((EXP_PER_GROUP, n_s)),
                   _const_spec((n_s, KV_WIDTH)), _const_spec((n_s, KV_WIDTH)), _const_spec((n_s, GATE_WIDTH)),
                   _const_spec(cnt_shape)],
        out_shape=[tok_out(n_s, d, f32), tok_out(n_s, dh, u32),
                   jax.ShapeDtypeStruct((EXP_PER_GROUP, n_s), f32),
                   jax.ShapeDtypeStruct((n_s, KV_WIDTH), f32), jax.ShapeDtypeStruct((n_s, KV_WIDTH), f32),
                   jax.ShapeDtypeStruct((n_s, GATE_WIDTH), f32), jax.ShapeDtypeStruct(cnt_shape, f32)],
        scratch_shapes=[pltpu.VMEM((N_KV_HEADS, ns * WINDOW, LANES), bf), pltpu.VMEM((N_KV_HEADS, ns * WINDOW, LANES), bf),
                        pltpu.VMEM((N_KV_HEADS, n_s, LANES), bf), pltpu.VMEM((N_KV_HEADS, n_s, LANES), bf),
                        pltpu.VMEM((n_s, Q_WIDTH), bf), pltpu.VMEM((n_s, GATE_WIDTH), f32),
                        pltpu.VMEM((n_s, d), bf)],
        compiler_params=pltpu.CompilerParams(dimension_semantics=("arbitrary",), vmem_limit_bytes=VMEM_LIMIT),
        name="mixer_sample",
    )(sinks, xs2, ck, cv, *weight_args, strict_tril(n_s))

    x1_p, h2_p, slabt_p, kwin, vwin, cnt = pl.pallas_call(
        _prompt_kernel,
        grid=(nb, nt),
        in_specs=[smem_spec, pl.BlockSpec((1, tt, d), lambda b, t: (b, t, 0))] + weight_specs
                 + [_const_spec((tt // ROW_SPLITS, tt // ROW_SPLITS)), _const_spec(cnt_shape)],
        out_specs=[pl.BlockSpec((tt, d), lambda b, t: (b * nt + t, 0)),
                   pl.BlockSpec((tt, dh), lambda b, t: (b * nt + t, 0)),
                   pl.BlockSpec((EXP_PER_GROUP, tt), lambda b, t: (0, b * nt + t)),
                   pl.BlockSpec((1, WINDOW, KV_WIDTH), lambda b, t: (b, 0, 0)),
                   pl.BlockSpec((1, WINDOW, KV_WIDTH), lambda b, t: (b, 0, 0)),
                   _const_spec(cnt_shape)],
        out_shape=[tok_out(n_p, d, f32), tok_out(n_p, dh, u32),
                   jax.ShapeDtypeStruct((EXP_PER_GROUP, n_p), f32),
                   jax.ShapeDtypeStruct((nb, WINDOW, KV_WIDTH), f32),
                   jax.ShapeDtypeStruct((nb, WINDOW, KV_WIDTH), f32),
                   jax.ShapeDtypeStruct(cnt_shape, f32)],
        scratch_shapes=[pltpu.VMEM((N_KV_HEADS, tt + WINDOW, LANES), bf),
                        pltpu.VMEM((N_KV_HEADS, tt + WINDOW, LANES), bf),
                        pltpu.VMEM((tt, Q_WIDTH), bf), pltpu.VMEM((tt, GATE_WIDTH), f32),
                        pltpu.VMEM((tt, GATE_WIDTH), bf), pltpu.VMEM((tt, d), bf),
                        pltpu.VMEM(cnt_shape, f32)],
        compiler_params=pltpu.CompilerParams(dimension_semantics=("arbitrary", "arbitrary"),
                                             vmem_limit_bytes=VMEM_LIMIT),
        name="mixer_prompt",
    )(sinks, x_prompt, *weight_args, strict_tril(tt // ROW_SPLITS), cnt_s)
    slabt = jnp.concatenate([slabt_p, slabt_s], axis=1)

    tm = EXPERT_TILE
    counts = cnt[:, 0].astype(jnp.int32)
    tiles = (counts + tm - 1) // tm
    tile_end = jnp.cumsum(tiles)
    offs = (tile_end - tiles) * tm
    n_used = tile_end[-1]
    n_tiles = (2 * n_tok) // tm + N_EXPERTS
    n_rows = n_tiles * tm
    lane_tiles = n_tok // LANES
    pos_blk = LANES * max(k for k in range(1, POS_BLOCK_TILES + 1) if lane_tiles % k == 0)
    offs_b = jnp.broadcast_to(offs.astype(f32)[:, None], cnt_shape)
    pos_row = jax.ShapeDtypeStruct((1, n_tok), jnp.int32)
    pos0, pos1 = pl.pallas_call(
        _pos_kernel,
        grid=(n_tok // pos_blk,),
        in_specs=[pl.BlockSpec((EXP_PER_GROUP, pos_blk), lambda i: (0, i)), _const_spec(cnt_shape)],
        out_specs=[pl.BlockSpec((1, pos_blk), lambda i: (0, i)), pl.BlockSpec((1, pos_blk), lambda i: (0, i))],
        out_shape=[pos_row, pos_row],
        compiler_params=pltpu.CompilerParams(dimension_semantics=("arbitrary",)),
        name="sorted_pos",
    )(slabt, offs_b)

    assert dh == 2 * SC_COLS and n_p % SC_WINDOW == 0 and n_s % SC_WINDOW == 0
    xs_a, xs_b = _sc_scatter_rows([h2_p, h2_s], pos0, pos1, n_rows)

    w_map = lambda e, *_: (e, 0, 0)
    hbm = pl.BlockSpec(memory_space=pl.ANY)
    piece = jax.ShapeDtypeStruct((n_rows, SC_COLS), u32)
    tile_buf = pltpu.VMEM((EXPERT_BUFFERS, tm, SC_COLS), u32)
    tile_sems = pltpu.SemaphoreType.DMA((2, EXPERT_BUFFERS))
    out_a, out_b = pl.pallas_call(
        _expert_kernel,
        grid_spec=pltpu.PrefetchScalarGridSpec(
            num_scalar_prefetch=4,
            grid=(N_EXPERTS,),
            in_specs=[hbm, hbm,
                      pl.BlockSpec((1, d, D_EXPERT), w_map),
                      pl.BlockSpec((1, d, D_EXPERT), w_map),
                      pl.BlockSpec((1, D_EXPERT, d), w_map)],
            out_specs=[hbm, hbm],
            scratch_shapes=[pltpu.VMEM((d, D_EXPERT), bf), pltpu.VMEM((d, D_EXPERT), bf),
                            pltpu.VMEM((D_EXPERT, d), bf),
                            tile_buf, tile_buf, tile_buf, tile_buf, tile_sems, tile_sems]),
        out_shape=[piece, piece],
        compiler_params=pltpu.CompilerParams(dimension_semantics=("arbitrary",), vmem_limit_bytes=VMEM_LIMIT),
        name="experts",
    )((tile_end - tiles).astype(jnp.int32), tile_end.astype(jnp.int32), counts,
      n_used.reshape(1).astype(jnp.int32), xs_a, xs_b, w_gate[l], w_up[l], w_down[l])

    (o1a, o1b), (o2a, o2b) = _sc_gather_rows([out_a, out_b], [pos0, pos1])

    def combine(x1, slab_t, blk0):
        n = x1.shape[0]
        tok = lambda i: (i, 0)
        off = lambda i: (blk0 + i, 0)
        return pl.pallas_call(
            _combine_kernel,
            grid=(n // tt,),
            in_specs=[pl.BlockSpec((tt, d), tok)] + [pl.BlockSpec((tt, SC_COLS), off)] * 4
                     + [pl.BlockSpec((EXP_PER_GROUP, tt), lambda i: (0, i))],
            out_specs=pl.BlockSpec((tt, d), tok),
            out_shape=jax.ShapeDtypeStruct((n, d), f32),
            compiler_params=pltpu.CompilerParams(dimension_semantics=("arbitrary",)),
            name="combine",
        )(x1, o1a, o1b, o2a, o2b, slab_t)

    y_p = combine(x1_p, slabt_p, 0).reshape(nb, seq, d)
    y_s = combine(x1_s, slabt_s, n_p // tt).reshape(ns, slen, d)

    kv_shape = (1, nb, WINDOW, N_KV_HEADS, HEAD_DIM)
    new_k_p = kwin.reshape(kv_shape)
    new_v_p = vwin.reshape(kv_shape)
    keep = WINDOW - slen
    ck4 = cache_attn_k[l][:, WINDOW - keep:]
    cv4 = cache_attn_v[l][:, WINDOW - keep:]
    new_k_s = jnp.concatenate([ck4, kn_s.reshape(ns, slen, N_KV_HEADS, HEAD_DIM)], axis=1)[None]
    new_v_s = jnp.concatenate([cv4, v_s.reshape(ns, slen, N_KV_HEADS, HEAD_DIM)], axis=1)[None]
    new_gv_s = gvn_s.reshape(1, ns, slen, GATE_HEADS, GATE_DIM)
    return (y_p, y_s, new_k_p, new_v_p, new_k_s, new_v_s, new_gv_s)
```

```python
import functools

import jax
import jax.numpy as jnp
from jax import lax
from jax.experimental import pallas as pl
from jax.experimental.pallas import tpu as pltpu
from jax.experimental.pallas import tpu_sc as plsc

D_MODEL = 1024
HEAD_DIM = 64
N_Q_HEADS = 8
N_KV_HEADS = 2
Q_WIDTH = N_Q_HEADS * HEAD_DIM
KV_WIDTH = N_KV_HEADS * HEAD_DIM
GATE_HEADS = 4
GATE_DIM = 128
GATE_WIDTH = GATE_HEADS * GATE_DIM
PROJ_COLS = Q_WIDTH + 2 * KV_WIDTH + 2 * GATE_WIDTH
CHUNK = 64
WINDOW = 128
MLP_CHUNK = 128
N_GROUPS = 8
EXP_PER_GROUP = 8
N_EXPERTS = N_GROUPS * EXP_PER_GROUP
D_EXPERT = 512
EPS = 1e-6

LANES = 128
SUBLANES = 8
ROUTE_LANES = 128
FINE_LANE0 = N_GROUPS
ROUTE_ROWS = SUBLANES
NEG = -1e30
assert N_GROUPS == SUBLANES and EXP_PER_GROUP == SUBLANES

TOKEN_TILE = 512
ROW_SPLITS = 2
EXPERT_TILE = 512
EXPERT_BUFFERS = 3
POS_BLOCK_TILES = 64
SC_WINDOW = 128
SC_COLS = 256
VMEM_LIMIT = 40 * 1024 * 1024


def _pack_halves(x):
    w = x.shape[1] // 2
    b = lax.bitcast_convert_type(x.astype(jnp.bfloat16).astype(jnp.float32), jnp.uint32)
    return (b[:, :w] & jnp.uint32(0xFFFF0000)) | (b[:, w:] >> 16)


def _unpack_halves(p):
    hi = lax.bitcast_convert_type(p & jnp.uint32(0xFFFF0000), jnp.float32)
    lo = lax.bitcast_convert_type(p << 16, jnp.float32)
    return hi, lo


def _rms(x, eps=EPS):
    return x * lax.rsqrt(jnp.mean(x * x, axis=-1, keepdims=True) + eps)


def _in_proj(x, gmix, win):
    h = (_rms(x) * gmix).astype(jnp.bfloat16)
    return jnp.dot(h, win, preferred_element_type=jnp.float32)


def _heads(z, gq, gk, gvg, blk):
    qk = z[:, :Q_WIDTH + KV_WIDTH]
    outs = []
    for j in range((Q_WIDTH + KV_WIDTH) // LANES):
        zj = qk[:, j * LANES:(j + 1) * LANES]
        ss = jnp.dot((zj * zj).astype(jnp.bfloat16), blk, preferred_element_type=jnp.float32)
        outs.append(zj * lax.rsqrt(ss * (1.0 / HEAD_DIM) + EPS))
    qn = jnp.concatenate(outs[:Q_WIDTH // LANES], axis=-1) * gq
    kn = outs[-1] * gk
    v = z[:, Q_WIDTH + KV_WIDTH:Q_WIDTH + 2 * KV_WIDTH]
    u0 = Q_WIDTH + 2 * KV_WIDTH
    ua = jax.nn.gelu(z[:, u0:u0 + GATE_WIDTH])
    ga = jax.nn.gelu(z[:, u0 + GATE_WIDTH:])
    gvn = jnp.concatenate(
        [_rms(ga[:, i * GATE_DIM:(i + 1) * GATE_DIM]) for i in range(GATE_HEADS)], axis=-1) * gvg
    return qn.astype(jnp.bfloat16), kn, v, ua, gvn


def _dup_halves(a):
    lo = lax.broadcasted_iota(jnp.int32, a.shape, 1) < HEAD_DIM
    r = pltpu.roll(a, HEAD_DIM, axis=1)
    return (jnp.where(lo, a, r).astype(jnp.bfloat16), jnp.where(lo, r, a).astype(jnp.bfloat16))


def _attn_block(qa, qb, k2, v2, bias, sk):
    r = qa.shape[0]
    lo = lax.broadcasted_iota(jnp.int32, (r, LANES), 1) < HEAD_DIM
    zero = jnp.zeros_like(qa)
    qs = jnp.concatenate([jnp.where(lo, qa, zero), jnp.where(lo, zero, qa),
                          jnp.where(lo, qb, zero), jnp.where(lo, zero, qb)], axis=0)
    s = lax.dot_general(qs, k2, (((1,), (1,)), ((), ())), preferred_element_type=jnp.float32)
    if bias is not None:
        s = s + bias
    m = jnp.maximum(jnp.max(s, axis=-1, keepdims=True), sk)
    e = jnp.exp(s - m)
    den = jnp.sum(e, axis=-1, keepdims=True) + jnp.exp(sk - m)
    o = jnp.dot(e.astype(jnp.bfloat16), v2, preferred_element_type=jnp.float32) * (1.0 / den)
    pa = jnp.where(lo, o[0:r], o[r:2 * r])
    pb = jnp.where(lo, o[2 * r:3 * r], o[3 * r:4 * r])
    return pa, pb


def _sink_rows(sink_ref, g, r):
    row = lax.broadcasted_iota(jnp.int32, (4 * r, 1), 0)
    s0, s1, s2, s3 = (sink_ref[4 * g + i] for i in range(4))
    return jnp.where(row < r, s0, jnp.where(row < 2 * r, s1, jnp.where(row < 3 * r, s2, s3)))


def _causal_ws(ws_ref, h, n):
    w = ws_ref[h][:n, :n]
    keep = (lax.broadcasted_iota(jnp.int32, (n, n), 0) >= lax.broadcasted_iota(jnp.int32, (n, n), 1))
    return jnp.where(keep, w, jnp.zeros_like(w))


def _out_proj(cat, x, wout):
    return x + jnp.dot(cat, wout, preferred_element_type=jnp.float32)


def _route(x1, gffn, wrt, brt, triu, cnt):
    h2 = _rms(x1) * gffn
    t = x1.shape[0]
    reps = t // LANES
    lgt = lax.dot_general(wrt, h2.astype(jnp.bfloat16), (((1,), (1,)), ((), ())),
                          preferred_element_type=jnp.float32) + jnp.tile(brt, (1, reps))
    ng = float(EXP_PER_GROUP)
    sub = lax.broadcasted_iota(jnp.int32, (EXP_PER_GROUP, t), 0).astype(jnp.float32)
    c = lgt[0:N_GROUPS]
    mc = jnp.max(c, axis=0, keepdims=True)
    grp = jnp.min(jnp.where(c == mc, sub, ng), axis=0, keepdims=True)
    pg = 1.0 / jnp.sum(jnp.exp(c - mc), axis=0, keepdims=True)
    lf = lgt[FINE_LANE0:FINE_LANE0 + EXP_PER_GROUP]
    for g in range(1, N_GROUPS):
        r0 = FINE_LANE0 + g * EXP_PER_GROUP
        lf = jnp.where(grp == float(g), lgt[r0:r0 + EXP_PER_GROUP], lf)
    v1 = jnp.max(lf, axis=0, keepdims=True)
    i1 = jnp.min(jnp.where(lf == v1, sub, ng), axis=0, keepdims=True)
    lf2 = jnp.where(sub == i1, -jnp.inf, lf)
    v2 = jnp.max(lf2, axis=0, keepdims=True)
    i2 = jnp.min(jnp.where(lf2 == v2, sub, ng), axis=0, keepdims=True)
    tt = jnp.exp(v2 - v1)
    w1 = 1.0 / (1.0 + tt)
    w2 = tt * w1
    e1 = grp * ng + i1
    e2 = grp * ng + i2
    row = lax.broadcasted_iota(jnp.int32, (N_EXPERTS, t), 0).astype(jnp.float32)
    sel1 = row == e1
    sel2 = row == e2
    oh = jnp.where(sel1 | sel2, 1.0, 0.0)
    cum = jnp.dot(oh.astype(jnp.bfloat16), triu, preferred_element_type=jnp.float32) + jnp.tile(cnt, (1, reps))
    r1 = jnp.sum(jnp.where(sel1, cum, 0.0), axis=0, keepdims=True)
    r2 = jnp.sum(jnp.where(sel2, cum, 0.0), axis=0, keepdims=True)
    new_cnt = cnt + jnp.sum(oh, axis=1, keepdims=True)
    slab_t = jnp.where(sub == 0.0, e1,
             jnp.where(sub == 1.0, e2,
             jnp.where(sub == 2.0, r1,
             jnp.where(sub == 3.0, r2,
             jnp.where(sub == 4.0, pg * w1,
             jnp.where(sub == 5.0, pg * w2, 0.0))))))
    return _pack_halves(h2), slab_t, new_cnt


def _prompt_kernel(sink_ref, x_ref, gmix_ref, win_ref, gq_ref, gk_ref, gvg_ref, blk_ref, ws_ref, bs_ref,
                   wout_ref, gffn_ref, wr_ref, br_ref, triu_ref, cnt_in_ref,
                   x1_ref, h2_ref, slabt_ref, kwin_ref, vwin_ref, cnt_ref,
                   k2_scr, v2_scr, qn_scr, ua_scr, gvn_scr, cat_scr, cnt_scr):
    b = pl.program_id(0)
    t = pl.program_id(1)
    nt = pl.num_programs(1)
    tt = x_ref.shape[1]
    nblk = tt // WINDOW
    sub = tt // ROW_SPLITS

    @pl.when((b == 0) & (t == 0))
    def _():
        cnt_scr[...] = cnt_in_ref[...]

    @pl.when(t == 0)
    def _():
        k2_scr[:, 0:WINDOW, :] = jnp.zeros((N_KV_HEADS, WINDOW, LANES), jnp.bfloat16)
        v2_scr[:, 0:WINDOW, :] = jnp.zeros((N_KV_HEADS, WINDOW, LANES), jnp.bfloat16)

    pieces = [slice(s * sub, (s + 1) * sub) for s in range(ROW_SPLITS)]
    for s, rs in enumerate(pieces):
        ks = slice(WINDOW + s * sub, WINDOW + (s + 1) * sub)
        z = _in_proj(x_ref[0, rs, :], gmix_ref[...], win_ref[...])
        qn, kn, v, ua, gvn = _heads(z, gq_ref[...], gk_ref[...], gvg_ref[...], blk_ref[...])
        qn_scr[rs, :] = qn
        ua_scr[rs, :] = ua
        gvn_scr[rs, :] = gvn.astype(jnp.bfloat16)
        k0, k1 = _dup_halves(kn)
        v0, v1 = _dup_halves(v)
        k2_scr[0, ks, :] = k0
        k2_scr[1, ks, :] = k1
        v2_scr[0, ks, :] = v0
        v2_scr[1, ks, :] = v1
        if s == ROW_SPLITS - 1:
            @pl.when(t == nt - 1)
            def _():
                kwin_ref[0] = kn[sub - WINDOW:]
                vwin_ref[0] = v[sub - WINDOW:]

    rows = 4 * WINDOW
    band = 2 * WINDOW
    rr = lax.broadcasted_iota(jnp.int32, (rows, band), 0)
    kk = lax.broadcasted_iota(jnp.int32, (rows, band), 1)
    half = ((rr % WINDOW) >= CHUNK).astype(jnp.int32)
    allowed = (kk >= half * CHUNK) & (kk < (3 + half) * CHUNK)
    bias_mid = jnp.where(allowed, 0.0, NEG)
    bias_first = jnp.where(allowed & (kk >= WINDOW), 0.0, NEG)

    def attn_body(j, carry):
        r0 = pl.multiple_of(j * WINDOW, WINDOW)
        bias = jnp.where((t == 0) & (j == 0), bias_first, bias_mid)
        for g in range(N_KV_HEADS):
            c0 = g * 2 * LANES
            qa = qn_scr[pl.ds(r0, WINDOW), c0:c0 + LANES]
            qb = qn_scr[pl.ds(r0, WINDOW), c0 + LANES:c0 + 2 * LANES]
            k2 = k2_scr[g, pl.ds(r0, band), :]
            v2 = v2_scr[g, pl.ds(r0, band), :]
            pa, pb = _attn_block(qa, qb, k2, v2, bias, _sink_rows(sink_ref, g, WINDOW))
            cat_scr[pl.ds(r0, WINDOW), c0:c0 + LANES] = pa.astype(jnp.bfloat16)
            cat_scr[pl.ds(r0, WINDOW), c0 + LANES:c0 + 2 * LANES] = pb.astype(jnp.bfloat16)
        return carry

    lax.fori_loop(0, nblk, attn_body, 0, unroll=True)

    k2_scr[:, 0:WINDOW, :] = k2_scr[:, tt:tt + WINDOW, :]
    v2_scr[:, 0:WINDOW, :] = v2_scr[:, tt:tt + WINDOW, :]

    for h in range(GATE_HEADS):
        w = _causal_ws(ws_ref, h, MLP_CHUNK)
        bcol = bs_ref[:, h:h + 1]
        for c in range(tt // MLP_CHUNK):
            rs = slice(c * MLP_CHUNK, (c + 1) * MLP_CHUNK)
            cs = slice(h * GATE_DIM, (h + 1) * GATE_DIM)
            mix = jnp.dot(w, gvn_scr[rs, cs], preferred_element_type=jnp.float32) + bcol
            cat_scr[rs, Q_WIDTH + h * GATE_DIM:Q_WIDTH + (h + 1) * GATE_DIM] = (ua_scr[rs, cs] * mix).astype(jnp.bfloat16)

    cnt = cnt_scr[...]
    x1_next = _out_proj(cat_scr[pieces[0], :], x_ref[0, pieces[0], :], wout_ref[...])
    for s, rs in enumerate(pieces):
        x1 = x1_next
        if s + 1 < ROW_SPLITS:
            x1_next = _out_proj(cat_scr[pieces[s + 1], :], x_ref[0, pieces[s + 1], :], wout_ref[...])
        x1_ref[rs, :] = x1
        hp, slab_t, cnt = _route(x1, gffn_ref[...], wr_ref[...], br_ref[...], triu_ref[...], cnt)
        h2_ref[rs, :] = hp
        slabt_ref[:, rs] = slab_t
    cnt_scr[...] = cnt
    cnt_ref[...] = cnt


def _sample_kernel(sink_ref, x_ref, ck_ref, cv_ref, gmix_ref, win_ref, gq_ref, gk_ref, gvg_ref, blk_ref, ws_ref,
                   bs_ref, wout_ref, gffn_ref, wr_ref, br_ref, triu_ref,
                   x1_ref, h2_ref, slabt_ref, kn_ref, v_ref, gvn_ref, cnt_ref,
                   ck2_scr, cv2_scr, k2_scr, v2_scr, qn_scr, ua_scr, cat_scr, *, n_seq, seq_len):
    x = x_ref[...]
    qn, kn, v, ua, gvn = _heads(_in_proj(x, gmix_ref[...], win_ref[...]), gq_ref[...], gk_ref[...], gvg_ref[...],
                                blk_ref[...])
    kn_ref[...] = kn
    v_ref[...] = v
    gvn_ref[...] = gvn
    qn_scr[...] = qn
    ua_scr[...] = ua
    for scr, val in ((k2_scr, kn), (v2_scr, v), (ck2_scr, ck_ref[...]), (cv2_scr, cv_ref[...])):
        a0, a1 = _dup_halves(val)
        scr[0] = a0
        scr[1] = a1

    ws = [_causal_ws(ws_ref, h, seq_len) for h in range(GATE_HEADS)]

    def seq_body(i, carry):
        r0 = pl.multiple_of(i * seq_len, seq_len)
        c0r = pl.multiple_of(i * WINDOW, WINDOW)
        for g in range(N_KV_HEADS):
            c0 = g * 2 * LANES
            qa = qn_scr[pl.ds(r0, seq_len), c0:c0 + LANES]
            qb = qn_scr[pl.ds(r0, seq_len), c0 + LANES:c0 + 2 * LANES]
            k2 = jnp.concatenate([ck2_scr[g, pl.ds(c0r, WINDOW), :], k2_scr[g, pl.ds(r0, seq_len), :]], axis=0)
            v2 = jnp.concatenate([cv2_scr[g, pl.ds(c0r, WINDOW), :], v2_scr[g, pl.ds(r0, seq_len), :]], axis=0)
            pa, pb = _attn_block(qa, qb, k2, v2, None, _sink_rows(sink_ref, g, seq_len))
            cat_scr[pl.ds(r0, seq_len), c0:c0 + LANES] = pa.astype(jnp.bfloat16)
            cat_scr[pl.ds(r0, seq_len), c0 + LANES:c0 + 2 * LANES] = pb.astype(jnp.bfloat16)
        for h in range(GATE_HEADS):
            cs = slice(h * GATE_DIM, (h + 1) * GATE_DIM)
            gv_h = gvn_ref[pl.ds(r0, seq_len), cs].astype(jnp.bfloat16)
            mix = jnp.dot(ws[h], gv_h, preferred_element_type=jnp.float32) + bs_ref[0:seq_len, h:h + 1]
            cat_scr[pl.ds(r0, seq_len), Q_WIDTH + h * GATE_DIM:Q_WIDTH + (h + 1) * GATE_DIM] = (
                ua_scr[pl.ds(r0, seq_len), cs] * mix).astype(jnp.bfloat16)
        return carry

    lax.fori_loop(0, n_seq, seq_body, 0)

    cnt0 = jnp.zeros((N_EXPERTS, LANES), jnp.float32)
    x1 = _out_proj(cat_scr[...], x, wout_ref[...])
    hp, slab_t, new_cnt = _route(x1, gffn_ref[...], wr_ref[...], br_ref[...], triu_ref[...], cnt0)
    x1_ref[...] = x1
    h2_ref[...] = hp
    slabt_ref[...] = slab_t
    cnt_ref[...] = new_cnt


def _expert_kernel(ts_ref, te_ref, cnt_ref, nu_ref, xa_hbm, xb_hbm, wg_ref, wu_ref, wd_ref, oa_hbm, ob_hbm,
                   wg_s, wu_s, wd_s, xa_buf, xb_buf, oa_buf, ob_buf, in_sem, out_sem):
    e = pl.program_id(0)
    nbuf, tm, _ = xa_buf.shape
    n_used = nu_ref[0]

    def rows_of(g):
        return pl.ds(pl.multiple_of(g * tm, tm), tm)

    def in_copies(g, slot):
        return (pltpu.make_async_copy(xa_hbm.at[rows_of(g)], xa_buf.at[slot], in_sem.at[0, slot]),
                pltpu.make_async_copy(xb_hbm.at[rows_of(g)], xb_buf.at[slot], in_sem.at[1, slot]))

    def out_copies(g, slot):
        return (pltpu.make_async_copy(oa_buf.at[slot], oa_hbm.at[rows_of(g)], out_sem.at[0, slot]),
                pltpu.make_async_copy(ob_buf.at[slot], ob_hbm.at[rows_of(g)], out_sem.at[1, slot]))

    @pl.when(e == 0)
    def _():
        for g in range(nbuf - 1):
            @pl.when(g < n_used)
            def _():
                for c in in_copies(g, g):
                    c.start()

    g_lo = ts_ref[e]
    g_hi = te_ref[e]

    @pl.when(g_hi > g_lo)
    def _():
        wg_s[...] = wg_ref[0].astype(jnp.bfloat16)
        wu_s[...] = wu_ref[0].astype(jnp.bfloat16)
        wd_s[...] = wd_ref[0].astype(jnp.bfloat16)

    def tile_body(g, slot):
        for c in in_copies(g, slot):
            c.wait()
        ahead = g + (nbuf - 1)
        ahead_slot = jnp.where(slot == 0, nbuf - 1, slot - 1)

        @pl.when(ahead < n_used)
        def _():
            for c in in_copies(ahead, ahead_slot):
                c.start()

        live = lax.broadcasted_iota(jnp.int32, (tm, xa_buf.shape[2]), 0) < cnt_ref[e] - (g - g_lo) * tm
        ha, la = _unpack_halves(jnp.where(live, xa_buf[slot], jnp.uint32(0)))
        hb, lb = _unpack_halves(jnp.where(live, xb_buf[slot], jnp.uint32(0)))
        xs = jnp.concatenate([ha, hb, la, lb], axis=1).astype(jnp.bfloat16)
        acts = []
        for r in range(0, tm, tm // 2):
            hg = jnp.dot(xs[r:r + tm // 2], wg_s[...], preferred_element_type=jnp.float32)
            hu = jnp.dot(xs[r:r + tm // 2], wu_s[...], preferred_element_type=jnp.float32)
            acts.append((jax.nn.silu(hg) * hu).astype(jnp.bfloat16))
        packed = jnp.concatenate(
            [_pack_halves(jnp.dot(a, wd_s[...], preferred_element_type=jnp.float32)) for a in acts], axis=0)
        q = packed.shape[1] // 2

        @pl.when(g >= nbuf)
        def _():
            for c in out_copies(g - nbuf, slot):
                c.wait()

        oa_buf[slot] = packed[:, :q]
        ob_buf[slot] = packed[:, q:]
        for c in out_copies(g, slot):
            c.start()
        return jnp.where(slot == nbuf - 1, 0, slot + 1)

    lax.fori_loop(g_lo, g_hi, tile_body, g_lo % nbuf)

    @pl.when(e == pl.num_programs(0) - 1)
    def _():
        for back in range(1, nbuf + 1):
            @pl.when(n_used >= back)
            def _():
                g = n_used - back
                for c in out_copies(g, g % nbuf):
                    c.wait()


def _pos_kernel(slabt_ref, offs_ref, pos0_ref, pos1_ref):
    st = slabt_ref[...]
    t = st.shape[1]
    row = lax.broadcasted_iota(jnp.int32, (N_EXPERTS, t), 0).astype(jnp.float32)
    offs = jnp.tile(offs_ref[...], (1, t // LANES))
    for s, out in enumerate((pos0_ref, pos1_ref)):
        first = jnp.sum(jnp.where(row == st[s:s + 1], offs, 0.0), axis=0, keepdims=True)
        out[...] = (first + st[2 + s:3 + s]).astype(jnp.int32)


def _combine_kernel(x1_ref, o1a_ref, o1b_ref, o2a_ref, o2b_ref, slabt_ref, y_ref):
    st = slabt_ref[...]
    slab = jnp.concatenate([st, jnp.zeros((LANES - st.shape[0], st.shape[1]), jnp.float32)], axis=0).T
    g1 = slab[:, 4:5]
    g2 = slab[:, 5:6]
    q = o1a_ref.shape[1]
    for c, (r1, r2) in enumerate(((o1a_ref, o2a_ref), (o1b_ref, o2b_ref))):
        h1, l1 = _unpack_halves(r1[...])
        h2, l2 = _unpack_halves(r2[...])
        hs = slice(c * q, (c + 1) * q)
        ls = slice(2 * q + c * q, 2 * q + (c + 1) * q)
        y_ref[:, hs] = x1_ref[:, hs] + (g1 * h1 + g2 * h2)
        y_ref[:, ls] = x1_ref[:, ls] + (g1 * l1 + g2 * l2)


def _sc_mesh():
    return plsc.VectorSubcoreMesh(core_axis_name="core", subcore_axis_name="subcore")


def _sc_scatter_rows(parts, pos0, pos1, n_rows):
    n_piece = parts[0].shape[1] // SC_COLS

    def sc_kernel(*refs):
        x_refs = refs[:len(parts)]
        i0_hbm, i1_hbm = refs[len(parts):len(parts) + 2]
        o_refs = refs[len(parts) + 2:]
        tok0 = 0
        for x_hbm, arr in zip(x_refs, parts):
            blk0 = tok0 // SC_WINDOW
            for c in range(n_piece):
                def body(x_vmem, i0_vmem, i1_vmem, o_hbm=o_refs[c]):
                    pltpu.sync_copy(x_vmem, o_hbm.at[i0_vmem.at[0]])
                    pltpu.sync_copy(x_vmem, o_hbm.at[i1_vmem.at[0]])

                pltpu.emit_pipeline(
                    body,
                    grid=(arr.shape[0] // SC_WINDOW,),
                    in_specs=[pl.BlockSpec((SC_WINDOW, SC_COLS), lambda i, c=c: (i, c)),
                              pl.BlockSpec((1, SC_WINDOW), lambda i, blk0=blk0: (0, blk0 + i)),
                              pl.BlockSpec((1, SC_WINDOW), lambda i, blk0=blk0: (0, blk0 + i))],
                    out_specs=[],
                    core_axis_name=("core", "subcore"),
                    dimension_semantics=(pltpu.PARALLEL,),
                )(x_hbm, i0_hbm, i1_hbm)
            tok0 += arr.shape[0]

    piece = jax.ShapeDtypeStruct((n_rows, SC_COLS), jnp.uint32)
    return pl.kernel(sc_kernel, out_type=(piece,) * n_piece, mesh=_sc_mesh(), name="scatter_rows")(
        *parts, pos0, pos1)


def _sc_gather_rows(pieces, pos_list):
    n_tok = pos_list[0].shape[1]

    def sc_kernel(*refs):
        s_refs = refs[:len(pieces)]
        i_refs = refs[len(pieces):len(pieces) + len(pos_list)]
        o_refs = refs[len(pieces) + len(pos_list):]
        k = 0
        for i_hbm in i_refs:
            for s_hbm in s_refs:
                def body(i_vmem, o_vmem, s_hbm=s_hbm):
                    pltpu.sync_copy(s_hbm.at[i_vmem.at[0]], o_vmem)

                pltpu.emit_pipeline(
                    body,
                    grid=(n_tok // SC_WINDOW,),
                    in_specs=[pl.BlockSpec((1, SC_WINDOW), lambda i: (0, i))],
                    out_specs=[pl.BlockSpec((SC_WINDOW, SC_COLS), lambda i: (i, 0))],
                    core_axis_name=("core", "subcore"),
                    dimension_semantics=(pltpu.PARALLEL,),
                )(i_hbm, o_refs[k])
                k += 1

    out = jax.ShapeDtypeStruct((n_tok, SC_COLS), jnp.uint32)
    outs = pl.kernel(sc_kernel, out_type=(out,) * (len(pieces) * len(pos_list)), mesh=_sc_mesh(),
                     name="gather_rows")(*pieces, *pos_list)
    return [outs[j * len(pieces):(j + 1) * len(pieces)] for j in range(len(pos_list))]


def _const_spec(shape, single_buffer=False):
    nd = len(shape)
    mode = pl.Buffered(1) if single_buffer else None
    return pl.BlockSpec(shape, lambda *_: (0,) * nd, pipeline_mode=mode)


def kernel(x_prompt, x_sample, cache_attn_k, cache_attn_v, g_mix, w_in, g_q, g_k, g_v, attn_sinks, w_s, b_s,
           w_out, g_ffn, w_coarse, b_coarse, w_fine, b_fine, w_gate, w_up, w_down):
    nb, seq, d = x_prompt.shape
    ns, slen, _ = x_sample.shape
    n_p = nb * seq
    n_s = ns * slen
    n_tok = n_p + n_s
    tt = TOKEN_TILE
    assert seq % tt == 0 and n_s % tt == 0 and d == D_MODEL
    nt = seq // tt
    bf = jnp.bfloat16
    f32 = jnp.float32

    l = 0
    gmix = g_mix[l].reshape(1, d)
    win = w_in[l].astype(bf)
    gq = (jnp.tile(g_q[l], N_Q_HEADS) * (HEAD_DIM ** -0.5)).reshape(1, Q_WIDTH)
    gk = jnp.tile(g_k[l], N_KV_HEADS).reshape(1, KV_WIDTH)
    gvg = g_v[l].reshape(1, GATE_WIDTH)
    sinks = attn_sinks[l].reshape(N_Q_HEADS).astype(f32)
    ws = w_s[l].astype(bf)
    bs = b_s[l].T
    wout = w_out[l].astype(bf)
    gffn = g_ffn[l].reshape(1, d)
    wr = jnp.concatenate([w_coarse[l], jnp.transpose(w_fine[l], (1, 0, 2)).reshape(d, N_EXPERTS),
                          jnp.zeros((d, ROUTE_LANES - N_GROUPS - N_EXPERTS), f32)], axis=1)
    wr = wr.astype(bf).T
    br = jnp.concatenate([b_coarse[l], b_fine[l].reshape(-1),
                          jnp.zeros((ROUTE_LANES - N_GROUPS - N_EXPERTS,), f32)])
    br = jnp.broadcast_to(br[:, None], (ROUTE_LANES, LANES))
    cnt_shape = (N_EXPERTS, LANES)
    ii = jnp.arange(LANES)
    blk = (ii[:, None] // HEAD_DIM == ii[None, :] // HEAD_DIM).astype(bf)
    u32 = jnp.uint32
    dh = d // 2

    def strict_triu(n):
        r = jnp.arange(n)
        return (r[:, None] < r[None, :]).astype(bf)

    weight_args = (gmix, win, gq, gk, gvg, blk, ws, bs, wout, gffn, wr, br)
    weight_specs = [_const_spec(a.shape, single_buffer=True) for a in weight_args]
    smem_spec = pl.BlockSpec(memory_space=pltpu.SMEM)

    xs2 = x_sample.reshape(n_s, d)
    ck = cache_attn_k[l].reshape(ns * WINDOW, KV_WIDTH)
    cv = cache_attn_v[l].reshape(ns * WINDOW, KV_WIDTH)
    tok_out = lambda n, w, dt: jax.ShapeDtypeStruct((n, w), dt)
    x1_s, h2_s, slabt_s, kn_s, v_s, gvn_s, cnt_s = pl.pallas_call(
        functools.partial(_sample_kernel, n_seq=ns, seq_len=slen),
        grid=(1,),
        in_specs=[smem_spec, _const_spec((n_s, d)), _const_spec(ck.shape), _const_spec(cv.shape)]
                 + weight_specs + [_const_spec((n_s, n_s))],
        out_specs=[_const_spec((n_s, d)), _const_spec((n_s, dh)),
                   _const_spec((ROUTE_ROWS, n_s)),
                   _const_spec((n_s, KV_WIDTH)), _const_spec((n_s, KV_WIDTH)), _const_spec((n_s, GATE_WIDTH)),
                   _const_spec(cnt_shape)],
        out_shape=[tok_out(n_s, d, f32), tok_out(n_s, dh, u32),
                   jax.ShapeDtypeStruct((ROUTE_ROWS, n_s), f32),
                   jax.ShapeDtypeStruct((n_s, KV_WIDTH), f32), jax.ShapeDtypeStruct((n_s, KV_WIDTH), f32),
                   jax.ShapeDtypeStruct((n_s, GATE_WIDTH), f32), jax.ShapeDtypeStruct(cnt_shape, f32)],
        scratch_shapes=[pltpu.VMEM((N_KV_HEADS, ns * WINDOW, LANES), bf), pltpu.VMEM((N_KV_HEADS, ns * WINDOW, LANES), bf),
                        pltpu.VMEM((N_KV_HEADS, n_s, LANES), bf), pltpu.VMEM((N_KV_HEADS, n_s, LANES), bf),
                        pltpu.VMEM((n_s, Q_WIDTH), bf), pltpu.VMEM((n_s, GATE_WIDTH), f32),
                        pltpu.VMEM((n_s, d), bf)],
        compiler_params=pltpu.CompilerParams(dimension_semantics=("arbitrary",), vmem_limit_bytes=VMEM_LIMIT),
        name="mixer_sample",
    )(sinks, xs2, ck, cv, *weight_args, strict_triu(n_s))

    x1_p, h2_p, slabt_p, kwin, vwin, cnt = pl.pallas_call(
        _prompt_kernel,
        grid=(nb, nt),
        in_specs=[smem_spec, pl.BlockSpec((1, tt, d), lambda b, t: (b, t, 0))] + weight_specs
                 + [_const_spec((tt // ROW_SPLITS, tt // ROW_SPLITS)), _const_spec(cnt_shape)],
        out_specs=[pl.BlockSpec((tt, d), lambda b, t: (b * nt + t, 0)),
                   pl.BlockSpec((tt, dh), lambda b, t: (b * nt + t, 0)),
                   pl.BlockSpec((ROUTE_ROWS, tt), lambda b, t: (0, b * nt + t)),
                   pl.BlockSpec((1, WINDOW, KV_WIDTH), lambda b, t: (b, 0, 0)),
                   pl.BlockSpec((1, WINDOW, KV_WIDTH), lambda b, t: (b, 0, 0)),
                   _const_spec(cnt_shape)],
        out_shape=[tok_out(n_p, d, f32), tok_out(n_p, dh, u32),
                   jax.ShapeDtypeStruct((ROUTE_ROWS, n_p), f32),
                   jax.ShapeDtypeStruct((nb, WINDOW, KV_WIDTH), f32),
                   jax.ShapeDtypeStruct((nb, WINDOW, KV_WIDTH), f32),
                   jax.ShapeDtypeStruct(cnt_shape, f32)],
        scratch_shapes=[pltpu.VMEM((N_KV_HEADS, tt + WINDOW, LANES), bf),
                        pltpu.VMEM((N_KV_HEADS, tt + WINDOW, LANES), bf),
                        pltpu.VMEM((tt, Q_WIDTH), bf), pltpu.VMEM((tt, GATE_WIDTH), f32),
                        pltpu.VMEM((tt, GATE_WIDTH), bf), pltpu.VMEM((tt, d), bf),
                        pltpu.VMEM(cnt_shape, f32)],
        compiler_params=pltpu.CompilerParams(dimension_semantics=("arbitrary", "arbitrary"),
                                             vmem_limit_bytes=VMEM_LIMIT),
        name="mixer_prompt",
    )(sinks, x_prompt, *weight_args, strict_triu(tt // ROW_SPLITS), cnt_s)
    slabt = jnp.concatenate([slabt_p, slabt_s], axis=1)

    tm = EXPERT_TILE
    counts = cnt[:, 0].astype(jnp.int32)
    tiles = (counts + tm - 1) // tm
    tile_end = jnp.cumsum(tiles)
    offs = (tile_end - tiles) * tm
    n_used = tile_end[-1]
    n_tiles = (2 * n_tok) // tm + N_EXPERTS
    n_rows = n_tiles * tm
    lane_tiles = n_tok // LANES
    pos_blk = LANES * max(k for k in range(1, POS_BLOCK_TILES + 1) if lane_tiles % k == 0)
    offs_b = jnp.broadcast_to(offs.astype(f32)[:, None], cnt_shape)
    pos_row = jax.ShapeDtypeStruct((1, n_tok), jnp.int32)
    pos0, pos1 = pl.pallas_call(
        _pos_kernel,
        grid=(n_tok // pos_blk,),
        in_specs=[pl.BlockSpec((ROUTE_ROWS, pos_blk), lambda i: (0, i)), _const_spec(cnt_shape)],
        out_specs=[pl.BlockSpec((1, pos_blk), lambda i: (0, i)), pl.BlockSpec((1, pos_blk), lambda i: (0, i))],
        out_shape=[pos_row, pos_row],
        compiler_params=pltpu.CompilerParams(dimension_semantics=("arbitrary",)),
        name="sorted_pos",
    )(slabt, offs_b)

    assert dh == 2 * SC_COLS and n_p % SC_WINDOW == 0 and n_s % SC_WINDOW == 0
    xs_a, xs_b = _sc_scatter_rows([h2_p, h2_s], pos0, pos1, n_rows)

    w_map = lambda e, *_: (e, 0, 0)
    hbm = pl.BlockSpec(memory_space=pl.ANY)
    piece = jax.ShapeDtypeStruct((n_rows, SC_COLS), u32)
    tile_buf = pltpu.VMEM((EXPERT_BUFFERS, tm, SC_COLS), u32)
    tile_sems = pltpu.SemaphoreType.DMA((2, EXPERT_BUFFERS))
    out_a, out_b = pl.pallas_call(
        _expert_kernel,
        grid_spec=pltpu.PrefetchScalarGridSpec(
            num_scalar_prefetch=4,
            grid=(N_EXPERTS,),
            in_specs=[hbm, hbm,
                      pl.BlockSpec((1, d, D_EXPERT), w_map),
                      pl.BlockSpec((1, d, D_EXPERT), w_map),
                      pl.BlockSpec((1, D_EXPERT, d), w_map)],
            out_specs=[hbm, hbm],
            scratch_shapes=[pltpu.VMEM((d, D_EXPERT), bf), pltpu.VMEM((d, D_EXPERT), bf),
                            pltpu.VMEM((D_EXPERT, d), bf),
                            tile_buf, tile_buf, tile_buf, tile_buf, tile_sems, tile_sems]),
        out_shape=[piece, piece],
        compiler_params=pltpu.CompilerParams(dimension_semantics=("arbitrary",), vmem_limit_bytes=VMEM_LIMIT),
        name="experts",
    )((tile_end - tiles).astype(jnp.int32), tile_end.astype(jnp.int32), counts,
      n_used.reshape(1).astype(jnp.int32), xs_a, xs_b, w_gate[l], w_up[l], w_down[l])

    (o1a, o1b), (o2a, o2b) = _sc_gather_rows([out_a, out_b], [pos0, pos1])

    def combine(x1, slab_t, blk0):
        n = x1.shape[0]
        tok = lambda i: (i, 0)
        off = lambda i: (blk0 + i, 0)
        return pl.pallas_call(
            _combine_kernel,
            grid=(n // tt,),
            in_specs=[pl.BlockSpec((tt, d), tok)] + [pl.BlockSpec((tt, SC_COLS), off)] * 4
                     + [pl.BlockSpec((ROUTE_ROWS, tt), lambda i: (0, i))],
            out_specs=pl.BlockSpec((tt, d), tok),
            out_shape=jax.ShapeDtypeStruct((n, d), f32),
            compiler_params=pltpu.CompilerParams(dimension_semantics=("arbitrary",)),
            name="combine",
        )(x1, o1a, o1b, o2a, o2b, slab_t)

    y_p = combine(x1_p, slabt_p, 0).reshape(nb, seq, d)
    y_s = combine(x1_s, slabt_s, n_p // tt).reshape(ns, slen, d)

    kv_shape = (1, nb, WINDOW, N_KV_HEADS, HEAD_DIM)
    new_k_p = kwin.reshape(kv_shape)
    new_v_p = vwin.reshape(kv_shape)
    keep = WINDOW - slen
    ck4 = cache_attn_k[l][:, WINDOW - keep:]
    cv4 = cache_attn_v[l][:, WINDOW - keep:]
    new_k_s = jnp.concatenate([ck4, kn_s.reshape(ns, slen, N_KV_HEADS, HEAD_DIM)], axis=1)[None]
    new_v_s = jnp.concatenate([cv4, v_s.reshape(ns, slen, N_KV_HEADS, HEAD_DIM)], axis=1)[None]
    new_gv_s = gvn_s.reshape(1, ns, slen, GATE_HEADS, GATE_DIM)
    return (y_p, y_s, new_k_p, new_v_p, new_k_s, new_v_s, new_gv_s)
```

```python
import functools

import jax
import jax.numpy as jnp
from jax import lax
from jax.experimental import pallas as pl
from jax.experimental.pallas import tpu as pltpu
from jax.experimental.pallas import tpu_sc as plsc

D_MODEL = 1024
HEAD_DIM = 64
N_Q_HEADS = 8
N_KV_HEADS = 2
Q_WIDTH = N_Q_HEADS * HEAD_DIM
KV_WIDTH = N_KV_HEADS * HEAD_DIM
GATE_HEADS = 4
GATE_DIM = 128
GATE_WIDTH = GATE_HEADS * GATE_DIM
PROJ_COLS = Q_WIDTH + 2 * KV_WIDTH + 2 * GATE_WIDTH
CHUNK = 64
WINDOW = 128
MLP_CHUNK = 128
N_GROUPS = 8
EXP_PER_GROUP = 8
N_EXPERTS = N_GROUPS * EXP_PER_GROUP
D_EXPERT = 512
EPS = 1e-6

LANES = 128
SUBLANES = 8
ROUTE_LANES = 128
FINE_LANE0 = N_GROUPS
ROUTE_ROWS = SUBLANES
NEG = -1e30
assert N_GROUPS == SUBLANES and EXP_PER_GROUP == SUBLANES

PROMPT_WAVES = 2
TOKEN_TILE = 512
ROW_SPLITS = 2
EXPERT_TILE = 512
EXPERT_BUFFERS = 3
POS_BLOCK_TILES = 64
SC_WINDOW = 128
SC_COLS = 256
VMEM_LIMIT = 40 * 1024 * 1024


def _pack_halves(x):
    w = x.shape[1] // 2
    b = lax.bitcast_convert_type(x.astype(jnp.bfloat16).astype(jnp.float32), jnp.uint32)
    return (b[:, :w] & jnp.uint32(0xFFFF0000)) | (b[:, w:] >> 16)


def _unpack_halves(p):
    hi = lax.bitcast_convert_type(p & jnp.uint32(0xFFFF0000), jnp.float32)
    lo = lax.bitcast_convert_type(p << 16, jnp.float32)
    return hi, lo


def _rms(x, eps=EPS):
    return x * lax.rsqrt(jnp.mean(x * x, axis=-1, keepdims=True) + eps)


def _in_proj(x, gmix, win):
    h = (_rms(x) * gmix).astype(jnp.bfloat16)
    return jnp.dot(h, win, preferred_element_type=jnp.float32)


def _heads(z, gq, gk, gvg, blk):
    qk = z[:, :Q_WIDTH + KV_WIDTH]
    outs = []
    for j in range((Q_WIDTH + KV_WIDTH) // LANES):
        zj = qk[:, j * LANES:(j + 1) * LANES]
        ss = jnp.dot((zj * zj).astype(jnp.bfloat16), blk, preferred_element_type=jnp.float32)
        outs.append(zj * lax.rsqrt(ss * (1.0 / HEAD_DIM) + EPS))
    qn = jnp.concatenate(outs[:Q_WIDTH // LANES], axis=-1) * gq
    kn = outs[-1] * gk
    v = z[:, Q_WIDTH + KV_WIDTH:Q_WIDTH + 2 * KV_WIDTH]
    u0 = Q_WIDTH + 2 * KV_WIDTH
    ua = jax.nn.gelu(z[:, u0:u0 + GATE_WIDTH])
    ga = jax.nn.gelu(z[:, u0 + GATE_WIDTH:])
    gvn = jnp.concatenate(
        [_rms(ga[:, i * GATE_DIM:(i + 1) * GATE_DIM]) for i in range(GATE_HEADS)], axis=-1) * gvg
    return qn.astype(jnp.bfloat16), kn, v, ua, gvn


def _dup_halves(a):
    lo = lax.broadcasted_iota(jnp.int32, a.shape, 1) < HEAD_DIM
    r = pltpu.roll(a, HEAD_DIM, axis=1)
    return (jnp.where(lo, a, r).astype(jnp.bfloat16), jnp.where(lo, r, a).astype(jnp.bfloat16))


def _attn_block(qa, qb, k2, v2, bias, sk):
    r = qa.shape[0]
    lo = lax.broadcasted_iota(jnp.int32, (r, LANES), 1) < HEAD_DIM
    zero = jnp.zeros_like(qa)
    qs = jnp.concatenate([jnp.where(lo, qa, zero), jnp.where(lo, zero, qa),
                          jnp.where(lo, qb, zero), jnp.where(lo, zero, qb)], axis=0)
    s = lax.dot_general(qs, k2, (((1,), (1,)), ((), ())), preferred_element_type=jnp.float32)
    if bias is not None:
        s = s + bias
    m = jnp.maximum(jnp.max(s, axis=-1, keepdims=True), sk)
    e = jnp.exp(s - m)
    den = jnp.sum(e, axis=-1, keepdims=True) + jnp.exp(sk - m)
    o = jnp.dot(e.astype(jnp.bfloat16), v2, preferred_element_type=jnp.float32) * (1.0 / den)
    pa = jnp.where(lo, o[0:r], o[r:2 * r])
    pb = jnp.where(lo, o[2 * r:3 * r], o[3 * r:4 * r])
    return pa, pb


def _sink_rows(sink_ref, g, r):
    row = lax.broadcasted_iota(jnp.int32, (4 * r, 1), 0)
    s0, s1, s2, s3 = (sink_ref[4 * g + i] for i in range(4))
    return jnp.where(row < r, s0, jnp.where(row < 2 * r, s1, jnp.where(row < 3 * r, s2, s3)))


def _causal_ws(ws_ref, h, n):
    w = ws_ref[h][:n, :n]
    keep = (lax.broadcasted_iota(jnp.int32, (n, n), 0) >= lax.broadcasted_iota(jnp.int32, (n, n), 1))
    return jnp.where(keep, w, jnp.zeros_like(w))


def _out_proj(cat, x, wout):
    return x + jnp.dot(cat, wout, preferred_element_type=jnp.float32)


def _route(x1, gffn, wrt, brt, triu, cnt):
    h2 = _rms(x1) * gffn
    t = x1.shape[0]
    reps = t // LANES
    lgt = lax.dot_general(wrt, h2.astype(jnp.bfloat16), (((1,), (1,)), ((), ())),
                          preferred_element_type=jnp.float32) + jnp.tile(brt, (1, reps))
    ng = float(EXP_PER_GROUP)
    sub = lax.broadcasted_iota(jnp.int32, (EXP_PER_GROUP, t), 0).astype(jnp.float32)
    c = lgt[0:N_GROUPS]
    mc = jnp.max(c, axis=0, keepdims=True)
    grp = jnp.min(jnp.where(c == mc, sub, ng), axis=0, keepdims=True)
    pg = 1.0 / jnp.sum(jnp.exp(c - mc), axis=0, keepdims=True)
    lf = lgt[FINE_LANE0:FINE_LANE0 + EXP_PER_GROUP]
    for g in range(1, N_GROUPS):
        r0 = FINE_LANE0 + g * EXP_PER_GROUP
        lf = jnp.where(grp == float(g), lgt[r0:r0 + EXP_PER_GROUP], lf)
    v1 = jnp.max(lf, axis=0, keepdims=True)
    i1 = jnp.min(jnp.where(lf == v1, sub, ng), axis=0, keepdims=True)
    lf2 = jnp.where(sub == i1, -jnp.inf, lf)
    v2 = jnp.max(lf2, axis=0, keepdims=True)
    i2 = jnp.min(jnp.where(lf2 == v2, sub, ng), axis=0, keepdims=True)
    tt = jnp.exp(v2 - v1)
    w1 = 1.0 / (1.0 + tt)
    w2 = tt * w1
    e1 = grp * ng + i1
    e2 = grp * ng + i2
    row = lax.broadcasted_iota(jnp.int32, (N_EXPERTS, t), 0).astype(jnp.float32)
    sel1 = row == e1
    sel2 = row == e2
    oh = jnp.where(sel1 | sel2, 1.0, 0.0)
    cum = jnp.dot(oh.astype(jnp.bfloat16), triu, preferred_element_type=jnp.float32) + jnp.tile(cnt, (1, reps))
    r1 = jnp.sum(jnp.where(sel1, cum, 0.0), axis=0, keepdims=True)
    r2 = jnp.sum(jnp.where(sel2, cum, 0.0), axis=0, keepdims=True)
    new_cnt = cnt + jnp.sum(oh, axis=1, keepdims=True)
    slab_t = jnp.where(sub == 0.0, e1,
             jnp.where(sub == 1.0, e2,
             jnp.where(sub == 2.0, r1,
             jnp.where(sub == 3.0, r2,
             jnp.where(sub == 4.0, pg * w1,
             jnp.where(sub == 5.0, pg * w2, 0.0))))))
    return _pack_halves(h2), slab_t, new_cnt


def _prompt_kernel(sink_ref, x_ref, gmix_ref, win_ref, gq_ref, gk_ref, gvg_ref, blk_ref, ws_ref, bs_ref,
                   wout_ref, gffn_ref, wr_ref, br_ref, triu_ref, cnt_in_ref,
                   x1_ref, h2_ref, slabt_ref, kwin_ref, vwin_ref, cnt_ref,
                   k2_scr, v2_scr, qn_scr, ua_scr, gvn_scr, cat_scr, cnt_scr):
    b = pl.program_id(0)
    t = pl.program_id(1)
    nt = pl.num_programs(1)
    tt = x_ref.shape[1]
    nblk = tt // WINDOW
    sub = tt // ROW_SPLITS

    @pl.when((b == 0) & (t == 0))
    def _():
        cnt_scr[...] = cnt_in_ref[...]

    @pl.when(t == 0)
    def _():
        k2_scr[:, 0:WINDOW, :] = jnp.zeros((N_KV_HEADS, WINDOW, LANES), jnp.bfloat16)
        v2_scr[:, 0:WINDOW, :] = jnp.zeros((N_KV_HEADS, WINDOW, LANES), jnp.bfloat16)

    pieces = [slice(s * sub, (s + 1) * sub) for s in range(ROW_SPLITS)]
    for s, rs in enumerate(pieces):
        ks = slice(WINDOW + s * sub, WINDOW + (s + 1) * sub)
        z = _in_proj(x_ref[0, rs, :], gmix_ref[...], win_ref[...])
        qn, kn, v, ua, gvn = _heads(z, gq_ref[...], gk_ref[...], gvg_ref[...], blk_ref[...])
        qn_scr[rs, :] = qn
        ua_scr[rs, :] = ua
        gvn_scr[rs, :] = gvn.astype(jnp.bfloat16)
        k0, k1 = _dup_halves(kn)
        v0, v1 = _dup_halves(v)
        k2_scr[0, ks, :] = k0
        k2_scr[1, ks, :] = k1
        v2_scr[0, ks, :] = v0
        v2_scr[1, ks, :] = v1
        if s == ROW_SPLITS - 1:
            @pl.when(t == nt - 1)
            def _():
                kwin_ref[0] = kn[sub - WINDOW:]
                vwin_ref[0] = v[sub - WINDOW:]

    rows = 4 * WINDOW
    band = 2 * WINDOW
    rr = lax.broadcasted_iota(jnp.int32, (rows, band), 0)
    kk = lax.broadcasted_iota(jnp.int32, (rows, band), 1)
    half = ((rr % WINDOW) >= CHUNK).astype(jnp.int32)
    allowed = (kk >= half * CHUNK) & (kk < (3 + half) * CHUNK)
    bias_mid = jnp.where(allowed, 0.0, NEG)
    bias_first = jnp.where(allowed & (kk >= WINDOW), 0.0, NEG)

    def attn_body(j, carry):
        r0 = pl.multiple_of(j * WINDOW, WINDOW)
        bias = jnp.where((t == 0) & (j == 0), bias_first, bias_mid)
        for g in range(N_KV_HEADS):
            c0 = g * 2 * LANES
            qa = qn_scr[pl.ds(r0, WINDOW), c0:c0 + LANES]
            qb = qn_scr[pl.ds(r0, WINDOW), c0 + LANES:c0 + 2 * LANES]
            k2 = k2_scr[g, pl.ds(r0, band), :]
            v2 = v2_scr[g, pl.ds(r0, band), :]
            pa, pb = _attn_block(qa, qb, k2, v2, bias, _sink_rows(sink_ref, g, WINDOW))
            cat_scr[pl.ds(r0, WINDOW), c0:c0 + LANES] = pa.astype(jnp.bfloat16)
            cat_scr[pl.ds(r0, WINDOW), c0 + LANES:c0 + 2 * LANES] = pb.astype(jnp.bfloat16)
        return carry

    lax.fori_loop(0, nblk, attn_body, 0, unroll=True)

    k2_scr[:, 0:WINDOW, :] = k2_scr[:, tt:tt + WINDOW, :]
    v2_scr[:, 0:WINDOW, :] = v2_scr[:, tt:tt + WINDOW, :]

    for h in range(GATE_HEADS):
        w = _causal_ws(ws_ref, h, MLP_CHUNK)
        bcol = bs_ref[:, h:h + 1]
        for c in range(tt // MLP_CHUNK):
            rs = slice(c * MLP_CHUNK, (c + 1) * MLP_CHUNK)
            cs = slice(h * GATE_DIM, (h + 1) * GATE_DIM)
            mix = jnp.dot(w, gvn_scr[rs, cs], preferred_element_type=jnp.float32) + bcol
            cat_scr[rs, Q_WIDTH + h * GATE_DIM:Q_WIDTH + (h + 1) * GATE_DIM] = (ua_scr[rs, cs] * mix).astype(jnp.bfloat16)

    cnt = cnt_scr[...]
    x1_next = _out_proj(cat_scr[pieces[0], :], x_ref[0, pieces[0], :], wout_ref[...])
    for s, rs in enumerate(pieces):
        x1 = x1_next
        if s + 1 < ROW_SPLITS:
            x1_next = _out_proj(cat_scr[pieces[s + 1], :], x_ref[0, pieces[s + 1], :], wout_ref[...])
        x1_ref[rs, :] = x1
        hp, slab_t, cnt = _route(x1, gffn_ref[...], wr_ref[...], br_ref[...], triu_ref[...], cnt)
        h2_ref[rs, :] = hp
        slabt_ref[:, rs] = slab_t
    cnt_scr[...] = cnt
    cnt_ref[...] = cnt


def _sample_kernel(sink_ref, x_ref, ck_ref, cv_ref, gmix_ref, win_ref, gq_ref, gk_ref, gvg_ref, blk_ref, ws_ref,
                   bs_ref, wout_ref, gffn_ref, wr_ref, br_ref, triu_ref,
                   x1_ref, h2_ref, slabt_ref, kn_ref, v_ref, gvn_ref, cnt_ref,
                   ck2_scr, cv2_scr, k2_scr, v2_scr, qn_scr, ua_scr, cat_scr, *, n_seq, seq_len):
    x = x_ref[...]
    qn, kn, v, ua, gvn = _heads(_in_proj(x, gmix_ref[...], win_ref[...]), gq_ref[...], gk_ref[...], gvg_ref[...],
                                blk_ref[...])
    kn_ref[...] = kn
    v_ref[...] = v
    gvn_ref[...] = gvn
    qn_scr[...] = qn
    ua_scr[...] = ua
    for scr, val in ((k2_scr, kn), (v2_scr, v), (ck2_scr, ck_ref[...]), (cv2_scr, cv_ref[...])):
        a0, a1 = _dup_halves(val)
        scr[0] = a0
        scr[1] = a1

    ws = [_causal_ws(ws_ref, h, seq_len) for h in range(GATE_HEADS)]

    def seq_body(i, carry):
        r0 = pl.multiple_of(i * seq_len, seq_len)
        c0r = pl.multiple_of(i * WINDOW, WINDOW)
        for g in range(N_KV_HEADS):
            c0 = g * 2 * LANES
            qa = qn_scr[pl.ds(r0, seq_len), c0:c0 + LANES]
            qb = qn_scr[pl.ds(r0, seq_len), c0 + LANES:c0 + 2 * LANES]
            k2 = jnp.concatenate([ck2_scr[g, pl.ds(c0r, WINDOW), :], k2_scr[g, pl.ds(r0, seq_len), :]], axis=0)
            v2 = jnp.concatenate([cv2_scr[g, pl.ds(c0r, WINDOW), :], v2_scr[g, pl.ds(r0, seq_len), :]], axis=0)
            pa, pb = _attn_block(qa, qb, k2, v2, None, _sink_rows(sink_ref, g, seq_len))
            cat_scr[pl.ds(r0, seq_len), c0:c0 + LANES] = pa.astype(jnp.bfloat16)
            cat_scr[pl.ds(r0, seq_len), c0 + LANES:c0 + 2 * LANES] = pb.astype(jnp.bfloat16)
        for h in range(GATE_HEADS):
            cs = slice(h * GATE_DIM, (h + 1) * GATE_DIM)
            gv_h = gvn_ref[pl.ds(r0, seq_len), cs].astype(jnp.bfloat16)
            mix = jnp.dot(ws[h], gv_h, preferred_element_type=jnp.float32) + bs_ref[0:seq_len, h:h + 1]
            cat_scr[pl.ds(r0, seq_len), Q_WIDTH + h * GATE_DIM:Q_WIDTH + (h + 1) * GATE_DIM] = (
                ua_scr[pl.ds(r0, seq_len), cs] * mix).astype(jnp.bfloat16)
        return carry

    lax.fori_loop(0, n_seq, seq_body, 0)

    cnt0 = jnp.zeros((N_EXPERTS, LANES), jnp.float32)
    x1 = _out_proj(cat_scr[...], x, wout_ref[...])
    hp, slab_t, new_cnt = _route(x1, gffn_ref[...], wr_ref[...], br_ref[...], triu_ref[...], cnt0)
    x1_ref[...] = x1
    h2_ref[...] = hp
    slabt_ref[...] = slab_t
    cnt_ref[...] = new_cnt


def _expert_kernel(ts_ref, te_ref, cnt_ref, nu_ref, xa_hbm, xb_hbm, wg_ref, wu_ref, wd_ref, oa_hbm, ob_hbm,
                   wg_s, wu_s, wd_s, xa_buf, xb_buf, oa_buf, ob_buf, in_sem, out_sem):
    e = pl.program_id(0)
    nbuf, tm, _ = xa_buf.shape
    n_used = nu_ref[0]

    def rows_of(g):
        return pl.ds(pl.multiple_of(g * tm, tm), tm)

    def in_copies(g, slot):
        return (pltpu.make_async_copy(xa_hbm.at[rows_of(g)], xa_buf.at[slot], in_sem.at[0, slot]),
                pltpu.make_async_copy(xb_hbm.at[rows_of(g)], xb_buf.at[slot], in_sem.at[1, slot]))

    def out_copies(g, slot):
        return (pltpu.make_async_copy(oa_buf.at[slot], oa_hbm.at[rows_of(g)], out_sem.at[0, slot]),
                pltpu.make_async_copy(ob_buf.at[slot], ob_hbm.at[rows_of(g)], out_sem.at[1, slot]))

    @pl.when(e == 0)
    def _():
        for g in range(nbuf - 1):
            @pl.when(g < n_used)
            def _():
                for c in in_copies(g, g):
                    c.start()

    g_lo = ts_ref[e]
    g_hi = te_ref[e]

    @pl.when(g_hi > g_lo)
    def _():
        wg_s[...] = wg_ref[0].astype(jnp.bfloat16)
        wu_s[...] = wu_ref[0].astype(jnp.bfloat16)
        wd_s[...] = wd_ref[0].astype(jnp.bfloat16)

    def tile_body(g, slot):
        for c in in_copies(g, slot):
            c.wait()
        ahead = g + (nbuf - 1)
        ahead_slot = jnp.where(slot == 0, nbuf - 1, slot - 1)

        @pl.when(ahead < n_used)
        def _():
            for c in in_copies(ahead, ahead_slot):
                c.start()

        live = lax.broadcasted_iota(jnp.int32, (tm, xa_buf.shape[2]), 0) < cnt_ref[e] - (g - g_lo) * tm
        ha, la = _unpack_halves(jnp.where(live, xa_buf[slot], jnp.uint32(0)))
        hb, lb = _unpack_halves(jnp.where(live, xb_buf[slot], jnp.uint32(0)))
        xs = jnp.concatenate([ha, hb, la, lb], axis=1).astype(jnp.bfloat16)
        acts = []
        for r in range(0, tm, tm // 2):
            hg = jnp.dot(xs[r:r + tm // 2], wg_s[...], preferred_element_type=jnp.float32)
            hu = jnp.dot(xs[r:r + tm // 2], wu_s[...], preferred_element_type=jnp.float32)
            acts.append((jax.nn.silu(hg) * hu).astype(jnp.bfloat16))
        packed = jnp.concatenate(
            [_pack_halves(jnp.dot(a, wd_s[...], preferred_element_type=jnp.float32)) for a in acts], axis=0)
        q = packed.shape[1] // 2

        @pl.when(g >= nbuf)
        def _():
            for c in out_copies(g - nbuf, slot):
                c.wait()

        oa_buf[slot] = packed[:, :q]
        ob_buf[slot] = packed[:, q:]
        for c in out_copies(g, slot):
            c.start()
        return jnp.where(slot == nbuf - 1, 0, slot + 1)

    lax.fori_loop(g_lo, g_hi, tile_body, g_lo % nbuf)

    @pl.when(e == pl.num_programs(0) - 1)
    def _():
        for back in range(1, nbuf + 1):
            @pl.when(n_used >= back)
            def _():
                g = n_used - back
                for c in out_copies(g, g % nbuf):
                    c.wait()


def _pos_kernel(slabt_ref, offs_ref, pos0_ref, pos1_ref):
    st = slabt_ref[...]
    t = st.shape[1]
    row = lax.broadcasted_iota(jnp.int32, (N_EXPERTS, t), 0).astype(jnp.float32)
    offs = jnp.tile(offs_ref[...], (1, t // LANES))
    for s, out in enumerate((pos0_ref, pos1_ref)):
        first = jnp.sum(jnp.where(row == st[s:s + 1], offs, 0.0), axis=0, keepdims=True)
        out[...] = (first + st[2 + s:3 + s]).astype(jnp.int32)


def _combine_kernel(x1_ref, o1a_ref, o1b_ref, o2a_ref, o2b_ref, slabt_ref, *rest):
    y_ref = rest[-1]
    st = slabt_ref[...]
    slab = jnp.concatenate([st, jnp.zeros((LANES - st.shape[0], st.shape[1]), jnp.float32)], axis=0).T
    g1 = slab[:, 4:5]
    g2 = slab[:, 5:6]
    q = o1a_ref.shape[1]
    for c, (r1, r2) in enumerate(((o1a_ref, o2a_ref), (o1b_ref, o2b_ref))):
        h1, l1 = _unpack_halves(r1[...])
        h2, l2 = _unpack_halves(r2[...])
        hs = slice(c * q, (c + 1) * q)
        ls = slice(2 * q + c * q, 2 * q + (c + 1) * q)
        y_ref[:, hs] = x1_ref[:, hs] + (g1 * h1 + g2 * h2)
        y_ref[:, ls] = x1_ref[:, ls] + (g1 * l1 + g2 * l2)


def _sc_mesh():
    return plsc.VectorSubcoreMesh(core_axis_name="core", subcore_axis_name="subcore")


def _sc_scatter_rows(parts, pos0, pos1, n_rows):
    n_piece = parts[0].shape[1] // SC_COLS

    def sc_kernel(*refs):
        x_refs = refs[:len(parts)]
        i0_hbm, i1_hbm = refs[len(parts):len(parts) + 2]
        o_refs = refs[len(parts) + 2:]
        tok0 = 0
        for x_hbm, arr in zip(x_refs, parts):
            blk0 = tok0 // SC_WINDOW
            for c in range(n_piece):
                def body(x_vmem, i0_vmem, i1_vmem, o_hbm=o_refs[c]):
                    pltpu.sync_copy(x_vmem, o_hbm.at[i0_vmem.at[0]])
                    pltpu.sync_copy(x_vmem, o_hbm.at[i1_vmem.at[0]])

                pltpu.emit_pipeline(
                    body,
                    grid=(arr.shape[0] // SC_WINDOW,),
                    in_specs=[pl.BlockSpec((SC_WINDOW, SC_COLS), lambda i, c=c: (i, c)),
                              pl.BlockSpec((1, SC_WINDOW), lambda i, blk0=blk0: (0, blk0 + i)),
                              pl.BlockSpec((1, SC_WINDOW), lambda i, blk0=blk0: (0, blk0 + i))],
                    out_specs=[],
                    core_axis_name=("core", "subcore"),
                    dimension_semantics=(pltpu.PARALLEL,),
                )(x_hbm, i0_hbm, i1_hbm)
            tok0 += arr.shape[0]

    piece = jax.ShapeDtypeStruct((n_rows, SC_COLS), jnp.uint32)
    return pl.kernel(sc_kernel, out_type=(piece,) * n_piece, mesh=_sc_mesh(), name="scatter_rows")(
        *parts, pos0, pos1)


def _sc_gather_rows(pieces, pos_list):
    n_tok = pos_list[0].shape[1]

    def sc_kernel(*refs):
        s_refs = refs[:len(pieces)]
        i_refs = refs[len(pieces):len(pieces) + len(pos_list)]
        o_refs = refs[len(pieces) + len(pos_list):]
        k = 0
        for i_hbm in i_refs:
            for s_hbm in s_refs:
                def body(i_vmem, o_vmem, s_hbm=s_hbm):
                    pltpu.sync_copy(s_hbm.at[i_vmem.at[0]], o_vmem)

                pltpu.emit_pipeline(
                    body,
                    grid=(n_tok // SC_WINDOW,),
                    in_specs=[pl.BlockSpec((1, SC_WINDOW), lambda i: (0, i))],
                    out_specs=[pl.BlockSpec((SC_WINDOW, SC_COLS), lambda i: (i, 0))],
                    core_axis_name=("core", "subcore"),
                    dimension_semantics=(pltpu.PARALLEL,),
                )(i_hbm, o_refs[k])
                k += 1

    out = jax.ShapeDtypeStruct((n_tok, SC_COLS), jnp.uint32)
    outs = pl.kernel(sc_kernel, out_type=(out,) * (len(pieces) * len(pos_list)), mesh=_sc_mesh(),
                     name="gather_rows")(*pieces, *pos_list)
    return [outs[j * len(pieces):(j + 1) * len(pieces)] for j in range(len(pos_list))]


def _const_spec(shape, single_buffer=False):
    nd = len(shape)
    mode = pl.Buffered(1) if single_buffer else None
    return pl.BlockSpec(shape, lambda *_: (0,) * nd, pipeline_mode=mode)


def kernel(x_prompt, x_sample, cache_attn_k, cache_attn_v, g_mix, w_in, g_q, g_k, g_v, attn_sinks, w_s, b_s,
           w_out, g_ffn, w_coarse, b_coarse, w_fine, b_fine, w_gate, w_up, w_down):
    nb, seq, d = x_prompt.shape
    ns, slen, _ = x_sample.shape
    n_p = nb * seq
    n_s = ns * slen
    n_tok = n_p + n_s
    tt = TOKEN_TILE
    assert seq % tt == 0 and n_s % tt == 0 and d == D_MODEL
    nt = seq // tt
    bf = jnp.bfloat16
    f32 = jnp.float32

    l = 0
    gmix = g_mix[l].reshape(1, d)
    win = w_in[l].astype(bf)
    gq = (jnp.tile(g_q[l], N_Q_HEADS) * (HEAD_DIM ** -0.5)).reshape(1, Q_WIDTH)
    gk = jnp.tile(g_k[l], N_KV_HEADS).reshape(1, KV_WIDTH)
    gvg = g_v[l].reshape(1, GATE_WIDTH)
    sinks = attn_sinks[l].reshape(N_Q_HEADS).astype(f32)
    ws = w_s[l].astype(bf)
    bs = b_s[l].T
    wout = w_out[l].astype(bf)
    gffn = g_ffn[l].reshape(1, d)
    wr = jnp.concatenate([w_coarse[l], jnp.transpose(w_fine[l], (1, 0, 2)).reshape(d, N_EXPERTS),
                          jnp.zeros((d, ROUTE_LANES - N_GROUPS - N_EXPERTS), f32)], axis=1)
    wr = wr.astype(bf).T
    br = jnp.concatenate([b_coarse[l], b_fine[l].reshape(-1),
                          jnp.zeros((ROUTE_LANES - N_GROUPS - N_EXPERTS,), f32)])
    br = jnp.broadcast_to(br[:, None], (ROUTE_LANES, LANES))
    cnt_shape = (N_EXPERTS, LANES)
    ii = jnp.arange(LANES)
    blk = (ii[:, None] // HEAD_DIM == ii[None, :] // HEAD_DIM).astype(bf)
    u32 = jnp.uint32
    dh = d // 2

    def strict_triu(n):
        r = jnp.arange(n)
        return (r[:, None] < r[None, :]).astype(bf)

    weight_args = (gmix, win, gq, gk, gvg, blk, ws, bs, wout, gffn, wr, br)
    weight_specs = [_const_spec(a.shape, single_buffer=True) for a in weight_args]
    smem_spec = pl.BlockSpec(memory_space=pltpu.SMEM)

    xs2 = x_sample.reshape(n_s, d)
    ck = cache_attn_k[l].reshape(ns * WINDOW, KV_WIDTH)
    cv = cache_attn_v[l].reshape(ns * WINDOW, KV_WIDTH)
    tok_out = lambda n, w, dt: jax.ShapeDtypeStruct((n, w), dt)
    x1_s, h2_s, slabt_s, kn_s, v_s, gvn_s, cnt_s = pl.pallas_call(
        functools.partial(_sample_kernel, n_seq=ns, seq_len=slen),
        grid=(1,),
        in_specs=[smem_spec, _const_spec((n_s, d)), _const_spec(ck.shape), _const_spec(cv.shape)]
                 + weight_specs + [_const_spec((n_s, n_s))],
        out_specs=[_const_spec((n_s, d)), _const_spec((n_s, dh)),
                   _const_spec((ROUTE_ROWS, n_s)),
                   _const_spec((n_s, KV_WIDTH)), _const_spec((n_s, KV_WIDTH)), _const_spec((n_s, GATE_WIDTH)),
                   _const_spec(cnt_shape)],
        out_shape=[tok_out(n_s, d, f32), tok_out(n_s, dh, u32),
                   jax.ShapeDtypeStruct((ROUTE_ROWS, n_s), f32),
                   jax.ShapeDtypeStruct((n_s, KV_WIDTH), f32), jax.ShapeDtypeStruct((n_s, KV_WIDTH), f32),
                   jax.ShapeDtypeStruct((n_s, GATE_WIDTH), f32), jax.ShapeDtypeStruct(cnt_shape, f32)],
        scratch_shapes=[pltpu.VMEM((N_KV_HEADS, ns * WINDOW, LANES), bf), pltpu.VMEM((N_KV_HEADS, ns * WINDOW, LANES), bf),
                        pltpu.VMEM((N_KV_HEADS, n_s, LANES), bf), pltpu.VMEM((N_KV_HEADS, n_s, LANES), bf),
                        pltpu.VMEM((n_s, Q_WIDTH), bf), pltpu.VMEM((n_s, GATE_WIDTH), f32),
                        pltpu.VMEM((n_s, d), bf)],
        compiler_params=pltpu.CompilerParams(dimension_semantics=("arbitrary",), vmem_limit_bytes=VMEM_LIMIT),
        name="mixer_sample",
    )(sinks, xs2, ck, cv, *weight_args, strict_triu(n_s))

    def prompt_mixer(b0, nbw, cnt_in):
        n_w = nbw * seq
        return pl.pallas_call(
            _prompt_kernel,
            grid=(nbw, nt),
            in_specs=[smem_spec, pl.BlockSpec((1, tt, d), lambda b, t: (b0 + b, t, 0))] + weight_specs
                     + [_const_spec((tt // ROW_SPLITS, tt // ROW_SPLITS)), _const_spec(cnt_shape)],
            out_specs=[pl.BlockSpec((tt, d), lambda b, t: (b * nt + t, 0)),
                       pl.BlockSpec((tt, dh), lambda b, t: (b * nt + t, 0)),
                       pl.BlockSpec((ROUTE_ROWS, tt), lambda b, t: (0, b * nt + t)),
                       pl.BlockSpec((1, WINDOW, KV_WIDTH), lambda b, t: (b, 0, 0)),
                       pl.BlockSpec((1, WINDOW, KV_WIDTH), lambda b, t: (b, 0, 0)),
                       _const_spec(cnt_shape)],
            out_shape=[tok_out(n_w, d, f32), tok_out(n_w, dh, u32),
                       jax.ShapeDtypeStruct((ROUTE_ROWS, n_w), f32),
                       jax.ShapeDtypeStruct((nbw, WINDOW, KV_WIDTH), f32),
                       jax.ShapeDtypeStruct((nbw, WINDOW, KV_WIDTH), f32),
                       jax.ShapeDtypeStruct(cnt_shape, f32)],
            scratch_shapes=[pltpu.VMEM((N_KV_HEADS, tt + WINDOW, LANES), bf),
                            pltpu.VMEM((N_KV_HEADS, tt + WINDOW, LANES), bf),
                            pltpu.VMEM((tt, Q_WIDTH), bf), pltpu.VMEM((tt, GATE_WIDTH), f32),
                            pltpu.VMEM((tt, GATE_WIDTH), bf), pltpu.VMEM((tt, d), bf),
                            pltpu.VMEM(cnt_shape, f32)],
            compiler_params=pltpu.CompilerParams(dimension_semantics=("arbitrary", "arbitrary"),
                                                 vmem_limit_bytes=VMEM_LIMIT),
            name="mixer_prompt",
        )(sinks, x_prompt, *weight_args, strict_triu(tt // ROW_SPLITS), cnt_in)

    tm = EXPERT_TILE
    assert dh == 2 * SC_COLS and n_s % SC_WINDOW == 0 and seq % SC_WINDOW == 0

    def expert_pass(h2_parts, slabt_parts, cnt):
        n_w = sum(a.shape[0] for a in h2_parts)
        slabt = jnp.concatenate(slabt_parts, axis=1) if len(slabt_parts) > 1 else slabt_parts[0]
        counts = cnt[:, 0].astype(jnp.int32)
        tiles = (counts + tm - 1) // tm
        tile_end = jnp.cumsum(tiles)
        offs = (tile_end - tiles) * tm
        n_used = tile_end[-1]
        n_rows = ((2 * n_w) // tm + N_EXPERTS) * tm
        lane_tiles = n_w // LANES
        pos_blk = LANES * max(k for k in range(1, POS_BLOCK_TILES + 1) if lane_tiles % k == 0)
        offs_b = jnp.broadcast_to(offs.astype(f32)[:, None], cnt_shape)
        pos_row = jax.ShapeDtypeStruct((1, n_w), jnp.int32)
        pos0, pos1 = pl.pallas_call(
            _pos_kernel,
            grid=(n_w // pos_blk,),
            in_specs=[pl.BlockSpec((ROUTE_ROWS, pos_blk), lambda i: (0, i)), _const_spec(cnt_shape)],
            out_specs=[pl.BlockSpec((1, pos_blk), lambda i: (0, i)), pl.BlockSpec((1, pos_blk), lambda i: (0, i))],
            out_shape=[pos_row, pos_row],
            compiler_params=pltpu.CompilerParams(dimension_semantics=("arbitrary",)),
            name="sorted_pos",
        )(slabt, offs_b)

        xs_a, xs_b = _sc_scatter_rows(h2_parts, pos0, pos1, n_rows)

        w_map = lambda e, *_: (e, 0, 0)
        hbm = pl.BlockSpec(memory_space=pl.ANY)
        piece = jax.ShapeDtypeStruct((n_rows, SC_COLS), u32)
        tile_buf = pltpu.VMEM((EXPERT_BUFFERS, tm, SC_COLS), u32)
        tile_sems = pltpu.SemaphoreType.DMA((2, EXPERT_BUFFERS))
        out_a, out_b = pl.pallas_call(
            _expert_kernel,
            grid_spec=pltpu.PrefetchScalarGridSpec(
                num_scalar_prefetch=4,
                grid=(N_EXPERTS,),
                in_specs=[hbm, hbm,
                          pl.BlockSpec((1, d, D_EXPERT), w_map),
                          pl.BlockSpec((1, d, D_EXPERT), w_map),
                          pl.BlockSpec((1, D_EXPERT, d), w_map)],
                out_specs=[hbm, hbm],
                scratch_shapes=[pltpu.VMEM((d, D_EXPERT), bf), pltpu.VMEM((d, D_EXPERT), bf),
                                pltpu.VMEM((D_EXPERT, d), bf),
                                tile_buf, tile_buf, tile_buf, tile_buf, tile_sems, tile_sems]),
            out_shape=[piece, piece],
            compiler_params=pltpu.CompilerParams(dimension_semantics=("arbitrary",), vmem_limit_bytes=VMEM_LIMIT),
            name="experts",
        )((tile_end - tiles).astype(jnp.int32), tile_end.astype(jnp.int32), counts,
          n_used.reshape(1).astype(jnp.int32), xs_a, xs_b, w_gate[l], w_up[l], w_down[l])

        return _sc_gather_rows([out_a, out_b], [pos0, pos1])

    def combine(x1, slab_t, gathered, g_blk0, y_rows, y_blk0, y_prev):
        (o1a, o1b), (o2a, o2b) = gathered
        tok = lambda i: (i, 0)
        args = [x1, o1a, o1b, o2a, o2b, slab_t]
        in_specs = ([pl.BlockSpec((tt, d), tok)] + [pl.BlockSpec((tt, SC_COLS), lambda i: (g_blk0 + i, 0))] * 4
                    + [pl.BlockSpec((ROUTE_ROWS, tt), lambda i: (0, i))])
        aliases = {}
        if y_prev is not None:
            aliases = {len(args): 0}
            args.append(y_prev)
            in_specs.append(pl.BlockSpec(memory_space=pl.ANY))
        return pl.pallas_call(
            _combine_kernel,
            grid=(x1.shape[0] // tt,),
            in_specs=in_specs,
            out_specs=pl.BlockSpec((tt, d), lambda i: (y_blk0 + i, 0)),
            out_shape=jax.ShapeDtypeStruct((y_rows, d), f32),
            input_output_aliases=aliases,
            compiler_params=pltpu.CompilerParams(dimension_semantics=("arbitrary",)),
            name="combine",
        )(*args)

    assert nb % PROMPT_WAVES == 0
    nbw = nb // PROMPT_WAVES
    n_pw = nbw * seq
    y_p = None
    y_s = None
    kwins, vwins = [], []
    mixed = []
    for w in range(PROMPT_WAVES):
        cnt_in = cnt_s if w == 0 else jnp.zeros(cnt_shape, f32)
        mixed.append(prompt_mixer(w * nbw, nbw, cnt_in))
    for w, (x1_w, h2_w, slabt_w, kwin_w, vwin_w, cnt_w) in enumerate(mixed):
        kwins.append(kwin_w)
        vwins.append(vwin_w)
        h2_parts, slabt_parts = [h2_w], [slabt_w]
        if w == 0:
            h2_parts.append(h2_s)
            slabt_parts.append(slabt_s)
        gathered = expert_pass(h2_parts, slabt_parts, cnt_w)
        y_p = combine(x1_w, slabt_w, gathered, 0, n_p, w * (n_pw // tt), y_p)
        if w == 0:
            y_s = combine(x1_s, slabt_s, gathered, n_pw // tt, n_s, 0, None)
    y_p = y_p.reshape(nb, seq, d)
    y_s = y_s.reshape(ns, slen, d)
    kwin = jnp.concatenate(kwins, axis=0) if PROMPT_WAVES > 1 else kwins[0]
    vwin = jnp.concatenate(vwins, axis=0) if PROMPT_WAVES > 1 else vwins[0]

    kv_shape = (1, nb, WINDOW, N_KV_HEADS, HEAD_DIM)
    new_k_p = kwin.reshape(kv_shape)
    new_v_p = vwin.reshape(kv_shape)
    keep = WINDOW - slen
    ck4 = cache_attn_k[l][:, WINDOW - keep:]
    cv4 = cache_attn_v[l][:, WINDOW - keep:]
    new_k_s = jnp.concatenate([ck4, kn_s.reshape(ns, slen, N_KV_HEADS, HEAD_DIM)], axis=1)[None]
    new_v_s = jnp.concatenate([cv4, v_s.reshape(ns, slen, N_KV_HEADS, HEAD_DIM)], axis=1)[None]
    new_gv_s = gvn_s.reshape(1, ns, slen, GATE_HEADS, GATE_DIM)
    return (y_p, y_s, new_k_p, new_v_p, new_k_s, new_v_s, new_gv_s)
```

```python
import functools

import jax
import jax.numpy as jnp
from jax import lax
from jax.experimental import pallas as pl
from jax.experimental.pallas import tpu as pltpu
from jax.experimental.pallas import tpu_sc as plsc

D_MODEL = 1024
HEAD_DIM = 64
N_Q_HEADS = 8
N_KV_HEADS = 2
Q_WIDTH = N_Q_HEADS * HEAD_DIM
KV_WIDTH = N_KV_HEADS * HEAD_DIM
GATE_HEADS = 4
GATE_DIM = 128
GATE_WIDTH = GATE_HEADS * GATE_DIM
PROJ_COLS = Q_WIDTH + 2 * KV_WIDTH + 2 * GATE_WIDTH
CHUNK = 64
WINDOW = 128
MLP_CHUNK = 128
N_GROUPS = 8
EXP_PER_GROUP = 8
N_EXPERTS = N_GROUPS * EXP_PER_GROUP
D_EXPERT = 512
EPS = 1e-6

LANES = 128
SUBLANES = 8
ROUTE_LANES = 128
FINE_LANE0 = N_GROUPS
ROUTE_ROWS = SUBLANES
NEG = -1e30
assert N_GROUPS == SUBLANES and EXP_PER_GROUP == SUBLANES

PROMPT_WAVES = 2
TOKEN_TILE = 512
ROW_SPLITS = 2
EXPERT_TILE = 512
EXPERT_BUFFERS = 5
POS_BLOCK_TILES = 64
SC_WINDOW = 128
SC_COLS = 256
VMEM_LIMIT = 40 * 1024 * 1024


def _pack_halves(x):
    w = x.shape[1] // 2
    b = lax.bitcast_convert_type(x.astype(jnp.bfloat16).astype(jnp.float32), jnp.uint32)
    return (b[:, :w] & jnp.uint32(0xFFFF0000)) | (b[:, w:] >> 16)


def _unpack_halves(p):
    hi = lax.bitcast_convert_type(p & jnp.uint32(0xFFFF0000), jnp.float32)
    lo = lax.bitcast_convert_type(p << 16, jnp.float32)
    return hi, lo


def _rms(x, eps=EPS):
    return x * lax.rsqrt(jnp.mean(x * x, axis=-1, keepdims=True) + eps)


def _in_proj(x, gmix, win):
    h = (_rms(x) * gmix).astype(jnp.bfloat16)
    return jnp.dot(h, win, preferred_element_type=jnp.float32)


def _heads(z, gq, gk, gvg, blk):
    qk = z[:, :Q_WIDTH + KV_WIDTH]
    outs = []
    for j in range((Q_WIDTH + KV_WIDTH) // LANES):
        zj = qk[:, j * LANES:(j + 1) * LANES]
        ss = jnp.dot((zj * zj).astype(jnp.bfloat16), blk, preferred_element_type=jnp.float32)
        outs.append(zj * lax.rsqrt(ss * (1.0 / HEAD_DIM) + EPS))
    qn = jnp.concatenate(outs[:Q_WIDTH // LANES], axis=-1) * gq
    kn = outs[-1] * gk
    v = z[:, Q_WIDTH + KV_WIDTH:Q_WIDTH + 2 * KV_WIDTH]
    u0 = Q_WIDTH + 2 * KV_WIDTH
    ua = jax.nn.gelu(z[:, u0:u0 + GATE_WIDTH])
    ga = jax.nn.gelu(z[:, u0 + GATE_WIDTH:])
    gvn = jnp.concatenate(
        [_rms(ga[:, i * GATE_DIM:(i + 1) * GATE_DIM]) for i in range(GATE_HEADS)], axis=-1) * gvg
    return qn.astype(jnp.bfloat16), kn, v, ua, gvn


def _dup_halves(a):
    lo = lax.broadcasted_iota(jnp.int32, a.shape, 1) < HEAD_DIM
    r = pltpu.roll(a, HEAD_DIM, axis=1)
    return (jnp.where(lo, a, r).astype(jnp.bfloat16), jnp.where(lo, r, a).astype(jnp.bfloat16))


def _attn_block(qa, qb, k2, v2, bias, sk):
    r = qa.shape[0]
    lo = lax.broadcasted_iota(jnp.int32, (r, LANES), 1) < HEAD_DIM
    zero = jnp.zeros_like(qa)
    qs = jnp.concatenate([jnp.where(lo, qa, zero), jnp.where(lo, zero, qa),
                          jnp.where(lo, qb, zero), jnp.where(lo, zero, qb)], axis=0)
    s = lax.dot_general(qs, k2, (((1,), (1,)), ((), ())), preferred_element_type=jnp.float32)
    if bias is not None:
        s = s + bias
    m = jnp.maximum(jnp.max(s, axis=-1, keepdims=True), sk)
    e = jnp.exp(s - m)
    den = jnp.sum(e, axis=-1, keepdims=True) + jnp.exp(sk - m)
    o = jnp.dot(e.astype(jnp.bfloat16), v2, preferred_element_type=jnp.float32) * (1.0 / den)
    pa = jnp.where(lo, o[0:r], o[r:2 * r])
    pb = jnp.where(lo, o[2 * r:3 * r], o[3 * r:4 * r])
    return pa, pb


def _sink_rows(sink_ref, g, r):
    row = lax.broadcasted_iota(jnp.int32, (4 * r, 1), 0)
    s0, s1, s2, s3 = (sink_ref[4 * g + i] for i in range(4))
    return jnp.where(row < r, s0, jnp.where(row < 2 * r, s1, jnp.where(row < 3 * r, s2, s3)))


def _causal_ws(ws_ref, h, n):
    w = ws_ref[h][:n, :n]
    keep = (lax.broadcasted_iota(jnp.int32, (n, n), 0) >= lax.broadcasted_iota(jnp.int32, (n, n), 1))
    return jnp.where(keep, w, jnp.zeros_like(w))


def _out_proj(cat, x, wout):
    return x + jnp.dot(cat, wout, preferred_element_type=jnp.float32)


def _route(x1, gffn, wrt, brt, triu, cnt):
    h2 = _rms(x1) * gffn
    t = x1.shape[0]
    reps = t // LANES
    lgt = lax.dot_general(wrt, h2.astype(jnp.bfloat16), (((1,), (1,)), ((), ())),
                          preferred_element_type=jnp.float32) + jnp.tile(brt, (1, reps))
    ng = float(EXP_PER_GROUP)
    sub = lax.broadcasted_iota(jnp.int32, (EXP_PER_GROUP, t), 0).astype(jnp.float32)
    c = lgt[0:N_GROUPS]
    mc = jnp.max(c, axis=0, keepdims=True)
    grp = jnp.min(jnp.where(c == mc, sub, ng), axis=0, keepdims=True)
    pg = 1.0 / jnp.sum(jnp.exp(c - mc), axis=0, keepdims=True)
    lf = lgt[FINE_LANE0:FINE_LANE0 + EXP_PER_GROUP]
    for g in range(1, N_GROUPS):
        r0 = FINE_LANE0 + g * EXP_PER_GROUP
        lf = jnp.where(grp == float(g), lgt[r0:r0 + EXP_PER_GROUP], lf)
    v1 = jnp.max(lf, axis=0, keepdims=True)
    i1 = jnp.min(jnp.where(lf == v1, sub, ng), axis=0, keepdims=True)
    lf2 = jnp.where(sub == i1, -jnp.inf, lf)
    v2 = jnp.max(lf2, axis=0, keepdims=True)
    i2 = jnp.min(jnp.where(lf2 == v2, sub, ng), axis=0, keepdims=True)
    tt = jnp.exp(v2 - v1)
    w1 = 1.0 / (1.0 + tt)
    w2 = tt * w1
    e1 = grp * ng + i1
    e2 = grp * ng + i2
    row = lax.broadcasted_iota(jnp.int32, (N_EXPERTS, t), 0).astype(jnp.float32)
    sel1 = row == e1
    sel2 = row == e2
    oh = jnp.where(sel1 | sel2, 1.0, 0.0)
    cum = jnp.dot(oh.astype(jnp.bfloat16), triu, preferred_element_type=jnp.float32) + jnp.tile(cnt, (1, reps))
    r1 = jnp.sum(jnp.where(sel1, cum, 0.0), axis=0, keepdims=True)
    r2 = jnp.sum(jnp.where(sel2, cum, 0.0), axis=0, keepdims=True)
    new_cnt = cnt + jnp.sum(oh, axis=1, keepdims=True)
    slab_t = jnp.where(sub == 0.0, e1,
             jnp.where(sub == 1.0, e2,
             jnp.where(sub == 2.0, r1,
             jnp.where(sub == 3.0, r2,
             jnp.where(sub == 4.0, pg * w1,
             jnp.where(sub == 5.0, pg * w2, 0.0))))))
    return _pack_halves(h2), slab_t, new_cnt


def _prompt_kernel(sink_ref, x_ref, gmix_ref, win_ref, gq_ref, gk_ref, gvg_ref, blk_ref, ws_ref, bs_ref,
                   wout_ref, gffn_ref, wr_ref, br_ref, triu_ref, cnt_in_ref,
                   x1_ref, h2_ref, slabt_ref, kwin_ref, vwin_ref, cnt_ref,
                   k2_scr, v2_scr, qn_scr, ua_scr, gvn_scr, cat_scr, cnt_scr):
    b = pl.program_id(0)
    t = pl.program_id(1)
    nt = pl.num_programs(1)
    tt = x_ref.shape[1]
    nblk = tt // WINDOW
    sub = tt // ROW_SPLITS

    @pl.when((b == 0) & (t == 0))
    def _():
        cnt_scr[...] = cnt_in_ref[...]

    @pl.when(t == 0)
    def _():
        k2_scr[:, 0:WINDOW, :] = jnp.zeros((N_KV_HEADS, WINDOW, LANES), jnp.bfloat16)
        v2_scr[:, 0:WINDOW, :] = jnp.zeros((N_KV_HEADS, WINDOW, LANES), jnp.bfloat16)

    pieces = [slice(s * sub, (s + 1) * sub) for s in range(ROW_SPLITS)]
    for s, rs in enumerate(pieces):
        ks = slice(WINDOW + s * sub, WINDOW + (s + 1) * sub)
        z = _in_proj(x_ref[0, rs, :], gmix_ref[...], win_ref[...])
        qn, kn, v, ua, gvn = _heads(z, gq_ref[...], gk_ref[...], gvg_ref[...], blk_ref[...])
        qn_scr[rs, :] = qn
        ua_scr[rs, :] = ua
        gvn_scr[rs, :] = gvn.astype(jnp.bfloat16)
        k0, k1 = _dup_halves(kn)
        v0, v1 = _dup_halves(v)
        k2_scr[0, ks, :] = k0
        k2_scr[1, ks, :] = k1
        v2_scr[0, ks, :] = v0
        v2_scr[1, ks, :] = v1
        if s == ROW_SPLITS - 1:
            @pl.when(t == nt - 1)
            def _():
                kwin_ref[0] = kn[sub - WINDOW:]
                vwin_ref[0] = v[sub - WINDOW:]

    rows = 4 * WINDOW
    band = 2 * WINDOW
    rr = lax.broadcasted_iota(jnp.int32, (rows, band), 0)
    kk = lax.broadcasted_iota(jnp.int32, (rows, band), 1)
    half = ((rr % WINDOW) >= CHUNK).astype(jnp.int32)
    allowed = (kk >= half * CHUNK) & (kk < (3 + half) * CHUNK)
    bias_mid = jnp.where(allowed, 0.0, NEG)
    bias_first = jnp.where(allowed & (kk >= WINDOW), 0.0, NEG)

    def attn_body(j, carry):
        r0 = pl.multiple_of(j * WINDOW, WINDOW)
        bias = jnp.where((t == 0) & (j == 0), bias_first, bias_mid)
        for g in range(N_KV_HEADS):
            c0 = g * 2 * LANES
            qa = qn_scr[pl.ds(r0, WINDOW), c0:c0 + LANES]
            qb = qn_scr[pl.ds(r0, WINDOW), c0 + LANES:c0 + 2 * LANES]
            k2 = k2_scr[g, pl.ds(r0, band), :]
            v2 = v2_scr[g, pl.ds(r0, band), :]
            pa, pb = _attn_block(qa, qb, k2, v2, bias, _sink_rows(sink_ref, g, WINDOW))
            cat_scr[pl.ds(r0, WINDOW), c0:c0 + LANES] = pa.astype(jnp.bfloat16)
            cat_scr[pl.ds(r0, WINDOW), c0 + LANES:c0 + 2 * LANES] = pb.astype(jnp.bfloat16)
        return carry

    lax.fori_loop(0, nblk, attn_body, 0, unroll=True)

    k2_scr[:, 0:WINDOW, :] = k2_scr[:, tt:tt + WINDOW, :]
    v2_scr[:, 0:WINDOW, :] = v2_scr[:, tt:tt + WINDOW, :]

    for h in range(GATE_HEADS):
        w = _causal_ws(ws_ref, h, MLP_CHUNK)
        bcol = bs_ref[:, h:h + 1]
        for c in range(tt // MLP_CHUNK):
            rs = slice(c * MLP_CHUNK, (c + 1) * MLP_CHUNK)
            cs = slice(h * GATE_DIM, (h + 1) * GATE_DIM)
            mix = jnp.dot(w, gvn_scr[rs, cs], preferred_element_type=jnp.float32) + bcol
            cat_scr[rs, Q_WIDTH + h * GATE_DIM:Q_WIDTH + (h + 1) * GATE_DIM] = (ua_scr[rs, cs] * mix).astype(jnp.bfloat16)

    cnt = cnt_scr[...]
    x1_next = _out_proj(cat_scr[pieces[0], :], x_ref[0, pieces[0], :], wout_ref[...])
    for s, rs in enumerate(pieces):
        x1 = x1_next
        if s + 1 < ROW_SPLITS:
            x1_next = _out_proj(cat_scr[pieces[s + 1], :], x_ref[0, pieces[s + 1], :], wout_ref[...])
        x1_ref[rs, :] = x1
        hp, slab_t, cnt = _route(x1, gffn_ref[...], wr_ref[...], br_ref[...], triu_ref[...], cnt)
        h2_ref[rs, :] = hp
        slabt_ref[:, rs] = slab_t
    cnt_scr[...] = cnt
    cnt_ref[...] = cnt


def _sample_kernel(sink_ref, x_ref, ck_ref, cv_ref, gmix_ref, win_ref, gq_ref, gk_ref, gvg_ref, blk_ref, ws_ref,
                   bs_ref, wout_ref, gffn_ref, wr_ref, br_ref, triu_ref,
                   x1_ref, h2_ref, slabt_ref, kn_ref, v_ref, gvn_ref, cnt_ref,
                   ck2_scr, cv2_scr, k2_scr, v2_scr, qn_scr, ua_scr, cat_scr, *, n_seq, seq_len):
    x = x_ref[...]
    qn, kn, v, ua, gvn = _heads(_in_proj(x, gmix_ref[...], win_ref[...]), gq_ref[...], gk_ref[...], gvg_ref[...],
                                blk_ref[...])
    kn_ref[...] = kn
    v_ref[...] = v
    gvn_ref[...] = gvn
    qn_scr[...] = qn
    ua_scr[...] = ua
    for scr, val in ((k2_scr, kn), (v2_scr, v), (ck2_scr, ck_ref[...]), (cv2_scr, cv_ref[...])):
        a0, a1 = _dup_halves(val)
        scr[0] = a0
        scr[1] = a1

    ws = [_causal_ws(ws_ref, h, seq_len) for h in range(GATE_HEADS)]

    def seq_body(i, carry):
        r0 = pl.multiple_of(i * seq_len, seq_len)
        c0r = pl.multiple_of(i * WINDOW, WINDOW)
        for g in range(N_KV_HEADS):
            c0 = g * 2 * LANES
            qa = qn_scr[pl.ds(r0, seq_len), c0:c0 + LANES]
            qb = qn_scr[pl.ds(r0, seq_len), c0 + LANES:c0 + 2 * LANES]
            k2 = jnp.concatenate([ck2_scr[g, pl.ds(c0r, WINDOW), :], k2_scr[g, pl.ds(r0, seq_len), :]], axis=0)
            v2 = jnp.concatenate([cv2_scr[g, pl.ds(c0r, WINDOW), :], v2_scr[g, pl.ds(r0, seq_len), :]], axis=0)
            pa, pb = _attn_block(qa, qb, k2, v2, None, _sink_rows(sink_ref, g, seq_len))
            cat_scr[pl.ds(r0, seq_len), c0:c0 + LANES] = pa.astype(jnp.bfloat16)
            cat_scr[pl.ds(r0, seq_len), c0 + LANES:c0 + 2 * LANES] = pb.astype(jnp.bfloat16)
        for h in range(GATE_HEADS):
            cs = slice(h * GATE_DIM, (h + 1) * GATE_DIM)
            gv_h = gvn_ref[pl.ds(r0, seq_len), cs].astype(jnp.bfloat16)
            mix = jnp.dot(ws[h], gv_h, preferred_element_type=jnp.float32) + bs_ref[0:seq_len, h:h + 1]
            cat_scr[pl.ds(r0, seq_len), Q_WIDTH + h * GATE_DIM:Q_WIDTH + (h + 1) * GATE_DIM] = (
                ua_scr[pl.ds(r0, seq_len), cs] * mix).astype(jnp.bfloat16)
        return carry

    lax.fori_loop(0, n_seq, seq_body, 0)

    cnt0 = jnp.zeros((N_EXPERTS, LANES), jnp.float32)
    x1 = _out_proj(cat_scr[...], x, wout_ref[...])
    hp, slab_t, new_cnt = _route(x1, gffn_ref[...], wr_ref[...], br_ref[...], triu_ref[...], cnt0)
    x1_ref[...] = x1
    h2_ref[...] = hp
    slabt_ref[...] = slab_t
    cnt_ref[...] = new_cnt


def _expert_kernel(ts_ref, te_ref, cnt_ref, nu_ref, xa_hbm, xb_hbm, wg_ref, wu_ref, wd_ref, oa_hbm, ob_hbm,
                   wg_s, wu_s, wd_s, xa_buf, xb_buf, oa_buf, ob_buf, in_sem, out_sem):
    e = pl.program_id(0)
    nbuf, tm, _ = xa_buf.shape
    n_used = nu_ref[0]

    def rows_of(g):
        return pl.ds(pl.multiple_of(g * tm, tm), tm)

    def in_copies(g, slot):
        return (pltpu.make_async_copy(xa_hbm.at[rows_of(g)], xa_buf.at[slot], in_sem.at[0, slot]),
                pltpu.make_async_copy(xb_hbm.at[rows_of(g)], xb_buf.at[slot], in_sem.at[1, slot]))

    def out_copies(g, slot):
        return (pltpu.make_async_copy(oa_buf.at[slot], oa_hbm.at[rows_of(g)], out_sem.at[0, slot]),
                pltpu.make_async_copy(ob_buf.at[slot], ob_hbm.at[rows_of(g)], out_sem.at[1, slot]))

    @pl.when(e == 0)
    def _():
        for g in range(nbuf - 1):
            @pl.when(g < n_used)
            def _():
                for c in in_copies(g, g):
                    c.start()

    g_lo = ts_ref[e]
    g_hi = te_ref[e]

    @pl.when(g_hi > g_lo)
    def _():
        wg_s[...] = wg_ref[0].astype(jnp.bfloat16)
        wu_s[...] = wu_ref[0].astype(jnp.bfloat16)
        wd_s[...] = wd_ref[0].astype(jnp.bfloat16)

    def tile_body(g, slot):
        for c in in_copies(g, slot):
            c.wait()
        ahead = g + (nbuf - 1)
        ahead_slot = jnp.where(slot == 0, nbuf - 1, slot - 1)

        @pl.when(ahead < n_used)
        def _():
            for c in in_copies(ahead, ahead_slot):
                c.start()

        live = lax.broadcasted_iota(jnp.int32, (tm, xa_buf.shape[2]), 0) < cnt_ref[e] - (g - g_lo) * tm
        ha, la = _unpack_halves(jnp.where(live, xa_buf[slot], jnp.uint32(0)))
        hb, lb = _unpack_halves(jnp.where(live, xb_buf[slot], jnp.uint32(0)))
        xs = jnp.concatenate([ha, hb, la, lb], axis=1).astype(jnp.bfloat16)
        acts = []
        for r in range(0, tm, tm // 2):
            hg = jnp.dot(xs[r:r + tm // 2], wg_s[...], preferred_element_type=jnp.float32)
            hu = jnp.dot(xs[r:r + tm // 2], wu_s[...], preferred_element_type=jnp.float32)
            acts.append((jax.nn.silu(hg) * hu).astype(jnp.bfloat16))
        packed = jnp.concatenate(
            [_pack_halves(jnp.dot(a, wd_s[...], preferred_element_type=jnp.float32)) for a in acts], axis=0)
        q = packed.shape[1] // 2

        @pl.when(g >= nbuf)
        def _():
            for c in out_copies(g - nbuf, slot):
                c.wait()

        oa_buf[slot] = packed[:, :q]
        ob_buf[slot] = packed[:, q:]
        for c in out_copies(g, slot):
            c.start()
        return jnp.where(slot == nbuf - 1, 0, slot + 1)

    lax.fori_loop(g_lo, g_hi, tile_body, g_lo % nbuf)

    @pl.when(e == pl.num_programs(0) - 1)
    def _():
        for back in range(1, nbuf + 1):
            @pl.when(n_used >= back)
            def _():
                g = n_used - back
                for c in out_copies(g, g % nbuf):
                    c.wait()


def _pos_kernel(slabt_ref, offs_ref, pos0_ref, pos1_ref):
    st = slabt_ref[...]
    t = st.shape[1]
    row = lax.broadcasted_iota(jnp.int32, (N_EXPERTS, t), 0).astype(jnp.float32)
    offs = jnp.tile(offs_ref[...], (1, t // LANES))
    for s, out in enumerate((pos0_ref, pos1_ref)):
        first = jnp.sum(jnp.where(row == st[s:s + 1], offs, 0.0), axis=0, keepdims=True)
        out[...] = (first + st[2 + s:3 + s]).astype(jnp.int32)


def _combine_kernel(x1_ref, o1a_ref, o1b_ref, o2a_ref, o2b_ref, slabt_ref, *rest):
    y_ref = rest[-1]
    st = slabt_ref[...]
    slab = jnp.concatenate([st, jnp.zeros((LANES - st.shape[0], st.shape[1]), jnp.float32)], axis=0).T
    g1 = slab[:, 4:5]
    g2 = slab[:, 5:6]
    q = o1a_ref.shape[1]
    for c, (r1, r2) in enumerate(((o1a_ref, o2a_ref), (o1b_ref, o2b_ref))):
        h1, l1 = _unpack_halves(r1[...])
        h2, l2 = _unpack_halves(r2[...])
        hs = slice(c * q, (c + 1) * q)
        ls = slice(2 * q + c * q, 2 * q + (c + 1) * q)
        y_ref[:, hs] = x1_ref[:, hs] + (g1 * h1 + g2 * h2)
        y_ref[:, ls] = x1_ref[:, ls] + (g1 * l1 + g2 * l2)


def _sc_mesh():
    return plsc.VectorSubcoreMesh(core_axis_name="core", subcore_axis_name="subcore")


def _sc_scatter_rows(parts, pos0, pos1, n_rows):
    n_piece = parts[0].shape[1] // SC_COLS

    def sc_kernel(*refs):
        x_refs = refs[:len(parts)]
        i0_hbm, i1_hbm = refs[len(parts):len(parts) + 2]
        o_refs = refs[len(parts) + 2:]
        tok0 = 0
        for x_hbm, arr in zip(x_refs, parts):
            blk0 = tok0 // SC_WINDOW
            for c in range(n_piece):
                def body(x_vmem, i0_vmem, i1_vmem, o_hbm=o_refs[c]):
                    pltpu.sync_copy(x_vmem, o_hbm.at[i0_vmem.at[0]])
                    pltpu.sync_copy(x_vmem, o_hbm.at[i1_vmem.at[0]])

                pltpu.emit_pipeline(
                    body,
                    grid=(arr.shape[0] // SC_WINDOW,),
                    in_specs=[pl.BlockSpec((SC_WINDOW, SC_COLS), lambda i, c=c: (i, c)),
                              pl.BlockSpec((1, SC_WINDOW), lambda i, blk0=blk0: (0, blk0 + i)),
                              pl.BlockSpec((1, SC_WINDOW), lambda i, blk0=blk0: (0, blk0 + i))],
                    out_specs=[],
                    core_axis_name=("core", "subcore"),
                    dimension_semantics=(pltpu.PARALLEL,),
                )(x_hbm, i0_hbm, i1_hbm)
            tok0 += arr.shape[0]

    piece = jax.ShapeDtypeStruct((n_rows, SC_COLS), jnp.uint32)
    return pl.kernel(sc_kernel, out_type=(piece,) * n_piece, mesh=_sc_mesh(), name="scatter_rows")(
        *parts, pos0, pos1)


def _sc_gather_rows(pieces, pos_list):
    n_tok = pos_list[0].shape[1]

    def sc_kernel(*refs):
        s_refs = refs[:len(pieces)]
        i_refs = refs[len(pieces):len(pieces) + len(pos_list)]
        o_refs = refs[len(pieces) + len(pos_list):]
        k = 0
        for i_hbm in i_refs:
            for s_hbm in s_refs:
                def body(i_vmem, o_vmem, s_hbm=s_hbm):
                    pltpu.sync_copy(s_hbm.at[i_vmem.at[0]], o_vmem)

                pltpu.emit_pipeline(
                    body,
                    grid=(n_tok // SC_WINDOW,),
                    in_specs=[pl.BlockSpec((1, SC_WINDOW), lambda i: (0, i))],
                    out_specs=[pl.BlockSpec((SC_WINDOW, SC_COLS), lambda i: (i, 0))],
                    core_axis_name=("core", "subcore"),
                    dimension_semantics=(pltpu.PARALLEL,),
                )(i_hbm, o_refs[k])
                k += 1

    out = jax.ShapeDtypeStruct((n_tok, SC_COLS), jnp.uint32)
    outs = pl.kernel(sc_kernel, out_type=(out,) * (len(pieces) * len(pos_list)), mesh=_sc_mesh(),
                     name="gather_rows")(*pieces, *pos_list)
    return [outs[j * len(pieces):(j + 1) * len(pieces)] for j in range(len(pos_list))]


def _const_spec(shape, single_buffer=False):
    nd = len(shape)
    mode = pl.Buffered(1) if single_buffer else None
    return pl.BlockSpec(shape, lambda *_: (0,) * nd, pipeline_mode=mode)


def kernel(x_prompt, x_sample, cache_attn_k, cache_attn_v, g_mix, w_in, g_q, g_k, g_v, attn_sinks, w_s, b_s,
           w_out, g_ffn, w_coarse, b_coarse, w_fine, b_fine, w_gate, w_up, w_down):
    nb, seq, d = x_prompt.shape
    ns, slen, _ = x_sample.shape
    n_p = nb * seq
    n_s = ns * slen
    n_tok = n_p + n_s
    tt = TOKEN_TILE
    assert seq % tt == 0 and n_s % tt == 0 and d == D_MODEL
    nt = seq // tt
    bf = jnp.bfloat16
    f32 = jnp.float32

    l = 0
    gmix = g_mix[l].reshape(1, d)
    win = w_in[l].astype(bf)
    gq = (jnp.tile(g_q[l], N_Q_HEADS) * (HEAD_DIM ** -0.5)).reshape(1, Q_WIDTH)
    gk = jnp.tile(g_k[l], N_KV_HEADS).reshape(1, KV_WIDTH)
    gvg = g_v[l].reshape(1, GATE_WIDTH)
    sinks = attn_sinks[l].reshape(N_Q_HEADS).astype(f32)
    ws = w_s[l].astype(bf)
    bs = b_s[l].T
    wout = w_out[l].astype(bf)
    gffn = g_ffn[l].reshape(1, d)
    wr = jnp.concatenate([w_coarse[l], jnp.transpose(w_fine[l], (1, 0, 2)).reshape(d, N_EXPERTS),
                          jnp.zeros((d, ROUTE_LANES - N_GROUPS - N_EXPERTS), f32)], axis=1)
    wr = wr.astype(bf).T
    br = jnp.concatenate([b_coarse[l], b_fine[l].reshape(-1),
                          jnp.zeros((ROUTE_LANES - N_GROUPS - N_EXPERTS,), f32)])
    br = jnp.broadcast_to(br[:, None], (ROUTE_LANES, LANES))
    cnt_shape = (N_EXPERTS, LANES)
    ii = jnp.arange(LANES)
    blk = (ii[:, None] // HEAD_DIM == ii[None, :] // HEAD_DIM).astype(bf)
    u32 = jnp.uint32
    dh = d // 2

    def strict_triu(n):
        r = jnp.arange(n)
        return (r[:, None] < r[None, :]).astype(bf)

    weight_args = (gmix, win, gq, gk, gvg, blk, ws, bs, wout, gffn, wr, br)
    weight_specs = [_const_spec(a.shape, single_buffer=True) for a in weight_args]
    smem_spec = pl.BlockSpec(memory_space=pltpu.SMEM)

    xs2 = x_sample.reshape(n_s, d)
    ck = cache_attn_k[l].reshape(ns * WINDOW, KV_WIDTH)
    cv = cache_attn_v[l].reshape(ns * WINDOW, KV_WIDTH)
    tok_out = lambda n, w, dt: jax.ShapeDtypeStruct((n, w), dt)
    x1_s, h2_s, slabt_s, kn_s, v_s, gvn_s, cnt_s = pl.pallas_call(
        functools.partial(_sample_kernel, n_seq=ns, seq_len=slen),
        grid=(1,),
        in_specs=[smem_spec, _const_spec((n_s, d)), _const_spec(ck.shape), _const_spec(cv.shape)]
                 + weight_specs + [_const_spec((n_s, n_s))],
        out_specs=[_const_spec((n_s, d)), _const_spec((n_s, dh)),
                   _const_spec((ROUTE_ROWS, n_s)),
                   _const_spec((n_s, KV_WIDTH)), _const_spec((n_s, KV_WIDTH)), _const_spec((n_s, GATE_WIDTH)),
                   _const_spec(cnt_shape)],
        out_shape=[tok_out(n_s, d, f32), tok_out(n_s, dh, u32),
                   jax.ShapeDtypeStruct((ROUTE_ROWS, n_s), f32),
                   jax.ShapeDtypeStruct((n_s, KV_WIDTH), f32), jax.ShapeDtypeStruct((n_s, KV_WIDTH), f32),
                   jax.ShapeDtypeStruct((n_s, GATE_WIDTH), f32), jax.ShapeDtypeStruct(cnt_shape, f32)],
        scratch_shapes=[pltpu.VMEM((N_KV_HEADS, ns * WINDOW, LANES), bf), pltpu.VMEM((N_KV_HEADS, ns * WINDOW, LANES), bf),
                        pltpu.VMEM((N_KV_HEADS, n_s, LANES), bf), pltpu.VMEM((N_KV_HEADS, n_s, LANES), bf),
                        pltpu.VMEM((n_s, Q_WIDTH), bf), pltpu.VMEM((n_s, GATE_WIDTH), f32),
                        pltpu.VMEM((n_s, d), bf)],
        compiler_params=pltpu.CompilerParams(dimension_semantics=("arbitrary",), vmem_limit_bytes=VMEM_LIMIT),
        name="mixer_sample",
    )(sinks, xs2, ck, cv, *weight_args, strict_triu(n_s))

    def prompt_mixer(b0, nbw, cnt_in):
        n_w = nbw * seq
        return pl.pallas_call(
            _prompt_kernel,
            grid=(nbw, nt),
            in_specs=[smem_spec, pl.BlockSpec((1, tt, d), lambda b, t: (b0 + b, t, 0))] + weight_specs
                     + [_const_spec((tt // ROW_SPLITS, tt // ROW_SPLITS)), _const_spec(cnt_shape)],
            out_specs=[pl.BlockSpec((tt, d), lambda b, t: (b * nt + t, 0)),
                       pl.BlockSpec((tt, dh), lambda b, t: (b * nt + t, 0)),
                       pl.BlockSpec((ROUTE_ROWS, tt), lambda b, t: (0, b * nt + t)),
                       pl.BlockSpec((1, WINDOW, KV_WIDTH), lambda b, t: (b, 0, 0)),
                       pl.BlockSpec((1, WINDOW, KV_WIDTH), lambda b, t: (b, 0, 0)),
                       _const_spec(cnt_shape)],
            out_shape=[tok_out(n_w, d, f32), tok_out(n_w, dh, u32),
                       jax.ShapeDtypeStruct((ROUTE_ROWS, n_w), f32),
                       jax.ShapeDtypeStruct((nbw, WINDOW, KV_WIDTH), f32),
                       jax.ShapeDtypeStruct((nbw, WINDOW, KV_WIDTH), f32),
                       jax.ShapeDtypeStruct(cnt_shape, f32)],
            scratch_shapes=[pltpu.VMEM((N_KV_HEADS, tt + WINDOW, LANES), bf),
                            pltpu.VMEM((N_KV_HEADS, tt + WINDOW, LANES), bf),
                            pltpu.VMEM((tt, Q_WIDTH), bf), pltpu.VMEM((tt, GATE_WIDTH), f32),
                            pltpu.VMEM((tt, GATE_WIDTH), bf), pltpu.VMEM((tt, d), bf),
                            pltpu.VMEM(cnt_shape, f32)],
            compiler_params=pltpu.CompilerParams(dimension_semantics=("arbitrary", "arbitrary"),
                                                 vmem_limit_bytes=VMEM_LIMIT),
            name="mixer_prompt",
        )(sinks, x_prompt, *weight_args, strict_triu(tt // ROW_SPLITS), cnt_in)

    tm = EXPERT_TILE
    assert dh == 2 * SC_COLS and n_s % SC_WINDOW == 0 and seq % SC_WINDOW == 0

    def expert_pass(h2_parts, slabt_parts, cnt):
        n_w = sum(a.shape[0] for a in h2_parts)
        slabt = jnp.concatenate(slabt_parts, axis=1) if len(slabt_parts) > 1 else slabt_parts[0]
        counts = cnt[:, 0].astype(jnp.int32)
        tiles = (counts + tm - 1) // tm
        tile_end = jnp.cumsum(tiles)
        offs = (tile_end - tiles) * tm
        n_used = tile_end[-1]
        n_rows = ((2 * n_w) // tm + N_EXPERTS) * tm
        lane_tiles = n_w // LANES
        pos_blk = LANES * max(k for k in range(1, POS_BLOCK_TILES + 1) if lane_tiles % k == 0)
        offs_b = jnp.broadcast_to(offs.astype(f32)[:, None], cnt_shape)
        pos_row = jax.ShapeDtypeStruct((1, n_w), jnp.int32)
        pos0, pos1 = pl.pallas_call(
            _pos_kernel,
            grid=(n_w // pos_blk,),
            in_specs=[pl.BlockSpec((ROUTE_ROWS, pos_blk), lambda i: (0, i)), _const_spec(cnt_shape)],
            out_specs=[pl.BlockSpec((1, pos_blk), lambda i: (0, i)), pl.BlockSpec((1, pos_blk), lambda i: (0, i))],
            out_shape=[pos_row, pos_row],
            compiler_params=pltpu.CompilerParams(dimension_semantics=("arbitrary",)),
            name="sorted_pos",
        )(slabt, offs_b)

        xs_a, xs_b = _sc_scatter_rows(h2_parts, pos0, pos1, n_rows)

        w_map = lambda e, *_: (e, 0, 0)
        hbm = pl.BlockSpec(memory_space=pl.ANY)
        piece = jax.ShapeDtypeStruct((n_rows, SC_COLS), u32)
        tile_buf = pltpu.VMEM((EXPERT_BUFFERS, tm, SC_COLS), u32)
        tile_sems = pltpu.SemaphoreType.DMA((2, EXPERT_BUFFERS))
        out_a, out_b = pl.pallas_call(
            _expert_kernel,
            grid_spec=pltpu.PrefetchScalarGridSpec(
                num_scalar_prefetch=4,
                grid=(N_EXPERTS,),
                in_specs=[hbm, hbm,
                          pl.BlockSpec((1, d, D_EXPERT), w_map),
                          pl.BlockSpec((1, d, D_EXPERT), w_map),
                          pl.BlockSpec((1, D_EXPERT, d), w_map)],
                out_specs=[hbm, hbm],
                scratch_shapes=[pltpu.VMEM((d, D_EXPERT), bf), pltpu.VMEM((d, D_EXPERT), bf),
                                pltpu.VMEM((D_EXPERT, d), bf),
                                tile_buf, tile_buf, tile_buf, tile_buf, tile_sems, tile_sems]),
            out_shape=[piece, piece],
            compiler_params=pltpu.CompilerParams(dimension_semantics=("arbitrary",), vmem_limit_bytes=VMEM_LIMIT),
            name="experts",
        )((tile_end - tiles).astype(jnp.int32), tile_end.astype(jnp.int32), counts,
          n_used.reshape(1).astype(jnp.int32), xs_a, xs_b, w_gate[l], w_up[l], w_down[l])

        return _sc_gather_rows([out_a, out_b], [pos0, pos1])

    def combine(x1, slab_t, gathered, g_blk0, y_rows, y_blk0, y_prev):
        (o1a, o1b), (o2a, o2b) = gathered
        tok = lambda i: (i, 0)
        args = [x1, o1a, o1b, o2a, o2b, slab_t]
        in_specs = ([pl.BlockSpec((tt, d), tok)] + [pl.BlockSpec((tt, SC_COLS), lambda i: (g_blk0 + i, 0))] * 4
                    + [pl.BlockSpec((ROUTE_ROWS, tt), lambda i: (0, i))])
        aliases = {}
        if y_prev is not None:
            aliases = {len(args): 0}
            args.append(y_prev)
            in_specs.append(pl.BlockSpec(memory_space=pl.ANY))
        return pl.pallas_call(
            _combine_kernel,
            grid=(x1.shape[0] // tt,),
            in_specs=in_specs,
            out_specs=pl.BlockSpec((tt, d), lambda i: (y_blk0 + i, 0)),
            out_shape=jax.ShapeDtypeStruct((y_rows, d), f32),
            input_output_aliases=aliases,
            compiler_params=pltpu.CompilerParams(dimension_semantics=("arbitrary",)),
            name="combine",
        )(*args)

    assert nb % PROMPT_WAVES == 0
    nbw = nb // PROMPT_WAVES
    n_pw = nbw * seq
    y_p = None
    y_s = None
    kwins, vwins = [], []
    mixed = []
    for w in range(PROMPT_WAVES):
        cnt_in = cnt_s if w == 0 else jnp.zeros(cnt_shape, f32)
        mixed.append(prompt_mixer(w * nbw, nbw, cnt_in))
    for w, (x1_w, h2_w, slabt_w, kwin_w, vwin_w, cnt_w) in enumerate(mixed):
        kwins.append(kwin_w)
        vwins.append(vwin_w)
        h2_parts, slabt_parts = [h2_w], [slabt_w]
        if w == 0:
            h2_parts.append(h2_s)
            slabt_parts.append(slabt_s)
        gathered = expert_pass(h2_parts, slabt_parts, cnt_w)
        y_p = combine(x1_w, slabt_w, gathered, 0, n_p, w * (n_pw // tt), y_p)
        if w == 0:
            y_s = combine(x1_s, slabt_s, gathered, n_pw // tt, n_s, 0, None)
    y_p = y_p.reshape(nb, seq, d)
    y_s = y_s.reshape(ns, slen, d)
    kwin = jnp.concatenate(kwins, axis=0) if PROMPT_WAVES > 1 else kwins[0]
    vwin = jnp.concatenate(vwins, axis=0) if PROMPT_WAVES > 1 else vwins[0]

    kv_shape = (1, nb, WINDOW, N_KV_HEADS, HEAD_DIM)
    new_k_p = kwin.reshape(kv_shape)
    new_v_p = vwin.reshape(kv_shape)
    keep = WINDOW - slen
    ck4 = cache_attn_k[l][:, WINDOW - keep:]
    cv4 = cache_attn_v[l][:, WINDOW - keep:]
    new_k_s = jnp.concatenate([ck4, kn_s.reshape(ns, slen, N_KV_HEADS, HEAD_DIM)], axis=1)[None]
    new_v_s = jnp.concatenate([cv4, v_s.reshape(ns, slen, N_KV_HEADS, HEAD_DIM)], axis=1)[None]
    new_gv_s = gvn_s.reshape(1, ns, slen, GATE_HEADS, GATE_DIM)
    return (y_p, y_s, new_k_p, new_v_p, new_k_s, new_v_s, new_gv_s)
```

```python
import functools

import jax
import jax.numpy as jnp
from jax import lax
from jax.experimental import pallas as pl
from jax.experimental.pallas import tpu as pltpu
from jax.experimental.pallas import tpu_sc as plsc

D_MODEL = 1024
HEAD_DIM = 64
N_Q_HEADS = 8
N_KV_HEADS = 2
Q_WIDTH = N_Q_HEADS * HEAD_DIM
KV_WIDTH = N_KV_HEADS * HEAD_DIM
GATE_HEADS = 4
GATE_DIM = 128
GATE_WIDTH = GATE_HEADS * GATE_DIM
PROJ_COLS = Q_WIDTH + 2 * KV_WIDTH + 2 * GATE_WIDTH
CHUNK = 64
WINDOW = 128
MLP_CHUNK = 128
N_GROUPS = 8
EXP_PER_GROUP = 8
N_EXPERTS = N_GROUPS * EXP_PER_GROUP
D_EXPERT = 512
EPS = 1e-6

LANES = 128
SUBLANES = 8
ROUTE_LANES = 128
FINE_LANE0 = N_GROUPS
ROUTE_ROWS = SUBLANES
NEG = -1e30
assert N_GROUPS == SUBLANES and EXP_PER_GROUP == SUBLANES

PROMPT_WAVES = (5, 3)
TOKEN_TILE = 512
ROW_SPLITS = 2
EXPERT_TILE = 512
EXPERT_BUFFERS = 3
POS_BLOCK_TILES = 64
SC_WINDOW = 128
SC_COLS = 256
VMEM_LIMIT = 40 * 1024 * 1024


def _pack_halves(x):
    w = x.shape[1] // 2
    b = lax.bitcast_convert_type(x.astype(jnp.bfloat16).astype(jnp.float32), jnp.uint32)
    return (b[:, :w] & jnp.uint32(0xFFFF0000)) | (b[:, w:] >> 16)


def _unpack_halves(p):
    hi = lax.bitcast_convert_type(p & jnp.uint32(0xFFFF0000), jnp.float32)
    lo = lax.bitcast_convert_type(p << 16, jnp.float32)
    return hi, lo


def _rms(x, eps=EPS):
    return x * lax.rsqrt(jnp.mean(x * x, axis=-1, keepdims=True) + eps)


def _in_proj(x, gmix, win):
    h = (_rms(x) * gmix).astype(jnp.bfloat16)
    return jnp.dot(h, win, preferred_element_type=jnp.float32)


def _heads(z, gq, gk, gvg, blk):
    qk = z[:, :Q_WIDTH + KV_WIDTH]
    outs = []
    for j in range((Q_WIDTH + KV_WIDTH) // LANES):
        zj = qk[:, j * LANES:(j + 1) * LANES]
        ss = jnp.dot((zj * zj).astype(jnp.bfloat16), blk, preferred_element_type=jnp.float32)
        outs.append(zj * lax.rsqrt(ss * (1.0 / HEAD_DIM) + EPS))
    qn = jnp.concatenate(outs[:Q_WIDTH // LANES], axis=-1) * gq
    kn = outs[-1] * gk
    v = z[:, Q_WIDTH + KV_WIDTH:Q_WIDTH + 2 * KV_WIDTH]
    u0 = Q_WIDTH + 2 * KV_WIDTH
    ua = jax.nn.gelu(z[:, u0:u0 + GATE_WIDTH])
    ga = jax.nn.gelu(z[:, u0 + GATE_WIDTH:])
    gvn = jnp.concatenate(
        [_rms(ga[:, i * GATE_DIM:(i + 1) * GATE_DIM]) for i in range(GATE_HEADS)], axis=-1) * gvg
    return qn.astype(jnp.bfloat16), kn, v, ua, gvn


def _dup_halves(a):
    lo = lax.broadcasted_iota(jnp.int32, a.shape, 1) < HEAD_DIM
    r = pltpu.roll(a, HEAD_DIM, axis=1)
    return (jnp.where(lo, a, r).astype(jnp.bfloat16), jnp.where(lo, r, a).astype(jnp.bfloat16))


def _attn_block(qa, qb, k2, v2, bias, sk):
    r = qa.shape[0]
    lo = lax.broadcasted_iota(jnp.int32, (r, LANES), 1) < HEAD_DIM
    zero = jnp.zeros_like(qa)
    qs = jnp.concatenate([jnp.where(lo, qa, zero), jnp.where(lo, zero, qa),
                          jnp.where(lo, qb, zero), jnp.where(lo, zero, qb)], axis=0)
    s = lax.dot_general(qs, k2, (((1,), (1,)), ((), ())), preferred_element_type=jnp.float32)
    if bias is not None:
        s = s + bias
    m = jnp.maximum(jnp.max(s, axis=-1, keepdims=True), sk)
    e = jnp.exp(s - m)
    den = jnp.sum(e, axis=-1, keepdims=True) + jnp.exp(sk - m)
    o = jnp.dot(e.astype(jnp.bfloat16), v2, preferred_element_type=jnp.float32) * (1.0 / den)
    pa = jnp.where(lo, o[0:r], o[r:2 * r])
    pb = jnp.where(lo, o[2 * r:3 * r], o[3 * r:4 * r])
    return pa, pb


def _sink_rows(sink_ref, g, r):
    row = lax.broadcasted_iota(jnp.int32, (4 * r, 1), 0)
    s0, s1, s2, s3 = (sink_ref[4 * g + i] for i in range(4))
    return jnp.where(row < r, s0, jnp.where(row < 2 * r, s1, jnp.where(row < 3 * r, s2, s3)))


def _causal_ws(ws_ref, h, n):
    w = ws_ref[h][:n, :n]
    keep = (lax.broadcasted_iota(jnp.int32, (n, n), 0) >= lax.broadcasted_iota(jnp.int32, (n, n), 1))
    return jnp.where(keep, w, jnp.zeros_like(w))


def _out_proj(cat, x, wout):
    return x + jnp.dot(cat, wout, preferred_element_type=jnp.float32)


def _route(x1, gffn, wrt, brt, triu, cnt):
    h2 = _rms(x1) * gffn
    t = x1.shape[0]
    reps = t // LANES
    lgt = lax.dot_general(wrt, h2.astype(jnp.bfloat16), (((1,), (1,)), ((), ())),
                          preferred_element_type=jnp.float32) + jnp.tile(brt, (1, reps))
    ng = float(EXP_PER_GROUP)
    sub = lax.broadcasted_iota(jnp.int32, (EXP_PER_GROUP, t), 0).astype(jnp.float32)
    c = lgt[0:N_GROUPS]
    mc = jnp.max(c, axis=0, keepdims=True)
    grp = jnp.min(jnp.where(c == mc, sub, ng), axis=0, keepdims=True)
    pg = 1.0 / jnp.sum(jnp.exp(c - mc), axis=0, keepdims=True)
    lf = lgt[FINE_LANE0:FINE_LANE0 + EXP_PER_GROUP]
    for g in range(1, N_GROUPS):
        r0 = FINE_LANE0 + g * EXP_PER_GROUP
        lf = jnp.where(grp == float(g), lgt[r0:r0 + EXP_PER_GROUP], lf)
    v1 = jnp.max(lf, axis=0, keepdims=True)
    i1 = jnp.min(jnp.where(lf == v1, sub, ng), axis=0, keepdims=True)
    lf2 = jnp.where(sub == i1, -jnp.inf, lf)
    v2 = jnp.max(lf2, axis=0, keepdims=True)
    i2 = jnp.min(jnp.where(lf2 == v2, sub, ng), axis=0, keepdims=True)
    tt = jnp.exp(v2 - v1)
    w1 = 1.0 / (1.0 + tt)
    w2 = tt * w1
    e1 = grp * ng + i1
    e2 = grp * ng + i2
    row = lax.broadcasted_iota(jnp.int32, (N_EXPERTS, t), 0).astype(jnp.float32)
    sel1 = row == e1
    sel2 = row == e2
    oh = jnp.where(sel1 | sel2, 1.0, 0.0)
    cum = jnp.dot(oh.astype(jnp.bfloat16), triu, preferred_element_type=jnp.float32) + jnp.tile(cnt, (1, reps))
    r1 = jnp.sum(jnp.where(sel1, cum, 0.0), axis=0, keepdims=True)
    r2 = jnp.sum(jnp.where(sel2, cum, 0.0), axis=0, keepdims=True)
    new_cnt = cnt + jnp.sum(oh, axis=1, keepdims=True)
    slab_t = jnp.where(sub == 0.0, e1,
             jnp.where(sub == 1.0, e2,
             jnp.where(sub == 2.0, r1,
             jnp.where(sub == 3.0, r2,
             jnp.where(sub == 4.0, pg * w1,
             jnp.where(sub == 5.0, pg * w2, 0.0))))))
    return _pack_halves(h2), slab_t, new_cnt


def _prompt_kernel(sink_ref, x_ref, gmix_ref, win_ref, gq_ref, gk_ref, gvg_ref, blk_ref, ws_ref, bs_ref,
                   wout_ref, gffn_ref, wr_ref, br_ref, triu_ref, cnt_in_ref,
                   x1_ref, h2_ref, slabt_ref, kwin_ref, vwin_ref, cnt_ref,
                   k2_scr, v2_scr, qn_scr, ua_scr, gvn_scr, cat_scr, cnt_scr):
    b = pl.program_id(0)
    t = pl.program_id(1)
    nt = pl.num_programs(1)
    tt = x_ref.shape[1]
    nblk = tt // WINDOW
    sub = tt // ROW_SPLITS

    @pl.when((b == 0) & (t == 0))
    def _():
        cnt_scr[...] = cnt_in_ref[...]

    @pl.when(t == 0)
    def _():
        k2_scr[:, 0:WINDOW, :] = jnp.zeros((N_KV_HEADS, WINDOW, LANES), jnp.bfloat16)
        v2_scr[:, 0:WINDOW, :] = jnp.zeros((N_KV_HEADS, WINDOW, LANES), jnp.bfloat16)

    pieces = [slice(s * sub, (s + 1) * sub) for s in range(ROW_SPLITS)]
    for s, rs in enumerate(pieces):
        ks = slice(WINDOW + s * sub, WINDOW + (s + 1) * sub)
        z = _in_proj(x_ref[0, rs, :], gmix_ref[...], win_ref[...])
        qn, kn, v, ua, gvn = _heads(z, gq_ref[...], gk_ref[...], gvg_ref[...], blk_ref[...])
        qn_scr[rs, :] = qn
        ua_scr[rs, :] = ua
        gvn_scr[rs, :] = gvn.astype(jnp.bfloat16)
        k0, k1 = _dup_halves(kn)
        v0, v1 = _dup_halves(v)
        k2_scr[0, ks, :] = k0
        k2_scr[1, ks, :] = k1
        v2_scr[0, ks, :] = v0
        v2_scr[1, ks, :] = v1
        if s == ROW_SPLITS - 1:
            @pl.when(t == nt - 1)
            def _():
                kwin_ref[0] = kn[sub - WINDOW:]
                vwin_ref[0] = v[sub - WINDOW:]

    rows = 4 * WINDOW
    band = 2 * WINDOW
    rr = lax.broadcasted_iota(jnp.int32, (rows, band), 0)
    kk = lax.broadcasted_iota(jnp.int32, (rows, band), 1)
    half = ((rr % WINDOW) >= CHUNK).astype(jnp.int32)
    allowed = (kk >= half * CHUNK) & (kk < (3 + half) * CHUNK)
    bias_mid = jnp.where(allowed, 0.0, NEG)
    bias_first = jnp.where(allowed & (kk >= WINDOW), 0.0, NEG)

    def attn_body(j, carry):
        r0 = pl.multiple_of(j * WINDOW, WINDOW)
        bias = jnp.where((t == 0) & (j == 0), bias_first, bias_mid)
        for g in range(N_KV_HEADS):
            c0 = g * 2 * LANES
            qa = qn_scr[pl.ds(r0, WINDOW), c0:c0 + LANES]
            qb = qn_scr[pl.ds(r0, WINDOW), c0 + LANES:c0 + 2 * LANES]
            k2 = k2_scr[g, pl.ds(r0, band), :]
            v2 = v2_scr[g, pl.ds(r0, band), :]
            pa, pb = _attn_block(qa, qb, k2, v2, bias, _sink_rows(sink_ref, g, WINDOW))
            cat_scr[pl.ds(r0, WINDOW), c0:c0 + LANES] = pa.astype(jnp.bfloat16)
            cat_scr[pl.ds(r0, WINDOW), c0 + LANES:c0 + 2 * LANES] = pb.astype(jnp.bfloat16)
        return carry

    lax.fori_loop(0, nblk, attn_body, 0, unroll=True)

    k2_scr[:, 0:WINDOW, :] = k2_scr[:, tt:tt + WINDOW, :]
    v2_scr[:, 0:WINDOW, :] = v2_scr[:, tt:tt + WINDOW, :]

    for h in range(GATE_HEADS):
        w = _causal_ws(ws_ref, h, MLP_CHUNK)
        bcol = bs_ref[:, h:h + 1]
        for c in range(tt // MLP_CHUNK):
            rs = slice(c * MLP_CHUNK, (c + 1) * MLP_CHUNK)
            cs = slice(h * GATE_DIM, (h + 1) * GATE_DIM)
            mix = jnp.dot(w, gvn_scr[rs, cs], preferred_element_type=jnp.float32) + bcol
            cat_scr[rs, Q_WIDTH + h * GATE_DIM:Q_WIDTH + (h + 1) * GATE_DIM] = (ua_scr[rs, cs] * mix).astype(jnp.bfloat16)

    cnt = cnt_scr[...]
    x1_next = _out_proj(cat_scr[pieces[0], :], x_ref[0, pieces[0], :], wout_ref[...])
    for s, rs in enumerate(pieces):
        x1 = x1_next
        if s + 1 < ROW_SPLITS:
            x1_next = _out_proj(cat_scr[pieces[s + 1], :], x_ref[0, pieces[s + 1], :], wout_ref[...])
        x1_ref[rs, :] = x1
        hp, slab_t, cnt = _route(x1, gffn_ref[...], wr_ref[...], br_ref[...], triu_ref[...], cnt)
        h2_ref[rs, :] = hp
        slabt_ref[:, rs] = slab_t
    cnt_scr[...] = cnt
    cnt_ref[...] = cnt


def _sample_kernel(sink_ref, x_ref, ck_ref, cv_ref, gmix_ref, win_ref, gq_ref, gk_ref, gvg_ref, blk_ref, ws_ref,
                   bs_ref, wout_ref, gffn_ref, wr_ref, br_ref, triu_ref,
                   x1_ref, h2_ref, slabt_ref, kn_ref, v_ref, gvn_ref, cnt_ref,
                   ck2_scr, cv2_scr, k2_scr, v2_scr, qn_scr, ua_scr, cat_scr, *, n_seq, seq_len):
    x = x_ref[...]
    qn, kn, v, ua, gvn = _heads(_in_proj(x, gmix_ref[...], win_ref[...]), gq_ref[...], gk_ref[...], gvg_ref[...],
                                blk_ref[...])
    kn_ref[...] = kn
    v_ref[...] = v
    gvn_ref[...] = gvn
    qn_scr[...] = qn
    ua_scr[...] = ua
    for scr, val in ((k2_scr, kn), (v2_scr, v), (ck2_scr, ck_ref[...]), (cv2_scr, cv_ref[...])):
        a0, a1 = _dup_halves(val)
        scr[0] = a0
        scr[1] = a1

    ws = [_causal_ws(ws_ref, h, seq_len) for h in range(GATE_HEADS)]

    def seq_body(i, carry):
        r0 = pl.multiple_of(i * seq_len, seq_len)
        c0r = pl.multiple_of(i * WINDOW, WINDOW)
        for g in range(N_KV_HEADS):
            c0 = g * 2 * LANES
            qa = qn_scr[pl.ds(r0, seq_len), c0:c0 + LANES]
            qb = qn_scr[pl.ds(r0, seq_len), c0 + LANES:c0 + 2 * LANES]
            k2 = jnp.concatenate([ck2_scr[g, pl.ds(c0r, WINDOW), :], k2_scr[g, pl.ds(r0, seq_len), :]], axis=0)
            v2 = jnp.concatenate([cv2_scr[g, pl.ds(c0r, WINDOW), :], v2_scr[g, pl.ds(r0, seq_len), :]], axis=0)
            pa, pb = _attn_block(qa, qb, k2, v2, None, _sink_rows(sink_ref, g, seq_len))
            cat_scr[pl.ds(r0, seq_len), c0:c0 + LANES] = pa.astype(jnp.bfloat16)
            cat_scr[pl.ds(r0, seq_len), c0 + LANES:c0 + 2 * LANES] = pb.astype(jnp.bfloat16)
        for h in range(GATE_HEADS):
            cs = slice(h * GATE_DIM, (h + 1) * GATE_DIM)
            gv_h = gvn_ref[pl.ds(r0, seq_len), cs].astype(jnp.bfloat16)
            mix = jnp.dot(ws[h], gv_h, preferred_element_type=jnp.float32) + bs_ref[0:seq_len, h:h + 1]
            cat_scr[pl.ds(r0, seq_len), Q_WIDTH + h * GATE_DIM:Q_WIDTH + (h + 1) * GATE_DIM] = (
                ua_scr[pl.ds(r0, seq_len), cs] * mix).astype(jnp.bfloat16)
        return carry

    lax.fori_loop(0, n_seq, seq_body, 0)

    cnt0 = jnp.zeros((N_EXPERTS, LANES), jnp.float32)
    x1 = _out_proj(cat_scr[...], x, wout_ref[...])
    hp, slab_t, new_cnt = _route(x1, gffn_ref[...], wr_ref[...], br_ref[...], triu_ref[...], cnt0)
    x1_ref[...] = x1
    h2_ref[...] = hp
    slabt_ref[...] = slab_t
    cnt_ref[...] = new_cnt


def _expert_kernel(ts_ref, te_ref, cnt_ref, nu_ref, xa_hbm, xb_hbm, wg_ref, wu_ref, wd_ref, oa_hbm, ob_hbm,
                   wg_s, wu_s, wd_s, xa_buf, xb_buf, oa_buf, ob_buf, in_sem, out_sem):
    e = pl.program_id(0)
    nbuf, tm, _ = xa_buf.shape
    n_used = nu_ref[0]

    def rows_of(g):
        return pl.ds(pl.multiple_of(g * tm, tm), tm)

    def in_copies(g, slot):
        return (pltpu.make_async_copy(xa_hbm.at[rows_of(g)], xa_buf.at[slot], in_sem.at[0, slot]),
                pltpu.make_async_copy(xb_hbm.at[rows_of(g)], xb_buf.at[slot], in_sem.at[1, slot]))

    def out_copies(g, slot):
        return (pltpu.make_async_copy(oa_buf.at[slot], oa_hbm.at[rows_of(g)], out_sem.at[0, slot]),
                pltpu.make_async_copy(ob_buf.at[slot], ob_hbm.at[rows_of(g)], out_sem.at[1, slot]))

    @pl.when(e == 0)
    def _():
        for g in range(nbuf - 1):
            @pl.when(g < n_used)
            def _():
                for c in in_copies(g, g):
                    c.start()

    g_lo = ts_ref[e]
    g_hi = te_ref[e]

    @pl.when(g_hi > g_lo)
    def _():
        wg_s[...] = wg_ref[0].astype(jnp.bfloat16)
        wu_s[...] = wu_ref[0].astype(jnp.bfloat16)
        wd_s[...] = wd_ref[0].astype(jnp.bfloat16)

    def tile_body(g, slot):
        for c in in_copies(g, slot):
            c.wait()
        ahead = g + (nbuf - 1)
        ahead_slot = jnp.where(slot == 0, nbuf - 1, slot - 1)

        @pl.when(ahead < n_used)
        def _():
            for c in in_copies(ahead, ahead_slot):
                c.start()

        live = lax.broadcasted_iota(jnp.int32, (tm, xa_buf.shape[2]), 0) < cnt_ref[e] - (g - g_lo) * tm
        ha, la = _unpack_halves(jnp.where(live, xa_buf[slot], jnp.uint32(0)))
        hb, lb = _unpack_halves(jnp.where(live, xb_buf[slot], jnp.uint32(0)))
        xs = jnp.concatenate([ha, hb, la, lb], axis=1).astype(jnp.bfloat16)
        acts = []
        for r in range(0, tm, tm // 2):
            hg = jnp.dot(xs[r:r + tm // 2], wg_s[...], preferred_element_type=jnp.float32)
            hu = jnp.dot(xs[r:r + tm // 2], wu_s[...], preferred_element_type=jnp.float32)
            acts.append((jax.nn.silu(hg) * hu).astype(jnp.bfloat16))
        packed = jnp.concatenate(
            [_pack_halves(jnp.dot(a, wd_s[...], preferred_element_type=jnp.float32)) for a in acts], axis=0)
        q = packed.shape[1] // 2

        @pl.when(g >= nbuf)
        def _():
            for c in out_copies(g - nbuf, slot):
                c.wait()

        oa_buf[slot] = packed[:, :q]
        ob_buf[slot] = packed[:, q:]
        for c in out_copies(g, slot):
            c.start()
        return jnp.where(slot == nbuf - 1, 0, slot + 1)

    lax.fori_loop(g_lo, g_hi, tile_body, g_lo % nbuf)

    @pl.when(e == pl.num_programs(0) - 1)
    def _():
        for back in range(1, nbuf + 1):
            @pl.when(n_used >= back)
            def _():
                g = n_used - back
                for c in out_copies(g, g % nbuf):
                    c.wait()


def _pos_kernel(slabt_ref, offs_ref, pos0_ref, pos1_ref):
    st = slabt_ref[...]
    t = st.shape[1]
    row = lax.broadcasted_iota(jnp.int32, (N_EXPERTS, t), 0).astype(jnp.float32)
    offs = jnp.tile(offs_ref[...], (1, t // LANES))
    for s, out in enumerate((pos0_ref, pos1_ref)):
        first = jnp.sum(jnp.where(row == st[s:s + 1], offs, 0.0), axis=0, keepdims=True)
        out[...] = (first + st[2 + s:3 + s]).astype(jnp.int32)


def _combine_kernel(x1_ref, o1a_ref, o1b_ref, o2a_ref, o2b_ref, slabt_ref, *rest):
    y_ref = rest[-1]
    st = slabt_ref[...]
    slab = jnp.concatenate([st, jnp.zeros((LANES - st.shape[0], st.shape[1]), jnp.float32)], axis=0).T
    g1 = slab[:, 4:5]
    g2 = slab[:, 5:6]
    q = o1a_ref.shape[1]
    for c, (r1, r2) in enumerate(((o1a_ref, o2a_ref), (o1b_ref, o2b_ref))):
        h1, l1 = _unpack_halves(r1[...])
        h2, l2 = _unpack_halves(r2[...])
        hs = slice(c * q, (c + 1) * q)
        ls = slice(2 * q + c * q, 2 * q + (c + 1) * q)
        y_ref[:, hs] = x1_ref[:, hs] + (g1 * h1 + g2 * h2)
        y_ref[:, ls] = x1_ref[:, ls] + (g1 * l1 + g2 * l2)


def _sc_mesh():
    return plsc.VectorSubcoreMesh(core_axis_name="core", subcore_axis_name="subcore")


def _sc_scatter_rows(parts, pos0, pos1, n_rows):
    n_piece = parts[0].shape[1] // SC_COLS

    def sc_kernel(*refs):
        x_refs = refs[:len(parts)]
        i0_hbm, i1_hbm = refs[len(parts):len(parts) + 2]
        o_refs = refs[len(parts) + 2:]
        tok0 = 0
        for x_hbm, arr in zip(x_refs, parts):
            blk0 = tok0 // SC_WINDOW
            for c in range(n_piece):
                def body(x_vmem, i0_vmem, i1_vmem, o_hbm=o_refs[c]):
                    pltpu.sync_copy(x_vmem, o_hbm.at[i0_vmem.at[0]])
                    pltpu.sync_copy(x_vmem, o_hbm.at[i1_vmem.at[0]])

                pltpu.emit_pipeline(
                    body,
                    grid=(arr.shape[0] // SC_WINDOW,),
                    in_specs=[pl.BlockSpec((SC_WINDOW, SC_COLS), lambda i, c=c: (i, c)),
                              pl.BlockSpec((1, SC_WINDOW), lambda i, blk0=blk0: (0, blk0 + i)),
                              pl.BlockSpec((1, SC_WINDOW), lambda i, blk0=blk0: (0, blk0 + i))],
                    out_specs=[],
                    core_axis_name=("core", "subcore"),
                    dimension_semantics=(pltpu.PARALLEL,),
                )(x_hbm, i0_hbm, i1_hbm)
            tok0 += arr.shape[0]

    piece = jax.ShapeDtypeStruct((n_rows, SC_COLS), jnp.uint32)
    return pl.kernel(sc_kernel, out_type=(piece,) * n_piece, mesh=_sc_mesh(), name="scatter_rows")(
        *parts, pos0, pos1)


def _sc_gather_rows(pieces, pos_list):
    n_tok = pos_list[0].shape[1]

    def sc_kernel(*refs):
        s_refs = refs[:len(pieces)]
        i_refs = refs[len(pieces):len(pieces) + len(pos_list)]
        o_refs = refs[len(pieces) + len(pos_list):]
        k = 0
        for i_hbm in i_refs:
            for s_hbm in s_refs:
                def body(i_vmem, o_vmem, s_hbm=s_hbm):
                    pltpu.sync_copy(s_hbm.at[i_vmem.at[0]], o_vmem)

                pltpu.emit_pipeline(
                    body,
                    grid=(n_tok // SC_WINDOW,),
                    in_specs=[pl.BlockSpec((1, SC_WINDOW), lambda i: (0, i))],
                    out_specs=[pl.BlockSpec((SC_WINDOW, SC_COLS), lambda i: (i, 0))],
                    core_axis_name=("core", "subcore"),
                    dimension_semantics=(pltpu.PARALLEL,),
                )(i_hbm, o_refs[k])
                k += 1

    out = jax.ShapeDtypeStruct((n_tok, SC_COLS), jnp.uint32)
    outs = pl.kernel(sc_kernel, out_type=(out,) * (len(pieces) * len(pos_list)), mesh=_sc_mesh(),
                     name="gather_rows")(*pieces, *pos_list)
    return [outs[j * len(pieces):(j + 1) * len(pieces)] for j in range(len(pos_list))]


def _const_spec(shape, single_buffer=False):
    nd = len(shape)
    mode = pl.Buffered(1) if single_buffer else None
    return pl.BlockSpec(shape, lambda *_: (0,) * nd, pipeline_mode=mode)


def kernel(x_prompt, x_sample, cache_attn_k, cache_attn_v, g_mix, w_in, g_q, g_k, g_v, attn_sinks, w_s, b_s,
           w_out, g_ffn, w_coarse, b_coarse, w_fine, b_fine, w_gate, w_up, w_down):
    nb, seq, d = x_prompt.shape
    ns, slen, _ = x_sample.shape
    n_p = nb * seq
    n_s = ns * slen
    n_tok = n_p + n_s
    tt = TOKEN_TILE
    assert seq % tt == 0 and n_s % tt == 0 and d == D_MODEL
    nt = seq // tt
    bf = jnp.bfloat16
    f32 = jnp.float32

    l = 0
    gmix = g_mix[l].reshape(1, d)
    win = w_in[l].astype(bf)
    gq = (jnp.tile(g_q[l], N_Q_HEADS) * (HEAD_DIM ** -0.5)).reshape(1, Q_WIDTH)
    gk = jnp.tile(g_k[l], N_KV_HEADS).reshape(1, KV_WIDTH)
    gvg = g_v[l].reshape(1, GATE_WIDTH)
    sinks = attn_sinks[l].reshape(N_Q_HEADS).astype(f32)
    ws = w_s[l].astype(bf)
    bs = b_s[l].T
    wout = w_out[l].astype(bf)
    gffn = g_ffn[l].reshape(1, d)
    wr = jnp.concatenate([w_coarse[l], jnp.transpose(w_fine[l], (1, 0, 2)).reshape(d, N_EXPERTS),
                          jnp.zeros((d, ROUTE_LANES - N_GROUPS - N_EXPERTS), f32)], axis=1)
    wr = wr.astype(bf).T
    br = jnp.concatenate([b_coarse[l], b_fine[l].reshape(-1),
                          jnp.zeros((ROUTE_LANES - N_GROUPS - N_EXPERTS,), f32)])
    br = jnp.broadcast_to(br[:, None], (ROUTE_LANES, LANES))
    cnt_shape = (N_EXPERTS, LANES)
    ii = jnp.arange(LANES)
    blk = (ii[:, None] // HEAD_DIM == ii[None, :] // HEAD_DIM).astype(bf)
    u32 = jnp.uint32
    dh = d // 2

    def strict_triu(n):
        r = jnp.arange(n)
        return (r[:, None] < r[None, :]).astype(bf)

    weight_args = (gmix, win, gq, gk, gvg, blk, ws, bs, wout, gffn, wr, br)
    weight_specs = [_const_spec(a.shape, single_buffer=True) for a in weight_args]
    smem_spec = pl.BlockSpec(memory_space=pltpu.SMEM)

    xs2 = x_sample.reshape(n_s, d)
    ck = cache_attn_k[l].reshape(ns * WINDOW, KV_WIDTH)
    cv = cache_attn_v[l].reshape(ns * WINDOW, KV_WIDTH)
    tok_out = lambda n, w, dt: jax.ShapeDtypeStruct((n, w), dt)
    x1_s, h2_s, slabt_s, kn_s, v_s, gvn_s, cnt_s = pl.pallas_call(
        functools.partial(_sample_kernel, n_seq=ns, seq_len=slen),
        grid=(1,),
        in_specs=[smem_spec, _const_spec((n_s, d)), _const_spec(ck.shape), _const_spec(cv.shape)]
                 + weight_specs + [_const_spec((n_s, n_s))],
        out_specs=[_const_spec((n_s, d)), _const_spec((n_s, dh)),
                   _const_spec((ROUTE_ROWS, n_s)),
                   _const_spec((n_s, KV_WIDTH)), _const_spec((n_s, KV_WIDTH)), _const_spec((n_s, GATE_WIDTH)),
                   _const_spec(cnt_shape)],
        out_shape=[tok_out(n_s, d, f32), tok_out(n_s, dh, u32),
                   jax.ShapeDtypeStruct((ROUTE_ROWS, n_s), f32),
                   jax.ShapeDtypeStruct((n_s, KV_WIDTH), f32), jax.ShapeDtypeStruct((n_s, KV_WIDTH), f32),
                   jax.ShapeDtypeStruct((n_s, GATE_WIDTH), f32), jax.ShapeDtypeStruct(cnt_shape, f32)],
        scratch_shapes=[pltpu.VMEM((N_KV_HEADS, ns * WINDOW, LANES), bf), pltpu.VMEM((N_KV_HEADS, ns * WINDOW, LANES), bf),
                        pltpu.VMEM((N_KV_HEADS, n_s, LANES), bf), pltpu.VMEM((N_KV_HEADS, n_s, LANES), bf),
                        pltpu.VMEM((n_s, Q_WIDTH), bf), pltpu.VMEM((n_s, GATE_WIDTH), f32),
                        pltpu.VMEM((n_s, d), bf)],
        compiler_params=pltpu.CompilerParams(dimension_semantics=("arbitrary",), vmem_limit_bytes=VMEM_LIMIT),
        name="mixer_sample",
    )(sinks, xs2, ck, cv, *weight_args, strict_triu(n_s))

    def prompt_mixer(b0, nbw, cnt_in):
        n_w = nbw * seq
        return pl.pallas_call(
            _prompt_kernel,
            grid=(nbw, nt),
            in_specs=[smem_spec, pl.BlockSpec((1, tt, d), lambda b, t: (b0 + b, t, 0))] + weight_specs
                     + [_const_spec((tt // ROW_SPLITS, tt // ROW_SPLITS)), _const_spec(cnt_shape)],
            out_specs=[pl.BlockSpec((tt, d), lambda b, t: (b * nt + t, 0)),
                       pl.BlockSpec((tt, dh), lambda b, t: (b * nt + t, 0)),
                       pl.BlockSpec((ROUTE_ROWS, tt), lambda b, t: (0, b * nt + t)),
                       pl.BlockSpec((1, WINDOW, KV_WIDTH), lambda b, t: (b, 0, 0)),
                       pl.BlockSpec((1, WINDOW, KV_WIDTH), lambda b, t: (b, 0, 0)),
                       _const_spec(cnt_shape)],
            out_shape=[tok_out(n_w, d, f32), tok_out(n_w, dh, u32),
                       jax.ShapeDtypeStruct((ROUTE_ROWS, n_w), f32),
                       jax.ShapeDtypeStruct((nbw, WINDOW, KV_WIDTH), f32),
                       jax.ShapeDtypeStruct((nbw, WINDOW, KV_WIDTH), f32),
                       jax.ShapeDtypeStruct(cnt_shape, f32)],
            scratch_shapes=[pltpu.VMEM((N_KV_HEADS, tt + WINDOW, LANES), bf),
                            pltpu.VMEM((N_KV_HEADS, tt + WINDOW, LANES), bf),
                            pltpu.VMEM((tt, Q_WIDTH), bf), pltpu.VMEM((tt, GATE_WIDTH), f32),
                            pltpu.VMEM((tt, GATE_WIDTH), bf), pltpu.VMEM((tt, d), bf),
                            pltpu.VMEM(cnt_shape, f32)],
            compiler_params=pltpu.CompilerParams(dimension_semantics=("arbitrary", "arbitrary"),
                                                 vmem_limit_bytes=VMEM_LIMIT),
            name="mixer_prompt",
        )(sinks, x_prompt, *weight_args, strict_triu(tt // ROW_SPLITS), cnt_in)

    tm = EXPERT_TILE
    assert dh == 2 * SC_COLS and n_s % SC_WINDOW == 0 and seq % SC_WINDOW == 0

    def expert_pass(h2_parts, slabt_parts, cnt):
        n_w = sum(a.shape[0] for a in h2_parts)
        slabt = jnp.concatenate(slabt_parts, axis=1) if len(slabt_parts) > 1 else slabt_parts[0]
        counts = cnt[:, 0].astype(jnp.int32)
        tiles = (counts + tm - 1) // tm
        tile_end = jnp.cumsum(tiles)
        offs = (tile_end - tiles) * tm
        n_used = tile_end[-1]
        n_rows = ((2 * n_w) // tm + N_EXPERTS) * tm
        lane_tiles = n_w // LANES
        pos_blk = LANES * max(k for k in range(1, POS_BLOCK_TILES + 1) if lane_tiles % k == 0)
        offs_b = jnp.broadcast_to(offs.astype(f32)[:, None], cnt_shape)
        pos_row = jax.ShapeDtypeStruct((1, n_w), jnp.int32)
        pos0, pos1 = pl.pallas_call(
            _pos_kernel,
            grid=(n_w // pos_blk,),
            in_specs=[pl.BlockSpec((ROUTE_ROWS, pos_blk), lambda i: (0, i)), _const_spec(cnt_shape)],
            out_specs=[pl.BlockSpec((1, pos_blk), lambda i: (0, i)), pl.BlockSpec((1, pos_blk), lambda i: (0, i))],
            out_shape=[pos_row, pos_row],
            compiler_params=pltpu.CompilerParams(dimension_semantics=("arbitrary",)),
            name="sorted_pos",
        )(slabt, offs_b)

        xs_a, xs_b = _sc_scatter_rows(h2_parts, pos0, pos1, n_rows)

        w_map = lambda e, *_: (e, 0, 0)
        hbm = pl.BlockSpec(memory_space=pl.ANY)
        piece = jax.ShapeDtypeStruct((n_rows, SC_COLS), u32)
        tile_buf = pltpu.VMEM((EXPERT_BUFFERS, tm, SC_COLS), u32)
        tile_sems = pltpu.SemaphoreType.DMA((2, EXPERT_BUFFERS))
        out_a, out_b = pl.pallas_call(
            _expert_kernel,
            grid_spec=pltpu.PrefetchScalarGridSpec(
                num_scalar_prefetch=4,
                grid=(N_EXPERTS,),
                in_specs=[hbm, hbm,
                          pl.BlockSpec((1, d, D_EXPERT), w_map),
                          pl.BlockSpec((1, d, D_EXPERT), w_map),
                          pl.BlockSpec((1, D_EXPERT, d), w_map)],
                out_specs=[hbm, hbm],
                scratch_shapes=[pltpu.VMEM((d, D_EXPERT), bf), pltpu.VMEM((d, D_EXPERT), bf),
                                pltpu.VMEM((D_EXPERT, d), bf),
                                tile_buf, tile_buf, tile_buf, tile_buf, tile_sems, tile_sems]),
            out_shape=[piece, piece],
            compiler_params=pltpu.CompilerParams(dimension_semantics=("arbitrary",), vmem_limit_bytes=VMEM_LIMIT),
            name="experts",
        )((tile_end - tiles).astype(jnp.int32), tile_end.astype(jnp.int32), counts,
          n_used.reshape(1).astype(jnp.int32), xs_a, xs_b, w_gate[l], w_up[l], w_down[l])

        return _sc_gather_rows([out_a, out_b], [pos0, pos1])

    def combine(x1, slab_t, gathered, g_blk0, y_rows, y_blk0, y_prev):
        (o1a, o1b), (o2a, o2b) = gathered
        tok = lambda i: (i, 0)
        args = [x1, o1a, o1b, o2a, o2b, slab_t]
        in_specs = ([pl.BlockSpec((tt, d), tok)] + [pl.BlockSpec((tt, SC_COLS), lambda i: (g_blk0 + i, 0))] * 4
                    + [pl.BlockSpec((ROUTE_ROWS, tt), lambda i: (0, i))])
        aliases = {}
        if y_prev is not None:
            aliases = {len(args): 0}
            args.append(y_prev)
            in_specs.append(pl.BlockSpec(memory_space=pl.ANY))
        return pl.pallas_call(
            _combine_kernel,
            grid=(x1.shape[0] // tt,),
            in_specs=in_specs,
            out_specs=pl.BlockSpec((tt, d), lambda i: (y_blk0 + i, 0)),
            out_shape=jax.ShapeDtypeStruct((y_rows, d), f32),
            input_output_aliases=aliases,
            compiler_params=pltpu.CompilerParams(dimension_semantics=("arbitrary",)),
            name="combine",
        )(*args)

    unit = nb // sum(PROMPT_WAVES)
    assert unit * sum(PROMPT_WAVES) == nb
    wave_seqs = [unit * r for r in PROMPT_WAVES]
    wave_b0 = [sum(wave_seqs[:w]) for w in range(len(wave_seqs))]
    y_p = None
    y_s = None
    kwins, vwins = [], []
    mixed = []
    for w, (b0, nbw) in enumerate(zip(wave_b0, wave_seqs)):
        cnt_in = cnt_s if w == 0 else jnp.zeros(cnt_shape, f32)
        mixed.append(prompt_mixer(b0, nbw, cnt_in))
    for w, (x1_w, h2_w, slabt_w, kwin_w, vwin_w, cnt_w) in enumerate(mixed):
        kwins.append(kwin_w)
        vwins.append(vwin_w)
        h2_parts, slabt_parts = [h2_w], [slabt_w]
        if w == 0:
            h2_parts.append(h2_s)
            slabt_parts.append(slabt_s)
        gathered = expert_pass(h2_parts, slabt_parts, cnt_w)
        y_p = combine(x1_w, slabt_w, gathered, 0, n_p, wave_b0[w] * nt, y_p)
        if w == 0:
            y_s = combine(x1_s, slabt_s, gathered, wave_seqs[0] * nt, n_s, 0, None)
    y_p = y_p.reshape(nb, seq, d)
    y_s = y_s.reshape(ns, slen, d)
    kwin = jnp.concatenate(kwins, axis=0) if len(kwins) > 1 else kwins[0]
    vwin = jnp.concatenate(vwins, axis=0) if len(vwins) > 1 else vwins[0]

    kv_shape = (1, nb, WINDOW, N_KV_HEADS, HEAD_DIM)
    new_k_p = kwin.reshape(kv_shape)
    new_v_p = vwin.reshape(kv_shape)
    keep = WINDOW - slen
    ck4 = cache_attn_k[l][:, WINDOW - keep:]
    cv4 = cache_attn_v[l][:, WINDOW - keep:]
    new_k_s = jnp.concatenate([ck4, kn_s.reshape(ns, slen, N_KV_HEADS, HEAD_DIM)], axis=1)[None]
    new_v_s = jnp.concatenate([cv4, v_s.reshape(ns, slen, N_KV_HEADS, HEAD_DIM)], axis=1)[None]
    new_gv_s = gvn_s.reshape(1, ns, slen, GATE_HEADS, GATE_DIM)
    return (y_p, y_s, new_k_p, new_v_p, new_k_s, new_v_s, new_gv_s)
```

```python
import functools

import jax
import jax.numpy as jnp
from jax import lax
from jax.experimental import pallas as pl
from jax.experimental.pallas import tpu as pltpu
from jax.experimental.pallas import tpu_sc as plsc

D_MODEL = 1024
HEAD_DIM = 64
N_Q_HEADS = 8
N_KV_HEADS = 2
Q_WIDTH = N_Q_HEADS * HEAD_DIM
KV_WIDTH = N_KV_HEADS * HEAD_DIM
GATE_HEADS = 4
GATE_DIM = 128
GATE_WIDTH = GATE_HEADS * GATE_DIM
PROJ_COLS = Q_WIDTH + 2 * KV_WIDTH + 2 * GATE_WIDTH
CHUNK = 64
WINDOW = 128
MLP_CHUNK = 128
N_GROUPS = 8
EXP_PER_GROUP = 8
N_EXPERTS = N_GROUPS * EXP_PER_GROUP
D_EXPERT = 512
EPS = 1e-6

LANES = 128
SUBLANES = 8
ROUTE_LANES = 128
FINE_LANE0 = N_GROUPS
ROUTE_ROWS = SUBLANES
NEG = -1e30
assert N_GROUPS == SUBLANES and EXP_PER_GROUP == SUBLANES

PROMPT_WAVES = (3, 1)
TOKEN_TILE = 512
ROW_SPLITS = 2
EXPERT_TILE = 512
EXPERT_BUFFERS = 3
POS_BLOCK_TILES = 64
SC_WINDOW = 128
SC_COLS = 256
VMEM_LIMIT = 40 * 1024 * 1024


def _pack_halves(x):
    w = x.shape[1] // 2
    b = lax.bitcast_convert_type(x.astype(jnp.bfloat16).astype(jnp.float32), jnp.uint32)
    return (b[:, :w] & jnp.uint32(0xFFFF0000)) | (b[:, w:] >> 16)


def _unpack_halves(p):
    hi = lax.bitcast_convert_type(p & jnp.uint32(0xFFFF0000), jnp.float32)
    lo = lax.bitcast_convert_type(p << 16, jnp.float32)
    return hi, lo


def _rms(x, eps=EPS):
    return x * lax.rsqrt(jnp.mean(x * x, axis=-1, keepdims=True) + eps)


def _in_proj(x, gmix, win):
    h = (_rms(x) * gmix).astype(jnp.bfloat16)
    return jnp.dot(h, win, preferred_element_type=jnp.float32)


def _heads(z, gq, gk, gvg, blk):
    qk = z[:, :Q_WIDTH + KV_WIDTH]
    outs = []
    for j in range((Q_WIDTH + KV_WIDTH) // LANES):
        zj = qk[:, j * LANES:(j + 1) * LANES]
        ss = jnp.dot((zj * zj).astype(jnp.bfloat16), blk, preferred_element_type=jnp.float32)
        outs.append(zj * lax.rsqrt(ss * (1.0 / HEAD_DIM) + EPS))
    qn = jnp.concatenate(outs[:Q_WIDTH // LANES], axis=-1) * gq
    kn = outs[-1] * gk
    v = z[:, Q_WIDTH + KV_WIDTH:Q_WIDTH + 2 * KV_WIDTH]
    u0 = Q_WIDTH + 2 * KV_WIDTH
    ua = jax.nn.gelu(z[:, u0:u0 + GATE_WIDTH])
    ga = jax.nn.gelu(z[:, u0 + GATE_WIDTH:])
    gvn = jnp.concatenate(
        [_rms(ga[:, i * GATE_DIM:(i + 1) * GATE_DIM]) for i in range(GATE_HEADS)], axis=-1) * gvg
    return qn.astype(jnp.bfloat16), kn, v, ua, gvn


def _dup_halves(a):
    lo = lax.broadcasted_iota(jnp.int32, a.shape, 1) < HEAD_DIM
    r = pltpu.roll(a, HEAD_DIM, axis=1)
    return (jnp.where(lo, a, r).astype(jnp.bfloat16), jnp.where(lo, r, a).astype(jnp.bfloat16))


def _attn_block(qa, qb, k2, v2, bias, sk):
    r = qa.shape[0]
    lo = lax.broadcasted_iota(jnp.int32, (r, LANES), 1) < HEAD_DIM
    zero = jnp.zeros_like(qa)
    qs = jnp.concatenate([jnp.where(lo, qa, zero), jnp.where(lo, zero, qa),
                          jnp.where(lo, qb, zero), jnp.where(lo, zero, qb)], axis=0)
    s = lax.dot_general(qs, k2, (((1,), (1,)), ((), ())), preferred_element_type=jnp.float32)
    if bias is not None:
        s = s + bias
    m = jnp.maximum(jnp.max(s, axis=-1, keepdims=True), sk)
    e = jnp.exp(s - m)
    den = jnp.sum(e, axis=-1, keepdims=True) + jnp.exp(sk - m)
    o = jnp.dot(e.astype(jnp.bfloat16), v2, preferred_element_type=jnp.float32) * (1.0 / den)
    pa = jnp.where(lo, o[0:r], o[r:2 * r])
    pb = jnp.where(lo, o[2 * r:3 * r], o[3 * r:4 * r])
    return pa, pb


def _sink_rows(sink_ref, g, r):
    row = lax.broadcasted_iota(jnp.int32, (4 * r, 1), 0)
    s0, s1, s2, s3 = (sink_ref[4 * g + i] for i in range(4))
    return jnp.where(row < r, s0, jnp.where(row < 2 * r, s1, jnp.where(row < 3 * r, s2, s3)))


def _causal_ws(ws_ref, h, n):
    w = ws_ref[h][:n, :n]
    keep = (lax.broadcasted_iota(jnp.int32, (n, n), 0) >= lax.broadcasted_iota(jnp.int32, (n, n), 1))
    return jnp.where(keep, w, jnp.zeros_like(w))


def _out_proj(cat, x, wout):
    return x + jnp.dot(cat, wout, preferred_element_type=jnp.float32)


def _route(x1, gffn, wrt, brt, triu, cnt):
    h2 = _rms(x1) * gffn
    t = x1.shape[0]
    reps = t // LANES
    lgt = lax.dot_general(wrt, h2.astype(jnp.bfloat16), (((1,), (1,)), ((), ())),
                          preferred_element_type=jnp.float32) + jnp.tile(brt, (1, reps))
    ng = float(EXP_PER_GROUP)
    sub = lax.broadcasted_iota(jnp.int32, (EXP_PER_GROUP, t), 0).astype(jnp.float32)
    c = lgt[0:N_GROUPS]
    mc = jnp.max(c, axis=0, keepdims=True)
    grp = jnp.min(jnp.where(c == mc, sub, ng), axis=0, keepdims=True)
    pg = 1.0 / jnp.sum(jnp.exp(c - mc), axis=0, keepdims=True)
    lf = lgt[FINE_LANE0:FINE_LANE0 + EXP_PER_GROUP]
    for g in range(1, N_GROUPS):
        r0 = FINE_LANE0 + g * EXP_PER_GROUP
        lf = jnp.where(grp == float(g), lgt[r0:r0 + EXP_PER_GROUP], lf)
    v1 = jnp.max(lf, axis=0, keepdims=True)
    i1 = jnp.min(jnp.where(lf == v1, sub, ng), axis=0, keepdims=True)
    lf2 = jnp.where(sub == i1, -jnp.inf, lf)
    v2 = jnp.max(lf2, axis=0, keepdims=True)
    i2 = jnp.min(jnp.where(lf2 == v2, sub, ng), axis=0, keepdims=True)
    tt = jnp.exp(v2 - v1)
    w1 = 1.0 / (1.0 + tt)
    w2 = tt * w1
    e1 = grp * ng + i1
    e2 = grp * ng + i2
    row = lax.broadcasted_iota(jnp.int32, (N_EXPERTS, t), 0).astype(jnp.float32)
    sel1 = row == e1
    sel2 = row == e2
    oh = jnp.where(sel1 | sel2, 1.0, 0.0)
    cum = jnp.dot(oh.astype(jnp.bfloat16), triu, preferred_element_type=jnp.float32) + jnp.tile(cnt, (1, reps))
    r1 = jnp.sum(jnp.where(sel1, cum, 0.0), axis=0, keepdims=True)
    r2 = jnp.sum(jnp.where(sel2, cum, 0.0), axis=0, keepdims=True)
    new_cnt = cnt + jnp.sum(oh, axis=1, keepdims=True)
    slab_t = jnp.where(sub == 0.0, e1,
             jnp.where(sub == 1.0, e2,
             jnp.where(sub == 2.0, r1,
             jnp.where(sub == 3.0, r2,
             jnp.where(sub == 4.0, pg * w1,
             jnp.where(sub == 5.0, pg * w2, 0.0))))))
    return _pack_halves(h2), slab_t, new_cnt


def _prompt_kernel(sink_ref, x_ref, gmix_ref, win_ref, gq_ref, gk_ref, gvg_ref, blk_ref, ws_ref, bs_ref,
                   wout_ref, gffn_ref, wr_ref, br_ref, triu_ref, cnt_in_ref,
                   x1_ref, h2_ref, slabt_ref, kwin_ref, vwin_ref, cnt_ref,
                   k2_scr, v2_scr, qn_scr, ua_scr, gvn_scr, cat_scr, cnt_scr):
    b = pl.program_id(0)
    t = pl.program_id(1)
    nt = pl.num_programs(1)
    tt = x_ref.shape[1]
    nblk = tt // WINDOW
    sub = tt // ROW_SPLITS

    @pl.when((b == 0) & (t == 0))
    def _():
        cnt_scr[...] = cnt_in_ref[...]

    @pl.when(t == 0)
    def _():
        k2_scr[:, 0:WINDOW, :] = jnp.zeros((N_KV_HEADS, WINDOW, LANES), jnp.bfloat16)
        v2_scr[:, 0:WINDOW, :] = jnp.zeros((N_KV_HEADS, WINDOW, LANES), jnp.bfloat16)

    pieces = [slice(s * sub, (s + 1) * sub) for s in range(ROW_SPLITS)]
    for s, rs in enumerate(pieces):
        ks = slice(WINDOW + s * sub, WINDOW + (s + 1) * sub)
        z = _in_proj(x_ref[0, rs, :], gmix_ref[...], win_ref[...])
        qn, kn, v, ua, gvn = _heads(z, gq_ref[...], gk_ref[...], gvg_ref[...], blk_ref[...])
        qn_scr[rs, :] = qn
        ua_scr[rs, :] = ua
        gvn_scr[rs, :] = gvn.astype(jnp.bfloat16)
        k0, k1 = _dup_halves(kn)
        v0, v1 = _dup_halves(v)
        k2_scr[0, ks, :] = k0
        k2_scr[1, ks, :] = k1
        v2_scr[0, ks, :] = v0
        v2_scr[1, ks, :] = v1
        if s == ROW_SPLITS - 1:
            @pl.when(t == nt - 1)
            def _():
                kwin_ref[0] = kn[sub - WINDOW:]
                vwin_ref[0] = v[sub - WINDOW:]

    rows = 4 * WINDOW
    band = 2 * WINDOW
    rr = lax.broadcasted_iota(jnp.int32, (rows, band), 0)
    kk = lax.broadcasted_iota(jnp.int32, (rows, band), 1)
    half = ((rr % WINDOW) >= CHUNK).astype(jnp.int32)
    allowed = (kk >= half * CHUNK) & (kk < (3 + half) * CHUNK)
    bias_mid = jnp.where(allowed, 0.0, NEG)
    bias_first = jnp.where(allowed & (kk >= WINDOW), 0.0, NEG)

    def attn_body(j, carry):
        r0 = pl.multiple_of(j * WINDOW, WINDOW)
        bias = jnp.where((t == 0) & (j == 0), bias_first, bias_mid)
        for g in range(N_KV_HEADS):
            c0 = g * 2 * LANES
            qa = qn_scr[pl.ds(r0, WINDOW), c0:c0 + LANES]
            qb = qn_scr[pl.ds(r0, WINDOW), c0 + LANES:c0 + 2 * LANES]
            k2 = k2_scr[g, pl.ds(r0, band), :]
            v2 = v2_scr[g, pl.ds(r0, band), :]
            pa, pb = _attn_block(qa, qb, k2, v2, bias, _sink_rows(sink_ref, g, WINDOW))
            cat_scr[pl.ds(r0, WINDOW), c0:c0 + LANES] = pa.astype(jnp.bfloat16)
            cat_scr[pl.ds(r0, WINDOW), c0 + LANES:c0 + 2 * LANES] = pb.astype(jnp.bfloat16)
        return carry

    lax.fori_loop(0, nblk, attn_body, 0, unroll=True)

    k2_scr[:, 0:WINDOW, :] = k2_scr[:, tt:tt + WINDOW, :]
    v2_scr[:, 0:WINDOW, :] = v2_scr[:, tt:tt + WINDOW, :]

    for h in range(GATE_HEADS):
        w = _causal_ws(ws_ref, h, MLP_CHUNK)
        bcol = bs_ref[:, h:h + 1]
        for c in range(tt // MLP_CHUNK):
            rs = slice(c * MLP_CHUNK, (c + 1) * MLP_CHUNK)
            cs = slice(h * GATE_DIM, (h + 1) * GATE_DIM)
            mix = jnp.dot(w, gvn_scr[rs, cs], preferred_element_type=jnp.float32) + bcol
            cat_scr[rs, Q_WIDTH + h * GATE_DIM:Q_WIDTH + (h + 1) * GATE_DIM] = (ua_scr[rs, cs] * mix).astype(jnp.bfloat16)

    cnt = cnt_scr[...]
    x1_next = _out_proj(cat_scr[pieces[0], :], x_ref[0, pieces[0], :], wout_ref[...])
    for s, rs in enumerate(pieces):
        x1 = x1_next
        if s + 1 < ROW_SPLITS:
            x1_next = _out_proj(cat_scr[pieces[s + 1], :], x_ref[0, pieces[s + 1], :], wout_ref[...])
        x1_ref[rs, :] = x1
        hp, slab_t, cnt = _route(x1, gffn_ref[...], wr_ref[...], br_ref[...], triu_ref[...], cnt)
        h2_ref[rs, :] = hp
        slabt_ref[:, rs] = slab_t
    cnt_scr[...] = cnt
    cnt_ref[...] = cnt


def _sample_kernel(sink_ref, x_ref, ck_ref, cv_ref, gmix_ref, win_ref, gq_ref, gk_ref, gvg_ref, blk_ref, ws_ref,
                   bs_ref, wout_ref, gffn_ref, wr_ref, br_ref, triu_ref,
                   x1_ref, h2_ref, slabt_ref, kn_ref, v_ref, gvn_ref, cnt_ref,
                   ck2_scr, cv2_scr, k2_scr, v2_scr, qn_scr, ua_scr, cat_scr, *, n_seq, seq_len):
    x = x_ref[...]
    qn, kn, v, ua, gvn = _heads(_in_proj(x, gmix_ref[...], win_ref[...]), gq_ref[...], gk_ref[...], gvg_ref[...],
                                blk_ref[...])
    kn_ref[...] = kn
    v_ref[...] = v
    gvn_ref[...] = gvn
    qn_scr[...] = qn
    ua_scr[...] = ua
    for scr, val in ((k2_scr, kn), (v2_scr, v), (ck2_scr, ck_ref[...]), (cv2_scr, cv_ref[...])):
        a0, a1 = _dup_halves(val)
        scr[0] = a0
        scr[1] = a1

    ws = [_causal_ws(ws_ref, h, seq_len) for h in range(GATE_HEADS)]

    def seq_body(i, carry):
        r0 = pl.multiple_of(i * seq_len, seq_len)
        c0r = pl.multiple_of(i * WINDOW, WINDOW)
        for g in range(N_KV_HEADS):
            c0 = g * 2 * LANES
            qa = qn_scr[pl.ds(r0, seq_len), c0:c0 + LANES]
            qb = qn_scr[pl.ds(r0, seq_len), c0 + LANES:c0 + 2 * LANES]
            k2 = jnp.concatenate([ck2_scr[g, pl.ds(c0r, WINDOW), :], k2_scr[g, pl.ds(r0, seq_len), :]], axis=0)
            v2 = jnp.concatenate([cv2_scr[g, pl.ds(c0r, WINDOW), :], v2_scr[g, pl.ds(r0, seq_len), :]], axis=0)
            pa, pb = _attn_block(qa, qb, k2, v2, None, _sink_rows(sink_ref, g, seq_len))
            cat_scr[pl.ds(r0, seq_len), c0:c0 + LANES] = pa.astype(jnp.bfloat16)
            cat_scr[pl.ds(r0, seq_len), c0 + LANES:c0 + 2 * LANES] = pb.astype(jnp.bfloat16)
        for h in range(GATE_HEADS):
            cs = slice(h * GATE_DIM, (h + 1) * GATE_DIM)
            gv_h = gvn_ref[pl.ds(r0, seq_len), cs].astype(jnp.bfloat16)
            mix = jnp.dot(ws[h], gv_h, preferred_element_type=jnp.float32) + bs_ref[0:seq_len, h:h + 1]
            cat_scr[pl.ds(r0, seq_len), Q_WIDTH + h * GATE_DIM:Q_WIDTH + (h + 1) * GATE_DIM] = (
                ua_scr[pl.ds(r0, seq_len), cs] * mix).astype(jnp.bfloat16)
        return carry

    lax.fori_loop(0, n_seq, seq_body, 0)

    cnt0 = jnp.zeros((N_EXPERTS, LANES), jnp.float32)
    x1 = _out_proj(cat_scr[...], x, wout_ref[...])
    hp, slab_t, new_cnt = _route(x1, gffn_ref[...], wr_ref[...], br_ref[...], triu_ref[...], cnt0)
    x1_ref[...] = x1
    h2_ref[...] = hp
    slabt_ref[...] = slab_t
    cnt_ref[...] = new_cnt


def _expert_kernel(ts_ref, te_ref, cnt_ref, nu_ref, xa_hbm, xb_hbm, wg_ref, wu_ref, wd_ref, oa_hbm, ob_hbm,
                   wg_s, wu_s, wd_s, xa_buf, xb_buf, oa_buf, ob_buf, in_sem, out_sem):
    e = pl.program_id(0)
    nbuf, tm, _ = xa_buf.shape
    n_used = nu_ref[0]

    def rows_of(g):
        return pl.ds(pl.multiple_of(g * tm, tm), tm)

    def in_copies(g, slot):
        return (pltpu.make_async_copy(xa_hbm.at[rows_of(g)], xa_buf.at[slot], in_sem.at[0, slot]),
                pltpu.make_async_copy(xb_hbm.at[rows_of(g)], xb_buf.at[slot], in_sem.at[1, slot]))

    def out_copies(g, slot):
        return (pltpu.make_async_copy(oa_buf.at[slot], oa_hbm.at[rows_of(g)], out_sem.at[0, slot]),
                pltpu.make_async_copy(ob_buf.at[slot], ob_hbm.at[rows_of(g)], out_sem.at[1, slot]))

    @pl.when(e == 0)
    def _():
        for g in range(nbuf - 1):
            @pl.when(g < n_used)
            def _():
                for c in in_copies(g, g):
                    c.start()

    g_lo = ts_ref[e]
    g_hi = te_ref[e]

    @pl.when(g_hi > g_lo)
    def _():
        wg_s[...] = wg_ref[0].astype(jnp.bfloat16)
        wu_s[...] = wu_ref[0].astype(jnp.bfloat16)
        wd_s[...] = wd_ref[0].astype(jnp.bfloat16)

    def tile_body(g, slot):
        for c in in_copies(g, slot):
            c.wait()
        ahead = g + (nbuf - 1)
        ahead_slot = jnp.where(slot == 0, nbuf - 1, slot - 1)

        @pl.when(ahead < n_used)
        def _():
            for c in in_copies(ahead, ahead_slot):
                c.start()

        live = lax.broadcasted_iota(jnp.int32, (tm, xa_buf.shape[2]), 0) < cnt_ref[e] - (g - g_lo) * tm
        ha, la = _unpack_halves(jnp.where(live, xa_buf[slot], jnp.uint32(0)))
        hb, lb = _unpack_halves(jnp.where(live, xb_buf[slot], jnp.uint32(0)))
        xs = jnp.concatenate([ha, hb, la, lb], axis=1).astype(jnp.bfloat16)
        acts = []
        for r in range(0, tm, tm // 2):
            hg = jnp.dot(xs[r:r + tm // 2], wg_s[...], preferred_element_type=jnp.float32)
            hu = jnp.dot(xs[r:r + tm // 2], wu_s[...], preferred_element_type=jnp.float32)
            acts.append((jax.nn.silu(hg) * hu).astype(jnp.bfloat16))
        packed = jnp.concatenate(
            [_pack_halves(jnp.dot(a, wd_s[...], preferred_element_type=jnp.float32)) for a in acts], axis=0)
        q = packed.shape[1] // 2

        @pl.when(g >= nbuf)
        def _():
            for c in out_copies(g - nbuf, slot):
                c.wait()

        oa_buf[slot] = packed[:, :q]
        ob_buf[slot] = packed[:, q:]
        for c in out_copies(g, slot):
            c.start()
        return jnp.where(slot == nbuf - 1, 0, slot + 1)

    lax.fori_loop(g_lo, g_hi, tile_body, g_lo % nbuf)

    @pl.when(e == pl.num_programs(0) - 1)
    def _():
        for back in range(1, nbuf + 1):
            @pl.when(n_used >= back)
            def _():
                g = n_used - back
                for c in out_copies(g, g % nbuf):
                    c.wait()


def _pos_kernel(slabt_ref, offs_ref, pos0_ref, pos1_ref):
    st = slabt_ref[...]
    t = st.shape[1]
    row = lax.broadcasted_iota(jnp.int32, (N_EXPERTS, t), 0).astype(jnp.float32)
    offs = jnp.tile(offs_ref[...], (1, t // LANES))
    for s, out in enumerate((pos0_ref, pos1_ref)):
        first = jnp.sum(jnp.where(row == st[s:s + 1], offs, 0.0), axis=0, keepdims=True)
        out[...] = (first + st[2 + s:3 + s]).astype(jnp.int32)


def _combine_kernel(x1_ref, o1a_ref, o1b_ref, o2a_ref, o2b_ref, slabt_ref, *rest):
    y_ref = rest[-1]
    st = slabt_ref[...]
    slab = jnp.concatenate([st, jnp.zeros((LANES - st.shape[0], st.shape[1]), jnp.float32)], axis=0).T
    g1 = slab[:, 4:5]
    g2 = slab[:, 5:6]
    q = o1a_ref.shape[1]
    for c, (r1, r2) in enumerate(((o1a_ref, o2a_ref), (o1b_ref, o2b_ref))):
        h1, l1 = _unpack_halves(r1[...])
        h2, l2 = _unpack_halves(r2[...])
        hs = slice(c * q, (c + 1) * q)
        ls = slice(2 * q + c * q, 2 * q + (c + 1) * q)
        y_ref[:, hs] = x1_ref[:, hs] + (g1 * h1 + g2 * h2)
        y_ref[:, ls] = x1_ref[:, ls] + (g1 * l1 + g2 * l2)


def _sc_mesh():
    return plsc.VectorSubcoreMesh(core_axis_name="core", subcore_axis_name="subcore")


def _sc_scatter_rows(parts, pos0, pos1, n_rows):
    n_piece = parts[0].shape[1] // SC_COLS

    def sc_kernel(*refs):
        x_refs = refs[:len(parts)]
        i0_hbm, i1_hbm = refs[len(parts):len(parts) + 2]
        o_refs = refs[len(parts) + 2:]
        tok0 = 0
        for x_hbm, arr in zip(x_refs, parts):
            blk0 = tok0 // SC_WINDOW
            for c in range(n_piece):
                def body(x_vmem, i0_vmem, i1_vmem, o_hbm=o_refs[c]):
                    pltpu.sync_copy(x_vmem, o_hbm.at[i0_vmem.at[0]])
                    pltpu.sync_copy(x_vmem, o_hbm.at[i1_vmem.at[0]])

                pltpu.emit_pipeline(
                    body,
                    grid=(arr.shape[0] // SC_WINDOW,),
                    in_specs=[pl.BlockSpec((SC_WINDOW, SC_COLS), lambda i, c=c: (i, c)),
                              pl.BlockSpec((1, SC_WINDOW), lambda i, blk0=blk0: (0, blk0 + i)),
                              pl.BlockSpec((1, SC_WINDOW), lambda i, blk0=blk0: (0, blk0 + i))],
                    out_specs=[],
                    core_axis_name=("core", "subcore"),
                    dimension_semantics=(pltpu.PARALLEL,),
                )(x_hbm, i0_hbm, i1_hbm)
            tok0 += arr.shape[0]

    piece = jax.ShapeDtypeStruct((n_rows, SC_COLS), jnp.uint32)
    return pl.kernel(sc_kernel, out_type=(piece,) * n_piece, mesh=_sc_mesh(), name="scatter_rows")(
        *parts, pos0, pos1)


def _sc_gather_rows(pieces, pos_list):
    n_tok = pos_list[0].shape[1]

    def sc_kernel(*refs):
        s_refs = refs[:len(pieces)]
        i_refs = refs[len(pieces):len(pieces) + len(pos_list)]
        o_refs = refs[len(pieces) + len(pos_list):]
        k = 0
        for i_hbm in i_refs:
            for s_hbm in s_refs:
                def body(i_vmem, o_vmem, s_hbm=s_hbm):
                    pltpu.sync_copy(s_hbm.at[i_vmem.at[0]], o_vmem)

                pltpu.emit_pipeline(
                    body,
                    grid=(n_tok // SC_WINDOW,),
                    in_specs=[pl.BlockSpec((1, SC_WINDOW), lambda i: (0, i))],
                    out_specs=[pl.BlockSpec((SC_WINDOW, SC_COLS), lambda i: (i, 0))],
                    core_axis_name=("core", "subcore"),
                    dimension_semantics=(pltpu.PARALLEL,),
                )(i_hbm, o_refs[k])
                k += 1

    out = jax.ShapeDtypeStruct((n_tok, SC_COLS), jnp.uint32)
    outs = pl.kernel(sc_kernel, out_type=(out,) * (len(pieces) * len(pos_list)), mesh=_sc_mesh(),
                     name="gather_rows")(*pieces, *pos_list)
    return [outs[j * len(pieces):(j + 1) * len(pieces)] for j in range(len(pos_list))]


def _const_spec(shape, single_buffer=False):
    nd = len(shape)
    mode = pl.Buffered(1) if single_buffer else None
    return pl.BlockSpec(shape, lambda *_: (0,) * nd, pipeline_mode=mode)


def kernel(x_prompt, x_sample, cache_attn_k, cache_attn_v, g_mix, w_in, g_q, g_k, g_v, attn_sinks, w_s, b_s,
           w_out, g_ffn, w_coarse, b_coarse, w_fine, b_fine, w_gate, w_up, w_down):
    nb, seq, d = x_prompt.shape
    ns, slen, _ = x_sample.shape
    n_p = nb * seq
    n_s = ns * slen
    n_tok = n_p + n_s
    tt = TOKEN_TILE
    assert seq % tt == 0 and n_s % tt == 0 and d == D_MODEL
    nt = seq // tt
    bf = jnp.bfloat16
    f32 = jnp.float32

    l = 0
    gmix = g_mix[l].reshape(1, d)
    win = w_in[l].astype(bf)
    gq = (jnp.tile(g_q[l], N_Q_HEADS) * (HEAD_DIM ** -0.5)).reshape(1, Q_WIDTH)
    gk = jnp.tile(g_k[l], N_KV_HEADS).reshape(1, KV_WIDTH)
    gvg = g_v[l].reshape(1, GATE_WIDTH)
    sinks = attn_sinks[l].reshape(N_Q_HEADS).astype(f32)
    ws = w_s[l].astype(bf)
    bs = b_s[l].T
    wout = w_out[l].astype(bf)
    gffn = g_ffn[l].reshape(1, d)
    wr = jnp.concatenate([w_coarse[l], jnp.transpose(w_fine[l], (1, 0, 2)).reshape(d, N_EXPERTS),
                          jnp.zeros((d, ROUTE_LANES - N_GROUPS - N_EXPERTS), f32)], axis=1)
    wr = wr.astype(bf).T
    br = jnp.concatenate([b_coarse[l], b_fine[l].reshape(-1),
                          jnp.zeros((ROUTE_LANES - N_GROUPS - N_EXPERTS,), f32)])
    br = jnp.broadcast_to(br[:, None], (ROUTE_LANES, LANES))
    cnt_shape = (N_EXPERTS, LANES)
    ii = jnp.arange(LANES)
    blk = (ii[:, None] // HEAD_DIM == ii[None, :] // HEAD_DIM).astype(bf)
    u32 = jnp.uint32
    dh = d // 2

    def strict_triu(n):
        r = jnp.arange(n)
        return (r[:, None] < r[None, :]).astype(bf)

    weight_args = (gmix, win, gq, gk, gvg, blk, ws, bs, wout, gffn, wr, br)
    weight_specs = [_const_spec(a.shape, single_buffer=True) for a in weight_args]
    smem_spec = pl.BlockSpec(memory_space=pltpu.SMEM)

    xs2 = x_sample.reshape(n_s, d)
    ck = cache_attn_k[l].reshape(ns * WINDOW, KV_WIDTH)
    cv = cache_attn_v[l].reshape(ns * WINDOW, KV_WIDTH)
    tok_out = lambda n, w, dt: jax.ShapeDtypeStruct((n, w), dt)
    x1_s, h2_s, slabt_s, kn_s, v_s, gvn_s, cnt_s = pl.pallas_call(
        functools.partial(_sample_kernel, n_seq=ns, seq_len=slen),
        grid=(1,),
        in_specs=[smem_spec, _const_spec((n_s, d)), _const_spec(ck.shape), _const_spec(cv.shape)]
                 + weight_specs + [_const_spec((n_s, n_s))],
        out_specs=[_const_spec((n_s, d)), _const_spec((n_s, dh)),
                   _const_spec((ROUTE_ROWS, n_s)),
                   _const_spec((n_s, KV_WIDTH)), _const_spec((n_s, KV_WIDTH)), _const_spec((n_s, GATE_WIDTH)),
                   _const_spec(cnt_shape)],
        out_shape=[tok_out(n_s, d, f32), tok_out(n_s, dh, u32),
                   jax.ShapeDtypeStruct((ROUTE_ROWS, n_s), f32),
                   jax.ShapeDtypeStruct((n_s, KV_WIDTH), f32), jax.ShapeDtypeStruct((n_s, KV_WIDTH), f32),
                   jax.ShapeDtypeStruct((n_s, GATE_WIDTH), f32), jax.ShapeDtypeStruct(cnt_shape, f32)],
        scratch_shapes=[pltpu.VMEM((N_KV_HEADS, ns * WINDOW, LANES), bf), pltpu.VMEM((N_KV_HEADS, ns * WINDOW, LANES), bf),
                        pltpu.VMEM((N_KV_HEADS, n_s, LANES), bf), pltpu.VMEM((N_KV_HEADS, n_s, LANES), bf),
                        pltpu.VMEM((n_s, Q_WIDTH), bf), pltpu.VMEM((n_s, GATE_WIDTH), f32),
                        pltpu.VMEM((n_s, d), bf)],
        compiler_params=pltpu.CompilerParams(dimension_semantics=("arbitrary",), vmem_limit_bytes=VMEM_LIMIT),
        name="mixer_sample",
    )(sinks, xs2, ck, cv, *weight_args, strict_triu(n_s))

    def prompt_mixer(b0, nbw, cnt_in):
        n_w = nbw * seq
        return pl.pallas_call(
            _prompt_kernel,
            grid=(nbw, nt),
            in_specs=[smem_spec, pl.BlockSpec((1, tt, d), lambda b, t: (b0 + b, t, 0))] + weight_specs
                     + [_const_spec((tt // ROW_SPLITS, tt // ROW_SPLITS)), _const_spec(cnt_shape)],
            out_specs=[pl.BlockSpec((tt, d), lambda b, t: (b * nt + t, 0)),
                       pl.BlockSpec((tt, dh), lambda b, t: (b * nt + t, 0)),
                       pl.BlockSpec((ROUTE_ROWS, tt), lambda b, t: (0, b * nt + t)),
                       pl.BlockSpec((1, WINDOW, KV_WIDTH), lambda b, t: (b, 0, 0)),
                       pl.BlockSpec((1, WINDOW, KV_WIDTH), lambda b, t: (b, 0, 0)),
                       _const_spec(cnt_shape)],
            out_shape=[tok_out(n_w, d, f32), tok_out(n_w, dh, u32),
                       jax.ShapeDtypeStruct((ROUTE_ROWS, n_w), f32),
                       jax.ShapeDtypeStruct((nbw, WINDOW, KV_WIDTH), f32),
                       jax.ShapeDtypeStruct((nbw, WINDOW, KV_WIDTH), f32),
                       jax.ShapeDtypeStruct(cnt_shape, f32)],
            scratch_shapes=[pltpu.VMEM((N_KV_HEADS, tt + WINDOW, LANES), bf),
                            pltpu.VMEM((N_KV_HEADS, tt + WINDOW, LANES), bf),
                            pltpu.VMEM((tt, Q_WIDTH), bf), pltpu.VMEM((tt, GATE_WIDTH), f32),
                            pltpu.VMEM((tt, GATE_WIDTH), bf), pltpu.VMEM((tt, d), bf),
                            pltpu.VMEM(cnt_shape, f32)],
            compiler_params=pltpu.CompilerParams(dimension_semantics=("arbitrary", "arbitrary"),
                                                 vmem_limit_bytes=VMEM_LIMIT),
            name="mixer_prompt",
        )(sinks, x_prompt, *weight_args, strict_triu(tt // ROW_SPLITS), cnt_in)

    tm = EXPERT_TILE
    assert dh == 2 * SC_COLS and n_s % SC_WINDOW == 0 and seq % SC_WINDOW == 0

    def expert_pass(h2_parts, slabt_parts, cnt):
        n_w = sum(a.shape[0] for a in h2_parts)
        slabt = jnp.concatenate(slabt_parts, axis=1) if len(slabt_parts) > 1 else slabt_parts[0]
        counts = cnt[:, 0].astype(jnp.int32)
        tiles = (counts + tm - 1) // tm
        tile_end = jnp.cumsum(tiles)
        offs = (tile_end - tiles) * tm
        n_used = tile_end[-1]
        n_rows = ((2 * n_w) // tm + N_EXPERTS) * tm
        lane_tiles = n_w // LANES
        pos_blk = LANES * max(k for k in range(1, POS_BLOCK_TILES + 1) if lane_tiles % k == 0)
        offs_b = jnp.broadcast_to(offs.astype(f32)[:, None], cnt_shape)
        pos_row = jax.ShapeDtypeStruct((1, n_w), jnp.int32)
        pos0, pos1 = pl.pallas_call(
            _pos_kernel,
            grid=(n_w // pos_blk,),
            in_specs=[pl.BlockSpec((ROUTE_ROWS, pos_blk), lambda i: (0, i)), _const_spec(cnt_shape)],
            out_specs=[pl.BlockSpec((1, pos_blk), lambda i: (0, i)), pl.BlockSpec((1, pos_blk), lambda i: (0, i))],
            out_shape=[pos_row, pos_row],
            compiler_params=pltpu.CompilerParams(dimension_semantics=("arbitrary",)),
            name="sorted_pos",
        )(slabt, offs_b)

        xs_a, xs_b = _sc_scatter_rows(h2_parts, pos0, pos1, n_rows)

        w_map = lambda e, *_: (e, 0, 0)
        hbm = pl.BlockSpec(memory_space=pl.ANY)
        piece = jax.ShapeDtypeStruct((n_rows, SC_COLS), u32)
        tile_buf = pltpu.VMEM((EXPERT_BUFFERS, tm, SC_COLS), u32)
        tile_sems = pltpu.SemaphoreType.DMA((2, EXPERT_BUFFERS))
        out_a, out_b = pl.pallas_call(
            _expert_kernel,
            grid_spec=pltpu.PrefetchScalarGridSpec(
                num_scalar_prefetch=4,
                grid=(N_EXPERTS,),
                in_specs=[hbm, hbm,
                          pl.BlockSpec((1, d, D_EXPERT), w_map),
                          pl.BlockSpec((1, d, D_EXPERT), w_map),
                          pl.BlockSpec((1, D_EXPERT, d), w_map)],
                out_specs=[hbm, hbm],
                scratch_shapes=[pltpu.VMEM((d, D_EXPERT), bf), pltpu.VMEM((d, D_EXPERT), bf),
                                pltpu.VMEM((D_EXPERT, d), bf),
                                tile_buf, tile_buf, tile_buf, tile_buf, tile_sems, tile_sems]),
            out_shape=[piece, piece],
            compiler_params=pltpu.CompilerParams(dimension_semantics=("arbitrary",), vmem_limit_bytes=VMEM_LIMIT),
            name="experts",
        )((tile_end - tiles).astype(jnp.int32), tile_end.astype(jnp.int32), counts,
          n_used.reshape(1).astype(jnp.int32), xs_a, xs_b, w_gate[l], w_up[l], w_down[l])

        return _sc_gather_rows([out_a, out_b], [pos0, pos1])

    def combine(x1, slab_t, gathered, g_blk0, y_rows, y_blk0, y_prev):
        (o1a, o1b), (o2a, o2b) = gathered
        tok = lambda i: (i, 0)
        args = [x1, o1a, o1b, o2a, o2b, slab_t]
        in_specs = ([pl.BlockSpec((tt, d), tok)] + [pl.BlockSpec((tt, SC_COLS), lambda i: (g_blk0 + i, 0))] * 4
                    + [pl.BlockSpec((ROUTE_ROWS, tt), lambda i: (0, i))])
        aliases = {}
        if y_prev is not None:
            aliases = {len(args): 0}
            args.append(y_prev)
            in_specs.append(pl.BlockSpec(memory_space=pl.ANY))
        return pl.pallas_call(
            _combine_kernel,
            grid=(x1.shape[0] // tt,),
            in_specs=in_specs,
            out_specs=pl.BlockSpec((tt, d), lambda i: (y_blk0 + i, 0)),
            out_shape=jax.ShapeDtypeStruct((y_rows, d), f32),
            input_output_aliases=aliases,
            compiler_params=pltpu.CompilerParams(dimension_semantics=("arbitrary",)),
            name="combine",
        )(*args)

    unit = nb // sum(PROMPT_WAVES)
    assert unit * sum(PROMPT_WAVES) == nb
    wave_seqs = [unit * r for r in PROMPT_WAVES]
    wave_b0 = [sum(wave_seqs[:w]) for w in range(len(wave_seqs))]
    y_p = None
    y_s = None
    kwins, vwins = [], []
    mixed = []
    for w, (b0, nbw) in enumerate(zip(wave_b0, wave_seqs)):
        cnt_in = cnt_s if w == 0 else jnp.zeros(cnt_shape, f32)
        mixed.append(prompt_mixer(b0, nbw, cnt_in))
    for w, (x1_w, h2_w, slabt_w, kwin_w, vwin_w, cnt_w) in enumerate(mixed):
        kwins.append(kwin_w)
        vwins.append(vwin_w)
        h2_parts, slabt_parts = [h2_w], [slabt_w]
        if w == 0:
            h2_parts.append(h2_s)
            slabt_parts.append(slabt_s)
        gathered = expert_pass(h2_parts, slabt_parts, cnt_w)
        y_p = combine(x1_w, slabt_w, gathered, 0, n_p, wave_b0[w] * nt, y_p)
        if w == 0:
            y_s = combine(x1_s, slabt_s, gathered, wave_seqs[0] * nt, n_s, 0, None)
    y_p = y_p.reshape(nb, seq, d)
    y_s = y_s.reshape(ns, slen, d)
    kwin = jnp.concatenate(kwins, axis=0) if len(kwins) > 1 else kwins[0]
    vwin = jnp.concatenate(vwins, axis=0) if len(vwins) > 1 else vwins[0]

    kv_shape = (1, nb, WINDOW, N_KV_HEADS, HEAD_DIM)
    new_k_p = kwin.reshape(kv_shape)
    new_v_p = vwin.reshape(kv_shape)
    keep = WINDOW - slen
    ck4 = cache_attn_k[l][:, WINDOW - keep:]
    cv4 = cache_attn_v[l][:, WINDOW - keep:]
    new_k_s = jnp.concatenate([ck4, kn_s.reshape(ns, slen, N_KV_HEADS, HEAD_DIM)], axis=1)[None]
    new_v_s = jnp.concatenate([cv4, v_s.reshape(ns, slen, N_KV_HEADS, HEAD_DIM)], axis=1)[None]
    new_gv_s = gvn_s.reshape(1, ns, slen, GATE_HEADS, GATE_DIM)
    return (y_p, y_s, new_k_p, new_v_p, new_k_s, new_v_s, new_gv_s)
```

```python
import functools

import jax
import jax.numpy as jnp
from jax import lax
from jax.experimental import pallas as pl
from jax.experimental.pallas import tpu as pltpu
from jax.experimental.pallas import tpu_sc as plsc

D_MODEL = 1024
HEAD_DIM = 64
N_Q_HEADS = 8
N_KV_HEADS = 2
Q_WIDTH = N_Q_HEADS * HEAD_DIM
KV_WIDTH = N_KV_HEADS * HEAD_DIM
GATE_HEADS = 4
GATE_DIM = 128
GATE_WIDTH = GATE_HEADS * GATE_DIM
PROJ_COLS = Q_WIDTH + 2 * KV_WIDTH + 2 * GATE_WIDTH
CHUNK = 64
WINDOW = 128
MLP_CHUNK = 128
N_GROUPS = 8
EXP_PER_GROUP = 8
N_EXPERTS = N_GROUPS * EXP_PER_GROUP
D_EXPERT = 512
EPS = 1e-6

LANES = 128
SUBLANES = 8
ROUTE_LANES = 128
FINE_LANE0 = N_GROUPS
ROUTE_ROWS = SUBLANES
NEG = -1e30
assert N_GROUPS == SUBLANES and EXP_PER_GROUP == SUBLANES

PROMPT_WAVES = (5, 3)
TOKEN_TILE = 512
ROW_SPLITS = 2
EXPERT_TILE = 512
EXPERT_BUFFERS = 3
POS_BLOCK_TILES = 64
SC_WINDOW = 128
SC_COLS = 256
VMEM_LIMIT = 40 * 1024 * 1024


def _pack_halves(x):
    w = x.shape[1] // 2
    b = lax.bitcast_convert_type(x.astype(jnp.bfloat16).astype(jnp.float32), jnp.uint32)
    return (b[:, :w] & jnp.uint32(0xFFFF0000)) | (b[:, w:] >> 16)


def _unpack_halves(p):
    hi = lax.bitcast_convert_type(p & jnp.uint32(0xFFFF0000), jnp.float32)
    lo = lax.bitcast_convert_type(p << 16, jnp.float32)
    return hi, lo


def _rms(x, eps=EPS):
    return x * lax.rsqrt(jnp.mean(x * x, axis=-1, keepdims=True) + eps)


def _in_proj(x, gmix, win):
    h = (_rms(x) * gmix).astype(jnp.bfloat16)
    return jnp.dot(h, win, preferred_element_type=jnp.float32)


def _heads(z, gq, gk, gvg, blk):
    qk = z[:, :Q_WIDTH + KV_WIDTH]
    outs = []
    for j in range((Q_WIDTH + KV_WIDTH) // LANES):
        zj = qk[:, j * LANES:(j + 1) * LANES]
        ss = jnp.dot((zj * zj).astype(jnp.bfloat16), blk, preferred_element_type=jnp.float32)
        outs.append(zj * lax.rsqrt(ss * (1.0 / HEAD_DIM) + EPS))
    qn = jnp.concatenate(outs[:Q_WIDTH // LANES], axis=-1) * gq
    kn = outs[-1] * gk
    v = z[:, Q_WIDTH + KV_WIDTH:Q_WIDTH + 2 * KV_WIDTH]
    u0 = Q_WIDTH + 2 * KV_WIDTH
    ua = jax.nn.gelu(z[:, u0:u0 + GATE_WIDTH])
    ga = jax.nn.gelu(z[:, u0 + GATE_WIDTH:])
    gvn = jnp.concatenate(
        [_rms(ga[:, i * GATE_DIM:(i + 1) * GATE_DIM]) for i in range(GATE_HEADS)], axis=-1) * gvg
    return qn.astype(jnp.bfloat16), kn, v, ua, gvn


def _dup_halves(a):
    lo = lax.broadcasted_iota(jnp.int32, a.shape, 1) < HEAD_DIM
    r = pltpu.roll(a, HEAD_DIM, axis=1)
    return (jnp.where(lo, a, r).astype(jnp.bfloat16), jnp.where(lo, r, a).astype(jnp.bfloat16))


def _attn_block(qa, qb, k2, v2, bias, sk):
    r = qa.shape[0]
    lo = lax.broadcasted_iota(jnp.int32, (r, LANES), 1) < HEAD_DIM
    zero = jnp.zeros_like(qa)
    qs = jnp.concatenate([jnp.where(lo, qa, zero), jnp.where(lo, zero, qa),
                          jnp.where(lo, qb, zero), jnp.where(lo, zero, qb)], axis=0)
    s = lax.dot_general(qs, k2, (((1,), (1,)), ((), ())), preferred_element_type=jnp.float32)
    if bias is not None:
        s = s + bias
    m = jnp.maximum(jnp.max(s, axis=-1, keepdims=True), sk)
    e = jnp.exp(s - m)
    den = jnp.sum(e, axis=-1, keepdims=True) + jnp.exp(sk - m)
    o = jnp.dot(e.astype(jnp.bfloat16), v2, preferred_element_type=jnp.float32) * (1.0 / den)
    pa = jnp.where(lo, o[0:r], o[r:2 * r])
    pb = jnp.where(lo, o[2 * r:3 * r], o[3 * r:4 * r])
    return pa, pb


def _sink_rows(sink_ref, g, r):
    row = lax.broadcasted_iota(jnp.int32, (4 * r, 1), 0)
    s0, s1, s2, s3 = (sink_ref[4 * g + i] for i in range(4))
    return jnp.where(row < r, s0, jnp.where(row < 2 * r, s1, jnp.where(row < 3 * r, s2, s3)))


def _causal_ws(ws_ref, h, n):
    w = ws_ref[h][:n, :n]
    keep = (lax.broadcasted_iota(jnp.int32, (n, n), 0) >= lax.broadcasted_iota(jnp.int32, (n, n), 1))
    return jnp.where(keep, w, jnp.zeros_like(w))


def _out_proj(cat, x, wout):
    return x + jnp.dot(cat, wout, preferred_element_type=jnp.float32)


def _route(x1, gffn, wrt, brt, triu, cnt):
    h2 = _rms(x1) * gffn
    t = x1.shape[0]
    reps = t // LANES
    lgt = lax.dot_general(wrt, h2.astype(jnp.bfloat16), (((1,), (1,)), ((), ())),
                          preferred_element_type=jnp.float32) + jnp.tile(brt, (1, reps))
    ng = float(EXP_PER_GROUP)
    sub = lax.broadcasted_iota(jnp.int32, (EXP_PER_GROUP, t), 0).astype(jnp.float32)
    c = lgt[0:N_GROUPS]
    mc = jnp.max(c, axis=0, keepdims=True)
    grp = jnp.min(jnp.where(c == mc, sub, ng), axis=0, keepdims=True)
    pg = 1.0 / jnp.sum(jnp.exp(c - mc), axis=0, keepdims=True)
    lf = lgt[FINE_LANE0:FINE_LANE0 + EXP_PER_GROUP]
    for g in range(1, N_GROUPS):
        r0 = FINE_LANE0 + g * EXP_PER_GROUP
        lf = jnp.where(grp == float(g), lgt[r0:r0 + EXP_PER_GROUP], lf)
    v1 = jnp.max(lf, axis=0, keepdims=True)
    i1 = jnp.min(jnp.where(lf == v1, sub, ng), axis=0, keepdims=True)
    lf2 = jnp.where(sub == i1, -jnp.inf, lf)
    v2 = jnp.max(lf2, axis=0, keepdims=True)
    i2 = jnp.min(jnp.where(lf2 == v2, sub, ng), axis=0, keepdims=True)
    tt = jnp.exp(v2 - v1)
    w1 = 1.0 / (1.0 + tt)
    w2 = tt * w1
    e1 = grp * ng + i1
    e2 = grp * ng + i2
    row = lax.broadcasted_iota(jnp.int32, (N_EXPERTS, t), 0).astype(jnp.float32)
    sel1 = row == e1
    sel2 = row == e2
    oh = jnp.where(sel1 | sel2, 1.0, 0.0)
    cum = jnp.dot(oh.astype(jnp.bfloat16), triu, preferred_element_type=jnp.float32) + jnp.tile(cnt, (1, reps))
    r1 = jnp.sum(jnp.where(sel1, cum, 0.0), axis=0, keepdims=True)
    r2 = jnp.sum(jnp.where(sel2, cum, 0.0), axis=0, keepdims=True)
    new_cnt = cnt + jnp.sum(oh, axis=1, keepdims=True)
    slab_t = jnp.where(sub == 0.0, e1,
             jnp.where(sub == 1.0, e2,
             jnp.where(sub == 2.0, r1,
             jnp.where(sub == 3.0, r2,
             jnp.where(sub == 4.0, pg * w1,
             jnp.where(sub == 5.0, pg * w2, 0.0))))))
    return _pack_halves(h2), slab_t, new_cnt


def _prompt_kernel(sink_ref, x_ref, gmix_ref, win_ref, gq_ref, gk_ref, gvg_ref, blk_ref, ws_ref, bs_ref,
                   wout_ref, gffn_ref, wr_ref, br_ref, triu_ref, cnt_in_ref,
                   x1_ref, h2_ref, slabt_ref, kwin_ref, vwin_ref, cnt_ref,
                   k2_scr, v2_scr, qn_scr, ua_scr, gvn_scr, cat_scr, cnt_scr):
    b = pl.program_id(0)
    t = pl.program_id(1)
    nt = pl.num_programs(1)
    tt = x_ref.shape[1]
    nblk = tt // WINDOW
    sub = tt // ROW_SPLITS

    @pl.when((b == 0) & (t == 0))
    def _():
        cnt_scr[...] = cnt_in_ref[...]

    @pl.when(t == 0)
    def _():
        k2_scr[:, 0:WINDOW, :] = jnp.zeros((N_KV_HEADS, WINDOW, LANES), jnp.bfloat16)
        v2_scr[:, 0:WINDOW, :] = jnp.zeros((N_KV_HEADS, WINDOW, LANES), jnp.bfloat16)

    pieces = [slice(s * sub, (s + 1) * sub) for s in range(ROW_SPLITS)]
    for s, rs in enumerate(pieces):
        ks = slice(WINDOW + s * sub, WINDOW + (s + 1) * sub)
        z = _in_proj(x_ref[0, rs, :], gmix_ref[...], win_ref[...])
        qn, kn, v, ua, gvn = _heads(z, gq_ref[...], gk_ref[...], gvg_ref[...], blk_ref[...])
        qn_scr[rs, :] = qn
        ua_scr[rs, :] = ua
        gvn_scr[rs, :] = gvn.astype(jnp.bfloat16)
        k0, k1 = _dup_halves(kn)
        v0, v1 = _dup_halves(v)
        k2_scr[0, ks, :] = k0
        k2_scr[1, ks, :] = k1
        v2_scr[0, ks, :] = v0
        v2_scr[1, ks, :] = v1
        if s == ROW_SPLITS - 1:
            @pl.when(t == nt - 1)
            def _():
                kwin_ref[0] = kn[sub - WINDOW:]
                vwin_ref[0] = v[sub - WINDOW:]

    rows = 4 * WINDOW
    band = 2 * WINDOW
    rr = lax.broadcasted_iota(jnp.int32, (rows, band), 0)
    kk = lax.broadcasted_iota(jnp.int32, (rows, band), 1)
    half = ((rr % WINDOW) >= CHUNK).astype(jnp.int32)
    allowed = (kk >= half * CHUNK) & (kk < (3 + half) * CHUNK)
    bias_mid = jnp.where(allowed, 0.0, NEG)
    bias_first = jnp.where(allowed & (kk >= WINDOW), 0.0, NEG)

    def attn_body(j, carry):
        r0 = pl.multiple_of(j * WINDOW, WINDOW)
        bias = jnp.where((t == 0) & (j == 0), bias_first, bias_mid)
        for g in range(N_KV_HEADS):
            c0 = g * 2 * LANES
            qa = qn_scr[pl.ds(r0, WINDOW), c0:c0 + LANES]
            qb = qn_scr[pl.ds(r0, WINDOW), c0 + LANES:c0 + 2 * LANES]
            k2 = k2_scr[g, pl.ds(r0, band), :]
            v2 = v2_scr[g, pl.ds(r0, band), :]
            pa, pb = _attn_block(qa, qb, k2, v2, bias, _sink_rows(sink_ref, g, WINDOW))
            cat_scr[pl.ds(r0, WINDOW), c0:c0 + LANES] = pa.astype(jnp.bfloat16)
            cat_scr[pl.ds(r0, WINDOW), c0 + LANES:c0 + 2 * LANES] = pb.astype(jnp.bfloat16)
        return carry

    lax.fori_loop(0, nblk, attn_body, 0, unroll=True)

    k2_scr[:, 0:WINDOW, :] = k2_scr[:, tt:tt + WINDOW, :]
    v2_scr[:, 0:WINDOW, :] = v2_scr[:, tt:tt + WINDOW, :]

    for h in range(GATE_HEADS):
        w = _causal_ws(ws_ref, h, MLP_CHUNK)
        bcol = bs_ref[:, h:h + 1]
        for c in range(tt // MLP_CHUNK):
            rs = slice(c * MLP_CHUNK, (c + 1) * MLP_CHUNK)
            cs = slice(h * GATE_DIM, (h + 1) * GATE_DIM)
            mix = jnp.dot(w, gvn_scr[rs, cs], preferred_element_type=jnp.float32) + bcol
            cat_scr[rs, Q_WIDTH + h * GATE_DIM:Q_WIDTH + (h + 1) * GATE_DIM] = (ua_scr[rs, cs] * mix).astype(jnp.bfloat16)

    cnt = cnt_scr[...]
    x1_next = _out_proj(cat_scr[pieces[0], :], x_ref[0, pieces[0], :], wout_ref[...])
    for s, rs in enumerate(pieces):
        x1 = x1_next
        if s + 1 < ROW_SPLITS:
            x1_next = _out_proj(cat_scr[pieces[s + 1], :], x_ref[0, pieces[s + 1], :], wout_ref[...])
        x1_ref[rs, :] = x1
        hp, slab_t, cnt = _route(x1, gffn_ref[...], wr_ref[...], br_ref[...], triu_ref[...], cnt)
        h2_ref[rs, :] = hp
        slabt_ref[:, rs] = slab_t
    cnt_scr[...] = cnt
    cnt_ref[...] = cnt


def _sample_kernel(sink_ref, x_ref, ck_ref, cv_ref, gmix_ref, win_ref, gq_ref, gk_ref, gvg_ref, blk_ref, ws_ref,
                   bs_ref, wout_ref, gffn_ref, wr_ref, br_ref, triu_ref,
                   x1_ref, h2_ref, slabt_ref, kn_ref, v_ref, gvn_ref, cnt_ref,
                   ck2_scr, cv2_scr, k2_scr, v2_scr, qn_scr, ua_scr, cat_scr, *, n_seq, seq_len):
    x = x_ref[...]
    qn, kn, v, ua, gvn = _heads(_in_proj(x, gmix_ref[...], win_ref[...]), gq_ref[...], gk_ref[...], gvg_ref[...],
                                blk_ref[...])
    kn_ref[...] = kn
    v_ref[...] = v
    gvn_ref[...] = gvn
    qn_scr[...] = qn
    ua_scr[...] = ua
    for scr, val in ((k2_scr, kn), (v2_scr, v), (ck2_scr, ck_ref[...]), (cv2_scr, cv_ref[...])):
        a0, a1 = _dup_halves(val)
        scr[0] = a0
        scr[1] = a1

    ws = [_causal_ws(ws_ref, h, seq_len) for h in range(GATE_HEADS)]

    def seq_body(i, carry):
        r0 = pl.multiple_of(i * seq_len, seq_len)
        c0r = pl.multiple_of(i * WINDOW, WINDOW)
        for g in range(N_KV_HEADS):
            c0 = g * 2 * LANES
            qa = qn_scr[pl.ds(r0, seq_len), c0:c0 + LANES]
            qb = qn_scr[pl.ds(r0, seq_len), c0 + LANES:c0 + 2 * LANES]
            k2 = jnp.concatenate([ck2_scr[g, pl.ds(c0r, WINDOW), :], k2_scr[g, pl.ds(r0, seq_len), :]], axis=0)
            v2 = jnp.concatenate([cv2_scr[g, pl.ds(c0r, WINDOW), :], v2_scr[g, pl.ds(r0, seq_len), :]], axis=0)
            pa, pb = _attn_block(qa, qb, k2, v2, None, _sink_rows(sink_ref, g, seq_len))
            cat_scr[pl.ds(r0, seq_len), c0:c0 + LANES] = pa.astype(jnp.bfloat16)
            cat_scr[pl.ds(r0, seq_len), c0 + LANES:c0 + 2 * LANES] = pb.astype(jnp.bfloat16)
        for h in range(GATE_HEADS):
            cs = slice(h * GATE_DIM, (h + 1) * GATE_DIM)
            gv_h = gvn_ref[pl.ds(r0, seq_len), cs].astype(jnp.bfloat16)
            mix = jnp.dot(ws[h], gv_h, preferred_element_type=jnp.float32) + bs_ref[0:seq_len, h:h + 1]
            cat_scr[pl.ds(r0, seq_len), Q_WIDTH + h * GATE_DIM:Q_WIDTH + (h + 1) * GATE_DIM] = (
                ua_scr[pl.ds(r0, seq_len), cs] * mix).astype(jnp.bfloat16)
        return carry

    lax.fori_loop(0, n_seq, seq_body, 0)

    cnt0 = jnp.zeros((N_EXPERTS, LANES), jnp.float32)
    x1 = _out_proj(cat_scr[...], x, wout_ref[...])
    hp, slab_t, new_cnt = _route(x1, gffn_ref[...], wr_ref[...], br_ref[...], triu_ref[...], cnt0)
    x1_ref[...] = x1
    h2_ref[...] = hp
    slabt_ref[...] = slab_t
    cnt_ref[...] = new_cnt


def _expert_kernel(ts_ref, te_ref, cnt_ref, nu_ref, xa_hbm, xb_hbm, wg_ref, wu_ref, wd_ref, oa_hbm, ob_hbm,
                   *rest, emit_bf16, weights_bf16):
    rest = list(rest)
    bf16_out = [rest.pop(0) for _ in range(3)] if emit_bf16 else None
    if not weights_bf16:
        wg_s, wu_s, wd_s = (rest.pop(0) for _ in range(3))
    xa_buf, xb_buf, oa_buf, ob_buf, in_sem, out_sem = rest
    e = pl.program_id(0)
    nbuf, tm, _ = xa_buf.shape
    n_used = nu_ref[0]

    def rows_of(g):
        return pl.ds(pl.multiple_of(g * tm, tm), tm)

    def in_copies(g, slot):
        return (pltpu.make_async_copy(xa_hbm.at[rows_of(g)], xa_buf.at[slot], in_sem.at[0, slot]),
                pltpu.make_async_copy(xb_hbm.at[rows_of(g)], xb_buf.at[slot], in_sem.at[1, slot]))

    def out_copies(g, slot):
        return (pltpu.make_async_copy(oa_buf.at[slot], oa_hbm.at[rows_of(g)], out_sem.at[0, slot]),
                pltpu.make_async_copy(ob_buf.at[slot], ob_hbm.at[rows_of(g)], out_sem.at[1, slot]))

    @pl.when(e == 0)
    def _():
        for g in range(nbuf - 1):
            @pl.when(g < n_used)
            def _():
                for c in in_copies(g, g):
                    c.start()

    g_lo = ts_ref[e]
    g_hi = te_ref[e]

    if weights_bf16:
        wg_s, wu_s, wd_s = wg_ref.at[0], wu_ref.at[0], wd_ref.at[0]
    elif bf16_out:
        wg_s[...] = wg_ref[0].astype(jnp.bfloat16)
        wu_s[...] = wu_ref[0].astype(jnp.bfloat16)
        wd_s[...] = wd_ref[0].astype(jnp.bfloat16)
        bf16_out[0][0] = wg_s[...]
        bf16_out[1][0] = wu_s[...]
        bf16_out[2][0] = wd_s[...]
    else:
        @pl.when(g_hi > g_lo)
        def _():
            wg_s[...] = wg_ref[0].astype(jnp.bfloat16)
            wu_s[...] = wu_ref[0].astype(jnp.bfloat16)
            wd_s[...] = wd_ref[0].astype(jnp.bfloat16)

    def tile_body(g, slot):
        for c in in_copies(g, slot):
            c.wait()
        ahead = g + (nbuf - 1)
        ahead_slot = jnp.where(slot == 0, nbuf - 1, slot - 1)

        @pl.when(ahead < n_used)
        def _():
            for c in in_copies(ahead, ahead_slot):
                c.start()

        live = lax.broadcasted_iota(jnp.int32, (tm, xa_buf.shape[2]), 0) < cnt_ref[e] - (g - g_lo) * tm
        ha, la = _unpack_halves(jnp.where(live, xa_buf[slot], jnp.uint32(0)))
        hb, lb = _unpack_halves(jnp.where(live, xb_buf[slot], jnp.uint32(0)))
        xs = jnp.concatenate([ha, hb, la, lb], axis=1).astype(jnp.bfloat16)
        acts = []
        for r in range(0, tm, tm // 2):
            hg = jnp.dot(xs[r:r + tm // 2], wg_s[...], preferred_element_type=jnp.float32)
            hu = jnp.dot(xs[r:r + tm // 2], wu_s[...], preferred_element_type=jnp.float32)
            acts.append((jax.nn.silu(hg) * hu).astype(jnp.bfloat16))
        packed = jnp.concatenate(
            [_pack_halves(jnp.dot(a, wd_s[...], preferred_element_type=jnp.float32)) for a in acts], axis=0)
        q = packed.shape[1] // 2

        @pl.when(g >= nbuf)
        def _():
            for c in out_copies(g - nbuf, slot):
                c.wait()

        oa_buf[slot] = packed[:, :q]
        ob_buf[slot] = packed[:, q:]
        for c in out_copies(g, slot):
            c.start()
        return jnp.where(slot == nbuf - 1, 0, slot + 1)

    lax.fori_loop(g_lo, g_hi, tile_body, g_lo % nbuf)

    @pl.when(e == pl.num_programs(0) - 1)
    def _():
        for back in range(1, nbuf + 1):
            @pl.when(n_used >= back)
            def _():
                g = n_used - back
                for c in out_copies(g, g % nbuf):
                    c.wait()


def _pos_kernel(slabt_ref, offs_ref, pos0_ref, pos1_ref):
    st = slabt_ref[...]
    t = st.shape[1]
    row = lax.broadcasted_iota(jnp.int32, (N_EXPERTS, t), 0).astype(jnp.float32)
    offs = jnp.tile(offs_ref[...], (1, t // LANES))
    for s, out in enumerate((pos0_ref, pos1_ref)):
        first = jnp.sum(jnp.where(row == st[s:s + 1], offs, 0.0), axis=0, keepdims=True)
        out[...] = (first + st[2 + s:3 + s]).astype(jnp.int32)


def _combine_kernel(x1_ref, o1a_ref, o1b_ref, o2a_ref, o2b_ref, slabt_ref, *rest):
    y_ref = rest[-1]
    st = slabt_ref[...]
    slab = jnp.concatenate([st, jnp.zeros((LANES - st.shape[0], st.shape[1]), jnp.float32)], axis=0).T
    g1 = slab[:, 4:5]
    g2 = slab[:, 5:6]
    q = o1a_ref.shape[1]
    for c, (r1, r2) in enumerate(((o1a_ref, o2a_ref), (o1b_ref, o2b_ref))):
        h1, l1 = _unpack_halves(r1[...])
        h2, l2 = _unpack_halves(r2[...])
        hs = slice(c * q, (c + 1) * q)
        ls = slice(2 * q + c * q, 2 * q + (c + 1) * q)
        y_ref[:, hs] = x1_ref[:, hs] + (g1 * h1 + g2 * h2)
        y_ref[:, ls] = x1_ref[:, ls] + (g1 * l1 + g2 * l2)


def _sc_mesh():
    return plsc.VectorSubcoreMesh(core_axis_name="core", subcore_axis_name="subcore")


def _sc_scatter_rows(parts, pos0, pos1, n_rows):
    n_piece = parts[0].shape[1] // SC_COLS

    def sc_kernel(*refs):
        x_refs = refs[:len(parts)]
        i0_hbm, i1_hbm = refs[len(parts):len(parts) + 2]
        o_refs = refs[len(parts) + 2:]
        tok0 = 0
        for x_hbm, arr in zip(x_refs, parts):
            blk0 = tok0 // SC_WINDOW
            for c in range(n_piece):
                def body(x_vmem, i0_vmem, i1_vmem, o_hbm=o_refs[c]):
                    pltpu.sync_copy(x_vmem, o_hbm.at[i0_vmem.at[0]])
                    pltpu.sync_copy(x_vmem, o_hbm.at[i1_vmem.at[0]])

                pltpu.emit_pipeline(
                    body,
                    grid=(arr.shape[0] // SC_WINDOW,),
                    in_specs=[pl.BlockSpec((SC_WINDOW, SC_COLS), lambda i, c=c: (i, c)),
                              pl.BlockSpec((1, SC_WINDOW), lambda i, blk0=blk0: (0, blk0 + i)),
                              pl.BlockSpec((1, SC_WINDOW), lambda i, blk0=blk0: (0, blk0 + i))],
                    out_specs=[],
                    core_axis_name=("core", "subcore"),
                    dimension_semantics=(pltpu.PARALLEL,),
                )(x_hbm, i0_hbm, i1_hbm)
            tok0 += arr.shape[0]

    piece = jax.ShapeDtypeStruct((n_rows, SC_COLS), jnp.uint32)
    return pl.kernel(sc_kernel, out_type=(piece,) * n_piece, mesh=_sc_mesh(), name="scatter_rows")(
        *parts, pos0, pos1)


def _sc_gather_rows(pieces, pos_list):
    n_tok = pos_list[0].shape[1]

    def sc_kernel(*refs):
        s_refs = refs[:len(pieces)]
        i_refs = refs[len(pieces):len(pieces) + len(pos_list)]
        o_refs = refs[len(pieces) + len(pos_list):]
        k = 0
        for i_hbm in i_refs:
            for s_hbm in s_refs:
                def body(i_vmem, o_vmem, s_hbm=s_hbm):
                    pltpu.sync_copy(s_hbm.at[i_vmem.at[0]], o_vmem)

                pltpu.emit_pipeline(
                    body,
                    grid=(n_tok // SC_WINDOW,),
                    in_specs=[pl.BlockSpec((1, SC_WINDOW), lambda i: (0, i))],
                    out_specs=[pl.BlockSpec((SC_WINDOW, SC_COLS), lambda i: (i, 0))],
                    core_axis_name=("core", "subcore"),
                    dimension_semantics=(pltpu.PARALLEL,),
                )(i_hbm, o_refs[k])
                k += 1

    out = jax.ShapeDtypeStruct((n_tok, SC_COLS), jnp.uint32)
    outs = pl.kernel(sc_kernel, out_type=(out,) * (len(pieces) * len(pos_list)), mesh=_sc_mesh(),
                     name="gather_rows")(*pieces, *pos_list)
    return [outs[j * len(pieces):(j + 1) * len(pieces)] for j in range(len(pos_list))]


def _const_spec(shape, single_buffer=False):
    nd = len(shape)
    mode = pl.Buffered(1) if single_buffer else None
    return pl.BlockSpec(shape, lambda *_: (0,) * nd, pipeline_mode=mode)


def kernel(x_prompt, x_sample, cache_attn_k, cache_attn_v, g_mix, w_in, g_q, g_k, g_v, attn_sinks, w_s, b_s,
           w_out, g_ffn, w_coarse, b_coarse, w_fine, b_fine, w_gate, w_up, w_down):
    nb, seq, d = x_prompt.shape
    ns, slen, _ = x_sample.shape
    n_p = nb * seq
    n_s = ns * slen
    n_tok = n_p + n_s
    tt = TOKEN_TILE
    assert seq % tt == 0 and n_s % tt == 0 and d == D_MODEL
    nt = seq // tt
    bf = jnp.bfloat16
    f32 = jnp.float32

    l = 0
    gmix = g_mix[l].reshape(1, d)
    win = w_in[l].astype(bf)
    gq = (jnp.tile(g_q[l], N_Q_HEADS) * (HEAD_DIM ** -0.5)).reshape(1, Q_WIDTH)
    gk = jnp.tile(g_k[l], N_KV_HEADS).reshape(1, KV_WIDTH)
    gvg = g_v[l].reshape(1, GATE_WIDTH)
    sinks = attn_sinks[l].reshape(N_Q_HEADS).astype(f32)
    ws = w_s[l].astype(bf)
    bs = b_s[l].T
    wout = w_out[l].astype(bf)
    gffn = g_ffn[l].reshape(1, d)
    wr = jnp.concatenate([w_coarse[l], jnp.transpose(w_fine[l], (1, 0, 2)).reshape(d, N_EXPERTS),
                          jnp.zeros((d, ROUTE_LANES - N_GROUPS - N_EXPERTS), f32)], axis=1)
    wr = wr.astype(bf).T
    br = jnp.concatenate([b_coarse[l], b_fine[l].reshape(-1),
                          jnp.zeros((ROUTE_LANES - N_GROUPS - N_EXPERTS,), f32)])
    br = jnp.broadcast_to(br[:, None], (ROUTE_LANES, LANES))
    cnt_shape = (N_EXPERTS, LANES)
    ii = jnp.arange(LANES)
    blk = (ii[:, None] // HEAD_DIM == ii[None, :] // HEAD_DIM).astype(bf)
    u32 = jnp.uint32
    dh = d // 2

    def strict_triu(n):
        r = jnp.arange(n)
        return (r[:, None] < r[None, :]).astype(bf)

    weight_args = (gmix, win, gq, gk, gvg, blk, ws, bs, wout, gffn, wr, br)
    weight_specs = [_const_spec(a.shape, single_buffer=True) for a in weight_args]
    smem_spec = pl.BlockSpec(memory_space=pltpu.SMEM)

    xs2 = x_sample.reshape(n_s, d)
    ck = cache_attn_k[l].reshape(ns * WINDOW, KV_WIDTH)
    cv = cache_attn_v[l].reshape(ns * WINDOW, KV_WIDTH)
    tok_out = lambda n, w, dt: jax.ShapeDtypeStruct((n, w), dt)
    x1_s, h2_s, slabt_s, kn_s, v_s, gvn_s, cnt_s = pl.pallas_call(
        functools.partial(_sample_kernel, n_seq=ns, seq_len=slen),
        grid=(1,),
        in_specs=[smem_spec, _const_spec((n_s, d)), _const_spec(ck.shape), _const_spec(cv.shape)]
                 + weight_specs + [_const_spec((n_s, n_s))],
        out_specs=[_const_spec((n_s, d)), _const_spec((n_s, dh)),
                   _const_spec((ROUTE_ROWS, n_s)),
                   _const_spec((n_s, KV_WIDTH)), _const_spec((n_s, KV_WIDTH)), _const_spec((n_s, GATE_WIDTH)),
                   _const_spec(cnt_shape)],
        out_shape=[tok_out(n_s, d, f32), tok_out(n_s, dh, u32),
                   jax.ShapeDtypeStruct((ROUTE_ROWS, n_s), f32),
                   jax.ShapeDtypeStruct((n_s, KV_WIDTH), f32), jax.ShapeDtypeStruct((n_s, KV_WIDTH), f32),
                   jax.ShapeDtypeStruct((n_s, GATE_WIDTH), f32), jax.ShapeDtypeStruct(cnt_shape, f32)],
        scratch_shapes=[pltpu.VMEM((N_KV_HEADS, ns * WINDOW, LANES), bf), pltpu.VMEM((N_KV_HEADS, ns * WINDOW, LANES), bf),
                        pltpu.VMEM((N_KV_HEADS, n_s, LANES), bf), pltpu.VMEM((N_KV_HEADS, n_s, LANES), bf),
                        pltpu.VMEM((n_s, Q_WIDTH), bf), pltpu.VMEM((n_s, GATE_WIDTH), f32),
                        pltpu.VMEM((n_s, d), bf)],
        compiler_params=pltpu.CompilerParams(dimension_semantics=("arbitrary",), vmem_limit_bytes=VMEM_LIMIT),
        name="mixer_sample",
    )(sinks, xs2, ck, cv, *weight_args, strict_triu(n_s))

    def prompt_mixer(b0, nbw, cnt_in):
        n_w = nbw * seq
        return pl.pallas_call(
            _prompt_kernel,
            grid=(nbw, nt),
            in_specs=[smem_spec, pl.BlockSpec((1, tt, d), lambda b, t: (b0 + b, t, 0))] + weight_specs
                     + [_const_spec((tt // ROW_SPLITS, tt // ROW_SPLITS)), _const_spec(cnt_shape)],
            out_specs=[pl.BlockSpec((tt, d), lambda b, t: (b * nt + t, 0)),
                       pl.BlockSpec((tt, dh), lambda b, t: (b * nt + t, 0)),
                       pl.BlockSpec((ROUTE_ROWS, tt), lambda b, t: (0, b * nt + t)),
                       pl.BlockSpec((1, WINDOW, KV_WIDTH), lambda b, t: (b, 0, 0)),
                       pl.BlockSpec((1, WINDOW, KV_WIDTH), lambda b, t: (b, 0, 0)),
                       _const_spec(cnt_shape)],
            out_shape=[tok_out(n_w, d, f32), tok_out(n_w, dh, u32),
                       jax.ShapeDtypeStruct((ROUTE_ROWS, n_w), f32),
                       jax.ShapeDtypeStruct((nbw, WINDOW, KV_WIDTH), f32),
                       jax.ShapeDtypeStruct((nbw, WINDOW, KV_WIDTH), f32),
                       jax.ShapeDtypeStruct(cnt_shape, f32)],
            scratch_shapes=[pltpu.VMEM((N_KV_HEADS, tt + WINDOW, LANES), bf),
                            pltpu.VMEM((N_KV_HEADS, tt + WINDOW, LANES), bf),
                            pltpu.VMEM((tt, Q_WIDTH), bf), pltpu.VMEM((tt, GATE_WIDTH), f32),
                            pltpu.VMEM((tt, GATE_WIDTH), bf), pltpu.VMEM((tt, d), bf),
                            pltpu.VMEM(cnt_shape, f32)],
            compiler_params=pltpu.CompilerParams(dimension_semantics=("arbitrary", "arbitrary"),
                                                 vmem_limit_bytes=VMEM_LIMIT),
            name="mixer_prompt",
        )(sinks, x_prompt, *weight_args, strict_triu(tt // ROW_SPLITS), cnt_in)

    tm = EXPERT_TILE
    assert dh == 2 * SC_COLS and n_s % SC_WINDOW == 0 and seq % SC_WINDOW == 0

    def expert_pass(h2_parts, slabt_parts, cnt, weights, emit_bf16):
        n_w = sum(a.shape[0] for a in h2_parts)
        slabt = jnp.concatenate(slabt_parts, axis=1) if len(slabt_parts) > 1 else slabt_parts[0]
        counts = cnt[:, 0].astype(jnp.int32)
        tiles = (counts + tm - 1) // tm
        tile_end = jnp.cumsum(tiles)
        offs = (tile_end - tiles) * tm
        n_used = tile_end[-1]
        n_rows = ((2 * n_w) // tm + N_EXPERTS) * tm
        lane_tiles = n_w // LANES
        pos_blk = LANES * max(k for k in range(1, POS_BLOCK_TILES + 1) if lane_tiles % k == 0)
        offs_b = jnp.broadcast_to(offs.astype(f32)[:, None], cnt_shape)
        pos_row = jax.ShapeDtypeStruct((1, n_w), jnp.int32)
        pos0, pos1 = pl.pallas_call(
            _pos_kernel,
            grid=(n_w // pos_blk,),
            in_specs=[pl.BlockSpec((ROUTE_ROWS, pos_blk), lambda i: (0, i)), _const_spec(cnt_shape)],
            out_specs=[pl.BlockSpec((1, pos_blk), lambda i: (0, i)), pl.BlockSpec((1, pos_blk), lambda i: (0, i))],
            out_shape=[pos_row, pos_row],
            compiler_params=pltpu.CompilerParams(dimension_semantics=("arbitrary",)),
            name="sorted_pos",
        )(slabt, offs_b)

        xs_a, xs_b = _sc_scatter_rows(h2_parts, pos0, pos1, n_rows)

        w_map = lambda e, *_: (e, 0, 0)
        hbm = pl.BlockSpec(memory_space=pl.ANY)
        piece = jax.ShapeDtypeStruct((n_rows, SC_COLS), u32)
        tile_buf = pltpu.VMEM((EXPERT_BUFFERS, tm, SC_COLS), u32)
        tile_sems = pltpu.SemaphoreType.DMA((2, EXPERT_BUFFERS))
        weights_bf16 = weights[0].dtype == bf
        w_specs = [pl.BlockSpec((1, d, D_EXPERT), w_map), pl.BlockSpec((1, d, D_EXPERT), w_map),
                   pl.BlockSpec((1, D_EXPERT, d), w_map)]
        cast_scratch = [] if weights_bf16 else [pltpu.VMEM((d, D_EXPERT), bf), pltpu.VMEM((d, D_EXPERT), bf),
                                                pltpu.VMEM((D_EXPERT, d), bf)]
        outs = pl.pallas_call(
            functools.partial(_expert_kernel, emit_bf16=emit_bf16, weights_bf16=weights_bf16),
            grid_spec=pltpu.PrefetchScalarGridSpec(
                num_scalar_prefetch=4,
                grid=(N_EXPERTS,),
                in_specs=[hbm, hbm] + w_specs,
                out_specs=[hbm, hbm] + (w_specs if emit_bf16 else []),
                scratch_shapes=cast_scratch + [tile_buf, tile_buf, tile_buf, tile_buf, tile_sems, tile_sems]),
            out_shape=[piece, piece] + ([jax.ShapeDtypeStruct(w.shape, bf) for w in weights] if emit_bf16 else []),
            compiler_params=pltpu.CompilerParams(dimension_semantics=("arbitrary",), vmem_limit_bytes=VMEM_LIMIT),
            name="experts",
        )((tile_end - tiles).astype(jnp.int32), tile_end.astype(jnp.int32), counts,
          n_used.reshape(1).astype(jnp.int32), xs_a, xs_b, *weights)

        return _sc_gather_rows(outs[:2], [pos0, pos1]), tuple(outs[2:])

    def combine(x1, slab_t, gathered, g_blk0, y_rows, y_blk0, y_prev):
        (o1a, o1b), (o2a, o2b) = gathered
        tok = lambda i: (i, 0)
        args = [x1, o1a, o1b, o2a, o2b, slab_t]
        in_specs = ([pl.BlockSpec((tt, d), tok)] + [pl.BlockSpec((tt, SC_COLS), lambda i: (g_blk0 + i, 0))] * 4
                    + [pl.BlockSpec((ROUTE_ROWS, tt), lambda i: (0, i))])
        aliases = {}
        if y_prev is not None:
            aliases = {len(args): 0}
            args.append(y_prev)
            in_specs.append(pl.BlockSpec(memory_space=pl.ANY))
        return pl.pallas_call(
            _combine_kernel,
            grid=(x1.shape[0] // tt,),
            in_specs=in_specs,
            out_specs=pl.BlockSpec((tt, d), lambda i: (y_blk0 + i, 0)),
            out_shape=jax.ShapeDtypeStruct((y_rows, d), f32),
            input_output_aliases=aliases,
            compiler_params=pltpu.CompilerParams(dimension_semantics=("arbitrary",)),
            name="combine",
        )(*args)

    unit = nb // sum(PROMPT_WAVES)
    assert unit * sum(PROMPT_WAVES) == nb
    wave_seqs = [unit * r for r in PROMPT_WAVES]
    wave_b0 = [sum(wave_seqs[:w]) for w in range(len(wave_seqs))]
    y_p = None
    y_s = None
    kwins, vwins = [], []
    weights = (w_gate[l], w_up[l], w_down[l])
    mixed = []
    for w, (b0, nbw) in enumerate(zip(wave_b0, wave_seqs)):
        cnt_in = cnt_s if w == 0 else jnp.zeros(cnt_shape, f32)
        mixed.append(prompt_mixer(b0, nbw, cnt_in))
    for w, (x1_w, h2_w, slabt_w, kwin_w, vwin_w, cnt_w) in enumerate(mixed):
        kwins.append(kwin_w)
        vwins.append(vwin_w)
        h2_parts, slabt_parts = [h2_w], [slabt_w]
        if w == 0:
            h2_parts.append(h2_s)
            slabt_parts.append(slabt_s)
        last = w == len(mixed) - 1
        gathered, cast = expert_pass(h2_parts, slabt_parts, cnt_w, weights, emit_bf16=(w == 0 and not last))
        if cast:
            weights = cast
        y_p = combine(x1_w, slabt_w, gathered, 0, n_p, wave_b0[w] * nt, y_p)
        if w == 0:
            y_s = combine(x1_s, slabt_s, gathered, wave_seqs[0] * nt, n_s, 0, None)
    y_p = y_p.reshape(nb, seq, d)
    y_s = y_s.reshape(ns, slen, d)
    kwin = jnp.concatenate(kwins, axis=0) if len(kwins) > 1 else kwins[0]
    vwin = jnp.concatenate(vwins, axis=0) if len(vwins) > 1 else vwins[0]

    kv_shape = (1, nb, WINDOW, N_KV_HEADS, HEAD_DIM)
    new_k_p = kwin.reshape(kv_shape)
    new_v_p = vwin.reshape(kv_shape)
    keep = WINDOW - slen
    ck4 = cache_attn_k[l][:, WINDOW - keep:]
    cv4 = cache_attn_v[l][:, WINDOW - keep:]
    new_k_s = jnp.concatenate([ck4, kn_s.reshape(ns, slen, N_KV_HEADS, HEAD_DIM)], axis=1)[None]
    new_v_s = jnp.concatenate([cv4, v_s.reshape(ns, slen, N_KV_HEADS, HEAD_DIM)], axis=1)[None]
    new_gv_s = gvn_s.reshape(1, ns, slen, GATE_HEADS, GATE_DIM)
    return (y_p, y_s, new_k_p, new_v_p, new_k_s, new_v_s, new_gv_s)
```

```python
import functools

import jax
import jax.numpy as jnp
from jax import lax
from jax.experimental import pallas as pl
from jax.experimental.pallas import tpu as pltpu
from jax.experimental.pallas import tpu_sc as plsc

D_MODEL = 1024
HEAD_DIM = 64
N_Q_HEADS = 8
N_KV_HEADS = 2
Q_WIDTH = N_Q_HEADS * HEAD_DIM
KV_WIDTH = N_KV_HEADS * HEAD_DIM
GATE_HEADS = 4
GATE_DIM = 128
GATE_WIDTH = GATE_HEADS * GATE_DIM
PROJ_COLS = Q_WIDTH + 2 * KV_WIDTH + 2 * GATE_WIDTH
CHUNK = 64
WINDOW = 128
MLP_CHUNK = 128
N_GROUPS = 8
EXP_PER_GROUP = 8
N_EXPERTS = N_GROUPS * EXP_PER_GROUP
D_EXPERT = 512
EPS = 1e-6

LANES = 128
SUBLANES = 8
ROUTE_LANES = 128
FINE_LANE0 = N_GROUPS
ROUTE_ROWS = SUBLANES
NEG = -1e30
assert N_GROUPS == SUBLANES and EXP_PER_GROUP == SUBLANES

PROMPT_WAVES = (5, 3)
TOKEN_TILE = 512
ROW_SPLITS = 2
EXPERT_TILE = 512
EXPERT_BUFFERS = 3
TILE_LOAD_PRIORITY = 1
POS_BLOCK_TILES = 64
SC_WINDOW = 128
SC_COLS = 256
VMEM_LIMIT = 40 * 1024 * 1024


def _pack_halves(x):
    w = x.shape[1] // 2
    b = lax.bitcast_convert_type(x.astype(jnp.bfloat16).astype(jnp.float32), jnp.uint32)
    return (b[:, :w] & jnp.uint32(0xFFFF0000)) | (b[:, w:] >> 16)


def _unpack_halves(p):
    hi = lax.bitcast_convert_type(p & jnp.uint32(0xFFFF0000), jnp.float32)
    lo = lax.bitcast_convert_type(p << 16, jnp.float32)
    return hi, lo


def _rms(x, eps=EPS):
    return x * lax.rsqrt(jnp.mean(x * x, axis=-1, keepdims=True) + eps)


def _in_proj(x, gmix, win):
    h = (_rms(x) * gmix).astype(jnp.bfloat16)
    return jnp.dot(h, win, preferred_element_type=jnp.float32)


def _heads(z, gq, gk, gvg, blk):
    qk = z[:, :Q_WIDTH + KV_WIDTH]
    outs = []
    for j in range((Q_WIDTH + KV_WIDTH) // LANES):
        zj = qk[:, j * LANES:(j + 1) * LANES]
        ss = jnp.dot((zj * zj).astype(jnp.bfloat16), blk, preferred_element_type=jnp.float32)
        outs.append(zj * lax.rsqrt(ss * (1.0 / HEAD_DIM) + EPS))
    qn = jnp.concatenate(outs[:Q_WIDTH // LANES], axis=-1) * gq
    kn = outs[-1] * gk
    v = z[:, Q_WIDTH + KV_WIDTH:Q_WIDTH + 2 * KV_WIDTH]
    u0 = Q_WIDTH + 2 * KV_WIDTH
    ua = jax.nn.gelu(z[:, u0:u0 + GATE_WIDTH])
    ga = jax.nn.gelu(z[:, u0 + GATE_WIDTH:])
    gvn = jnp.concatenate(
        [_rms(ga[:, i * GATE_DIM:(i + 1) * GATE_DIM]) for i in range(GATE_HEADS)], axis=-1) * gvg
    return qn.astype(jnp.bfloat16), kn, v, ua, gvn


def _dup_halves(a):
    lo = lax.broadcasted_iota(jnp.int32, a.shape, 1) < HEAD_DIM
    r = pltpu.roll(a, HEAD_DIM, axis=1)
    return (jnp.where(lo, a, r).astype(jnp.bfloat16), jnp.where(lo, r, a).astype(jnp.bfloat16))


def _attn_block(qa, qb, k2, v2, bias, sk):
    r = qa.shape[0]
    lo = lax.broadcasted_iota(jnp.int32, (r, LANES), 1) < HEAD_DIM
    zero = jnp.zeros_like(qa)
    qs = jnp.concatenate([jnp.where(lo, qa, zero), jnp.where(lo, zero, qa),
                          jnp.where(lo, qb, zero), jnp.where(lo, zero, qb)], axis=0)
    s = lax.dot_general(qs, k2, (((1,), (1,)), ((), ())), preferred_element_type=jnp.float32)
    if bias is not None:
        s = s + bias
    m = jnp.maximum(jnp.max(s, axis=-1, keepdims=True), sk)
    e = jnp.exp(s - m)
    den = jnp.sum(e, axis=-1, keepdims=True) + jnp.exp(sk - m)
    o = jnp.dot(e.astype(jnp.bfloat16), v2, preferred_element_type=jnp.float32) * (1.0 / den)
    pa = jnp.where(lo, o[0:r], o[r:2 * r])
    pb = jnp.where(lo, o[2 * r:3 * r], o[3 * r:4 * r])
    return pa, pb


def _sink_rows(sink_ref, g, r):
    row = lax.broadcasted_iota(jnp.int32, (4 * r, 1), 0)
    s0, s1, s2, s3 = (sink_ref[4 * g + i] for i in range(4))
    return jnp.where(row < r, s0, jnp.where(row < 2 * r, s1, jnp.where(row < 3 * r, s2, s3)))


def _causal_ws(ws_ref, h, n):
    w = ws_ref[h][:n, :n]
    keep = (lax.broadcasted_iota(jnp.int32, (n, n), 0) >= lax.broadcasted_iota(jnp.int32, (n, n), 1))
    return jnp.where(keep, w, jnp.zeros_like(w))


def _out_proj(cat, x, wout):
    return x + jnp.dot(cat, wout, preferred_element_type=jnp.float32)


def _route(x1, gffn, wrt, brt, triu, cnt):
    h2 = _rms(x1) * gffn
    t = x1.shape[0]
    reps = t // LANES
    lgt = lax.dot_general(wrt, h2.astype(jnp.bfloat16), (((1,), (1,)), ((), ())),
                          preferred_element_type=jnp.float32) + jnp.tile(brt, (1, reps))
    ng = float(EXP_PER_GROUP)
    sub = lax.broadcasted_iota(jnp.int32, (EXP_PER_GROUP, t), 0).astype(jnp.float32)
    c = lgt[0:N_GROUPS]
    mc = jnp.max(c, axis=0, keepdims=True)
    grp = jnp.min(jnp.where(c == mc, sub, ng), axis=0, keepdims=True)
    pg = 1.0 / jnp.sum(jnp.exp(c - mc), axis=0, keepdims=True)
    lf = lgt[FINE_LANE0:FINE_LANE0 + EXP_PER_GROUP]
    for g in range(1, N_GROUPS):
        r0 = FINE_LANE0 + g * EXP_PER_GROUP
        lf = jnp.where(grp == float(g), lgt[r0:r0 + EXP_PER_GROUP], lf)
    v1 = jnp.max(lf, axis=0, keepdims=True)
    i1 = jnp.min(jnp.where(lf == v1, sub, ng), axis=0, keepdims=True)
    lf2 = jnp.where(sub == i1, -jnp.inf, lf)
    v2 = jnp.max(lf2, axis=0, keepdims=True)
    i2 = jnp.min(jnp.where(lf2 == v2, sub, ng), axis=0, keepdims=True)
    tt = jnp.exp(v2 - v1)
    w1 = 1.0 / (1.0 + tt)
    w2 = tt * w1
    e1 = grp * ng + i1
    e2 = grp * ng + i2
    row = lax.broadcasted_iota(jnp.int32, (N_EXPERTS, t), 0).astype(jnp.float32)
    sel1 = row == e1
    sel2 = row == e2
    oh = jnp.where(sel1 | sel2, 1.0, 0.0)
    cum = jnp.dot(oh.astype(jnp.bfloat16), triu, preferred_element_type=jnp.float32) + jnp.tile(cnt, (1, reps))
    r1 = jnp.sum(jnp.where(sel1, cum, 0.0), axis=0, keepdims=True)
    r2 = jnp.sum(jnp.where(sel2, cum, 0.0), axis=0, keepdims=True)
    new_cnt = cnt + jnp.sum(oh, axis=1, keepdims=True)
    slab_t = jnp.where(sub == 0.0, e1,
             jnp.where(sub == 1.0, e2,
             jnp.where(sub == 2.0, r1,
             jnp.where(sub == 3.0, r2,
             jnp.where(sub == 4.0, pg * w1,
             jnp.where(sub == 5.0, pg * w2, 0.0))))))
    return _pack_halves(h2), slab_t, new_cnt


def _prompt_kernel(sink_ref, x_ref, gmix_ref, win_ref, gq_ref, gk_ref, gvg_ref, blk_ref, ws_ref, bs_ref,
                   wout_ref, gffn_ref, wr_ref, br_ref, triu_ref, cnt_in_ref,
                   x1_ref, h2_ref, slabt_ref, kwin_ref, vwin_ref, cnt_ref,
                   k2_scr, v2_scr, qn_scr, ua_scr, gvn_scr, cat_scr, cnt_scr):
    b = pl.program_id(0)
    t = pl.program_id(1)
    nt = pl.num_programs(1)
    tt = x_ref.shape[1]
    nblk = tt // WINDOW
    sub = tt // ROW_SPLITS

    @pl.when((b == 0) & (t == 0))
    def _():
        cnt_scr[...] = cnt_in_ref[...]

    @pl.when(t == 0)
    def _():
        k2_scr[:, 0:WINDOW, :] = jnp.zeros((N_KV_HEADS, WINDOW, LANES), jnp.bfloat16)
        v2_scr[:, 0:WINDOW, :] = jnp.zeros((N_KV_HEADS, WINDOW, LANES), jnp.bfloat16)

    pieces = [slice(s * sub, (s + 1) * sub) for s in range(ROW_SPLITS)]
    for s, rs in enumerate(pieces):
        ks = slice(WINDOW + s * sub, WINDOW + (s + 1) * sub)
        z = _in_proj(x_ref[0, rs, :], gmix_ref[...], win_ref[...])
        qn, kn, v, ua, gvn = _heads(z, gq_ref[...], gk_ref[...], gvg_ref[...], blk_ref[...])
        qn_scr[rs, :] = qn
        ua_scr[rs, :] = ua
        gvn_scr[rs, :] = gvn.astype(jnp.bfloat16)
        k0, k1 = _dup_halves(kn)
        v0, v1 = _dup_halves(v)
        k2_scr[0, ks, :] = k0
        k2_scr[1, ks, :] = k1
        v2_scr[0, ks, :] = v0
        v2_scr[1, ks, :] = v1
        if s == ROW_SPLITS - 1:
            @pl.when(t == nt - 1)
            def _():
                kwin_ref[0] = kn[sub - WINDOW:]
                vwin_ref[0] = v[sub - WINDOW:]

    rows = 4 * WINDOW
    band = 2 * WINDOW
    rr = lax.broadcasted_iota(jnp.int32, (rows, band), 0)
    kk = lax.broadcasted_iota(jnp.int32, (rows, band), 1)
    half = ((rr % WINDOW) >= CHUNK).astype(jnp.int32)
    allowed = (kk >= half * CHUNK) & (kk < (3 + half) * CHUNK)
    bias_mid = jnp.where(allowed, 0.0, NEG)
    bias_first = jnp.where(allowed & (kk >= WINDOW), 0.0, NEG)

    def attn_body(j, carry):
        r0 = pl.multiple_of(j * WINDOW, WINDOW)
        bias = jnp.where((t == 0) & (j == 0), bias_first, bias_mid)
        for g in range(N_KV_HEADS):
            c0 = g * 2 * LANES
            qa = qn_scr[pl.ds(r0, WINDOW), c0:c0 + LANES]
            qb = qn_scr[pl.ds(r0, WINDOW), c0 + LANES:c0 + 2 * LANES]
            k2 = k2_scr[g, pl.ds(r0, band), :]
            v2 = v2_scr[g, pl.ds(r0, band), :]
            pa, pb = _attn_block(qa, qb, k2, v2, bias, _sink_rows(sink_ref, g, WINDOW))
            cat_scr[pl.ds(r0, WINDOW), c0:c0 + LANES] = pa.astype(jnp.bfloat16)
            cat_scr[pl.ds(r0, WINDOW), c0 + LANES:c0 + 2 * LANES] = pb.astype(jnp.bfloat16)
        return carry

    lax.fori_loop(0, nblk, attn_body, 0, unroll=True)

    k2_scr[:, 0:WINDOW, :] = k2_scr[:, tt:tt + WINDOW, :]
    v2_scr[:, 0:WINDOW, :] = v2_scr[:, tt:tt + WINDOW, :]

    for h in range(GATE_HEADS):
        w = _causal_ws(ws_ref, h, MLP_CHUNK)
        bcol = bs_ref[:, h:h + 1]
        for c in range(tt // MLP_CHUNK):
            rs = slice(c * MLP_CHUNK, (c + 1) * MLP_CHUNK)
            cs = slice(h * GATE_DIM, (h + 1) * GATE_DIM)
            mix = jnp.dot(w, gvn_scr[rs, cs], preferred_element_type=jnp.float32) + bcol
            cat_scr[rs, Q_WIDTH + h * GATE_DIM:Q_WIDTH + (h + 1) * GATE_DIM] = (ua_scr[rs, cs] * mix).astype(jnp.bfloat16)

    cnt = cnt_scr[...]
    x1_next = _out_proj(cat_scr[pieces[0], :], x_ref[0, pieces[0], :], wout_ref[...])
    for s, rs in enumerate(pieces):
        x1 = x1_next
        if s + 1 < ROW_SPLITS:
            x1_next = _out_proj(cat_scr[pieces[s + 1], :], x_ref[0, pieces[s + 1], :], wout_ref[...])
        x1_ref[rs, :] = x1
        hp, slab_t, cnt = _route(x1, gffn_ref[...], wr_ref[...], br_ref[...], triu_ref[...], cnt)
        h2_ref[rs, :] = hp
        slabt_ref[:, rs] = slab_t
    cnt_scr[...] = cnt
    cnt_ref[...] = cnt


def _sample_kernel(sink_ref, x_ref, ck_ref, cv_ref, gmix_ref, win_ref, gq_ref, gk_ref, gvg_ref, blk_ref, ws_ref,
                   bs_ref, wout_ref, gffn_ref, wr_ref, br_ref, triu_ref,
                   x1_ref, h2_ref, slabt_ref, kn_ref, v_ref, gvn_ref, cnt_ref,
                   ck2_scr, cv2_scr, k2_scr, v2_scr, qn_scr, ua_scr, cat_scr, *, n_seq, seq_len):
    x = x_ref[...]
    qn, kn, v, ua, gvn = _heads(_in_proj(x, gmix_ref[...], win_ref[...]), gq_ref[...], gk_ref[...], gvg_ref[...],
                                blk_ref[...])
    kn_ref[...] = kn
    v_ref[...] = v
    gvn_ref[...] = gvn
    qn_scr[...] = qn
    ua_scr[...] = ua
    for scr, val in ((k2_scr, kn), (v2_scr, v), (ck2_scr, ck_ref[...]), (cv2_scr, cv_ref[...])):
        a0, a1 = _dup_halves(val)
        scr[0] = a0
        scr[1] = a1

    ws = [_causal_ws(ws_ref, h, seq_len) for h in range(GATE_HEADS)]

    def seq_body(i, carry):
        r0 = pl.multiple_of(i * seq_len, seq_len)
        c0r = pl.multiple_of(i * WINDOW, WINDOW)
        for g in range(N_KV_HEADS):
            c0 = g * 2 * LANES
            qa = qn_scr[pl.ds(r0, seq_len), c0:c0 + LANES]
            qb = qn_scr[pl.ds(r0, seq_len), c0 + LANES:c0 + 2 * LANES]
            k2 = jnp.concatenate([ck2_scr[g, pl.ds(c0r, WINDOW), :], k2_scr[g, pl.ds(r0, seq_len), :]], axis=0)
            v2 = jnp.concatenate([cv2_scr[g, pl.ds(c0r, WINDOW), :], v2_scr[g, pl.ds(r0, seq_len), :]], axis=0)
            pa, pb = _attn_block(qa, qb, k2, v2, None, _sink_rows(sink_ref, g, seq_len))
            cat_scr[pl.ds(r0, seq_len), c0:c0 + LANES] = pa.astype(jnp.bfloat16)
            cat_scr[pl.ds(r0, seq_len), c0 + LANES:c0 + 2 * LANES] = pb.astype(jnp.bfloat16)
        for h in range(GATE_HEADS):
            cs = slice(h * GATE_DIM, (h + 1) * GATE_DIM)
            gv_h = gvn_ref[pl.ds(r0, seq_len), cs].astype(jnp.bfloat16)
            mix = jnp.dot(ws[h], gv_h, preferred_element_type=jnp.float32) + bs_ref[0:seq_len, h:h + 1]
            cat_scr[pl.ds(r0, seq_len), Q_WIDTH + h * GATE_DIM:Q_WIDTH + (h + 1) * GATE_DIM] = (
                ua_scr[pl.ds(r0, seq_len), cs] * mix).astype(jnp.bfloat16)
        return carry

    lax.fori_loop(0, n_seq, seq_body, 0)

    cnt0 = jnp.zeros((N_EXPERTS, LANES), jnp.float32)
    x1 = _out_proj(cat_scr[...], x, wout_ref[...])
    hp, slab_t, new_cnt = _route(x1, gffn_ref[...], wr_ref[...], br_ref[...], triu_ref[...], cnt0)
    x1_ref[...] = x1
    h2_ref[...] = hp
    slabt_ref[...] = slab_t
    cnt_ref[...] = new_cnt


def _expert_kernel(ts_ref, te_ref, cnt_ref, nu_ref, xa_hbm, xb_hbm, wg_ref, wu_ref, wd_ref, oa_hbm, ob_hbm,
                   *rest, emit_bf16, weights_bf16):
    rest = list(rest)
    bf16_out = [rest.pop(0) for _ in range(3)] if emit_bf16 else None
    if not weights_bf16:
        wg_s, wu_s, wd_s = (rest.pop(0) for _ in range(3))
    xa_buf, xb_buf, oa_buf, ob_buf, in_sem, out_sem = rest
    e = pl.program_id(0)
    nbuf, tm, _ = xa_buf.shape
    n_used = nu_ref[0]

    def rows_of(g):
        return pl.ds(pl.multiple_of(g * tm, tm), tm)

    def in_copies(g, slot):
        return (pltpu.make_async_copy(xa_hbm.at[rows_of(g)], xa_buf.at[slot], in_sem.at[0, slot]),
                pltpu.make_async_copy(xb_hbm.at[rows_of(g)], xb_buf.at[slot], in_sem.at[1, slot]))

    def out_copies(g, slot):
        return (pltpu.make_async_copy(oa_buf.at[slot], oa_hbm.at[rows_of(g)], out_sem.at[0, slot]),
                pltpu.make_async_copy(ob_buf.at[slot], ob_hbm.at[rows_of(g)], out_sem.at[1, slot]))

    @pl.when(e == 0)
    def _():
        for g in range(nbuf - 1):
            @pl.when(g < n_used)
            def _():
                for c in in_copies(g, g):
                    c.start(priority=TILE_LOAD_PRIORITY)

    g_lo = ts_ref[e]
    g_hi = te_ref[e]

    if weights_bf16:
        wg_s, wu_s, wd_s = wg_ref.at[0], wu_ref.at[0], wd_ref.at[0]
    elif bf16_out:
        wg_s[...] = wg_ref[0].astype(jnp.bfloat16)
        wu_s[...] = wu_ref[0].astype(jnp.bfloat16)
        wd_s[...] = wd_ref[0].astype(jnp.bfloat16)
        bf16_out[0][0] = wg_s[...]
        bf16_out[1][0] = wu_s[...]
        bf16_out[2][0] = wd_s[...]
    else:
        @pl.when(g_hi > g_lo)
        def _():
            wg_s[...] = wg_ref[0].astype(jnp.bfloat16)
            wu_s[...] = wu_ref[0].astype(jnp.bfloat16)
            wd_s[...] = wd_ref[0].astype(jnp.bfloat16)

    def tile_body(g, slot):
        for c in in_copies(g, slot):
            c.wait()
        ahead = g + (nbuf - 1)
        ahead_slot = jnp.where(slot == 0, nbuf - 1, slot - 1)

        @pl.when(ahead < n_used)
        def _():
            for c in in_copies(ahead, ahead_slot):
                c.start(priority=TILE_LOAD_PRIORITY)

        live = lax.broadcasted_iota(jnp.int32, (tm, xa_buf.shape[2]), 0) < cnt_ref[e] - (g - g_lo) * tm
        ha, la = _unpack_halves(jnp.where(live, xa_buf[slot], jnp.uint32(0)))
        hb, lb = _unpack_halves(jnp.where(live, xb_buf[slot], jnp.uint32(0)))
        xs = jnp.concatenate([ha, hb, la, lb], axis=1).astype(jnp.bfloat16)
        acts = []
        for r in range(0, tm, tm // 2):
            hg = jnp.dot(xs[r:r + tm // 2], wg_s[...], preferred_element_type=jnp.float32)
            hu = jnp.dot(xs[r:r + tm // 2], wu_s[...], preferred_element_type=jnp.float32)
            acts.append((jax.nn.silu(hg) * hu).astype(jnp.bfloat16))
        packed = jnp.concatenate(
            [_pack_halves(jnp.dot(a, wd_s[...], preferred_element_type=jnp.float32)) for a in acts], axis=0)
        q = packed.shape[1] // 2

        @pl.when(g >= nbuf)
        def _():
            for c in out_copies(g - nbuf, slot):
                c.wait()

        oa_buf[slot] = packed[:, :q]
        ob_buf[slot] = packed[:, q:]
        for c in out_copies(g, slot):
            c.start()
        return jnp.where(slot == nbuf - 1, 0, slot + 1)

    lax.fori_loop(g_lo, g_hi, tile_body, g_lo % nbuf)

    @pl.when(e == pl.num_programs(0) - 1)
    def _():
        for back in range(1, nbuf + 1):
            @pl.when(n_used >= back)
            def _():
                g = n_used - back
                for c in out_copies(g, g % nbuf):
                    c.wait()


def _pos_kernel(slabt_ref, offs_ref, pos0_ref, pos1_ref):
    st = slabt_ref[...]
    t = st.shape[1]
    row = lax.broadcasted_iota(jnp.int32, (N_EXPERTS, t), 0).astype(jnp.float32)
    offs = jnp.tile(offs_ref[...], (1, t // LANES))
    for s, out in enumerate((pos0_ref, pos1_ref)):
        first = jnp.sum(jnp.where(row == st[s:s + 1], offs, 0.0), axis=0, keepdims=True)
        out[...] = (first + st[2 + s:3 + s]).astype(jnp.int32)


def _combine_kernel(x1_ref, o1a_ref, o1b_ref, o2a_ref, o2b_ref, slabt_ref, *rest):
    y_ref = rest[-1]
    st = slabt_ref[...]
    slab = jnp.concatenate([st, jnp.zeros((LANES - st.shape[0], st.shape[1]), jnp.float32)], axis=0).T
    g1 = slab[:, 4:5]
    g2 = slab[:, 5:6]
    q = o1a_ref.shape[1]
    for c, (r1, r2) in enumerate(((o1a_ref, o2a_ref), (o1b_ref, o2b_ref))):
        h1, l1 = _unpack_halves(r1[...])
        h2, l2 = _unpack_halves(r2[...])
        hs = slice(c * q, (c + 1) * q)
        ls = slice(2 * q + c * q, 2 * q + (c + 1) * q)
        y_ref[:, hs] = x1_ref[:, hs] + (g1 * h1 + g2 * h2)
        y_ref[:, ls] = x1_ref[:, ls] + (g1 * l1 + g2 * l2)


def _sc_mesh():
    return plsc.VectorSubcoreMesh(core_axis_name="core", subcore_axis_name="subcore")


def _sc_scatter_rows(parts, pos0, pos1, n_rows):
    n_piece = parts[0].shape[1] // SC_COLS

    def sc_kernel(*refs):
        x_refs = refs[:len(parts)]
        i0_hbm, i1_hbm = refs[len(parts):len(parts) + 2]
        o_refs = refs[len(parts) + 2:]
        tok0 = 0
        for x_hbm, arr in zip(x_refs, parts):
            blk0 = tok0 // SC_WINDOW
            for c in range(n_piece):
                def body(x_vmem, i0_vmem, i1_vmem, o_hbm=o_refs[c]):
                    pltpu.sync_copy(x_vmem, o_hbm.at[i0_vmem.at[0]])
                    pltpu.sync_copy(x_vmem, o_hbm.at[i1_vmem.at[0]])

                pltpu.emit_pipeline(
                    body,
                    grid=(arr.shape[0] // SC_WINDOW,),
                    in_specs=[pl.BlockSpec((SC_WINDOW, SC_COLS), lambda i, c=c: (i, c)),
                              pl.BlockSpec((1, SC_WINDOW), lambda i, blk0=blk0: (0, blk0 + i)),
                              pl.BlockSpec((1, SC_WINDOW), lambda i, blk0=blk0: (0, blk0 + i))],
                    out_specs=[],
                    core_axis_name=("core", "subcore"),
                    dimension_semantics=(pltpu.PARALLEL,),
                )(x_hbm, i0_hbm, i1_hbm)
            tok0 += arr.shape[0]

    piece = jax.ShapeDtypeStruct((n_rows, SC_COLS), jnp.uint32)
    return pl.kernel(sc_kernel, out_type=(piece,) * n_piece, mesh=_sc_mesh(), name="scatter_rows")(
        *parts, pos0, pos1)


def _sc_gather_rows(pieces, pos_list):
    n_tok = pos_list[0].shape[1]

    def sc_kernel(*refs):
        s_refs = refs[:len(pieces)]
        i_refs = refs[len(pieces):len(pieces) + len(pos_list)]
        o_refs = refs[len(pieces) + len(pos_list):]
        k = 0
        for i_hbm in i_refs:
            for s_hbm in s_refs:
                def body(i_vmem, o_vmem, s_hbm=s_hbm):
                    pltpu.sync_copy(s_hbm.at[i_vmem.at[0]], o_vmem)

                pltpu.emit_pipeline(
                    body,
                    grid=(n_tok // SC_WINDOW,),
                    in_specs=[pl.BlockSpec((1, SC_WINDOW), lambda i: (0, i))],
                    out_specs=[pl.BlockSpec((SC_WINDOW, SC_COLS), lambda i: (i, 0))],
                    core_axis_name=("core", "subcore"),
                    dimension_semantics=(pltpu.PARALLEL,),
                )(i_hbm, o_refs[k])
                k += 1

    out = jax.ShapeDtypeStruct((n_tok, SC_COLS), jnp.uint32)
    outs = pl.kernel(sc_kernel, out_type=(out,) * (len(pieces) * len(pos_list)), mesh=_sc_mesh(),
                     name="gather_rows")(*pieces, *pos_list)
    return [outs[j * len(pieces):(j + 1) * len(pieces)] for j in range(len(pos_list))]


def _const_spec(shape, single_buffer=False):
    nd = len(shape)
    mode = pl.Buffered(1) if single_buffer else None
    return pl.BlockSpec(shape, lambda *_: (0,) * nd, pipeline_mode=mode)


def kernel(x_prompt, x_sample, cache_attn_k, cache_attn_v, g_mix, w_in, g_q, g_k, g_v, attn_sinks, w_s, b_s,
           w_out, g_ffn, w_coarse, b_coarse, w_fine, b_fine, w_gate, w_up, w_down):
    nb, seq, d = x_prompt.shape
    ns, slen, _ = x_sample.shape
    n_p = nb * seq
    n_s = ns * slen
    n_tok = n_p + n_s
    tt = TOKEN_TILE
    assert seq % tt == 0 and n_s % tt == 0 and d == D_MODEL
    nt = seq // tt
    bf = jnp.bfloat16
    f32 = jnp.float32

    l = 0
    gmix = g_mix[l].reshape(1, d)
    win = w_in[l].astype(bf)
    gq = (jnp.tile(g_q[l], N_Q_HEADS) * (HEAD_DIM ** -0.5)).reshape(1, Q_WIDTH)
    gk = jnp.tile(g_k[l], N_KV_HEADS).reshape(1, KV_WIDTH)
    gvg = g_v[l].reshape(1, GATE_WIDTH)
    sinks = attn_sinks[l].reshape(N_Q_HEADS).astype(f32)
    ws = w_s[l].astype(bf)
    bs = b_s[l].T
    wout = w_out[l].astype(bf)
    gffn = g_ffn[l].reshape(1, d)
    wr = jnp.concatenate([w_coarse[l], jnp.transpose(w_fine[l], (1, 0, 2)).reshape(d, N_EXPERTS),
                          jnp.zeros((d, ROUTE_LANES - N_GROUPS - N_EXPERTS), f32)], axis=1)
    wr = wr.astype(bf).T
    br = jnp.concatenate([b_coarse[l], b_fine[l].reshape(-1),
                          jnp.zeros((ROUTE_LANES - N_GROUPS - N_EXPERTS,), f32)])
    br = jnp.broadcast_to(br[:, None], (ROUTE_LANES, LANES))
    cnt_shape = (N_EXPERTS, LANES)
    ii = jnp.arange(LANES)
    blk = (ii[:, None] // HEAD_DIM == ii[None, :] // HEAD_DIM).astype(bf)
    u32 = jnp.uint32
    dh = d // 2

    def strict_triu(n):
        r = jnp.arange(n)
        return (r[:, None] < r[None, :]).astype(bf)

    weight_args = (gmix, win, gq, gk, gvg, blk, ws, bs, wout, gffn, wr, br)
    weight_specs = [_const_spec(a.shape, single_buffer=True) for a in weight_args]
    smem_spec = pl.BlockSpec(memory_space=pltpu.SMEM)

    xs2 = x_sample.reshape(n_s, d)
    ck = cache_attn_k[l].reshape(ns * WINDOW, KV_WIDTH)
    cv = cache_attn_v[l].reshape(ns * WINDOW, KV_WIDTH)
    tok_out = lambda n, w, dt: jax.ShapeDtypeStruct((n, w), dt)
    x1_s, h2_s, slabt_s, kn_s, v_s, gvn_s, cnt_s = pl.pallas_call(
        functools.partial(_sample_kernel, n_seq=ns, seq_len=slen),
        grid=(1,),
        in_specs=[smem_spec, _const_spec((n_s, d)), _const_spec(ck.shape), _const_spec(cv.shape)]
                 + weight_specs + [_const_spec((n_s, n_s))],
        out_specs=[_const_spec((n_s, d)), _const_spec((n_s, dh)),
                   _const_spec((ROUTE_ROWS, n_s)),
                   _const_spec((n_s, KV_WIDTH)), _const_spec((n_s, KV_WIDTH)), _const_spec((n_s, GATE_WIDTH)),
                   _const_spec(cnt_shape)],
        out_shape=[tok_out(n_s, d, f32), tok_out(n_s, dh, u32),
                   jax.ShapeDtypeStruct((ROUTE_ROWS, n_s), f32),
                   jax.ShapeDtypeStruct((n_s, KV_WIDTH), f32), jax.ShapeDtypeStruct((n_s, KV_WIDTH), f32),
                   jax.ShapeDtypeStruct((n_s, GATE_WIDTH), f32), jax.ShapeDtypeStruct(cnt_shape, f32)],
        scratch_shapes=[pltpu.VMEM((N_KV_HEADS, ns * WINDOW, LANES), bf), pltpu.VMEM((N_KV_HEADS, ns * WINDOW, LANES), bf),
                        pltpu.VMEM((N_KV_HEADS, n_s, LANES), bf), pltpu.VMEM((N_KV_HEADS, n_s, LANES), bf),
                        pltpu.VMEM((n_s, Q_WIDTH), bf), pltpu.VMEM((n_s, GATE_WIDTH), f32),
                        pltpu.VMEM((n_s, d), bf)],
        compiler_params=pltpu.CompilerParams(dimension_semantics=("arbitrary",), vmem_limit_bytes=VMEM_LIMIT),
        name="mixer_sample",
    )(sinks, xs2, ck, cv, *weight_args, strict_triu(n_s))

    def prompt_mixer(b0, nbw, cnt_in):
        n_w = nbw * seq
        return pl.pallas_call(
            _prompt_kernel,
            grid=(nbw, nt),
            in_specs=[smem_spec, pl.BlockSpec((1, tt, d), lambda b, t: (b0 + b, t, 0))] + weight_specs
                     + [_const_spec((tt // ROW_SPLITS, tt // ROW_SPLITS)), _const_spec(cnt_shape)],
            out_specs=[pl.BlockSpec((tt, d), lambda b, t: (b * nt + t, 0)),
                       pl.BlockSpec((tt, dh), lambda b, t: (b * nt + t, 0)),
                       pl.BlockSpec((ROUTE_ROWS, tt), lambda b, t: (0, b * nt + t)),
                       pl.BlockSpec((1, WINDOW, KV_WIDTH), lambda b, t: (b, 0, 0)),
                       pl.BlockSpec((1, WINDOW, KV_WIDTH), lambda b, t: (b, 0, 0)),
                       _const_spec(cnt_shape)],
            out_shape=[tok_out(n_w, d, f32), tok_out(n_w, dh, u32),
                       jax.ShapeDtypeStruct((ROUTE_ROWS, n_w), f32),
                       jax.ShapeDtypeStruct((nbw, WINDOW, KV_WIDTH), f32),
                       jax.ShapeDtypeStruct((nbw, WINDOW, KV_WIDTH), f32),
                       jax.ShapeDtypeStruct(cnt_shape, f32)],
            scratch_shapes=[pltpu.VMEM((N_KV_HEADS, tt + WINDOW, LANES), bf),
                            pltpu.VMEM((N_KV_HEADS, tt + WINDOW, LANES), bf),
                            pltpu.VMEM((tt, Q_WIDTH), bf), pltpu.VMEM((tt, GATE_WIDTH), f32),
                            pltpu.VMEM((tt, GATE_WIDTH), bf), pltpu.VMEM((tt, d), bf),
                            pltpu.VMEM(cnt_shape, f32)],
            compiler_params=pltpu.CompilerParams(dimension_semantics=("arbitrary", "arbitrary"),
                                                 vmem_limit_bytes=VMEM_LIMIT),
            name="mixer_prompt",
        )(sinks, x_prompt, *weight_args, strict_triu(tt // ROW_SPLITS), cnt_in)

    tm = EXPERT_TILE
    assert dh == 2 * SC_COLS and n_s % SC_WINDOW == 0 and seq % SC_WINDOW == 0

    def expert_pass(h2_parts, slabt_parts, cnt, weights, emit_bf16):
        n_w = sum(a.shape[0] for a in h2_parts)
        slabt = jnp.concatenate(slabt_parts, axis=1) if len(slabt_parts) > 1 else slabt_parts[0]
        counts = cnt[:, 0].astype(jnp.int32)
        tiles = (counts + tm - 1) // tm
        tile_end = jnp.cumsum(tiles)
        offs = (tile_end - tiles) * tm
        n_used = tile_end[-1]
        n_rows = ((2 * n_w) // tm + N_EXPERTS) * tm
        lane_tiles = n_w // LANES
        pos_blk = LANES * max(k for k in range(1, POS_BLOCK_TILES + 1) if lane_tiles % k == 0)
        offs_b = jnp.broadcast_to(offs.astype(f32)[:, None], cnt_shape)
        pos_row = jax.ShapeDtypeStruct((1, n_w), jnp.int32)
        pos0, pos1 = pl.pallas_call(
            _pos_kernel,
            grid=(n_w // pos_blk,),
            in_specs=[pl.BlockSpec((ROUTE_ROWS, pos_blk), lambda i: (0, i)), _const_spec(cnt_shape)],
            out_specs=[pl.BlockSpec((1, pos_blk), lambda i: (0, i)), pl.BlockSpec((1, pos_blk), lambda i: (0, i))],
            out_shape=[pos_row, pos_row],
            compiler_params=pltpu.CompilerParams(dimension_semantics=("arbitrary",)),
            name="sorted_pos",
        )(slabt, offs_b)

        xs_a, xs_b = _sc_scatter_rows(h2_parts, pos0, pos1, n_rows)

        w_map = lambda e, *_: (e, 0, 0)
        hbm = pl.BlockSpec(memory_space=pl.ANY)
        piece = jax.ShapeDtypeStruct((n_rows, SC_COLS), u32)
        tile_buf = pltpu.VMEM((EXPERT_BUFFERS, tm, SC_COLS), u32)
        tile_sems = pltpu.SemaphoreType.DMA((2, EXPERT_BUFFERS))
        weights_bf16 = weights[0].dtype == bf
        w_specs = [pl.BlockSpec((1, d, D_EXPERT), w_map), pl.BlockSpec((1, d, D_EXPERT), w_map),
                   pl.BlockSpec((1, D_EXPERT, d), w_map)]
        cast_scratch = [] if weights_bf16 else [pltpu.VMEM((d, D_EXPERT), bf), pltpu.VMEM((d, D_EXPERT), bf),
                                                pltpu.VMEM((D_EXPERT, d), bf)]
        outs = pl.pallas_call(
            functools.partial(_expert_kernel, emit_bf16=emit_bf16, weights_bf16=weights_bf16),
            grid_spec=pltpu.PrefetchScalarGridSpec(
                num_scalar_prefetch=4,
                grid=(N_EXPERTS,),
                in_specs=[hbm, hbm] + w_specs,
                out_specs=[hbm, hbm] + (w_specs if emit_bf16 else []),
                scratch_shapes=cast_scratch + [tile_buf, tile_buf, tile_buf, tile_buf, tile_sems, tile_sems]),
            out_shape=[piece, piece] + ([jax.ShapeDtypeStruct(w.shape, bf) for w in weights] if emit_bf16 else []),
            compiler_params=pltpu.CompilerParams(dimension_semantics=("arbitrary",), vmem_limit_bytes=VMEM_LIMIT),
            name="experts",
        )((tile_end - tiles).astype(jnp.int32), tile_end.astype(jnp.int32), counts,
          n_used.reshape(1).astype(jnp.int32), xs_a, xs_b, *weights)

        return _sc_gather_rows(outs[:2], [pos0, pos1]), tuple(outs[2:])

    def combine(x1, slab_t, gathered, g_blk0, y_rows, y_blk0, y_prev):
        (o1a, o1b), (o2a, o2b) = gathered
        tok = lambda i: (i, 0)
        args = [x1, o1a, o1b, o2a, o2b, slab_t]
        in_specs = ([pl.BlockSpec((tt, d), tok)] + [pl.BlockSpec((tt, SC_COLS), lambda i: (g_blk0 + i, 0))] * 4
                    + [pl.BlockSpec((ROUTE_ROWS, tt), lambda i: (0, i))])
        aliases = {}
        if y_prev is not None:
            aliases = {len(args): 0}
            args.append(y_prev)
            in_specs.append(pl.BlockSpec(memory_space=pl.ANY))
        return pl.pallas_call(
            _combine_kernel,
            grid=(x1.shape[0] // tt,),
            in_specs=in_specs,
            out_specs=pl.BlockSpec((tt, d), lambda i: (y_blk0 + i, 0)),
            out_shape=jax.ShapeDtypeStruct((y_rows, d), f32),
            input_output_aliases=aliases,
            compiler_params=pltpu.CompilerParams(dimension_semantics=("arbitrary",)),
            name="combine",
        )(*args)

    unit = nb // sum(PROMPT_WAVES)
    assert unit * sum(PROMPT_WAVES) == nb
    wave_seqs = [unit * r for r in PROMPT_WAVES]
    wave_b0 = [sum(wave_seqs[:w]) for w in range(len(wave_seqs))]
    y_p = None
    y_s = None
    kwins, vwins = [], []
    weights = (w_gate[l], w_up[l], w_down[l])
    mixed = []
    for w, (b0, nbw) in enumerate(zip(wave_b0, wave_seqs)):
        cnt_in = cnt_s if w == 0 else jnp.zeros(cnt_shape, f32)
        mixed.append(prompt_mixer(b0, nbw, cnt_in))
    for w, (x1_w, h2_w, slabt_w, kwin_w, vwin_w, cnt_w) in enumerate(mixed):
        kwins.append(kwin_w)
        vwins.append(vwin_w)
        h2_parts, slabt_parts = [h2_w], [slabt_w]
        if w == 0:
            h2_parts.append(h2_s)
            slabt_parts.append(slabt_s)
        last = w == len(mixed) - 1
        gathered, cast = expert_pass(h2_parts, slabt_parts, cnt_w, weights, emit_bf16=(w == 0 and not last))
        if cast:
            weights = cast
        y_p = combine(x1_w, slabt_w, gathered, 0, n_p, wave_b0[w] * nt, y_p)
        if w == 0:
            y_s = combine(x1_s, slabt_s, gathered, wave_seqs[0] * nt, n_s, 0, None)
    y_p = y_p.reshape(nb, seq, d)
    y_s = y_s.reshape(ns, slen, d)
    kwin = jnp.concatenate(kwins, axis=0) if len(kwins) > 1 else kwins[0]
    vwin = jnp.concatenate(vwins, axis=0) if len(vwins) > 1 else vwins[0]

    kv_shape = (1, nb, WINDOW, N_KV_HEADS, HEAD_DIM)
    new_k_p = kwin.reshape(kv_shape)
    new_v_p = vwin.reshape(kv_shape)
    keep = WINDOW - slen
    ck4 = cache_attn_k[l][:, WINDOW - keep:]
    cv4 = cache_attn_v[l][:, WINDOW - keep:]
    new_k_s = jnp.concatenate([ck4, kn_s.reshape(ns, slen, N_KV_HEADS, HEAD_DIM)], axis=1)[None]
    new_v_s = jnp.concatenate([cv4, v_s.reshape(ns, slen, N_KV_HEADS, HEAD_DIM)], axis=1)[None]
    new_gv_s = gvn_s.reshape(1, ns, slen, GATE_HEADS, GATE_DIM)
    return (y_p, y_s, new_k_p, new_v_p, new_k_s, new_v_s, new_gv_s)
```
